```python
import jax, jax.numpy as jnp
from jax import lax
import numpy as np

D_MODEL = 2048
BATCH = 2
SEQ = 8192
DEPTH = 1

CHUNK = 64
D_MIX = D_MODEL
D_RWKV = D_MIX // 2
D_CONV = D_MIX - D_RWKV
HEAD_DIM = 64
N_RWKV_HEADS = D_RWKV // HEAD_DIM
N_CONV_GROUPS = 16
CONV_GROUP_DIM = D_CONV // N_CONV_GROUPS
CONV_WIDTH = 3
D_DECAY_LORA = 64
D_AAA_LORA = 64
D_GATE_LORA = 160
N_GROUPS = 4
EXPERTS_PER_GROUP = 8
N_EXPERTS = N_GROUPS * EXPERTS_PER_GROUP
TOP_K = 2
D_EXPERT = D_MODEL // 4
EXPERT_BLOCK = 128
RMS_EPS = 1e-6
LNX_EPS = 64e-5

kernel_name = "hymba_rwkv7_shortconv_hmoe_adaln"


def rms_norm(x, w):
    xf = x.astype(jnp.float32)
    y = xf * lax.rsqrt(jnp.mean(xf * xf, axis=-1, keepdims=True) + RMS_EPS)
    return (y * w.astype(jnp.float32)).astype(x.dtype)


def token_shift(z):
    return jnp.pad(z, ((0, 0), (1, 0), (0, 0)))[:, :-1]


def wkv7_scan(r, w, k, v, a, b):
    bsz, t_len, n_h, n_d = r.shape

    def to_chunks(z):
        return jnp.moveaxis(z, 1, 0).reshape(t_len // CHUNK, CHUNK, bsz, n_h, n_d)

    def step(S, inp):
        r_t, w_t, k_t, v_t, a_t, b_t = inp
        sa = jnp.einsum('bhvk,bhk->bhv', S, a_t)
        S = S * w_t[:, :, None, :] + sa[..., None] * b_t[:, :, None, :] + v_t[..., None] * k_t[:, :, None, :]
        y_t = jnp.einsum('bhvk,bhk->bhv', S, r_t)
        return S, y_t

    def chunk_step(S, chunk_inp):
        return lax.scan(step, S, chunk_inp)

    S0 = jnp.zeros((bsz, n_h, n_d, n_d), jnp.float32)
    _, y = lax.scan(chunk_step, S0, tuple(to_chunks(z) for z in (r, w, k, v, a, b)))
    return jnp.moveaxis(y.reshape(t_len, bsz, n_h, n_d), 0, 1)


def rwkv7_group(h, r, k, v, mu_r, mu_k, mu_v, mu_w, mu_a, mu_g, w0, w_decay1, w_decay2,
                a0, w_aaa1, w_aaa2, w_gate1, w_gate2, k_k, k_a, r_k, lnx_w, lnx_b):
    f32 = jnp.float32
    bsz, t_len, _ = h.shape
    hs = (bsz, t_len, N_RWKV_HEADS, HEAD_DIM)
    r = r + (token_shift(r) - r) * mu_r
    k = k + (token_shift(k) - k) * mu_k
    v = v + (token_shift(v) - v) * mu_v
    dh = token_shift(h) - h
    xw = h + dh * mu_w
    xa = h + dh * mu_a
    xg = h + dh * mu_g
    w_log = -jax.nn.softplus(-(w0 + jnp.tanh(xw @ w_decay1) @ w_decay2).astype(f32)) - 0.5
    decay = jnp.exp(-jnp.exp(w_log)).reshape(hs)
    a_gate = jax.nn.sigmoid((a0 + (xa @ w_aaa1) @ w_aaa2).astype(f32)).reshape(hs)
    g = (jax.nn.sigmoid(xg @ w_gate1) @ w_gate2).astype(f32)
    rh = r.astype(f32).reshape(hs)
    kh = k.astype(f32).reshape(hs)
    vh = v.astype(f32).reshape(hs)
    kk = kh * k_k.astype(f32).reshape(N_RWKV_HEADS, HEAD_DIM)
    kk = kk * lax.rsqrt(jnp.maximum(jnp.sum(kk * kk, axis=-1, keepdims=True), 1e-24))
    kh = kh * (1.0 + (a_gate - 1.0) * k_a.astype(f32).reshape(N_RWKV_HEADS, HEAD_DIM))
    y = wkv7_scan(rh, decay, kh, vh, -kk, kk * a_gate)
    mean = jnp.mean(y, axis=-1, keepdims=True)
    var = jnp.mean(jnp.square(y - mean), axis=-1, keepdims=True)
    y = ((y - mean) * lax.rsqrt(var + LNX_EPS)).reshape(bsz, t_len, D_RWKV)
    y = y * lnx_w.astype(f32) + lnx_b.astype(f32)
    bonus = jnp.sum(rh * kh * r_k.astype(f32), axis=-1, keepdims=True) * vh
    y = y + bonus.reshape(bsz, t_len, D_RWKV)
    return (y * g).astype(h.dtype)


def short_conv_group(u, gate_b, gate_c, conv_w, conv_norm_w):
    pre = gate_c * u
    y = lax.conv_general_dilated(pre, conv_w[:, None, :], window_strides=(1,),
                                 padding=[(CONV_WIDTH - 1, 0)],
                                 dimension_numbers=('NWC', 'WIO', 'NWC'),
                                 feature_group_count=D_CONV)
    y = gate_b * y
    bsz, t_len, _ = y.shape
    yg = y.astype(jnp.float32).reshape(bsz, t_len, N_CONV_GROUPS, CONV_GROUP_DIM)
    yg = yg * lax.rsqrt(jnp.mean(yg * yg, axis=-1, keepdims=True) + RMS_EPS)
    return (yg.reshape(bsz, t_len, D_CONV) * conv_norm_w.astype(jnp.float32)).astype(u.dtype)


def hierarchical_moe(x2, w_grp, b_grp, w_exp, b_exp, exp_gate, exp_up, exp_down):
    f32 = jnp.float32
    n_tok = x2.shape[0]
    grp_logits = (x2 @ w_grp).astype(f32) + b_grp.astype(f32)
    grp_prob = jax.nn.softmax(grp_logits, axis=-1)
    grp = jnp.argmax(grp_logits, axis=-1).astype(jnp.int32)
    p_grp = jnp.take_along_axis(grp_prob, grp[:, None], axis=-1)
    exp_logits = ((x2 @ w_exp).astype(f32) + b_exp.astype(f32)).reshape(n_tok, N_GROUPS, EXPERTS_PER_GROUP)
    in_grp = jnp.take_along_axis(exp_logits, grp[:, None, None], axis=1)[:, 0]
    top_logit, top_local = lax.top_k(in_grp, TOP_K)
    top_w = jax.nn.softmax(top_logit, axis=-1) * p_grp
    top_e = grp[:, None] * EXPERTS_PER_GROUP + top_local.astype(jnp.int32)

    m = n_tok * TOP_K
    e_flat = top_e.reshape(m)
    tok_flat = jnp.repeat(jnp.arange(n_tok, dtype=jnp.int32), TOP_K)
    w_flat = top_w.reshape(m)
    order = jnp.argsort(e_flat)
    e_sorted, tok_sorted, w_sorted = e_flat[order], tok_flat[order], w_flat[order]
    counts = jnp.zeros((N_EXPERTS,), jnp.int32).at[e_flat].add(1)
    starts = jnp.cumsum(counts) - counts
    padded = (counts + EXPERT_BLOCK - 1) // EXPERT_BLOCK * EXPERT_BLOCK
    pad_ends = jnp.cumsum(padded)
    pad_starts = pad_ends - padded
    dest = pad_starts[e_sorted] + (jnp.arange(m, dtype=jnp.int32) - starts[e_sorted])
    n_rows = (m + EXPERT_BLOCK - 1) // EXPERT_BLOCK * EXPERT_BLOCK + N_EXPERTS * EXPERT_BLOCK
    n_blocks = n_rows // EXPERT_BLOCK
    row_tok = jnp.zeros((n_rows,), jnp.int32).at[dest].set(tok_sorted)
    row_w = jnp.zeros((n_rows,), f32).at[dest].set(w_sorted)
    block_start = jnp.arange(n_blocks, dtype=jnp.int32) * EXPERT_BLOCK
    block_e = jnp.minimum(jnp.searchsorted(pad_ends, block_start, side='right'), N_EXPERTS - 1).astype(jnp.int32)

    def expert_block(args):
        e, toks, wts = args
        xb = x2[toks]
        hid = jax.nn.silu(xb @ exp_gate[e]) * (xb @ exp_up[e])
        return (hid @ exp_down[e]) * wts[:, None].astype(xb.dtype)

    yb = lax.map(expert_block, (block_e, row_tok.reshape(n_blocks, EXPERT_BLOCK),
                                row_w.reshape(n_blocks, EXPERT_BLOCK)))
    return jax.ops.segment_sum(yb.reshape(n_rows, x2.shape[-1]), row_tok, num_segments=n_tok)


def setup_inputs(seed: int = 0) -> dict:
    key = jax.random.key(seed)
    ks = iter(jax.random.split(key, 48))
    L, D = DEPTH, D_MODEL
    n_in = 3 * D_RWKV + 3 * D_CONV

    def nrm(shape, scale):
        return jax.random.normal(next(ks), shape, jnp.float32) * scale

    def uni(shape, lo, hi):
        return jax.random.uniform(next(ks), shape, jnp.float32, lo, hi)

    return {
        "x": nrm((BATCH, SEQ, D), 1.0),
        "c": nrm((BATCH, D), 1.0),
        "w_ada": nrm((L, D, 6 * D), D ** -0.5),
        "b_ada": nrm((L, 6 * D), 0.02),
        "norm1_w": 1.0 + nrm((L, D), 0.02),
        "w_in": nrm((L, D, n_in), D ** -0.5),
        "mu_r": uni((L, D_RWKV), 0.0, 1.0),
        "mu_k": uni((L, D_RWKV), 0.0, 1.0),
        "mu_v": uni((L, D_RWKV), 0.0, 1.0),
        "mu_w": uni((L, D), 0.0, 1.0),
        "mu_a": uni((L, D), 0.0, 1.0),
        "mu_g": uni((L, D), 0.0, 1.0),
        "w0": uni((L, D_RWKV), -6.0, 1.0),
        "w_decay1": nrm((L, D, D_DECAY_LORA), D ** -0.5),
        "w_decay2": nrm((L, D_DECAY_LORA, D_RWKV), 0.5 * D_DECAY_LORA ** -0.5),
        "a0": nrm((L, D_RWKV), 0.5),
        "w_aaa1": nrm((L, D, D_AAA_LORA), D ** -0.5),
        "w_aaa2": nrm((L, D_AAA_LORA, D_RWKV), D_AAA_LORA ** -0.5),
        "w_gate1": nrm((L, D, D_GATE_LORA), D ** -0.5),
        "w_gate2": nrm((L, D_GATE_LORA, D_RWKV), D_GATE_LORA ** -0.5),
        "k_k": 0.85 + nrm((L, D_RWKV), 0.05),
        "k_a": 1.0 + nrm((L, D_RWKV), 0.05),
        "r_k": nrm((L, N_RWKV_HEADS, HEAD_DIM), 0.1),
        "lnx_w": 1.0 + nrm((L, D_RWKV), 0.02),
        "lnx_b": nrm((L, D_RWKV), 0.02),
        "conv_w": nrm((L, CONV_WIDTH, D_CONV), CONV_WIDTH ** -0.5),
        "conv_norm_w": 1.0 + nrm((L, D_CONV), 0.02),
        "w_out": nrm((L, D_MIX, D), D_MIX ** -0.5),
        "norm2_w": 1.0 + nrm((L, D), 0.02),
        "w_grp": nrm((L, D, N_GROUPS), D ** -0.5),
        "b_grp": nrm((L, N_GROUPS), 0.01),
        "w_exp": nrm((L, D, N_EXPERTS), D ** -0.5),
        "b_exp": nrm((L, N_EXPERTS), 0.01),
        "exp_gate": nrm((L, N_EXPERTS, D, D_EXPERT), D ** -0.5),
        "exp_up": nrm((L, N_EXPERTS, D, D_EXPERT), D ** -0.5),
        "exp_down": nrm((L, N_EXPERTS, D_EXPERT, D), D_EXPERT ** -0.5),
        "final_w": 1.0 + nrm((D,), 0.02),
    }


def reference(x, c, w_ada, b_ada, norm1_w, w_in, mu_r, mu_k, mu_v, mu_w, mu_a, mu_g, w0, w_decay1, w_decay2,
              a0, w_aaa1, w_aaa2, w_gate1, w_gate2, k_k, k_a, r_k, lnx_w, lnx_b, conv_w, conv_norm_w, w_out,
              norm2_w, w_grp, b_grp, w_exp, b_exp, exp_gate, exp_up, exp_down, final_w):
    bsz, t_len, d = x.shape
    split_at = list(np.cumsum([D_RWKV, D_RWKV, D_RWKV, D_CONV, D_CONV]))
    for l in range(DEPTH):
        mod = (jax.nn.silu(c) @ w_ada[l] + b_ada[l])[:, None, :]
        sh1, sc1, g1, sh2, sc2, g2 = jnp.split(mod, 6, axis=-1)
        h = rms_norm(x, norm1_w[l]) * (1.0 + sc1) + sh1
        proj = h @ w_in[l]
        r, k, v, u, gate_b, gate_c = jnp.split(proj, split_at, axis=-1)
        y_rwkv = rwkv7_group(h, r, k, v, mu_r[l], mu_k[l], mu_v[l], mu_w[l], mu_a[l], mu_g[l], w0[l],
                             w_decay1[l], w_decay2[l], a0[l], w_aaa1[l], w_aaa2[l], w_gate1[l], w_gate2[l],
                             k_k[l], k_a[l], r_k[l], lnx_w[l], lnx_b[l])
        y_conv = short_conv_group(u, gate_b, gate_c, conv_w[l], conv_norm_w[l])
        y_mix = jnp.concatenate([y_rwkv, y_conv], axis=-1) @ w_out[l]
        x = x + g1 * y_mix
        h2 = rms_norm(x, norm2_w[l]) * (1.0 + sc2) + sh2
        y_ffn = hierarchical_moe(h2.reshape(bsz * t_len, d), w_grp[l], b_grp[l], w_exp[l], b_exp[l],
                                 exp_gate[l], exp_up[l], exp_down[l]).reshape(bsz, t_len, d)
        x = x + g2 * y_ffn
    return rms_norm(x, final_w)
```

```python
import functools

import jax
import jax.numpy as jnp
from jax import lax
from jax.experimental import pallas as pl
from jax.experimental.pallas import tpu as pltpu

F32 = jnp.float32
BF16 = jnp.bfloat16

HEAD_DIM = 64
WKV_CHUNK = 64
GROUP_LANES = 256
N_ROUTE_GROUPS = 4
EXPERTS_PER_GROUP = 8
N_EXPERTS = N_ROUTE_GROUPS * EXPERTS_PER_GROUP
CONV_GROUP_DIM = 64
RMS_EPS = 1e-6
LNX_EPS = 64e-5
LANES = 128
LORA_PAD = 512
ROUTE_BLOCK = 256
VMEM_LIMIT = 56 * 1024 * 1024


def _dot(a, b, precision=None):
    return jnp.dot(a, b, preferred_element_type=F32, precision=precision)


def _dot_nt(a, b):
    return lax.dot_general(a, b, (((1,), (1,)), ((), ())), preferred_element_type=F32)


def _dot_tn(a, b):
    return lax.dot_general(a, b, (((0,), (0,)), ((), ())), preferred_element_type=F32)


def _sigmoid(x):
    return 1.0 / (1.0 + jnp.exp(-x))


def _shift_rows(x, carry_row):
    rolled = pltpu.roll(x, 1, axis=0)
    row = lax.broadcasted_iota(jnp.int32, x.shape, 0)
    return jnp.where(row == 0, carry_row, rolled)


def _seg_ones(n, seg):
    r = lax.broadcasted_iota(jnp.int32, (n, n), 0) // seg
    c = lax.broadcasted_iota(jnp.int32, (n, n), 1) // seg
    return jnp.where(r == c, 1.0, 0.0).astype(BF16)


def _seg_sum(x, ones_bd):
    hi = x.astype(BF16)
    lo = (x - hi.astype(F32)).astype(BF16)
    return _dot(hi, ones_bd) + _dot(lo, ones_bd)


def _ada_kernel(cb_ref, w_ref, b_ref, o_ref):
    nb = cb_ref.shape[0]
    tn = w_ref.shape[1]
    for b in range(nb):
        cv = cb_ref[b]
        s = cv * _sigmoid(cv)
        cols = [jnp.sum(s * w_ref[:, j * LANES:(j + 1) * LANES], axis=0, keepdims=True)
                for j in range(tn // LANES)]
        o_ref[b:b + 1, :] = jnp.concatenate(cols, axis=1) + b_ref[...]


def _ada(c, w_ada, b_ada):
    nb, d = c.shape
    n_out = w_ada.shape[1]
    tn = 512
    cb = jnp.broadcast_to(c[:, :, None], (nb, d, LANES))
    return pl.pallas_call(
        _ada_kernel,
        out_shape=jax.ShapeDtypeStruct((nb, n_out), F32),
        grid=(n_out // tn,),
        in_specs=[pl.BlockSpec((nb, d, LANES), lambda j: (0, 0, 0)),
                  pl.BlockSpec((d, tn), lambda j: (0, j)),
                  pl.BlockSpec((1, tn), lambda j: (0, j))],
        out_specs=pl.BlockSpec((nb, tn), lambda j: (0, j)),
        compiler_params=pltpu.CompilerParams(dimension_semantics=("arbitrary",),
                                             vmem_limit_bytes=VMEM_LIMIT),
        name="ada",
    )(cb, w_ada, b_ada.reshape(1, n_out))


def _inproj_kernel(x_ref, mod_ref, nw_ref, w_ref, h_ref, p_ref, hs_ref):
    @pl.when(pl.program_id(1) == 0)
    def _():
        x = x_ref[...]
        ms = jnp.mean(x * x, axis=-1, keepdims=True)
        y = x * lax.rsqrt(ms + RMS_EPS) * nw_ref[...]
        h = y * (1.0 + mod_ref[0, 1:2, :]) + mod_ref[0, 0:1, :]
        hb = h.astype(BF16)
        hs_ref[...] = hb
        h_ref[...] = hb

    p_ref[...] = _dot(hs_ref[...], w_ref[...])


def _inproj(x2, mod3, norm_w, w_in_bf, t_len):
    n, d = x2.shape
    n_in = w_in_bf.shape[1]
    tm = min(512, t_len)
    tn = 1024
    return pl.pallas_call(
        _inproj_kernel,
        out_shape=(jax.ShapeDtypeStruct((n, d), BF16), jax.ShapeDtypeStruct((n, n_in), F32)),
        grid=(n // tm, n_in // tn),
        in_specs=[pl.BlockSpec((tm, d), lambda i, j: (i, 0)),
                  pl.BlockSpec((1, 6, d), lambda i, j: (i * tm // t_len, 0, 0)),
                  pl.BlockSpec((1, d), lambda i, j: (0, 0)),
                  pl.BlockSpec((d, tn), lambda i, j: (0, j))],
        out_specs=(pl.BlockSpec((tm, d), lambda i, j: (i, 0)),
                   pl.BlockSpec((tm, tn), lambda i, j: (i, j))),
        scratch_shapes=[pltpu.VMEM((tm, d), BF16)],
        compiler_params=pltpu.CompilerParams(dimension_semantics=("arbitrary", "arbitrary"),
                                             vmem_limit_bytes=VMEM_LIMIT),
        name="inproj",
    )(x2, mod3, norm_w.reshape(1, d), w_in_bf)


def _lora_kernel(h_ref, w1_ref, mu_ref, wd2_ref, wa2_ref, wg2_ref, w0_ref, a0_ref,
                 lw_ref, ag_ref, gg_ref, carry_ref):
    @pl.when(pl.program_id(1) == 0)
    def _():
        carry_ref[...] = jnp.zeros_like(carry_ref)

    h = h_ref[...]
    w1 = w1_ref[...]
    p1 = _dot(h, w1.astype(BF16))
    p2 = _dot(h, (w1 * mu_ref[...]).astype(BF16))
    prev = _shift_rows(p2, carry_ref[...])
    carry_ref[...] = p2[p2.shape[0] - 1:, :]
    lo = p1 - p2 + prev
    hp = lax.Precision.HIGHEST
    dec = _dot(jnp.tanh(lo[:, 0:128]), wd2_ref[...], hp)
    z = -(w0_ref[...] + dec)
    softplus = jnp.maximum(z, 0.0) + jnp.log(1.0 + jnp.exp(-jnp.abs(z)))
    lw_ref[...] = -jnp.exp(-softplus - 0.5)
    ag_ref[...] = _sigmoid(a0_ref[...] + _dot(lo[:, 128:256], wa2_ref[...], hp))
    gg_ref[...] = _dot(_sigmoid(lo[:, 256:512]), wg2_ref[...], hp)


def _lora(h, w1p, mup, wd2p, wa2p, wg2p, w0, a0, bsz, t_len):
    n, d = h.shape
    dr = w0.shape[-1]
    tm = min(512, t_len)
    nt = t_len // tm
    row = lambda b, t: (b * nt + t, 0)
    const = lambda b, t: (0, 0)
    out = jax.ShapeDtypeStruct((n, dr), F32)
    return pl.pallas_call(
        _lora_kernel,
        out_shape=(out, out, out),
        grid=(bsz, nt),
        in_specs=[pl.BlockSpec((tm, d), row),
                  pl.BlockSpec((d, LORA_PAD), const),
                  pl.BlockSpec((d, LORA_PAD), const),
                  pl.BlockSpec((128, dr), const),
                  pl.BlockSpec((128, dr), const),
                  pl.BlockSpec((256, dr), const),
                  pl.BlockSpec((1, dr), const),
                  pl.BlockSpec((1, dr), const)],
        out_specs=(pl.BlockSpec((tm, dr), row),) * 3,
        scratch_shapes=[pltpu.VMEM((1, LORA_PAD), F32)],
        compiler_params=pltpu.CompilerParams(dimension_semantics=("arbitrary", "arbitrary"),
                                             vmem_limit_bytes=VMEM_LIMIT),
        name="lora",
    )(h, w1p, mup, wd2p, wa2p, wg2p, w0.reshape(1, dr), a0.reshape(1, dr))


def _wkv_kernel(r_ref, k_ref, v_ref, lw_ref, ag_ref, gg_ref, mur_ref, muk_ref, muv_ref,
                kk_ref, ka_ref, rk_ref, lnw_ref, lnb_ref, y_ref, s_ref, carry_ref):
    lt = r_ref.shape[0]
    gl = GROUP_LANES
    c_len = WKV_CHUNK
    heads = gl // HEAD_DIM

    @pl.when(pl.program_id(2) == 0)
    def _():
        s_ref[...] = jnp.zeros_like(s_ref)
        carry_ref[...] = jnp.zeros_like(carry_ref)

    ones_bd = _seg_ones(gl, HEAD_DIM)

    def lerp(ref, mu_ref, idx):
        z = ref[...]
        prev = _shift_rows(z, carry_ref[idx:idx + 1, :])
        carry_ref[idx:idx + 1, :] = z[lt - 1:, :]
        return z + (prev - z) * mu_ref[...]

    r1 = lerp(r_ref, mur_ref, 0)
    k1 = lerp(k_ref, muk_ref, 1)
    v1 = lerp(v_ref, muv_ref, 2)
    ag = ag_ref[...]
    kk = k1 * kk_ref[...]
    kk = kk * lax.rsqrt(jnp.maximum(_seg_sum(kk * kk, ones_bd), 1e-24))
    k2 = k1 * (1.0 + (ag - 1.0) * ka_ref[...])
    a_all = -kk
    b_all = kk * ag
    lw = lw_ref[...]

    rows = lax.broadcasted_iota(jnp.int32, (gl, gl), 0)
    cols = lax.broadcasted_iota(jnp.int32, (gl, gl), 1)
    same_head = (rows // c_len) == (cols // HEAD_DIM)
    tri_strict = same_head & ((rows % c_len) > (cols % c_len))
    tri_incl = same_head & ((rows % c_len) >= (cols % c_len))
    eye = jnp.where(rows == cols, 1.0, 0.0).astype(F32)
    cr = lax.broadcasted_iota(jnp.int32, (c_len, c_len), 0)
    cc = lax.broadcasted_iota(jnp.int32, (c_len, c_len), 1)
    cum_incl = jnp.where(cr >= cc, 1.0, 0.0).astype(F32)

    def bd(x):
        xb = x.astype(BF16)
        return jnp.where(same_head, jnp.concatenate([xb] * heads, axis=0), jnp.zeros((), BF16))

    def fold(x):
        out = x[0:c_len]
        for hh in range(1, heads):
            out = out + x[hh * c_len:(hh + 1) * c_len]
        return out

    y_chunks = []
    for c in range(lt // c_len):
        sl = slice(c * c_len, (c + 1) * c_len)
        lwc = lw[sl]
        cs = _dot(cum_incl, lwc, lax.Precision.HIGHEST)
        e_cs = jnp.exp(cs)
        e_neg = jnp.exp(-cs)
        wc = e_cs[c_len - 1:, :]
        at = a_all[sl] * jnp.exp(cs - lwc)
        rt = r1[sl] * e_cs
        bt = b_all[sl] * e_neg
        kt = k2[sl] * e_neg
        vc = v1[sl]

        bd_a = bd(at)
        bd_r = bd(rt)
        lhs = jnp.concatenate([bd_a, bd_r], axis=0)
        g_b = _dot_nt(lhs, bd(bt))
        g_k = _dot_nt(lhs, bd(kt))
        n_ab = jnp.where(tri_strict, g_b[:gl], 0.0)
        a_ak = jnp.where(tri_strict, g_k[:gl], 0.0).astype(BF16)
        a_rb = jnp.where(tri_incl, g_b[gl:], 0.0).astype(BF16)
        a_rk = jnp.where(tri_incl, g_k[gl:], 0.0).astype(BF16)

        nb = n_ab.astype(BF16)
        q = _dot(nb, nb)
        p = eye + n_ab
        for _ in range(4):
            qb = q.astype(BF16)
            pq = _dot(jnp.concatenate([p.astype(BF16), qb], axis=0), qb)
            p = p + pq[:gl]
            q = pq[gl:]
        p = p + _dot(p.astype(BF16), q.astype(BF16))
        tb = p.astype(BF16)

        z = _dot(tb, jnp.concatenate([a_ak, bd_a], axis=1))
        ap_bd = z[:, gl:].astype(BF16)
        bd_v = bd(vc)
        u0_bd = _dot(z[:, :gl].astype(BF16), bd_v).astype(BF16)
        w = _dot(a_rb, jnp.concatenate([ap_bd, u0_bd], axis=1))
        y0_bd = w[:, gl:] + _dot(a_rk, bd_v)
        ap = fold(z[:, gl:])
        u0 = fold(u0_bd.astype(F32))
        rp = rt + fold(w[:, :gl])
        y0 = fold(y0_bd)

        s = s_ref[...]
        uy = _dot_nt(jnp.concatenate([ap, rp], axis=0).astype(BF16), s.astype(BF16))
        u = uy[:c_len] + u0
        y_chunks.append(uy[c_len:] + y0)
        upd = _dot_tn(jnp.concatenate([u, vc], axis=0).astype(BF16),
                      jnp.concatenate([bt * wc, kt * wc], axis=0).astype(BF16))
        s_ref[...] = s * wc + jnp.where(same_head, upd, 0.0)

    y = jnp.concatenate(y_chunks, axis=0) if len(y_chunks) > 1 else y_chunks[0]
    inv_n = 1.0 / HEAD_DIM
    mean = _seg_sum(y, ones_bd) * inv_n
    yc = y - mean
    var = _seg_sum(yc * yc, ones_bd) * inv_n
    yn = yc * lax.rsqrt(var + LNX_EPS) * lnw_ref[...] + lnb_ref[...]
    bonus = _seg_sum(r1 * k2 * rk_ref[...], ones_bd) * v1
    y_ref[...] = ((yn + bonus) * gg_ref[...]).astype(y_ref.dtype)


def _wkv(proj, lw, ag, gg, mu_r, mu_k, mu_v, k_k, k_a, r_k, lnx_w, lnx_b, bsz, t_len):
    n = proj.shape[0]
    dr = lw.shape[1]
    gl = GROUP_LANES
    ng = dr // gl
    lt = min(256, t_len)
    nt = t_len // lt
    row = lambda off: (lambda b, g, t: (b * nt + t, off + g))
    par = lambda b, g, t: (0, g)
    vec = lambda a: a.reshape(1, dr)
    return pl.pallas_call(
        _wkv_kernel,
        out_shape=jax.ShapeDtypeStruct((n, dr), BF16),
        grid=(bsz, ng, nt),
        in_specs=[pl.BlockSpec((lt, gl), row(0)),
                  pl.BlockSpec((lt, gl), row(ng)),
                  pl.BlockSpec((lt, gl), row(2 * ng)),
                  pl.BlockSpec((lt, gl), row(0)),
                  pl.BlockSpec((lt, gl), row(0)),
                  pl.BlockSpec((lt, gl), row(0))] + [pl.BlockSpec((1, gl), par)] * 8,
        out_specs=pl.BlockSpec((lt, gl), row(0)),
        scratch_shapes=[pltpu.VMEM((gl, gl), F32), pltpu.VMEM((8, gl), F32)],
        compiler_params=pltpu.CompilerParams(
            dimension_semantics=("arbitrary", "arbitrary", "arbitrary"),
            vmem_limit_bytes=VMEM_LIMIT),
        name="wkv",
    )(proj, proj, proj, lw, ag, gg, vec(mu_r), vec(mu_k), vec(mu_v), vec(k_k), vec(k_a),
      vec(r_k), vec(lnx_w), vec(lnx_b))


def _outproj_kernel(yr_ref, u_ref, gb_ref, gc_ref, cw_ref, cnw_ref, wo_ref, x_ref, mod_ref,
                    n2w_ref, wr_ref, br_ref, x1_ref, h2_ref, meta_ref, cnt_ref,
                    ccarry_ref, cnt_acc_ref):
    tm = x_ref.shape[0]
    dc = u_ref.shape[1]

    @pl.when(pl.program_id(1) == 0)
    def _():
        ccarry_ref[...] = jnp.zeros_like(ccarry_ref)

    @pl.when((pl.program_id(0) == 0) & (pl.program_id(1) == 0))
    def _():
        cnt_acc_ref[...] = jnp.zeros_like(cnt_acc_ref)

    pre = gc_ref[...] * u_ref[...]
    prev1 = _shift_rows(pre, ccarry_ref[0:1, :])
    prev2 = _shift_rows(prev1, ccarry_ref[1:2, :])
    ccarry_ref[0:1, :] = pre[tm - 1:, :]
    ccarry_ref[1:2, :] = pre[tm - 2:tm - 1, :]
    yc = gb_ref[...] * (prev2 * cw_ref[0:1, :] + prev1 * cw_ref[1:2, :] + pre * cw_ref[2:3, :])
    ones_bd = _seg_ones(GROUP_LANES, CONV_GROUP_DIM)
    ycs = []
    for s in range(dc // GROUP_LANES):
        ys = yc[:, s * GROUP_LANES:(s + 1) * GROUP_LANES]
        ms = _seg_sum(ys * ys, ones_bd) * (1.0 / CONV_GROUP_DIM)
        ycs.append(ys * lax.rsqrt(ms + RMS_EPS))
    y_conv = (jnp.concatenate(ycs, axis=1) * cnw_ref[...]).astype(BF16)

    dr = yr_ref.shape[1]
    y_mix = _dot(yr_ref[...], wo_ref[0:dr, :]) + _dot(y_conv, wo_ref[dr:, :])
    x1 = x_ref[...] + mod_ref[0, 2:3, :] * y_mix
    x1_ref[...] = x1
    ms = jnp.mean(x1 * x1, axis=-1, keepdims=True)
    h2 = x1 * lax.rsqrt(ms + RMS_EPS) * n2w_ref[...]
    h2 = h2 * (1.0 + mod_ref[0, 4:5, :]) + mod_ref[0, 3:4, :]
    h2_ref[...] = h2

    logits = _dot(h2, wr_ref[...], lax.Precision.HIGHEST) + br_ref[...]
    lane = lax.broadcasted_iota(jnp.int32, logits.shape, 1)
    neg = jnp.float32(-jnp.inf)
    big = jnp.int32(1 << 20)
    glog = jnp.where(lane < N_ROUTE_GROUPS, logits, neg)
    gmax = jnp.max(glog, axis=-1, keepdims=True)
    grp = jnp.min(jnp.where(glog == gmax, lane, big), axis=-1, keepdims=True)
    p_grp = 1.0 / jnp.sum(jnp.exp(glog - gmax), axis=-1, keepdims=True)
    e_lane = lane - N_ROUTE_GROUPS
    in_grp = (e_lane >= grp * EXPERTS_PER_GROUP) & (e_lane < (grp + 1) * EXPERTS_PER_GROUP)
    elog = jnp.where(in_grp, logits, neg)
    m1 = jnp.max(elog, axis=-1, keepdims=True)
    i1 = jnp.min(jnp.where(elog == m1, lane, big), axis=-1, keepdims=True)
    elog2 = jnp.where(lane == i1, neg, elog)
    m2 = jnp.max(elog2, axis=-1, keepdims=True)
    i2 = jnp.min(jnp.where(elog2 == m2, lane, big), axis=-1, keepdims=True)
    t2 = jnp.exp(m2 - m1)
    w1 = p_grp / (1.0 + t2)
    w2 = p_grp * t2 / (1.0 + t2)
    e1 = i1 - N_ROUTE_GROUPS
    e2 = i2 - N_ROUTE_GROUPS

    oh1 = lane == e1
    oh2 = lane == e2
    oh = jnp.where(oh1 | oh2, 1.0, 0.0)
    tr = lax.broadcasted_iota(jnp.int32, (tm, tm), 0)
    tc = lax.broadcasted_iota(jnp.int32, (tm, tm), 1)
    lower = jnp.where(tr > tc, 1.0, 0.0).astype(BF16)
    before = _dot(lower, oh.astype(BF16)) + cnt_acc_ref[...]
    rank1 = jnp.sum(jnp.where(oh1, before, 0.0), axis=-1, keepdims=True)
    rank2 = jnp.sum(jnp.where(oh2, before, 0.0), axis=-1, keepdims=True)
    cnt = cnt_acc_ref[...] + jnp.sum(oh, axis=0, keepdims=True)
    cnt_acc_ref[...] = cnt
    cnt_ref[...] = jnp.broadcast_to(cnt, cnt_ref.shape)

    meta = jnp.where(lane == 0, e1.astype(F32), 0.0)
    meta = jnp.where(lane == 1, e2.astype(F32), meta)
    meta = jnp.where(lane == 2, w1, meta)
    meta = jnp.where(lane == 3, w2, meta)
    meta = jnp.where(lane == 4, rank1, meta)
    meta = jnp.where(lane == 5, rank2, meta)
    meta_ref[...] = meta


def _outproj(yr, proj, conv_w, conv_norm_w, w_out_bf, x2, mod3, norm2_w, wr, br, bsz, t_len):
    n, d = x2.shape
    dr = yr.shape[1]
    dc = conv_w.shape[1]
    tm = min(512, t_len)
    nt = t_len // tm
    nseg = dr // dc
    row = lambda off: (lambda b, t: (b * nt + t, off))
    const = lambda b, t: (0, 0)
    return pl.pallas_call(
        _outproj_kernel,
        out_shape=(jax.ShapeDtypeStruct((n, d), F32), jax.ShapeDtypeStruct((n, d), F32),
                   jax.ShapeDtypeStruct((n, LANES), F32), jax.ShapeDtypeStruct((8, LANES), F32)),
        grid=(bsz, nt),
        in_specs=[pl.BlockSpec((tm, dr), row(0)),
                  pl.BlockSpec((tm, dc), row(3 * nseg)),
                  pl.BlockSpec((tm, dc), row(4 * nseg)),
                  pl.BlockSpec((tm, dc), row(5 * nseg)),
                  pl.BlockSpec((3, dc), const),
                  pl.BlockSpec((1, dc), const),
                  pl.BlockSpec((dr + dc, d), const),
                  pl.BlockSpec((tm, d), row(0)),
                  pl.BlockSpec((1, 6, d), lambda b, t: (b, 0, 0)),
                  pl.BlockSpec((1, d), const),
                  pl.BlockSpec((d, LANES), const),
                  pl.BlockSpec((1, LANES), const)],
        out_specs=(pl.BlockSpec((tm, d), row(0)), pl.BlockSpec((tm, d), row(0)),
                   pl.BlockSpec((tm, LANES), row(0)), pl.BlockSpec((8, LANES), const)),
        scratch_shapes=[pltpu.VMEM((8, dc), F32), pltpu.VMEM((1, LANES), F32)],
        compiler_params=pltpu.CompilerParams(dimension_semantics=("arbitrary", "arbitrary"),
                                             vmem_limit_bytes=VMEM_LIMIT),
        name="outproj",
    )(yr, proj, proj, proj, conv_w, conv_norm_w.reshape(1, dc), w_out_bf, x2, mod3,
      norm2_w.reshape(1, d), wr, br)


def _dispatch_kernel(dest_hbm, h2_ref, xs_in_hbm, xs_hbm, idx_smem, sem_idx, sem_rows):
    del xs_in_hbm
    th = h2_ref.shape[0]
    cp = pltpu.make_async_copy(dest_hbm.at[pl.program_id(0)], idx_smem, sem_idx)
    cp.start()
    cp.wait()

    def body(t, carry):
        for s in range(2):
            d = idx_smem[2 * t + s]
            pltpu.make_async_copy(h2_ref.at[pl.ds(t, 1)], xs_hbm.at[pl.ds(d, 1)], sem_rows).start()
        return carry

    lax.fori_loop(0, th, body, 0)
    pltpu.make_async_copy(xs_hbm.at[pl.ds(0, 2 * th)], xs_hbm.at[pl.ds(0, 2 * th)], sem_rows).wait()


def _dispatch(dest, h2, n_rows):
    n, d = h2.shape
    th = min(256, n)
    xs0 = jnp.zeros((n_rows, d), h2.dtype)
    return pl.pallas_call(
        _dispatch_kernel,
        out_shape=jax.ShapeDtypeStruct((n_rows, d), h2.dtype),
        grid=(n // th,),
        in_specs=[pl.BlockSpec(memory_space=pl.ANY),
                  pl.BlockSpec((th, d), lambda i: (i, 0)),
                  pl.BlockSpec(memory_space=pl.ANY)],
        out_specs=pl.BlockSpec(memory_space=pl.ANY),
        scratch_shapes=[pltpu.SMEM((2 * th,), jnp.int32), pltpu.SemaphoreType.DMA,
                        pltpu.SemaphoreType.DMA],
        input_output_aliases={2: 0},
        compiler_params=pltpu.CompilerParams(dimension_semantics=("arbitrary",),
                                             vmem_limit_bytes=VMEM_LIMIT),
        name="dispatch",
    )(dest.reshape(n // th, 2 * th), h2, xs0)


def _expert_kernel(be_ref, na_ref, xs_ref, wg_ref, wu_ref, wd_ref, ys_ref):
    del be_ref

    @pl.when(pl.program_id(0) < na_ref[0])
    def _():
        x = xs_ref[...].astype(BF16)
        g = _dot(x, wg_ref[0].astype(BF16))
        u = _dot(x, wu_ref[0].astype(BF16))
        hid = (g * _sigmoid(g) * u).astype(BF16)
        ys_ref[...] = _dot(hid, wd_ref[0].astype(BF16))

    @pl.when(pl.program_id(0) >= na_ref[0])
    def _():
        ys_ref[...] = jnp.zeros_like(ys_ref)


def _experts(block_e, n_active, xs, exp_gate, exp_up, exp_down):
    n_rows, d = xs.shape
    de = exp_gate.shape[2]
    nblk = n_rows // ROUTE_BLOCK
    blk_row = lambda i, be, na: (jnp.minimum(i, na[0] - 1), 0)
    blk_exp = lambda i, be, na: (be[jnp.minimum(i, na[0] - 1)], 0, 0)
    return pl.pallas_call(
        _expert_kernel,
        out_shape=jax.ShapeDtypeStruct((n_rows, d), F32),
        grid_spec=pltpu.PrefetchScalarGridSpec(
            num_scalar_prefetch=2,
            grid=(nblk,),
            in_specs=[pl.BlockSpec((ROUTE_BLOCK, d), blk_row),
                      pl.BlockSpec((1, d, de), blk_exp),
                      pl.BlockSpec((1, d, de), blk_exp),
                      pl.BlockSpec((1, de, d), blk_exp)],
            out_specs=pl.BlockSpec((ROUTE_BLOCK, d), lambda i, be, na: (i, 0))),
        compiler_params=pltpu.CompilerParams(dimension_semantics=("arbitrary",),
                                             vmem_limit_bytes=VMEM_LIMIT),
        name="experts",
    )(block_e, n_active, xs, exp_gate, exp_up, exp_down)


def _combine_kernel(dest_hbm, ys_hbm, x1_ref, meta_ref, mod_ref, fw_ref, o_ref,
                    idx_smem, buf_ref, sem_idx, sem_rows):
    tj = x1_ref.shape[0]
    cp = pltpu.make_async_copy(dest_hbm.at[pl.program_id(0)], idx_smem, sem_idx)
    cp.start()
    cp.wait()

    def body(t, carry):
        for s in range(2):
            d = idx_smem[2 * t + s]
            pltpu.make_async_copy(ys_hbm.at[pl.ds(d, 1)], buf_ref.at[s, pl.ds(t, 1)], sem_rows).start()
        return carry

    lax.fori_loop(0, tj, body, 0)
    pltpu.make_async_copy(buf_ref, buf_ref, sem_rows).wait()

    meta = meta_ref[...]
    y = meta[:, 2:3] * buf_ref[0] + meta[:, 3:4] * buf_ref[1]
    x2 = x1_ref[...] + mod_ref[0, 5:6, :] * y
    ms = jnp.mean(x2 * x2, axis=-1, keepdims=True)
    o_ref[...] = x2 * lax.rsqrt(ms + RMS_EPS) * fw_ref[...]


def _combine(dest, ys, x1, meta, mod3, final_w, t_len):
    n, d = x1.shape
    tj = min(256, t_len)
    return pl.pallas_call(
        _combine_kernel,
        out_shape=jax.ShapeDtypeStruct((n, d), F32),
        grid=(n // tj,),
        in_specs=[pl.BlockSpec(memory_space=pl.ANY),
                  pl.BlockSpec(memory_space=pl.ANY),
                  pl.BlockSpec((tj, d), lambda i: (i, 0)),
                  pl.BlockSpec((tj, LANES), lambda i: (i, 0)),
                  pl.BlockSpec((1, 6, d), lambda i: (i * tj // t_len, 0, 0)),
                  pl.BlockSpec((1, d), lambda i: (0, 0))],
        out_specs=pl.BlockSpec((tj, d), lambda i: (i, 0)),
        scratch_shapes=[pltpu.SMEM((2 * tj,), jnp.int32), pltpu.VMEM((2, tj, d), F32),
                        pltpu.SemaphoreType.DMA, pltpu.SemaphoreType.DMA],
        compiler_params=pltpu.CompilerParams(dimension_semantics=("arbitrary",),
                                             vmem_limit_bytes=VMEM_LIMIT),
        name="combine",
    )(dest.reshape(n // tj, 2 * tj), ys, x1, meta, mod3, final_w.reshape(1, d))


def _pad_cols(w, width):
    return jnp.pad(w, ((0, 0), (0, width - w.shape[1])))


def _pad_rows(w, height):
    return jnp.pad(w, ((0, height - w.shape[0]), (0, 0)))


def _layer(x2, c, bsz, t_len, w_ada, b_ada, norm1_w, w_in, mu_r, mu_k, mu_v, mu_w, mu_a, mu_g, w0,
           w_decay1, w_decay2, a0, w_aaa1, w_aaa2, w_gate1, w_gate2, k_k, k_a, r_k, lnx_w, lnx_b,
           conv_w, conv_norm_w, w_out, norm2_w, w_grp, b_grp, w_exp, b_exp, exp_gate, exp_up,
           exp_down, final_w):
    n, d = x2.shape
    dr = w0.shape[0]
    assert w_decay1.shape[1] <= 128 and w_aaa1.shape[1] <= 128 and w_gate1.shape[1] <= 256
    assert dr % GROUP_LANES == 0 and conv_w.shape[1] == dr and w_in.shape[1] == 6 * dr
    assert t_len % WKV_CHUNK == 0 and w_grp.shape[1] == N_ROUTE_GROUPS and w_exp.shape[1] == N_EXPERTS

    mod3 = _ada(c, w_ada, b_ada).reshape(bsz, 6, d)
    h, proj = _inproj(x2, mod3, norm1_w, w_in.astype(BF16), t_len)

    w1p = jnp.concatenate([_pad_cols(w_decay1, 128), _pad_cols(w_aaa1, 128), _pad_cols(w_gate1, 256)], axis=1)
    mup = jnp.concatenate([jnp.broadcast_to(mu_w[:, None], (d, 128)),
                           jnp.broadcast_to(mu_a[:, None], (d, 128)),
                           jnp.broadcast_to(mu_g[:, None], (d, 256))], axis=1)
    lw, ag, gg = _lora(h, w1p, mup, _pad_rows(w_decay2, 128), _pad_rows(w_aaa2, 128),
                       _pad_rows(w_gate2, 256), w0, a0, bsz, t_len)

    yr = _wkv(proj, lw, ag, gg, mu_r, mu_k, mu_v, k_k, k_a, r_k.reshape(dr), lnx_w, lnx_b, bsz, t_len)

    wr = _pad_cols(jnp.concatenate([w_grp, w_exp], axis=1), LANES)
    br = _pad_cols(jnp.concatenate([b_grp, b_exp]).reshape(1, -1), LANES)
    x1, h2, meta, cnt = _outproj(yr, proj, conv_w, conv_norm_w, w_out.astype(BF16), x2, mod3,
                                 norm2_w, wr, br, bsz, t_len)

    counts = cnt[0, :N_EXPERTS].astype(jnp.int32)
    padded = (counts + ROUTE_BLOCK - 1) // ROUTE_BLOCK * ROUTE_BLOCK
    pad_ends = jnp.cumsum(padded)
    pad_starts = pad_ends - padded
    top_e = meta[:, 0:2].astype(jnp.int32)
    rank = meta[:, 4:6].astype(jnp.int32)
    dest = (pad_starts[top_e] + rank).reshape(-1)
    nblk = (2 * n) // ROUTE_BLOCK + N_EXPERTS
    n_rows = nblk * ROUTE_BLOCK
    block_start = jnp.arange(nblk, dtype=jnp.int32) * ROUTE_BLOCK
    block_e = jnp.minimum(jnp.searchsorted(pad_ends, block_start, side='right'),
                          N_EXPERTS - 1).astype(jnp.int32)
    n_active = (pad_ends[-1:] // ROUTE_BLOCK).astype(jnp.int32)

    xs = _dispatch(dest, h2, n_rows)
    ys = _experts(block_e, n_active, xs, exp_gate, exp_up, exp_down)
    return _combine(dest, ys, x1, meta, mod3, final_w, t_len)


def kernel(x, c, w_ada, b_ada, norm1_w, w_in, mu_r, mu_k, mu_v, mu_w, mu_a, mu_g, w0, w_decay1, w_decay2, a0, w_aaa1, w_aaa2, w_gate1, w_gate2, k_k, k_a, r_k, lnx_w, lnx_b, conv_w, conv_norm_w, w_out, norm2_w, w_grp, b_grp, w_exp, b_exp, exp_gate, exp_up, exp_down, final_w):
    bsz, t_len, d = x.shape
    assert w_ada.shape[0] == 1, "single-layer block"
    out = _layer(x.reshape(bsz * t_len, d), c, bsz, t_len, w_ada[0], b_ada[0], norm1_w[0], w_in[0],
                 mu_r[0], mu_k[0], mu_v[0], mu_w[0], mu_a[0], mu_g[0], w0[0], w_decay1[0], w_decay2[0],
                 a0[0], w_aaa1[0], w_aaa2[0], w_gate1[0], w_gate2[0], k_k[0], k_a[0], r_k[0], lnx_w[0],
                 lnx_b[0], conv_w[0], conv_norm_w[0], w_out[0], norm2_w[0], w_grp[0], b_grp[0],
                 w_exp[0], b_exp[0], exp_gate[0], exp_up[0], exp_down[0], final_w)
    return out.reshape(bsz, t_len, d)
```

```python
import functools

import jax
import jax.numpy as jnp
from jax import lax
from jax.experimental import pallas as pl
from jax.experimental.pallas import tpu as pltpu

F32 = jnp.float32
BF16 = jnp.bfloat16

HEAD_DIM = 64
WKV_CHUNK = 64
GROUP_LANES = 256
N_ROUTE_GROUPS = 4
EXPERTS_PER_GROUP = 8
N_EXPERTS = N_ROUTE_GROUPS * EXPERTS_PER_GROUP
CONV_GROUP_DIM = 64
RMS_EPS = 1e-6
LNX_EPS = 64e-5
LANES = 128
LORA_PAD = 512
ROUTE_BLOCK = 256
VMEM_LIMIT = 56 * 1024 * 1024


def _dot(a, b, precision=None):
    return jnp.dot(a, b, preferred_element_type=F32, precision=precision)


def _dot_nt(a, b):
    return lax.dot_general(a, b, (((1,), (1,)), ((), ())), preferred_element_type=F32)


def _dot_tn(a, b):
    return lax.dot_general(a, b, (((0,), (0,)), ((), ())), preferred_element_type=F32)


def _split_bf16(x):
    hi = x.astype(BF16)
    return hi, (x - hi.astype(F32)).astype(BF16)


def _dot_x3(a, b):
    ah, al = _split_bf16(a)
    bh, bl = _split_bf16(b)
    return _dot(ah, bh) + _dot(al, bh) + _dot(ah, bl)


def _sigmoid(x):
    return 1.0 / (1.0 + jnp.exp(-x))


def _shift_rows(x, carry_row):
    rolled = pltpu.roll(x, 1, axis=0)
    row = lax.broadcasted_iota(jnp.int32, x.shape, 0)
    return jnp.where(row == 0, carry_row, rolled)


def _seg_ones(n, seg):
    r = lax.broadcasted_iota(jnp.int32, (n, n), 0) // seg
    c = lax.broadcasted_iota(jnp.int32, (n, n), 1) // seg
    return jnp.where(r == c, 1.0, 0.0).astype(BF16)


def _seg_sum(x, ones_bd):
    hi = x.astype(BF16)
    lo = (x - hi.astype(F32)).astype(BF16)
    return _dot(hi, ones_bd) + _dot(lo, ones_bd)


def _ada_kernel(cb_ref, w_ref, b_ref, o_ref):
    nb = cb_ref.shape[0]
    tn = w_ref.shape[1]
    for b in range(nb):
        cv = cb_ref[b]
        s = cv * _sigmoid(cv)
        cols = [jnp.sum(s * w_ref[:, j * LANES:(j + 1) * LANES], axis=0, keepdims=True)
                for j in range(tn // LANES)]
        o_ref[b:b + 1, :] = jnp.concatenate(cols, axis=1) + b_ref[...]


def _ada(c, w_ada, b_ada):
    nb, d = c.shape
    n_out = w_ada.shape[1]
    tn = 512
    cb = jnp.broadcast_to(c[:, :, None], (nb, d, LANES))
    return pl.pallas_call(
        _ada_kernel,
        out_shape=jax.ShapeDtypeStruct((nb, n_out), F32),
        grid=(n_out // tn,),
        in_specs=[pl.BlockSpec((nb, d, LANES), lambda j: (0, 0, 0)),
                  pl.BlockSpec((d, tn), lambda j: (0, j)),
                  pl.BlockSpec((1, tn), lambda j: (0, j))],
        out_specs=pl.BlockSpec((nb, tn), lambda j: (0, j)),
        compiler_params=pltpu.CompilerParams(dimension_semantics=("arbitrary",),
                                             vmem_limit_bytes=VMEM_LIMIT),
        name="ada",
    )(cb, w_ada, b_ada.reshape(1, n_out))


def _inproj_kernel(x_ref, mod_ref, nw_ref, w_ref, h_ref, p_ref, hs_ref):
    @pl.when(pl.program_id(1) == 0)
    def _():
        x = x_ref[...]
        ms = jnp.mean(x * x, axis=-1, keepdims=True)
        y = x * lax.rsqrt(ms + RMS_EPS) * nw_ref[...]
        h = y * (1.0 + mod_ref[0, 1:2, :]) + mod_ref[0, 0:1, :]
        hb = h.astype(BF16)
        hs_ref[...] = hb
        h_ref[...] = hb

    p_ref[...] = _dot(hs_ref[...], w_ref[...])


def _inproj(x2, mod3, norm_w, w_in_bf, t_len):
    n, d = x2.shape
    n_in = w_in_bf.shape[1]
    tm = min(512, t_len)
    tn = 1024
    return pl.pallas_call(
        _inproj_kernel,
        out_shape=(jax.ShapeDtypeStruct((n, d), BF16), jax.ShapeDtypeStruct((n, n_in), F32)),
        grid=(n // tm, n_in // tn),
        in_specs=[pl.BlockSpec((tm, d), lambda i, j: (i, 0)),
                  pl.BlockSpec((1, 6, d), lambda i, j: (i * tm // t_len, 0, 0)),
                  pl.BlockSpec((1, d), lambda i, j: (0, 0)),
                  pl.BlockSpec((d, tn), lambda i, j: (0, j))],
        out_specs=(pl.BlockSpec((tm, d), lambda i, j: (i, 0)),
                   pl.BlockSpec((tm, tn), lambda i, j: (i, j))),
        scratch_shapes=[pltpu.VMEM((tm, d), BF16)],
        compiler_params=pltpu.CompilerParams(dimension_semantics=("arbitrary", "arbitrary"),
                                             vmem_limit_bytes=VMEM_LIMIT),
        name="inproj",
    )(x2, mod3, norm_w.reshape(1, d), w_in_bf)


def _lora_kernel(h_ref, w1_ref, mu_ref, wd2_ref, wa2_ref, wg2_ref, w0_ref, a0_ref,
                 lw_ref, ag_ref, gg_ref, carry_ref):
    @pl.when(pl.program_id(1) == 0)
    def _():
        carry_ref[...] = jnp.zeros_like(carry_ref)

    h = h_ref[...]
    w1 = w1_ref[...]
    p1 = _dot(h, w1.astype(BF16))
    p2 = _dot(h, (w1 * mu_ref[...]).astype(BF16))
    prev = _shift_rows(p2, carry_ref[...])
    carry_ref[...] = p2[p2.shape[0] - 1:, :]
    lo = p1 - p2 + prev
    dec = _dot_x3(jnp.tanh(lo[:, 0:128]), wd2_ref[...])
    z = -(w0_ref[...] + dec)
    softplus = jnp.maximum(z, 0.0) + jnp.log(1.0 + jnp.exp(-jnp.abs(z)))
    lw_ref[...] = -jnp.exp(-softplus - 0.5)
    ag_ref[...] = _sigmoid(a0_ref[...] + _dot_x3(lo[:, 128:256], wa2_ref[...]))
    gg_ref[...] = _dot_x3(_sigmoid(lo[:, 256:512]), wg2_ref[...])


def _lora(h, w1p, mup, wd2p, wa2p, wg2p, w0, a0, bsz, t_len):
    n, d = h.shape
    dr = w0.shape[-1]
    tm = min(512, t_len)
    nt = t_len // tm
    row = lambda b, t: (b * nt + t, 0)
    const = lambda b, t: (0, 0)
    out = jax.ShapeDtypeStruct((n, dr), F32)
    return pl.pallas_call(
        _lora_kernel,
        out_shape=(out, out, out),
        grid=(bsz, nt),
        in_specs=[pl.BlockSpec((tm, d), row),
                  pl.BlockSpec((d, LORA_PAD), const),
                  pl.BlockSpec((d, LORA_PAD), const),
                  pl.BlockSpec((128, dr), const),
                  pl.BlockSpec((128, dr), const),
                  pl.BlockSpec((256, dr), const),
                  pl.BlockSpec((1, dr), const),
                  pl.BlockSpec((1, dr), const)],
        out_specs=(pl.BlockSpec((tm, dr), row),) * 3,
        scratch_shapes=[pltpu.VMEM((1, LORA_PAD), F32)],
        compiler_params=pltpu.CompilerParams(dimension_semantics=("arbitrary", "arbitrary"),
                                             vmem_limit_bytes=VMEM_LIMIT),
        name="lora",
    )(h, w1p, mup, wd2p, wa2p, wg2p, w0.reshape(1, dr), a0.reshape(1, dr))


def _wkv_kernel(r_ref, k_ref, v_ref, lw_ref, ag_ref, gg_ref, mur_ref, muk_ref, muv_ref,
                kk_ref, ka_ref, rk_ref, lnw_ref, lnb_ref, y_ref, s_ref, carry_ref):
    lt = r_ref.shape[0]
    gl = GROUP_LANES
    c_len = WKV_CHUNK
    heads = gl // HEAD_DIM

    @pl.when(pl.program_id(2) == 0)
    def _():
        s_ref[...] = jnp.zeros_like(s_ref)
        carry_ref[...] = jnp.zeros_like(carry_ref)

    ones_bd = _seg_ones(gl, HEAD_DIM)

    def lerp(ref, mu_ref, idx):
        z = ref[...]
        prev = _shift_rows(z, carry_ref[idx:idx + 1, :])
        carry_ref[idx:idx + 1, :] = z[lt - 1:, :]
        return z + (prev - z) * mu_ref[...]

    r1 = lerp(r_ref, mur_ref, 0)
    k1 = lerp(k_ref, muk_ref, 1)
    v1 = lerp(v_ref, muv_ref, 2)
    ag = ag_ref[...]
    kk = k1 * kk_ref[...]
    kk = kk * lax.rsqrt(jnp.maximum(_seg_sum(kk * kk, ones_bd), 1e-24))
    k2 = k1 * (1.0 + (ag - 1.0) * ka_ref[...])
    a_all = -kk
    b_all = kk * ag
    lw = lw_ref[...]

    rows = lax.broadcasted_iota(jnp.int32, (gl, gl), 0)
    cols = lax.broadcasted_iota(jnp.int32, (gl, gl), 1)
    same_head = (rows // c_len) == (cols // HEAD_DIM)
    tri_strict = same_head & ((rows % c_len) > (cols % c_len))
    tri_incl = same_head & ((rows % c_len) >= (cols % c_len))
    eye = jnp.where(rows == cols, 1.0, 0.0).astype(F32)
    cr = lax.broadcasted_iota(jnp.int32, (c_len, c_len), 0)
    cc = lax.broadcasted_iota(jnp.int32, (c_len, c_len), 1)
    cum_incl = jnp.where(cr >= cc, 1.0, 0.0).astype(F32)

    def bd(x):
        xb = x.astype(BF16)
        return jnp.where(same_head, jnp.concatenate([xb] * heads, axis=0), jnp.zeros((), BF16))

    def fold(x):
        out = x[0:c_len]
        for hh in range(1, heads):
            out = out + x[hh * c_len:(hh + 1) * c_len]
        return out

    nch = lt // c_len
    chunks = range(nch)
    sls = [slice(c * c_len, (c + 1) * c_len) for c in chunks]
    lws = [lw[sl] for sl in sls]
    cum_b = cum_incl.astype(BF16)

    def cumsum3(x):
        hi = x.astype(BF16)
        r = x - hi.astype(F32)
        mid = r.astype(BF16)
        lo = (r - mid.astype(F32)).astype(BF16)
        return _dot(cum_b, hi) + _dot(cum_b, mid) + _dot(cum_b, lo)

    cs = [cumsum3(x) for x in lws]
    e_cs = [jnp.exp(x) for x in cs]
    e_neg = [jnp.exp(-x) for x in cs]
    wc = [x[c_len - 1:, :] for x in e_cs]
    at = [a_all[sls[c]] * jnp.exp(cs[c] - lws[c]) for c in chunks]
    rt = [r1[sls[c]] * e_cs[c] for c in chunks]
    bt = [b_all[sls[c]] * e_neg[c] for c in chunks]
    kt = [k2[sls[c]] * e_neg[c] for c in chunks]
    vc = [v1[sl] for sl in sls]

    bd_a = [bd(x) for x in at]
    lhs = [jnp.concatenate([bd_a[c], bd(rt[c])], axis=0) for c in chunks]
    g_b = [_dot_nt(lhs[c], bd(bt[c])) for c in chunks]
    g_k = [_dot_nt(lhs[c], bd(kt[c])) for c in chunks]
    n_ab = [jnp.where(tri_strict, g[:gl], 0.0) for g in g_b]
    a_ak = [jnp.where(tri_strict, g[:gl], 0.0).astype(BF16) for g in g_k]
    a_rb = [jnp.where(tri_incl, g[gl:], 0.0).astype(BF16) for g in g_b]
    a_rk = [jnp.where(tri_incl, g[gl:], 0.0).astype(BF16) for g in g_k]

    nb = [x.astype(BF16) for x in n_ab]
    q = [_dot(x, x) for x in nb]
    p = [eye + x for x in n_ab]
    for _ in range(4):
        qb = [x.astype(BF16) for x in q]
        pq = [_dot(jnp.concatenate([p[c].astype(BF16), qb[c]], axis=0), qb[c]) for c in chunks]
        p = [p[c] + pq[c][:gl] for c in chunks]
        q = [x[gl:] for x in pq]
    p = [p[c] + _dot(p[c].astype(BF16), q[c].astype(BF16)) for c in chunks]
    tb = [x.astype(BF16) for x in p]

    z = [_dot(tb[c], jnp.concatenate([a_ak[c], bd_a[c]], axis=1)) for c in chunks]
    ap_bd = [x[:, gl:].astype(BF16) for x in z]
    bd_v = [bd(x) for x in vc]
    u0_bd = [_dot(z[c][:, :gl].astype(BF16), bd_v[c]).astype(BF16) for c in chunks]
    w = [_dot(a_rb[c], jnp.concatenate([ap_bd[c], u0_bd[c]], axis=1)) for c in chunks]
    y0 = [fold(w[c][:, gl:] + _dot(a_rk[c], bd_v[c])) for c in chunks]
    ap = [fold(x[:, gl:]) for x in z]
    u0 = [fold(x.astype(F32)) for x in u0_bd]
    rp = [rt[c] + fold(w[c][:, :gl]) for c in chunks]
    bh = [(bt[c] * wc[c]).astype(BF16) for c in chunks]
    bk = [jnp.concatenate([bh[c], (kt[c] * wc[c]).astype(BF16)], axis=0) for c in chunks]
    g_s = [jnp.where(same_head, _dot_tn(ap[c].astype(BF16), bh[c]), 0.0).astype(BF16) for c in chunks]
    n_s = [jnp.where(same_head, _dot_tn(jnp.concatenate([u0[c], vc[c]], axis=0).astype(BF16), bk[c]), 0.0)
           for c in chunks]

    s = s_ref[...]
    s_in = []
    for c in chunks:
        s_in.append(s.astype(BF16))
        s = s * wc[c] + _dot(s_in[c], g_s[c]) + n_s[c]
    s_ref[...] = s
    y_chunks = [_dot_nt(rp[c].astype(BF16), s_in[c]) + y0[c] for c in chunks]

    y = jnp.concatenate(y_chunks, axis=0) if len(y_chunks) > 1 else y_chunks[0]
    inv_n = 1.0 / HEAD_DIM
    mean = _seg_sum(y, ones_bd) * inv_n
    yc = y - mean
    var = _seg_sum(yc * yc, ones_bd) * inv_n
    yn = yc * lax.rsqrt(var + LNX_EPS) * lnw_ref[...] + lnb_ref[...]
    bonus = _seg_sum(r1 * k2 * rk_ref[...], ones_bd) * v1
    y_ref[...] = ((yn + bonus) * gg_ref[...]).astype(y_ref.dtype)


def _wkv(proj, lw, ag, gg, mu_r, mu_k, mu_v, k_k, k_a, r_k, lnx_w, lnx_b, bsz, t_len):
    n = proj.shape[0]
    dr = lw.shape[1]
    gl = GROUP_LANES
    ng = dr // gl
    lt = min(512, t_len)
    nt = t_len // lt
    row = lambda off: (lambda b, g, t: (b * nt + t, off + g))
    par = lambda b, g, t: (0, g)
    vec = lambda a: a.reshape(1, dr)
    return pl.pallas_call(
        _wkv_kernel,
        out_shape=jax.ShapeDtypeStruct((n, dr), BF16),
        grid=(bsz, ng, nt),
        in_specs=[pl.BlockSpec((lt, gl), row(0)),
                  pl.BlockSpec((lt, gl), row(ng)),
                  pl.BlockSpec((lt, gl), row(2 * ng)),
                  pl.BlockSpec((lt, gl), row(0)),
                  pl.BlockSpec((lt, gl), row(0)),
                  pl.BlockSpec((lt, gl), row(0))] + [pl.BlockSpec((1, gl), par)] * 8,
        out_specs=pl.BlockSpec((lt, gl), row(0)),
        scratch_shapes=[pltpu.VMEM((gl, gl), F32), pltpu.VMEM((8, gl), F32)],
        compiler_params=pltpu.CompilerParams(
            dimension_semantics=("arbitrary", "arbitrary", "arbitrary"),
            vmem_limit_bytes=VMEM_LIMIT),
        name="wkv",
    )(proj, proj, proj, lw, ag, gg, vec(mu_r), vec(mu_k), vec(mu_v), vec(k_k), vec(k_a),
      vec(r_k), vec(lnx_w), vec(lnx_b))


def _outproj_kernel(yr_ref, u_ref, gb_ref, gc_ref, cw_ref, cnw_ref, wo_ref, x_ref, mod_ref,
                    n2w_ref, wr_ref, br_ref, x1_ref, h2_ref, meta_ref, cnt_ref,
                    ccarry_ref, cnt_acc_ref):
    tm = x_ref.shape[0]
    dc = u_ref.shape[1]

    @pl.when(pl.program_id(1) == 0)
    def _():
        ccarry_ref[...] = jnp.zeros_like(ccarry_ref)

    @pl.when((pl.program_id(0) == 0) & (pl.program_id(1) == 0))
    def _():
        cnt_acc_ref[...] = jnp.zeros_like(cnt_acc_ref)

    pre = gc_ref[...] * u_ref[...]
    prev1 = _shift_rows(pre, ccarry_ref[0:1, :])
    prev2 = _shift_rows(prev1, ccarry_ref[1:2, :])
    ccarry_ref[0:1, :] = pre[tm - 1:, :]
    ccarry_ref[1:2, :] = pre[tm - 2:tm - 1, :]
    yc = gb_ref[...] * (prev2 * cw_ref[0:1, :] + prev1 * cw_ref[1:2, :] + pre * cw_ref[2:3, :])
    ones_bd = _seg_ones(GROUP_LANES, CONV_GROUP_DIM)
    ycs = []
    for s in range(dc // GROUP_LANES):
        ys = yc[:, s * GROUP_LANES:(s + 1) * GROUP_LANES]
        ms = _seg_sum(ys * ys, ones_bd) * (1.0 / CONV_GROUP_DIM)
        ycs.append(ys * lax.rsqrt(ms + RMS_EPS))
    y_conv = (jnp.concatenate(ycs, axis=1) * cnw_ref[...]).astype(BF16)

    dr = yr_ref.shape[1]
    y_mix = _dot(yr_ref[...], wo_ref[0:dr, :]) + _dot(y_conv, wo_ref[dr:, :])
    x1 = x_ref[...] + mod_ref[0, 2:3, :] * y_mix
    x1_ref[...] = x1
    ms = jnp.mean(x1 * x1, axis=-1, keepdims=True)
    h2 = x1 * lax.rsqrt(ms + RMS_EPS) * n2w_ref[...]
    h2 = h2 * (1.0 + mod_ref[0, 4:5, :]) + mod_ref[0, 3:4, :]
    h2_ref[...] = h2

    logits = _dot_x3(h2, wr_ref[...]) + br_ref[...]
    lane = lax.broadcasted_iota(jnp.int32, logits.shape, 1)
    neg = jnp.float32(-jnp.inf)
    big = jnp.int32(1 << 20)
    glog = jnp.where(lane < N_ROUTE_GROUPS, logits, neg)
    gmax = jnp.max(glog, axis=-1, keepdims=True)
    grp = jnp.min(jnp.where(glog == gmax, lane, big), axis=-1, keepdims=True)
    p_grp = 1.0 / jnp.sum(jnp.exp(glog - gmax), axis=-1, keepdims=True)
    e_lane = lane - N_ROUTE_GROUPS
    in_grp = (e_lane >= grp * EXPERTS_PER_GROUP) & (e_lane < (grp + 1) * EXPERTS_PER_GROUP)
    elog = jnp.where(in_grp, logits, neg)
    m1 = jnp.max(elog, axis=-1, keepdims=True)
    i1 = jnp.min(jnp.where(elog == m1, lane, big), axis=-1, keepdims=True)
    elog2 = jnp.where(lane == i1, neg, elog)
    m2 = jnp.max(elog2, axis=-1, keepdims=True)
    i2 = jnp.min(jnp.where(elog2 == m2, lane, big), axis=-1, keepdims=True)
    t2 = jnp.exp(m2 - m1)
    w1 = p_grp / (1.0 + t2)
    w2 = p_grp * t2 / (1.0 + t2)
    e1 = i1 - N_ROUTE_GROUPS
    e2 = i2 - N_ROUTE_GROUPS

    oh1 = lane == e1
    oh2 = lane == e2
    oh = jnp.where(oh1 | oh2, 1.0, 0.0)
    tr = lax.broadcasted_iota(jnp.int32, (tm, tm), 0)
    tc = lax.broadcasted_iota(jnp.int32, (tm, tm), 1)
    lower = jnp.where(tr > tc, 1.0, 0.0).astype(BF16)
    before = _dot(lower, oh.astype(BF16)) + cnt_acc_ref[...]
    rank1 = jnp.sum(jnp.where(oh1, before, 0.0), axis=-1, keepdims=True)
    rank2 = jnp.sum(jnp.where(oh2, before, 0.0), axis=-1, keepdims=True)
    cnt = cnt_acc_ref[...] + jnp.sum(oh, axis=0, keepdims=True)
    cnt_acc_ref[...] = cnt
    cnt_ref[...] = jnp.broadcast_to(cnt, cnt_ref.shape)

    meta = jnp.where(lane == 0, e1.astype(F32), 0.0)
    meta = jnp.where(lane == 1, e2.astype(F32), meta)
    meta = jnp.where(lane == 2, w1, meta)
    meta = jnp.where(lane == 3, w2, meta)
    meta = jnp.where(lane == 4, rank1, meta)
    meta = jnp.where(lane == 5, rank2, meta)
    meta_ref[...] = meta


def _outproj(yr, proj, conv_w, conv_norm_w, w_out_bf, x2, mod3, norm2_w, wr, br, bsz, t_len):
    n, d = x2.shape
    dr = yr.shape[1]
    dc = conv_w.shape[1]
    tm = min(512, t_len)
    nt = t_len // tm
    nseg = dr // dc
    row = lambda off: (lambda b, t: (b * nt + t, off))
    const = lambda b, t: (0, 0)
    return pl.pallas_call(
        _outproj_kernel,
        out_shape=(jax.ShapeDtypeStruct((n, d), F32), jax.ShapeDtypeStruct((n, d), F32),
                   jax.ShapeDtypeStruct((n, LANES), F32), jax.ShapeDtypeStruct((8, LANES), F32)),
        grid=(bsz, nt),
        in_specs=[pl.BlockSpec((tm, dr), row(0)),
                  pl.BlockSpec((tm, dc), row(3 * nseg)),
                  pl.BlockSpec((tm, dc), row(4 * nseg)),
                  pl.BlockSpec((tm, dc), row(5 * nseg)),
                  pl.BlockSpec((3, dc), const),
                  pl.BlockSpec((1, dc), const),
                  pl.BlockSpec((dr + dc, d), const),
                  pl.BlockSpec((tm, d), row(0)),
                  pl.BlockSpec((1, 6, d), lambda b, t: (b, 0, 0)),
                  pl.BlockSpec((1, d), const),
                  pl.BlockSpec((d, LANES), const),
                  pl.BlockSpec((1, LANES), const)],
        out_specs=(pl.BlockSpec((tm, d), row(0)), pl.BlockSpec((tm, d), row(0)),
                   pl.BlockSpec((tm, LANES), row(0)), pl.BlockSpec((8, LANES), const)),
        scratch_shapes=[pltpu.VMEM((8, dc), F32), pltpu.VMEM((1, LANES), F32)],
        compiler_params=pltpu.CompilerParams(dimension_semantics=("arbitrary", "arbitrary"),
                                             vmem_limit_bytes=VMEM_LIMIT),
        name="outproj",
    )(yr, proj, proj, proj, conv_w, conv_norm_w.reshape(1, dc), w_out_bf, x2, mod3,
      norm2_w.reshape(1, d), wr, br)


def _dispatch_kernel(zoff_ref, dest_hbm, h2_hbm, xs_hbm, idx_smem, zero_ref, zblk_ref,
                     sem_idx, sem_rows, sem_zero, sem_tail):
    i = pl.program_id(0)
    nsteps = pl.num_programs(0)
    th = idx_smem.shape[1] // 2
    slot = i % 2

    def idx_copy(step, sl):
        return pltpu.make_async_copy(dest_hbm.at[step], idx_smem.at[sl], sem_idx.at[sl])

    def rows_wait(sl):
        pltpu.make_async_copy(xs_hbm.at[pl.ds(0, 2 * th)], xs_hbm.at[pl.ds(0, 2 * th)],
                              sem_rows.at[sl]).wait()

    nblk = xs_hbm.shape[0] // ROUTE_BLOCK

    def tail_copy(b):
        start = pl.multiple_of(b * ROUTE_BLOCK, ROUTE_BLOCK)
        return pltpu.make_async_copy(zblk_ref, xs_hbm.at[pl.ds(start, ROUTE_BLOCK)], sem_tail)

    @pl.when(i == 0)
    def _():
        idx_copy(0, 0).start()
        zero_ref[...] = jnp.zeros_like(zero_ref)

        def zero_expert(e, carry):
            off = zoff_ref[e]

            def zero_row(j, c):
                pltpu.make_async_copy(zero_ref, xs_hbm.at[pl.ds(off + j, 1)], sem_zero).start()
                return c

            return lax.fori_loop(0, zoff_ref[N_EXPERTS + e], zero_row, carry)

        lax.fori_loop(0, N_EXPERTS, zero_expert, 0)
        total = zoff_ref[2 * N_EXPERTS]
        bulk = pl.multiple_of((total // 8) * 8, 8)

        @pl.when(bulk > 0)
        def _():
            pltpu.make_async_copy(xs_hbm.at[pl.ds(0, bulk)], xs_hbm.at[pl.ds(0, bulk)], sem_zero).wait()

        def wait_row(j, c):
            pltpu.make_async_copy(zero_ref, xs_hbm.at[pl.ds(0, 1)], sem_zero).wait()
            return c

        lax.fori_loop(0, total - bulk, wait_row, 0)

        zblk_ref[...] = jnp.zeros_like(zblk_ref)

        def tail_start(b, c):
            tail_copy(b).start()
            return c

        lax.fori_loop(zoff_ref[2 * N_EXPERTS + 1], nblk, tail_start, 0)

    idx_copy(i, slot).wait()

    @pl.when(i + 1 < nsteps)
    def _():
        idx_copy(i + 1, 1 - slot).start()

    def body(t, carry):
        for s in range(2):
            d = idx_smem[slot, 2 * t + s]
            pltpu.make_async_copy(h2_hbm.at[pl.ds(i * th + t, 1)], xs_hbm.at[pl.ds(d, 1)],
                                  sem_rows.at[slot]).start()
        return carry

    lax.fori_loop(0, th, body, 0, unroll=8)

    @pl.when(i > 0)
    def _():
        rows_wait(1 - slot)

    @pl.when(i == nsteps - 1)
    def _():
        rows_wait(slot)

        def tail_wait(b, c):
            tail_copy(b).wait()
            return c

        lax.fori_loop(zoff_ref[2 * N_EXPERTS + 1], nblk, tail_wait, 0)


def _dispatch(dest, zero_off, h2, n_rows):
    n, d = h2.shape
    th = min(256, n)
    return pl.pallas_call(
        _dispatch_kernel,
        out_shape=jax.ShapeDtypeStruct((n_rows, d), h2.dtype),
        grid_spec=pltpu.PrefetchScalarGridSpec(
            num_scalar_prefetch=1,
            grid=(n // th,),
            in_specs=[pl.BlockSpec(memory_space=pl.ANY), pl.BlockSpec(memory_space=pl.ANY)],
            out_specs=pl.BlockSpec(memory_space=pl.ANY),
            scratch_shapes=[pltpu.SMEM((2, 2 * th), jnp.int32),
                            pltpu.VMEM((1, d), h2.dtype),
                            pltpu.VMEM((ROUTE_BLOCK, d), h2.dtype),
                            pltpu.SemaphoreType.DMA((2,)), pltpu.SemaphoreType.DMA((2,)),
                            pltpu.SemaphoreType.DMA, pltpu.SemaphoreType.DMA]),
        compiler_params=pltpu.CompilerParams(dimension_semantics=("arbitrary",),
                                             vmem_limit_bytes=VMEM_LIMIT),
        name="dispatch",
    )(zero_off, dest.reshape(n // th, 2 * th), h2)


def _expert_kernel(be_ref, na_ref, xs_ref, wg_ref, wu_ref, wd_ref, ys_ref):
    del be_ref

    @pl.when(pl.program_id(0) < na_ref[0])
    def _():
        x = xs_ref[...].astype(BF16)
        g = _dot(x, wg_ref[0].astype(BF16))
        u = _dot(x, wu_ref[0].astype(BF16))
        hid = (g * _sigmoid(g) * u).astype(BF16)
        ys_ref[...] = _dot(hid, wd_ref[0].astype(BF16))

    @pl.when(pl.program_id(0) >= na_ref[0])
    def _():
        ys_ref[...] = jnp.zeros_like(ys_ref)


def _experts(block_e, n_active, xs, exp_gate, exp_up, exp_down):
    n_rows, d = xs.shape
    de = exp_gate.shape[2]
    nblk = n_rows // ROUTE_BLOCK
    blk_row = lambda i, be, na: (jnp.minimum(i, na[0] - 1), 0)
    blk_exp = lambda i, be, na: (be[jnp.minimum(i, na[0] - 1)], 0, 0)
    return pl.pallas_call(
        _expert_kernel,
        out_shape=jax.ShapeDtypeStruct((n_rows, d), F32),
        grid_spec=pltpu.PrefetchScalarGridSpec(
            num_scalar_prefetch=2,
            grid=(nblk,),
            in_specs=[pl.BlockSpec((ROUTE_BLOCK, d), blk_row),
                      pl.BlockSpec((1, d, de), blk_exp),
                      pl.BlockSpec((1, d, de), blk_exp),
                      pl.BlockSpec((1, de, d), blk_exp)],
            out_specs=pl.BlockSpec((ROUTE_BLOCK, d), lambda i, be, na: (i, 0))),
        compiler_params=pltpu.CompilerParams(dimension_semantics=("arbitrary",),
                                             vmem_limit_bytes=VMEM_LIMIT),
        name="experts",
    )(block_e, n_active, xs, exp_gate, exp_up, exp_down)


def _combine_kernel(dest_hbm, ys_hbm, x1_ref, meta_ref, mod_ref, fw_ref, o_ref,
                    idx_smem, buf_ref, sem_idx, sem_rows):
    i = pl.program_id(0)
    nsteps = pl.num_programs(0)
    tj = x1_ref.shape[0]
    slot = i % 2

    def gather(step, sl):
        cp = pltpu.make_async_copy(dest_hbm.at[step], idx_smem.at[sl], sem_idx)
        cp.start()
        cp.wait()

        def body(t, carry):
            for s in range(2):
                d = idx_smem[sl, 2 * t + s]
                pltpu.make_async_copy(ys_hbm.at[pl.ds(d, 1)], buf_ref.at[sl, s, pl.ds(t, 1)],
                                      sem_rows.at[sl]).start()
            return carry

        lax.fori_loop(0, tj, body, 0, unroll=8)

    @pl.when(i == 0)
    def _():
        gather(0, 0)

    @pl.when(i + 1 < nsteps)
    def _():
        gather(i + 1, 1 - slot)

    pltpu.make_async_copy(buf_ref.at[slot], buf_ref.at[slot], sem_rows.at[slot]).wait()

    meta = meta_ref[...]
    y = meta[:, 2:3] * buf_ref[slot, 0] + meta[:, 3:4] * buf_ref[slot, 1]
    x2 = x1_ref[...] + mod_ref[0, 5:6, :] * y
    ms = jnp.mean(x2 * x2, axis=-1, keepdims=True)
    o_ref[...] = x2 * lax.rsqrt(ms + RMS_EPS) * fw_ref[...]


def _combine(dest, ys, x1, meta, mod3, final_w, t_len):
    n, d = x1.shape
    tj = min(256, t_len)
    return pl.pallas_call(
        _combine_kernel,
        out_shape=jax.ShapeDtypeStruct((n, d), F32),
        grid=(n // tj,),
        in_specs=[pl.BlockSpec(memory_space=pl.ANY),
                  pl.BlockSpec(memory_space=pl.ANY),
                  pl.BlockSpec((tj, d), lambda i: (i, 0)),
                  pl.BlockSpec((tj, LANES), lambda i: (i, 0)),
                  pl.BlockSpec((1, 6, d), lambda i: (i * tj // t_len, 0, 0)),
                  pl.BlockSpec((1, d), lambda i: (0, 0))],
        out_specs=pl.BlockSpec((tj, d), lambda i: (i, 0)),
        scratch_shapes=[pltpu.SMEM((2, 2 * tj), jnp.int32), pltpu.VMEM((2, 2, tj, d), F32),
                        pltpu.SemaphoreType.DMA, pltpu.SemaphoreType.DMA((2,))],
        compiler_params=pltpu.CompilerParams(dimension_semantics=("arbitrary",),
                                             vmem_limit_bytes=VMEM_LIMIT),
        name="combine",
    )(dest.reshape(n // tj, 2 * tj), ys, x1, meta, mod3, final_w.reshape(1, d))


def _pad_cols(w, width):
    return jnp.pad(w, ((0, 0), (0, width - w.shape[1])))


def _pad_rows(w, height):
    return jnp.pad(w, ((0, height - w.shape[0]), (0, 0)))


def _layer(x2, c, bsz, t_len, w_ada, b_ada, norm1_w, w_in, mu_r, mu_k, mu_v, mu_w, mu_a, mu_g, w0,
           w_decay1, w_decay2, a0, w_aaa1, w_aaa2, w_gate1, w_gate2, k_k, k_a, r_k, lnx_w, lnx_b,
           conv_w, conv_norm_w, w_out, norm2_w, w_grp, b_grp, w_exp, b_exp, exp_gate, exp_up,
           exp_down, final_w):
    n, d = x2.shape
    dr = w0.shape[0]
    assert w_decay1.shape[1] <= 128 and w_aaa1.shape[1] <= 128 and w_gate1.shape[1] <= 256
    assert dr % GROUP_LANES == 0 and conv_w.shape[1] == dr and w_in.shape[1] == 6 * dr
    assert t_len % WKV_CHUNK == 0 and w_grp.shape[1] == N_ROUTE_GROUPS and w_exp.shape[1] == N_EXPERTS

    mod3 = _ada(c, w_ada, b_ada).reshape(bsz, 6, d)
    h, proj = _inproj(x2, mod3, norm1_w, w_in.astype(BF16), t_len)

    w1p = jnp.concatenate([_pad_cols(w_decay1, 128), _pad_cols(w_aaa1, 128), _pad_cols(w_gate1, 256)], axis=1)
    mup = jnp.concatenate([jnp.broadcast_to(mu_w[:, None], (d, 128)),
                           jnp.broadcast_to(mu_a[:, None], (d, 128)),
                           jnp.broadcast_to(mu_g[:, None], (d, 256))], axis=1)
    lw, ag, gg = _lora(h, w1p, mup, _pad_rows(w_decay2, 128), _pad_rows(w_aaa2, 128),
                       _pad_rows(w_gate2, 256), w0, a0, bsz, t_len)

    yr = _wkv(proj, lw, ag, gg, mu_r, mu_k, mu_v, k_k, k_a, r_k.reshape(dr), lnx_w, lnx_b, bsz, t_len)

    wr = _pad_cols(jnp.concatenate([w_grp, w_exp], axis=1), LANES)
    br = _pad_cols(jnp.concatenate([b_grp, b_exp]).reshape(1, -1), LANES)
    x1, h2, meta, cnt = _outproj(yr, proj, conv_w, conv_norm_w, w_out.astype(BF16), x2, mod3,
                                 norm2_w, wr, br, bsz, t_len)

    counts = cnt[0, :N_EXPERTS].astype(jnp.int32)
    padded = (counts + ROUTE_BLOCK - 1) // ROUTE_BLOCK * ROUTE_BLOCK
    pad_ends = jnp.cumsum(padded)
    pad_starts = pad_ends - padded
    top_e = meta[:, 0:2].astype(jnp.int32)
    rank = meta[:, 4:6].astype(jnp.int32)
    dest = (pad_starts[top_e] + rank).reshape(-1)
    nblk = (2 * n) // ROUTE_BLOCK + N_EXPERTS
    n_rows = nblk * ROUTE_BLOCK
    block_start = jnp.arange(nblk, dtype=jnp.int32) * ROUTE_BLOCK
    block_e = jnp.minimum(jnp.sum(pad_ends[None, :] <= block_start[:, None], axis=1),
                          N_EXPERTS - 1).astype(jnp.int32)
    n_active = (pad_ends[-1:] // ROUTE_BLOCK).astype(jnp.int32)

    pad_len = padded - counts
    zero_off = jnp.concatenate([pad_starts + counts, pad_len, jnp.sum(pad_len, keepdims=True),
                                n_active]).astype(jnp.int32)
    xs = _dispatch(dest, zero_off, h2, n_rows)
    ys = _experts(block_e, n_active, xs, exp_gate, exp_up, exp_down)
    return _combine(dest, ys, x1, meta, mod3, final_w, t_len)


def kernel(x, c, w_ada, b_ada, norm1_w, w_in, mu_r, mu_k, mu_v, mu_w, mu_a, mu_g, w0, w_decay1, w_decay2, a0, w_aaa1, w_aaa2, w_gate1, w_gate2, k_k, k_a, r_k, lnx_w, lnx_b, conv_w, conv_norm_w, w_out, norm2_w, w_grp, b_grp, w_exp, b_exp, exp_gate, exp_up, exp_down, final_w):
    bsz, t_len, d = x.shape
    assert w_ada.shape[0] == 1, "single-layer block"
    out = _layer(x.reshape(bsz * t_len, d), c, bsz, t_len, w_ada[0], b_ada[0], norm1_w[0], w_in[0],
                 mu_r[0], mu_k[0], mu_v[0], mu_w[0], mu_a[0], mu_g[0], w0[0], w_decay1[0], w_decay2[0],
                 a0[0], w_aaa1[0], w_aaa2[0], w_gate1[0], w_gate2[0], k_k[0], k_a[0], r_k[0], lnx_w[0],
                 lnx_b[0], conv_w[0], conv_norm_w[0], w_out[0], norm2_w[0], w_grp[0], b_grp[0],
                 w_exp[0], b_exp[0], exp_gate[0], exp_up[0], exp_down[0], final_w)
    return out.reshape(bsz, t_len, d)
```

```python
import functools

import jax
import jax.numpy as jnp
from jax import lax
from jax.experimental import pallas as pl
from jax.experimental.pallas import tpu as pltpu

F32 = jnp.float32
BF16 = jnp.bfloat16

HEAD_DIM = 64
WKV_CHUNK = 64
GROUP_LANES = 256
N_ROUTE_GROUPS = 4
EXPERTS_PER_GROUP = 8
N_EXPERTS = N_ROUTE_GROUPS * EXPERTS_PER_GROUP
CONV_GROUP_DIM = 64
RMS_EPS = 1e-6
LNX_EPS = 64e-5
LANES = 128
LORA_PAD = 512
ROUTE_BLOCK = 256
VMEM_LIMIT = 56 * 1024 * 1024


def _dot(a, b, precision=None):
    return jnp.dot(a, b, preferred_element_type=F32, precision=precision)


def _dot_nt(a, b):
    return lax.dot_general(a, b, (((1,), (1,)), ((), ())), preferred_element_type=F32)


def _dot_tn(a, b):
    return lax.dot_general(a, b, (((0,), (0,)), ((), ())), preferred_element_type=F32)


def _split_bf16(x):
    hi = x.astype(BF16)
    return hi, (x - hi.astype(F32)).astype(BF16)


def _dot_x3(a, b):
    ah, al = _split_bf16(a)
    bh, bl = _split_bf16(b)
    return _dot(ah, bh) + _dot(al, bh) + _dot(ah, bl)


def _sigmoid(x):
    return 1.0 / (1.0 + jnp.exp(-x))


def _shift_rows(x, carry_row):
    rolled = pltpu.roll(x, 1, axis=0)
    row = lax.broadcasted_iota(jnp.int32, x.shape, 0)
    return jnp.where(row == 0, carry_row, rolled)


def _seg_ones(n, seg):
    r = lax.broadcasted_iota(jnp.int32, (n, n), 0) // seg
    c = lax.broadcasted_iota(jnp.int32, (n, n), 1) // seg
    return jnp.where(r == c, 1.0, 0.0).astype(BF16)


def _seg_sum(x, ones_bd):
    hi = x.astype(BF16)
    lo = (x - hi.astype(F32)).astype(BF16)
    return _dot(hi, ones_bd) + _dot(lo, ones_bd)


def _ada_kernel(cb_ref, w_ref, b_ref, o_ref):
    nb = cb_ref.shape[0]
    tn = w_ref.shape[1]
    for b in range(nb):
        cv = cb_ref[b]
        s = cv * _sigmoid(cv)
        cols = [jnp.sum(s * w_ref[:, j * LANES:(j + 1) * LANES], axis=0, keepdims=True)
                for j in range(tn // LANES)]
        o_ref[b:b + 1, :] = jnp.concatenate(cols, axis=1) + b_ref[...]


def _ada(c, w_ada, b_ada):
    nb, d = c.shape
    n_out = w_ada.shape[1]
    tn = 512
    cb = jnp.broadcast_to(c[:, :, None], (nb, d, LANES))
    return pl.pallas_call(
        _ada_kernel,
        out_shape=jax.ShapeDtypeStruct((nb, n_out), F32),
        grid=(n_out // tn,),
        in_specs=[pl.BlockSpec((nb, d, LANES), lambda j: (0, 0, 0)),
                  pl.BlockSpec((d, tn), lambda j: (0, j)),
                  pl.BlockSpec((1, tn), lambda j: (0, j))],
        out_specs=pl.BlockSpec((nb, tn), lambda j: (0, j)),
        compiler_params=pltpu.CompilerParams(dimension_semantics=("arbitrary",),
                                             vmem_limit_bytes=VMEM_LIMIT),
        name="ada",
    )(cb, w_ada, b_ada.reshape(1, n_out))


def _inproj_kernel(x_ref, mod_ref, nw_ref, w_ref, h_ref, p_ref, hs_ref):
    @pl.when(pl.program_id(1) == 0)
    def _():
        x = x_ref[...]
        ms = jnp.mean(x * x, axis=-1, keepdims=True)
        y = x * lax.rsqrt(ms + RMS_EPS) * nw_ref[...]
        h = y * (1.0 + mod_ref[0, 1:2, :]) + mod_ref[0, 0:1, :]
        hb = h.astype(BF16)
        hs_ref[...] = hb
        h_ref[...] = hb

    p_ref[...] = _dot(hs_ref[...], w_ref[...])


def _inproj(x2, mod3, norm_w, w_in_bf, t_len):
    n, d = x2.shape
    n_in = w_in_bf.shape[1]
    tm = min(1024, t_len)
    tn = 1024
    return pl.pallas_call(
        _inproj_kernel,
        out_shape=(jax.ShapeDtypeStruct((n, d), BF16), jax.ShapeDtypeStruct((n, n_in), F32)),
        grid=(n // tm, n_in // tn),
        in_specs=[pl.BlockSpec((tm, d), lambda i, j: (i, 0)),
                  pl.BlockSpec((1, 6, d), lambda i, j: (i * tm // t_len, 0, 0)),
                  pl.BlockSpec((1, d), lambda i, j: (0, 0)),
                  pl.BlockSpec((d, tn), lambda i, j: (0, j))],
        out_specs=(pl.BlockSpec((tm, d), lambda i, j: (i, 0)),
                   pl.BlockSpec((tm, tn), lambda i, j: (i, j))),
        scratch_shapes=[pltpu.VMEM((tm, d), BF16)],
        compiler_params=pltpu.CompilerParams(dimension_semantics=("arbitrary", "arbitrary"),
                                             vmem_limit_bytes=VMEM_LIMIT),
        name="inproj",
    )(x2, mod3, norm_w.reshape(1, d), w_in_bf)


def _lora_kernel(h_ref, w1_ref, mu_ref, wd2_ref, wa2_ref, wg2_ref, w0_ref, a0_ref,
                 lw_ref, ag_ref, gg_ref, carry_ref):
    @pl.when(pl.program_id(1) == 0)
    def _():
        carry_ref[...] = jnp.zeros_like(carry_ref)

    h = h_ref[...]
    w1 = w1_ref[...]
    p1 = _dot(h, w1.astype(BF16))
    p2 = _dot(h, (w1 * mu_ref[...]).astype(BF16))
    prev = _shift_rows(p2, carry_ref[...])
    carry_ref[...] = p2[p2.shape[0] - 1:, :]
    lo = p1 - p2 + prev
    dec = _dot_x3(jnp.tanh(lo[:, 0:128]), wd2_ref[...])
    z = -(w0_ref[...] + dec)
    softplus = jnp.maximum(z, 0.0) + jnp.log(1.0 + jnp.exp(-jnp.abs(z)))
    lw_ref[...] = -jnp.exp(-softplus - 0.5)
    ag_ref[...] = _sigmoid(a0_ref[...] + _dot_x3(lo[:, 128:256], wa2_ref[...]))
    gg_ref[...] = _dot_x3(_sigmoid(lo[:, 256:512]), wg2_ref[...])


def _lora(h, w1p, mup, wd2p, wa2p, wg2p, w0, a0, bsz, t_len):
    n, d = h.shape
    dr = w0.shape[-1]
    tm = min(512, t_len)
    nt = t_len // tm
    row = lambda b, t: (b * nt + t, 0)
    const = lambda b, t: (0, 0)
    out = jax.ShapeDtypeStruct((n, dr), F32)
    return pl.pallas_call(
        _lora_kernel,
        out_shape=(out, out, out),
        grid=(bsz, nt),
        in_specs=[pl.BlockSpec((tm, d), row),
                  pl.BlockSpec((d, LORA_PAD), const),
                  pl.BlockSpec((d, LORA_PAD), const),
                  pl.BlockSpec((128, dr), const),
                  pl.BlockSpec((128, dr), const),
                  pl.BlockSpec((256, dr), const),
                  pl.BlockSpec((1, dr), const),
                  pl.BlockSpec((1, dr), const)],
        out_specs=(pl.BlockSpec((tm, dr), row),) * 3,
        scratch_shapes=[pltpu.VMEM((1, LORA_PAD), F32)],
        compiler_params=pltpu.CompilerParams(dimension_semantics=("arbitrary", "arbitrary"),
                                             vmem_limit_bytes=VMEM_LIMIT),
        name="lora",
    )(h, w1p, mup, wd2p, wa2p, wg2p, w0.reshape(1, dr), a0.reshape(1, dr))


def _wkv_kernel(r_ref, k_ref, v_ref, lw_ref, ag_ref, gg_ref, mur_ref, muk_ref, muv_ref,
                kk_ref, ka_ref, rk_ref, lnw_ref, lnb_ref, y_ref, s_ref, carry_ref):
    lt = r_ref.shape[0]
    gl = GROUP_LANES
    c_len = WKV_CHUNK
    heads = gl // HEAD_DIM

    @pl.when(pl.program_id(2) == 0)
    def _():
        s_ref[...] = jnp.zeros_like(s_ref)
        carry_ref[...] = jnp.zeros_like(carry_ref)

    ones_bd = _seg_ones(gl, HEAD_DIM)

    def lerp(ref, mu_ref, idx):
        z = ref[...]
        prev = _shift_rows(z, carry_ref[idx:idx + 1, :])
        carry_ref[idx:idx + 1, :] = z[lt - 1:, :]
        return z + (prev - z) * mu_ref[...]

    r1 = lerp(r_ref, mur_ref, 0)
    k1 = lerp(k_ref, muk_ref, 1)
    v1 = lerp(v_ref, muv_ref, 2)
    ag = ag_ref[...]
    kk = k1 * kk_ref[...]
    kk = kk * lax.rsqrt(jnp.maximum(_seg_sum(kk * kk, ones_bd), 1e-24))
    k2 = k1 * (1.0 + (ag - 1.0) * ka_ref[...])
    a_all = -kk
    b_all = kk * ag
    lw = lw_ref[...]

    rows = lax.broadcasted_iota(jnp.int32, (gl, gl), 0)
    cols = lax.broadcasted_iota(jnp.int32, (gl, gl), 1)
    same_head = (rows // c_len) == (cols // HEAD_DIM)
    tri_strict = same_head & ((rows % c_len) > (cols % c_len))
    tri_incl = same_head & ((rows % c_len) >= (cols % c_len))
    eye = jnp.where(rows == cols, 1.0, 0.0).astype(F32)
    cr = lax.broadcasted_iota(jnp.int32, (c_len, c_len), 0)
    cc = lax.broadcasted_iota(jnp.int32, (c_len, c_len), 1)
    cum_incl = jnp.where(cr >= cc, 1.0, 0.0).astype(F32)

    def bd(x):
        xb = x.astype(BF16)
        return jnp.where(same_head, jnp.concatenate([xb] * heads, axis=0), jnp.zeros((), BF16))

    def fold(x):
        out = x[0:c_len]
        for hh in range(1, heads):
            out = out + x[hh * c_len:(hh + 1) * c_len]
        return out

    nch = lt // c_len
    chunks = range(nch)
    sls = [slice(c * c_len, (c + 1) * c_len) for c in chunks]
    lws = [lw[sl] for sl in sls]
    cum_b = cum_incl.astype(BF16)

    def cumsum3(x):
        hi = x.astype(BF16)
        r = x - hi.astype(F32)
        mid = r.astype(BF16)
        lo = (r - mid.astype(F32)).astype(BF16)
        return _dot(cum_b, hi) + _dot(cum_b, mid) + _dot(cum_b, lo)

    cs = [cumsum3(x) for x in lws]
    e_cs = [jnp.exp(x) for x in cs]
    e_neg = [jnp.exp(-x) for x in cs]
    wc = [x[c_len - 1:, :] for x in e_cs]
    at = [a_all[sls[c]] * jnp.exp(cs[c] - lws[c]) for c in chunks]
    rt = [r1[sls[c]] * e_cs[c] for c in chunks]
    bt = [b_all[sls[c]] * e_neg[c] for c in chunks]
    kt = [k2[sls[c]] * e_neg[c] for c in chunks]
    vc = [v1[sl] for sl in sls]

    bd_a = [bd(x) for x in at]
    lhs = [jnp.concatenate([bd_a[c], bd(rt[c])], axis=0) for c in chunks]
    g_b = [_dot_nt(lhs[c], bd(bt[c])) for c in chunks]
    g_k = [_dot_nt(lhs[c], bd(kt[c])) for c in chunks]
    n_ab = [jnp.where(tri_strict, g[:gl], 0.0) for g in g_b]
    a_ak = [jnp.where(tri_strict, g[:gl], 0.0).astype(BF16) for g in g_k]
    a_rb = [jnp.where(tri_incl, g[gl:], 0.0).astype(BF16) for g in g_b]
    a_rk = [jnp.where(tri_incl, g[gl:], 0.0).astype(BF16) for g in g_k]

    nb = [x.astype(BF16) for x in n_ab]
    q = [_dot(x, x) for x in nb]
    p = [eye + x for x in n_ab]
    for _ in range(4):
        qb = [x.astype(BF16) for x in q]
        pq = [_dot(jnp.concatenate([p[c].astype(BF16), qb[c]], axis=0), qb[c]) for c in chunks]
        p = [p[c] + pq[c][:gl] for c in chunks]
        q = [x[gl:] for x in pq]
    p = [p[c] + _dot(p[c].astype(BF16), q[c].astype(BF16)) for c in chunks]
    tb = [x.astype(BF16) for x in p]

    z = [_dot(tb[c], jnp.concatenate([a_ak[c], bd_a[c]], axis=1)) for c in chunks]
    ap_bd = [x[:, gl:].astype(BF16) for x in z]
    bd_v = [bd(x) for x in vc]
    u0_bd = [_dot(z[c][:, :gl].astype(BF16), bd_v[c]).astype(BF16) for c in chunks]
    w = [_dot(a_rb[c], jnp.concatenate([ap_bd[c], u0_bd[c]], axis=1)) for c in chunks]
    y0 = [fold(w[c][:, gl:] + _dot(a_rk[c], bd_v[c])) for c in chunks]
    ap = [fold(x[:, gl:]) for x in z]
    u0 = [fold(x.astype(F32)) for x in u0_bd]
    rp = [rt[c] + fold(w[c][:, :gl]) for c in chunks]
    bh = [(bt[c] * wc[c]).astype(BF16) for c in chunks]
    bk = [jnp.concatenate([bh[c], (kt[c] * wc[c]).astype(BF16)], axis=0) for c in chunks]
    g_s = [jnp.where(same_head, _dot_tn(ap[c].astype(BF16), bh[c]), 0.0).astype(BF16) for c in chunks]
    n_s = [jnp.where(same_head, _dot_tn(jnp.concatenate([u0[c], vc[c]], axis=0).astype(BF16), bk[c]), 0.0)
           for c in chunks]

    s = s_ref[...]
    s_in = []
    for c in chunks:
        s_in.append(s.astype(BF16))
        s = s * wc[c] + _dot(s_in[c], g_s[c]) + n_s[c]
    s_ref[...] = s
    y_chunks = [_dot_nt(rp[c].astype(BF16), s_in[c]) + y0[c] for c in chunks]

    y = jnp.concatenate(y_chunks, axis=0) if len(y_chunks) > 1 else y_chunks[0]
    inv_n = 1.0 / HEAD_DIM
    mean = _seg_sum(y, ones_bd) * inv_n
    yc = y - mean
    var = _seg_sum(yc * yc, ones_bd) * inv_n
    yn = yc * lax.rsqrt(var + LNX_EPS) * lnw_ref[...] + lnb_ref[...]
    bonus = _seg_sum(r1 * k2 * rk_ref[...], ones_bd) * v1
    y_ref[...] = ((yn + bonus) * gg_ref[...]).astype(y_ref.dtype)


def _wkv(proj, lw, ag, gg, mu_r, mu_k, mu_v, k_k, k_a, r_k, lnx_w, lnx_b, bsz, t_len):
    n = proj.shape[0]
    dr = lw.shape[1]
    gl = GROUP_LANES
    ng = dr // gl
    lt = min(512, t_len)
    nt = t_len // lt
    row = lambda off: (lambda b, g, t: (b * nt + t, off + g))
    par = lambda b, g, t: (0, g)
    vec = lambda a: a.reshape(1, dr)
    return pl.pallas_call(
        _wkv_kernel,
        out_shape=jax.ShapeDtypeStruct((n, dr), BF16),
        grid=(bsz, ng, nt),
        in_specs=[pl.BlockSpec((lt, gl), row(0)),
                  pl.BlockSpec((lt, gl), row(ng)),
                  pl.BlockSpec((lt, gl), row(2 * ng)),
                  pl.BlockSpec((lt, gl), row(0)),
                  pl.BlockSpec((lt, gl), row(0)),
                  pl.BlockSpec((lt, gl), row(0))] + [pl.BlockSpec((1, gl), par)] * 8,
        out_specs=pl.BlockSpec((lt, gl), row(0)),
        scratch_shapes=[pltpu.VMEM((gl, gl), F32), pltpu.VMEM((8, gl), F32)],
        compiler_params=pltpu.CompilerParams(
            dimension_semantics=("arbitrary", "arbitrary", "arbitrary"),
            vmem_limit_bytes=VMEM_LIMIT),
        name="wkv",
    )(proj, proj, proj, lw, ag, gg, vec(mu_r), vec(mu_k), vec(mu_v), vec(k_k), vec(k_a),
      vec(r_k), vec(lnx_w), vec(lnx_b))


def _outproj_kernel(yr_ref, u_ref, gb_ref, gc_ref, cw_ref, cnw_ref, wo_ref, x_ref, mod_ref,
                    n2w_ref, wr_ref, br_ref, x1_ref, h2_ref, meta_ref, cnt_ref,
                    ccarry_ref, cnt_acc_ref):
    tm = x_ref.shape[0]
    dc = u_ref.shape[1]

    @pl.when(pl.program_id(1) == 0)
    def _():
        ccarry_ref[...] = jnp.zeros_like(ccarry_ref)

    @pl.when((pl.program_id(0) == 0) & (pl.program_id(1) == 0))
    def _():
        cnt_acc_ref[...] = jnp.zeros_like(cnt_acc_ref)

    pre = gc_ref[...] * u_ref[...]
    prev1 = _shift_rows(pre, ccarry_ref[0:1, :])
    prev2 = _shift_rows(prev1, ccarry_ref[1:2, :])
    ccarry_ref[0:1, :] = pre[tm - 1:, :]
    ccarry_ref[1:2, :] = pre[tm - 2:tm - 1, :]
    yc = gb_ref[...] * (prev2 * cw_ref[0:1, :] + prev1 * cw_ref[1:2, :] + pre * cw_ref[2:3, :])
    ones_bd = _seg_ones(GROUP_LANES, CONV_GROUP_DIM)
    ycs = []
    for s in range(dc // GROUP_LANES):
        ys = yc[:, s * GROUP_LANES:(s + 1) * GROUP_LANES]
        ms = _seg_sum(ys * ys, ones_bd) * (1.0 / CONV_GROUP_DIM)
        ycs.append(ys * lax.rsqrt(ms + RMS_EPS))
    y_conv = (jnp.concatenate(ycs, axis=1) * cnw_ref[...]).astype(BF16)

    dr = yr_ref.shape[1]
    y_mix = _dot(yr_ref[...], wo_ref[0:dr, :]) + _dot(y_conv, wo_ref[dr:, :])
    x1 = x_ref[...] + mod_ref[0, 2:3, :] * y_mix
    x1_ref[...] = x1
    ms = jnp.mean(x1 * x1, axis=-1, keepdims=True)
    h2 = x1 * lax.rsqrt(ms + RMS_EPS) * n2w_ref[...]
    h2 = h2 * (1.0 + mod_ref[0, 4:5, :]) + mod_ref[0, 3:4, :]
    h2_ref[...] = h2

    logits = _dot_x3(h2, wr_ref[...]) + br_ref[...]
    lane = lax.broadcasted_iota(jnp.int32, logits.shape, 1)
    neg = jnp.float32(-jnp.inf)
    big = jnp.int32(1 << 20)
    glog = jnp.where(lane < N_ROUTE_GROUPS, logits, neg)
    gmax = jnp.max(glog, axis=-1, keepdims=True)
    grp = jnp.min(jnp.where(glog == gmax, lane, big), axis=-1, keepdims=True)
    p_grp = 1.0 / jnp.sum(jnp.exp(glog - gmax), axis=-1, keepdims=True)
    e_lane = lane - N_ROUTE_GROUPS
    in_grp = (e_lane >= grp * EXPERTS_PER_GROUP) & (e_lane < (grp + 1) * EXPERTS_PER_GROUP)
    elog = jnp.where(in_grp, logits, neg)
    m1 = jnp.max(elog, axis=-1, keepdims=True)
    i1 = jnp.min(jnp.where(elog == m1, lane, big), axis=-1, keepdims=True)
    elog2 = jnp.where(lane == i1, neg, elog)
    m2 = jnp.max(elog2, axis=-1, keepdims=True)
    i2 = jnp.min(jnp.where(elog2 == m2, lane, big), axis=-1, keepdims=True)
    t2 = jnp.exp(m2 - m1)
    w1 = p_grp / (1.0 + t2)
    w2 = p_grp * t2 / (1.0 + t2)
    e1 = i1 - N_ROUTE_GROUPS
    e2 = i2 - N_ROUTE_GROUPS

    oh1 = lane == e1
    oh2 = lane == e2
    oh = jnp.where(oh1 | oh2, 1.0, 0.0)
    tr = lax.broadcasted_iota(jnp.int32, (tm, tm), 0)
    tc = lax.broadcasted_iota(jnp.int32, (tm, tm), 1)
    lower = jnp.where(tr > tc, 1.0, 0.0).astype(BF16)
    before = _dot(lower, oh.astype(BF16)) + cnt_acc_ref[...]
    rank1 = jnp.sum(jnp.where(oh1, before, 0.0), axis=-1, keepdims=True)
    rank2 = jnp.sum(jnp.where(oh2, before, 0.0), axis=-1, keepdims=True)
    cnt = cnt_acc_ref[...] + jnp.sum(oh, axis=0, keepdims=True)
    cnt_acc_ref[...] = cnt
    cnt_ref[...] = jnp.broadcast_to(cnt, cnt_ref.shape)

    meta = jnp.where(lane == 0, e1.astype(F32), 0.0)
    meta = jnp.where(lane == 1, e2.astype(F32), meta)
    meta = jnp.where(lane == 2, w1, meta)
    meta = jnp.where(lane == 3, w2, meta)
    meta = jnp.where(lane == 4, rank1, meta)
    meta = jnp.where(lane == 5, rank2, meta)
    meta_ref[...] = meta


def _outproj(yr, proj, conv_w, conv_norm_w, w_out_bf, x2, mod3, norm2_w, wr, br, bsz, t_len):
    n, d = x2.shape
    dr = yr.shape[1]
    dc = conv_w.shape[1]
    tm = min(512, t_len)
    nt = t_len // tm
    nseg = dr // dc
    row = lambda off: (lambda b, t: (b * nt + t, off))
    const = lambda b, t: (0, 0)
    return pl.pallas_call(
        _outproj_kernel,
        out_shape=(jax.ShapeDtypeStruct((n, d), F32), jax.ShapeDtypeStruct((n, d), F32),
                   jax.ShapeDtypeStruct((n, LANES), F32), jax.ShapeDtypeStruct((8, LANES), F32)),
        grid=(bsz, nt),
        in_specs=[pl.BlockSpec((tm, dr), row(0)),
                  pl.BlockSpec((tm, dc), row(3 * nseg)),
                  pl.BlockSpec((tm, dc), row(4 * nseg)),
                  pl.BlockSpec((tm, dc), row(5 * nseg)),
                  pl.BlockSpec((3, dc), const),
                  pl.BlockSpec((1, dc), const),
                  pl.BlockSpec((dr + dc, d), const),
                  pl.BlockSpec((tm, d), row(0)),
                  pl.BlockSpec((1, 6, d), lambda b, t: (b, 0, 0)),
                  pl.BlockSpec((1, d), const),
                  pl.BlockSpec((d, LANES), const),
                  pl.BlockSpec((1, LANES), const)],
        out_specs=(pl.BlockSpec((tm, d), row(0)), pl.BlockSpec((tm, d), row(0)),
                   pl.BlockSpec((tm, LANES), row(0)), pl.BlockSpec((8, LANES), const)),
        scratch_shapes=[pltpu.VMEM((8, dc), F32), pltpu.VMEM((1, LANES), F32)],
        compiler_params=pltpu.CompilerParams(dimension_semantics=("arbitrary", "arbitrary"),
                                             vmem_limit_bytes=VMEM_LIMIT),
        name="outproj",
    )(yr, proj, proj, proj, conv_w, conv_norm_w.reshape(1, dc), w_out_bf, x2, mod3,
      norm2_w.reshape(1, d), wr, br)


def _dispatch_kernel(zoff_ref, dest_hbm, h2_ref, xs_hbm, idx_smem, zero_ref, zblk_ref,
                     sem_idx, sem_rows, sem_zero, sem_tail):
    i = pl.program_id(0)
    nsteps = pl.num_programs(0)
    th = idx_smem.shape[1] // 2
    slot = i % 2

    def idx_copy(step, sl):
        return pltpu.make_async_copy(dest_hbm.at[step], idx_smem.at[sl], sem_idx.at[sl])

    def rows_wait():
        pltpu.make_async_copy(xs_hbm.at[pl.ds(0, 2 * th)], xs_hbm.at[pl.ds(0, 2 * th)], sem_rows).wait()

    nblk = xs_hbm.shape[0] // ROUTE_BLOCK

    def tail_copy(b):
        start = pl.multiple_of(b * ROUTE_BLOCK, ROUTE_BLOCK)
        return pltpu.make_async_copy(zblk_ref, xs_hbm.at[pl.ds(start, ROUTE_BLOCK)], sem_tail)

    @pl.when(i == 0)
    def _():
        idx_copy(0, 0).start()
        zero_ref[...] = jnp.zeros_like(zero_ref)

        def zero_expert(e, carry):
            off = zoff_ref[e]

            def zero_row(j, c):
                pltpu.make_async_copy(zero_ref, xs_hbm.at[pl.ds(off + j, 1)], sem_zero).start()
                return c

            return lax.fori_loop(0, zoff_ref[N_EXPERTS + e], zero_row, carry)

        lax.fori_loop(0, N_EXPERTS, zero_expert, 0)
        total = zoff_ref[2 * N_EXPERTS]
        bulk = pl.multiple_of((total // 8) * 8, 8)

        @pl.when(bulk > 0)
        def _():
            pltpu.make_async_copy(xs_hbm.at[pl.ds(0, bulk)], xs_hbm.at[pl.ds(0, bulk)], sem_zero).wait()

        def wait_row(j, c):
            pltpu.make_async_copy(zero_ref, xs_hbm.at[pl.ds(0, 1)], sem_zero).wait()
            return c

        lax.fori_loop(0, total - bulk, wait_row, 0)

        zblk_ref[...] = jnp.zeros_like(zblk_ref)

        def tail_start(b, c):
            tail_copy(b).start()
            return c

        lax.fori_loop(zoff_ref[2 * N_EXPERTS + 1], nblk, tail_start, 0)

    idx_copy(i, slot).wait()

    @pl.when(i + 1 < nsteps)
    def _():
        idx_copy(i + 1, 1 - slot).start()

    def body(t, carry):
        for s in range(2):
            d = idx_smem[slot, 2 * t + s]
            pltpu.make_async_copy(h2_ref.at[pl.ds(t, 1)], xs_hbm.at[pl.ds(d, 1)], sem_rows).start()
        return carry

    lax.fori_loop(0, th, body, 0, unroll=8)
    rows_wait()

    @pl.when(i == nsteps - 1)
    def _():
        def tail_wait(b, c):
            tail_copy(b).wait()
            return c

        lax.fori_loop(zoff_ref[2 * N_EXPERTS + 1], nblk, tail_wait, 0)


def _dispatch(dest, zero_off, h2, n_rows):
    n, d = h2.shape
    th = min(256, n)
    return pl.pallas_call(
        _dispatch_kernel,
        out_shape=jax.ShapeDtypeStruct((n_rows, d), h2.dtype),
        grid_spec=pltpu.PrefetchScalarGridSpec(
            num_scalar_prefetch=1,
            grid=(n // th,),
            in_specs=[pl.BlockSpec(memory_space=pl.ANY),
                      pl.BlockSpec((th, d), lambda i, zo: (i, 0))],
            out_specs=pl.BlockSpec(memory_space=pl.ANY),
            scratch_shapes=[pltpu.SMEM((2, 2 * th), jnp.int32),
                            pltpu.VMEM((1, d), h2.dtype),
                            pltpu.VMEM((ROUTE_BLOCK, d), h2.dtype),
                            pltpu.SemaphoreType.DMA((2,)), pltpu.SemaphoreType.DMA,
                            pltpu.SemaphoreType.DMA, pltpu.SemaphoreType.DMA]),
        compiler_params=pltpu.CompilerParams(dimension_semantics=("arbitrary",),
                                             vmem_limit_bytes=VMEM_LIMIT),
        name="dispatch",
    )(zero_off, dest.reshape(n // th, 2 * th), h2)


def _expert_kernel(be_ref, na_ref, xs_ref, wg_ref, wu_ref, wd_ref, ys_ref):
    del be_ref

    @pl.when(pl.program_id(0) < na_ref[0])
    def _():
        x = xs_ref[...].astype(BF16)
        g = _dot(x, wg_ref[0].astype(BF16))
        u = _dot(x, wu_ref[0].astype(BF16))
        hid = (g * _sigmoid(g) * u).astype(BF16)
        ys_ref[...] = _dot(hid, wd_ref[0].astype(BF16))

    @pl.when(pl.program_id(0) >= na_ref[0])
    def _():
        ys_ref[...] = jnp.zeros_like(ys_ref)


def _experts(block_e, n_active, xs, exp_gate, exp_up, exp_down):
    n_rows, d = xs.shape
    de = exp_gate.shape[2]
    nblk = n_rows // ROUTE_BLOCK
    last = lambda i, na: jnp.minimum(i, jnp.maximum(na[0] - 1, 0))
    blk_row = lambda i, be, na: (last(i, na), 0)
    blk_exp = lambda i, be, na: (be[last(i, na)], 0, 0)
    return pl.pallas_call(
        _expert_kernel,
        out_shape=jax.ShapeDtypeStruct((n_rows, d), F32),
        grid_spec=pltpu.PrefetchScalarGridSpec(
            num_scalar_prefetch=2,
            grid=(nblk,),
            in_specs=[pl.BlockSpec((ROUTE_BLOCK, d), blk_row),
                      pl.BlockSpec((1, d, de), blk_exp),
                      pl.BlockSpec((1, d, de), blk_exp),
                      pl.BlockSpec((1, de, d), blk_exp)],
            out_specs=pl.BlockSpec((ROUTE_BLOCK, d), lambda i, be, na: (i, 0))),
        compiler_params=pltpu.CompilerParams(dimension_semantics=("arbitrary",),
                                             vmem_limit_bytes=VMEM_LIMIT),
        name="experts",
    )(block_e, n_active, xs, exp_gate, exp_up, exp_down)


def _combine_kernel(dest_hbm, ys_hbm, x1_ref, meta_ref, mod_ref, fw_ref, o_ref,
                    idx_smem, buf_ref, sem_idx, sem_rows):
    i = pl.program_id(0)
    nsteps = pl.num_programs(0)
    tj = x1_ref.shape[0]
    slot = i % 2

    def gather(step, sl):
        cp = pltpu.make_async_copy(dest_hbm.at[step], idx_smem.at[sl], sem_idx)
        cp.start()
        cp.wait()

        def body(t, carry):
            for s in range(2):
                d = idx_smem[sl, 2 * t + s]
                pltpu.make_async_copy(ys_hbm.at[pl.ds(d, 1)], buf_ref.at[sl, s, pl.ds(t, 1)],
                                      sem_rows.at[sl]).start()
            return carry

        lax.fori_loop(0, tj, body, 0, unroll=8)

    @pl.when(i == 0)
    def _():
        gather(0, 0)

    @pl.when(i + 1 < nsteps)
    def _():
        gather(i + 1, 1 - slot)

    pltpu.make_async_copy(buf_ref.at[slot], buf_ref.at[slot], sem_rows.at[slot]).wait()

    meta = meta_ref[...]
    y = meta[:, 2:3] * buf_ref[slot, 0] + meta[:, 3:4] * buf_ref[slot, 1]
    x2 = x1_ref[...] + mod_ref[0, 5:6, :] * y
    ms = jnp.mean(x2 * x2, axis=-1, keepdims=True)
    o_ref[...] = x2 * lax.rsqrt(ms + RMS_EPS) * fw_ref[...]


def _combine(dest, ys, x1, meta, mod3, final_w, t_len):
    n, d = x1.shape
    tj = min(256, t_len)
    return pl.pallas_call(
        _combine_kernel,
        out_shape=jax.ShapeDtypeStruct((n, d), F32),
        grid=(n // tj,),
        in_specs=[pl.BlockSpec(memory_space=pl.ANY),
                  pl.BlockSpec(memory_space=pl.ANY),
                  pl.BlockSpec((tj, d), lambda i: (i, 0)),
                  pl.BlockSpec((tj, LANES), lambda i: (i, 0)),
                  pl.BlockSpec((1, 6, d), lambda i: (i * tj // t_len, 0, 0)),
                  pl.BlockSpec((1, d), lambda i: (0, 0))],
        out_specs=pl.BlockSpec((tj, d), lambda i: (i, 0)),
        scratch_shapes=[pltpu.SMEM((2, 2 * tj), jnp.int32), pltpu.VMEM((2, 2, tj, d), F32),
                        pltpu.SemaphoreType.DMA, pltpu.SemaphoreType.DMA((2,))],
        compiler_params=pltpu.CompilerParams(dimension_semantics=("arbitrary",),
                                             vmem_limit_bytes=VMEM_LIMIT),
        name="combine",
    )(dest.reshape(n // tj, 2 * tj), ys, x1, meta, mod3, final_w.reshape(1, d))


def _pad_cols(w, width):
    return jnp.pad(w, ((0, 0), (0, width - w.shape[1])))


def _pad_rows(w, height):
    return jnp.pad(w, ((0, height - w.shape[0]), (0, 0)))


def _layer(x2, c, bsz, t_len, w_ada, b_ada, norm1_w, w_in, mu_r, mu_k, mu_v, mu_w, mu_a, mu_g, w0,
           w_decay1, w_decay2, a0, w_aaa1, w_aaa2, w_gate1, w_gate2, k_k, k_a, r_k, lnx_w, lnx_b,
           conv_w, conv_norm_w, w_out, norm2_w, w_grp, b_grp, w_exp, b_exp, exp_gate, exp_up,
           exp_down, final_w):
    n, d = x2.shape
    dr = w0.shape[0]
    assert w_decay1.shape[1] <= 128 and w_aaa1.shape[1] <= 128 and w_gate1.shape[1] <= 256
    assert dr % GROUP_LANES == 0 and conv_w.shape[1] == dr and w_in.shape[1] == 6 * dr
    assert t_len % WKV_CHUNK == 0 and w_grp.shape[1] == N_ROUTE_GROUPS and w_exp.shape[1] == N_EXPERTS

    mod3 = _ada(c, w_ada, b_ada).reshape(bsz, 6, d)
    h, proj = _inproj(x2, mod3, norm1_w, w_in.astype(BF16), t_len)

    w1p = jnp.concatenate([_pad_cols(w_decay1, 128), _pad_cols(w_aaa1, 128), _pad_cols(w_gate1, 256)], axis=1)
    mup = jnp.concatenate([jnp.broadcast_to(mu_w[:, None], (d, 128)),
                           jnp.broadcast_to(mu_a[:, None], (d, 128)),
                           jnp.broadcast_to(mu_g[:, None], (d, 256))], axis=1)
    lw, ag, gg = _lora(h, w1p, mup, _pad_rows(w_decay2, 128), _pad_rows(w_aaa2, 128),
                       _pad_rows(w_gate2, 256), w0, a0, bsz, t_len)

    yr = _wkv(proj, lw, ag, gg, mu_r, mu_k, mu_v, k_k, k_a, r_k.reshape(dr), lnx_w, lnx_b, bsz, t_len)

    wr = _pad_cols(jnp.concatenate([w_grp, w_exp], axis=1), LANES)
    br = _pad_cols(jnp.concatenate([b_grp, b_exp]).reshape(1, -1), LANES)
    x1, h2, meta, cnt = _outproj(yr, proj, conv_w, conv_norm_w, w_out.astype(BF16), x2, mod3,
                                 norm2_w, wr, br, bsz, t_len)

    counts = cnt[0, :N_EXPERTS].astype(jnp.int32)
    padded = (counts + ROUTE_BLOCK - 1) // ROUTE_BLOCK * ROUTE_BLOCK
    pad_ends = jnp.cumsum(padded)
    pad_starts = pad_ends - padded
    top_e = meta[:, 0:2].astype(jnp.int32)
    rank = meta[:, 4:6].astype(jnp.int32)
    dest = (pad_starts[top_e] + rank).reshape(-1)
    nblk = (2 * n) // ROUTE_BLOCK + N_EXPERTS
    n_rows = nblk * ROUTE_BLOCK
    block_start = jnp.arange(nblk, dtype=jnp.int32) * ROUTE_BLOCK
    block_e = jnp.minimum(jnp.sum(pad_ends[None, :] <= block_start[:, None], axis=1),
                          N_EXPERTS - 1).astype(jnp.int32)
    n_active = (pad_ends[-1:] // ROUTE_BLOCK).astype(jnp.int32)

    pad_len = padded - counts
    zero_off = jnp.concatenate([pad_starts + counts, pad_len, jnp.sum(pad_len, keepdims=True),
                                n_active]).astype(jnp.int32)
    xs = _dispatch(dest, zero_off, h2, n_rows)
    ys = _experts(block_e, n_active, xs, exp_gate, exp_up, exp_down)
    return _combine(dest, ys, x1, meta, mod3, final_w, t_len)


def kernel(x, c, w_ada, b_ada, norm1_w, w_in, mu_r, mu_k, mu_v, mu_w, mu_a, mu_g, w0, w_decay1, w_decay2, a0, w_aaa1, w_aaa2, w_gate1, w_gate2, k_k, k_a, r_k, lnx_w, lnx_b, conv_w, conv_norm_w, w_out, norm2_w, w_grp, b_grp, w_exp, b_exp, exp_gate, exp_up, exp_down, final_w):
    bsz, t_len, d = x.shape
    assert w_ada.shape[0] == 1, "single-layer block"
    out = _layer(x.reshape(bsz * t_len, d), c, bsz, t_len, w_ada[0], b_ada[0], norm1_w[0], w_in[0],
                 mu_r[0], mu_k[0], mu_v[0], mu_w[0], mu_a[0], mu_g[0], w0[0], w_decay1[0], w_decay2[0],
                 a0[0], w_aaa1[0], w_aaa2[0], w_gate1[0], w_gate2[0], k_k[0], k_a[0], r_k[0], lnx_w[0],
                 lnx_b[0], conv_w[0], conv_norm_w[0], w_out[0], norm2_w[0], w_grp[0], b_grp[0],
                 w_exp[0], b_exp[0], exp_gate[0], exp_up[0], exp_down[0], final_w)
    return out.reshape(bsz, t_len, d)
```

```python
import functools

import jax
import jax.numpy as jnp
from jax import lax
from jax.experimental import pallas as pl
from jax.experimental.pallas import tpu as pltpu

F32 = jnp.float32
BF16 = jnp.bfloat16

HEAD_DIM = 64
WKV_CHUNK = 64
GROUP_LANES = 256
N_ROUTE_GROUPS = 4
EXPERTS_PER_GROUP = 8
N_EXPERTS = N_ROUTE_GROUPS * EXPERTS_PER_GROUP
CONV_GROUP_DIM = 64
RMS_EPS = 1e-6
LNX_EPS = 64e-5
LANES = 128
LORA_PAD = 512
ROUTE_BLOCK = 256
VMEM_LIMIT = 56 * 1024 * 1024


def _dot(a, b, precision=None):
    return jnp.dot(a, b, preferred_element_type=F32, precision=precision)


def _dot_nt(a, b):
    return lax.dot_general(a, b, (((1,), (1,)), ((), ())), preferred_element_type=F32)


def _dot_tn(a, b):
    return lax.dot_general(a, b, (((0,), (0,)), ((), ())), preferred_element_type=F32)


def _split_bf16(x):
    hi = x.astype(BF16)
    return hi, (x - hi.astype(F32)).astype(BF16)


def _dot_x3(a, b):
    ah, al = _split_bf16(a)
    bh, bl = _split_bf16(b)
    return _dot(ah, bh) + _dot(al, bh) + _dot(ah, bl)


def _pack_bf16_pair(x):
    c = x.shape[1] // 2
    lo = pltpu.bitcast(x[:, :c].astype(BF16).astype(F32), jnp.uint32)
    hi = pltpu.bitcast(x[:, c:].astype(BF16).astype(F32), jnp.uint32)
    return (lo >> 16) | (hi & jnp.uint32(0xFFFF0000))


def _unpack_bf16_pair(w):
    lo = pltpu.bitcast(w << 16, F32)
    hi = pltpu.bitcast(w & jnp.uint32(0xFFFF0000), F32)
    return jnp.concatenate([lo, hi], axis=1)


def _sigmoid(x):
    return 1.0 / (1.0 + jnp.exp(-x))


def _shift_rows(x, carry_row):
    rolled = pltpu.roll(x, 1, axis=0)
    row = lax.broadcasted_iota(jnp.int32, x.shape, 0)
    return jnp.where(row == 0, carry_row, rolled)


def _seg_ones(n, seg):
    r = lax.broadcasted_iota(jnp.int32, (n, n), 0) // seg
    c = lax.broadcasted_iota(jnp.int32, (n, n), 1) // seg
    return jnp.where(r == c, 1.0, 0.0).astype(BF16)


def _seg_sum(x, ones_bd):
    hi = x.astype(BF16)
    lo = (x - hi.astype(F32)).astype(BF16)
    return _dot(hi, ones_bd) + _dot(lo, ones_bd)


def _ada_kernel(cb_ref, w_ref, b_ref, o_ref):
    nb = cb_ref.shape[0]
    tn = w_ref.shape[1]
    for b in range(nb):
        cv = cb_ref[b]
        s = cv * _sigmoid(cv)
        cols = [jnp.sum(s * w_ref[:, j * LANES:(j + 1) * LANES], axis=0, keepdims=True)
                for j in range(tn // LANES)]
        o_ref[b:b + 1, :] = jnp.concatenate(cols, axis=1) + b_ref[...]


def _ada(c, w_ada, b_ada):
    nb, d = c.shape
    n_out = w_ada.shape[1]
    tn = 512
    cb = jnp.broadcast_to(c[:, :, None], (nb, d, LANES))
    return pl.pallas_call(
        _ada_kernel,
        out_shape=jax.ShapeDtypeStruct((nb, n_out), F32),
        grid=(n_out // tn,),
        in_specs=[pl.BlockSpec((nb, d, LANES), lambda j: (0, 0, 0)),
                  pl.BlockSpec((d, tn), lambda j: (0, j)),
                  pl.BlockSpec((1, tn), lambda j: (0, j))],
        out_specs=pl.BlockSpec((nb, tn), lambda j: (0, j)),
        compiler_params=pltpu.CompilerParams(dimension_semantics=("arbitrary",),
                                             vmem_limit_bytes=VMEM_LIMIT),
        name="ada",
    )(cb, w_ada, b_ada.reshape(1, n_out))


def _inproj_kernel(x_ref, mod_ref, nw_ref, w_ref, h_ref, p_ref, hs_ref):
    @pl.when(pl.program_id(1) == 0)
    def _():
        x = x_ref[...]
        ms = jnp.mean(x * x, axis=-1, keepdims=True)
        y = x * lax.rsqrt(ms + RMS_EPS) * nw_ref[...]
        h = y * (1.0 + mod_ref[0, 1:2, :]) + mod_ref[0, 0:1, :]
        hb = h.astype(BF16)
        hs_ref[...] = hb
        h_ref[...] = hb

    p_ref[...] = _dot(hs_ref[...], w_ref[...])


def _inproj(x2, mod3, norm_w, w_in_bf, t_len):
    n, d = x2.shape
    n_in = w_in_bf.shape[1]
    tm = min(1024, t_len)
    tn = 1024
    return pl.pallas_call(
        _inproj_kernel,
        out_shape=(jax.ShapeDtypeStruct((n, d), BF16), jax.ShapeDtypeStruct((n, n_in), F32)),
        grid=(n // tm, n_in // tn),
        in_specs=[pl.BlockSpec((tm, d), lambda i, j: (i, 0)),
                  pl.BlockSpec((1, 6, d), lambda i, j: (i * tm // t_len, 0, 0)),
                  pl.BlockSpec((1, d), lambda i, j: (0, 0)),
                  pl.BlockSpec((d, tn), lambda i, j: (0, j))],
        out_specs=(pl.BlockSpec((tm, d), lambda i, j: (i, 0)),
                   pl.BlockSpec((tm, tn), lambda i, j: (i, j))),
        scratch_shapes=[pltpu.VMEM((tm, d), BF16)],
        compiler_params=pltpu.CompilerParams(dimension_semantics=("arbitrary", "arbitrary"),
                                             vmem_limit_bytes=VMEM_LIMIT),
        name="inproj",
    )(x2, mod3, norm_w.reshape(1, d), w_in_bf)


def _lora_kernel(h_ref, w1_ref, mu_ref, wd2_ref, wa2_ref, wg2_ref, w0_ref, a0_ref,
                 lw_ref, ag_ref, gg_ref, carry_ref):
    @pl.when(pl.program_id(1) == 0)
    def _():
        carry_ref[...] = jnp.zeros_like(carry_ref)

    h = h_ref[...]
    w1 = w1_ref[...]
    p1 = _dot(h, w1.astype(BF16))
    p2 = _dot(h, (w1 * mu_ref[...]).astype(BF16))
    prev = _shift_rows(p2, carry_ref[...])
    carry_ref[...] = p2[p2.shape[0] - 1:, :]
    lo = p1 - p2 + prev
    dec = _dot_x3(jnp.tanh(lo[:, 0:128]), wd2_ref[...])
    z = -(w0_ref[...] + dec)
    softplus = jnp.maximum(z, 0.0) + jnp.log(1.0 + jnp.exp(-jnp.abs(z)))
    lw_ref[...] = -jnp.exp(-softplus - 0.5)
    ag_ref[...] = _sigmoid(a0_ref[...] + _dot_x3(lo[:, 128:256], wa2_ref[...]))
    gg_ref[...] = _dot_x3(_sigmoid(lo[:, 256:512]), wg2_ref[...])


def _lora(h, w1p, mup, wd2p, wa2p, wg2p, w0, a0, bsz, t_len):
    n, d = h.shape
    dr = w0.shape[-1]
    tm = min(512, t_len)
    nt = t_len // tm
    row = lambda b, t: (b * nt + t, 0)
    const = lambda b, t: (0, 0)
    out = jax.ShapeDtypeStruct((n, dr), F32)
    return pl.pallas_call(
        _lora_kernel,
        out_shape=(out, out, out),
        grid=(bsz, nt),
        in_specs=[pl.BlockSpec((tm, d), row),
                  pl.BlockSpec((d, LORA_PAD), const),
                  pl.BlockSpec((d, LORA_PAD), const),
                  pl.BlockSpec((128, dr), const),
                  pl.BlockSpec((128, dr), const),
                  pl.BlockSpec((256, dr), const),
                  pl.BlockSpec((1, dr), const),
                  pl.BlockSpec((1, dr), const)],
        out_specs=(pl.BlockSpec((tm, dr), row),) * 3,
        scratch_shapes=[pltpu.VMEM((1, LORA_PAD), F32)],
        compiler_params=pltpu.CompilerParams(dimension_semantics=("arbitrary", "arbitrary"),
                                             vmem_limit_bytes=VMEM_LIMIT),
        name="lora",
    )(h, w1p, mup, wd2p, wa2p, wg2p, w0.reshape(1, dr), a0.reshape(1, dr))


def _wkv_kernel(r_ref, k_ref, v_ref, lw_ref, ag_ref, gg_ref, mur_ref, muk_ref, muv_ref,
                kk_ref, ka_ref, rk_ref, lnw_ref, lnb_ref, y_ref, s_ref, carry_ref):
    lt = r_ref.shape[0]
    gl = GROUP_LANES
    c_len = WKV_CHUNK
    heads = gl // HEAD_DIM

    @pl.when(pl.program_id(2) == 0)
    def _():
        s_ref[...] = jnp.zeros_like(s_ref)
        carry_ref[...] = jnp.zeros_like(carry_ref)

    ones_bd = _seg_ones(gl, HEAD_DIM)

    def lerp(ref, mu_ref, idx):
        z = ref[...]
        prev = _shift_rows(z, carry_ref[idx:idx + 1, :])
        carry_ref[idx:idx + 1, :] = z[lt - 1:, :]
        return z + (prev - z) * mu_ref[...]

    r1 = lerp(r_ref, mur_ref, 0)
    k1 = lerp(k_ref, muk_ref, 1)
    v1 = lerp(v_ref, muv_ref, 2)
    ag = ag_ref[...]
    kk = k1 * kk_ref[...]
    kk = kk * lax.rsqrt(jnp.maximum(_seg_sum(kk * kk, ones_bd), 1e-24))
    k2 = k1 * (1.0 + (ag - 1.0) * ka_ref[...])
    a_all = -kk
    b_all = kk * ag
    lw = lw_ref[...]

    rows = lax.broadcasted_iota(jnp.int32, (gl, gl), 0)
    cols = lax.broadcasted_iota(jnp.int32, (gl, gl), 1)
    same_head = (rows // c_len) == (cols // HEAD_DIM)
    tri_strict = same_head & ((rows % c_len) > (cols % c_len))
    tri_incl = same_head & ((rows % c_len) >= (cols % c_len))
    eye = jnp.where(rows == cols, 1.0, 0.0).astype(F32)
    cr = lax.broadcasted_iota(jnp.int32, (c_len, c_len), 0)
    cc = lax.broadcasted_iota(jnp.int32, (c_len, c_len), 1)
    cum_incl = jnp.where(cr >= cc, 1.0, 0.0).astype(F32)

    def bd(x):
        xb = x.astype(BF16)
        return jnp.where(same_head, jnp.concatenate([xb] * heads, axis=0), jnp.zeros((), BF16))

    def fold(x):
        out = x[0:c_len]
        for hh in range(1, heads):
            out = out + x[hh * c_len:(hh + 1) * c_len]
        return out

    nch = lt // c_len
    chunks = range(nch)
    sls = [slice(c * c_len, (c + 1) * c_len) for c in chunks]
    lws = [lw[sl] for sl in sls]
    cum_b = cum_incl.astype(BF16)

    def cumsum3(x):
        hi = x.astype(BF16)
        r = x - hi.astype(F32)
        mid = r.astype(BF16)
        lo = (r - mid.astype(F32)).astype(BF16)
        return _dot(cum_b, hi) + _dot(cum_b, mid) + _dot(cum_b, lo)

    cs = [cumsum3(x) for x in lws]
    e_cs = [jnp.exp(x) for x in cs]
    e_neg = [jnp.exp(-x) for x in cs]
    wc = [x[c_len - 1:, :] for x in e_cs]
    at = [a_all[sls[c]] * jnp.exp(cs[c] - lws[c]) for c in chunks]
    rt = [r1[sls[c]] * e_cs[c] for c in chunks]
    bt = [b_all[sls[c]] * e_neg[c] for c in chunks]
    kt = [k2[sls[c]] * e_neg[c] for c in chunks]
    vc = [v1[sl] for sl in sls]

    bd_a = [bd(x) for x in at]
    lhs = [jnp.concatenate([bd_a[c], bd(rt[c])], axis=0) for c in chunks]
    g_b = [_dot_nt(lhs[c], bd(bt[c])) for c in chunks]
    g_k = [_dot_nt(lhs[c], bd(kt[c])) for c in chunks]
    n_ab = [jnp.where(tri_strict, g[:gl], 0.0) for g in g_b]
    a_ak = [jnp.where(tri_strict, g[:gl], 0.0).astype(BF16) for g in g_k]
    a_rb = [jnp.where(tri_incl, g[gl:], 0.0).astype(BF16) for g in g_b]
    a_rk = [jnp.where(tri_incl, g[gl:], 0.0).astype(BF16) for g in g_k]

    rm = rows % c_len
    cm = cols % c_len
    p = [eye + jnp.where(same_head & ((rm // 2) == (cm // 2)), x, 0.0) for x in n_ab]
    for half in (2, 4, 8, 16, 32):
        lower_left = (same_head & ((rm // (2 * half)) == (cm // (2 * half)))
                      & ((rm % (2 * half)) >= half) & ((cm % (2 * half)) < half))
        pb = [x.astype(BF16) for x in p]
        n21_ta = [_dot(jnp.where(lower_left, n_ab[c], 0.0).astype(BF16), pb[c]) for c in chunks]
        p = [p[c] + _dot(pb[c], n21_ta[c].astype(BF16)) for c in chunks]
    tb = [x.astype(BF16) for x in p]

    z = [_dot(tb[c], jnp.concatenate([a_ak[c], bd_a[c]], axis=1)) for c in chunks]
    ap_bd = [x[:, gl:].astype(BF16) for x in z]
    bd_v = [bd(x) for x in vc]
    u0_bd = [_dot(z[c][:, :gl].astype(BF16), bd_v[c]).astype(BF16) for c in chunks]
    w = [_dot(a_rb[c], jnp.concatenate([ap_bd[c], u0_bd[c]], axis=1)) for c in chunks]
    y0 = [fold(w[c][:, gl:] + _dot(a_rk[c], bd_v[c])) for c in chunks]
    ap = [fold(x[:, gl:]) for x in z]
    u0 = [fold(x.astype(F32)) for x in u0_bd]
    rp = [rt[c] + fold(w[c][:, :gl]) for c in chunks]
    bh = [(bt[c] * wc[c]).astype(BF16) for c in chunks]
    bk = [jnp.concatenate([bh[c], (kt[c] * wc[c]).astype(BF16)], axis=0) for c in chunks]
    g_s = [jnp.where(same_head, _dot_tn(ap[c].astype(BF16), bh[c]), 0.0).astype(BF16) for c in chunks]
    n_s = [jnp.where(same_head, _dot_tn(jnp.concatenate([u0[c], vc[c]], axis=0).astype(BF16), bk[c]), 0.0)
           for c in chunks]

    s = s_ref[...]
    s_in = []
    for c in chunks:
        s_in.append(s.astype(BF16))
        s = s * wc[c] + _dot(s_in[c], g_s[c]) + n_s[c]
    s_ref[...] = s
    y_chunks = [_dot_nt(rp[c].astype(BF16), s_in[c]) + y0[c] for c in chunks]

    y = jnp.concatenate(y_chunks, axis=0) if len(y_chunks) > 1 else y_chunks[0]
    inv_n = 1.0 / HEAD_DIM
    mean = _seg_sum(y, ones_bd) * inv_n
    yc = y - mean
    var = _seg_sum(yc * yc, ones_bd) * inv_n
    yn = yc * lax.rsqrt(var + LNX_EPS) * lnw_ref[...] + lnb_ref[...]
    bonus = _seg_sum(r1 * k2 * rk_ref[...], ones_bd) * v1
    y_ref[...] = ((yn + bonus) * gg_ref[...]).astype(y_ref.dtype)


def _wkv(proj, lw, ag, gg, mu_r, mu_k, mu_v, k_k, k_a, r_k, lnx_w, lnx_b, bsz, t_len):
    n = proj.shape[0]
    dr = lw.shape[1]
    gl = GROUP_LANES
    ng = dr // gl
    lt = min(512, t_len)
    nt = t_len // lt
    row = lambda off: (lambda b, g, t: (b * nt + t, off + g))
    par = lambda b, g, t: (0, g)
    vec = lambda a: a.reshape(1, dr)
    return pl.pallas_call(
        _wkv_kernel,
        out_shape=jax.ShapeDtypeStruct((n, dr), BF16),
        grid=(bsz, ng, nt),
        in_specs=[pl.BlockSpec((lt, gl), row(0)),
                  pl.BlockSpec((lt, gl), row(ng)),
                  pl.BlockSpec((lt, gl), row(2 * ng)),
                  pl.BlockSpec((lt, gl), row(0)),
                  pl.BlockSpec((lt, gl), row(0)),
                  pl.BlockSpec((lt, gl), row(0))] + [pl.BlockSpec((1, gl), par)] * 8,
        out_specs=pl.BlockSpec((lt, gl), row(0)),
        scratch_shapes=[pltpu.VMEM((gl, gl), F32), pltpu.VMEM((8, gl), F32)],
        compiler_params=pltpu.CompilerParams(
            dimension_semantics=("arbitrary", "arbitrary", "arbitrary"),
            vmem_limit_bytes=VMEM_LIMIT),
        name="wkv",
    )(proj, proj, proj, lw, ag, gg, vec(mu_r), vec(mu_k), vec(mu_v), vec(k_k), vec(k_a),
      vec(r_k), vec(lnx_w), vec(lnx_b))


def _outproj_kernel(yr_ref, u_ref, gb_ref, gc_ref, cw_ref, cnw_ref, wo_ref, x_ref, mod_ref,
                    n2w_ref, wr_ref, br_ref, x1_ref, h2_ref, meta_ref, cnt_ref,
                    ccarry_ref, cnt_acc_ref):
    tm = x_ref.shape[0]
    dc = u_ref.shape[1]

    @pl.when(pl.program_id(1) == 0)
    def _():
        ccarry_ref[...] = jnp.zeros_like(ccarry_ref)

    @pl.when((pl.program_id(0) == 0) & (pl.program_id(1) == 0))
    def _():
        cnt_acc_ref[...] = jnp.zeros_like(cnt_acc_ref)

    pre = gc_ref[...] * u_ref[...]
    prev1 = _shift_rows(pre, ccarry_ref[0:1, :])
    prev2 = _shift_rows(prev1, ccarry_ref[1:2, :])
    ccarry_ref[0:1, :] = pre[tm - 1:, :]
    ccarry_ref[1:2, :] = pre[tm - 2:tm - 1, :]
    yc = gb_ref[...] * (prev2 * cw_ref[0:1, :] + prev1 * cw_ref[1:2, :] + pre * cw_ref[2:3, :])
    ones_bd = _seg_ones(GROUP_LANES, CONV_GROUP_DIM)
    ycs = []
    for s in range(dc // GROUP_LANES):
        ys = yc[:, s * GROUP_LANES:(s + 1) * GROUP_LANES]
        ms = _seg_sum(ys * ys, ones_bd) * (1.0 / CONV_GROUP_DIM)
        ycs.append(ys * lax.rsqrt(ms + RMS_EPS))
    y_conv = (jnp.concatenate(ycs, axis=1) * cnw_ref[...]).astype(BF16)

    dr = yr_ref.shape[1]
    y_mix = _dot(yr_ref[...], wo_ref[0:dr, :]) + _dot(y_conv, wo_ref[dr:, :])
    x1 = x_ref[...] + mod_ref[0, 2:3, :] * y_mix
    x1_ref[...] = x1
    ms = jnp.mean(x1 * x1, axis=-1, keepdims=True)
    h2 = x1 * lax.rsqrt(ms + RMS_EPS) * n2w_ref[...]
    h2 = h2 * (1.0 + mod_ref[0, 4:5, :]) + mod_ref[0, 3:4, :]
    h2_ref[...] = _pack_bf16_pair(h2)

    logits = _dot_x3(h2, wr_ref[...]) + br_ref[...]
    lane = lax.broadcasted_iota(jnp.int32, logits.shape, 1)
    neg = jnp.float32(-jnp.inf)
    big = jnp.int32(1 << 20)
    glog = jnp.where(lane < N_ROUTE_GROUPS, logits, neg)
    gmax = jnp.max(glog, axis=-1, keepdims=True)
    grp = jnp.min(jnp.where(glog == gmax, lane, big), axis=-1, keepdims=True)
    p_grp = 1.0 / jnp.sum(jnp.exp(glog - gmax), axis=-1, keepdims=True)
    e_lane = lane - N_ROUTE_GROUPS
    in_grp = (e_lane >= grp * EXPERTS_PER_GROUP) & (e_lane < (grp + 1) * EXPERTS_PER_GROUP)
    elog = jnp.where(in_grp, logits, neg)
    m1 = jnp.max(elog, axis=-1, keepdims=True)
    i1 = jnp.min(jnp.where(elog == m1, lane, big), axis=-1, keepdims=True)
    elog2 = jnp.where(lane == i1, neg, elog)
    m2 = jnp.max(elog2, axis=-1, keepdims=True)
    i2 = jnp.min(jnp.where(elog2 == m2, lane, big), axis=-1, keepdims=True)
    t2 = jnp.exp(m2 - m1)
    w1 = p_grp / (1.0 + t2)
    w2 = p_grp * t2 / (1.0 + t2)
    e1 = i1 - N_ROUTE_GROUPS
    e2 = i2 - N_ROUTE_GROUPS

    oh1 = lane == e1
    oh2 = lane == e2
    oh = jnp.where(oh1 | oh2, 1.0, 0.0)
    tr = lax.broadcasted_iota(jnp.int32, (tm, tm), 0)
    tc = lax.broadcasted_iota(jnp.int32, (tm, tm), 1)
    lower = jnp.where(tr > tc, 1.0, 0.0).astype(BF16)
    before = _dot(lower, oh.astype(BF16)) + cnt_acc_ref[...]
    rank1 = jnp.sum(jnp.where(oh1, before, 0.0), axis=-1, keepdims=True)
    rank2 = jnp.sum(jnp.where(oh2, before, 0.0), axis=-1, keepdims=True)
    cnt = cnt_acc_ref[...] + jnp.sum(oh, axis=0, keepdims=True)
    cnt_acc_ref[...] = cnt
    cnt_ref[...] = jnp.broadcast_to(cnt, cnt_ref.shape)

    meta = jnp.where(lane == 0, e1.astype(F32), 0.0)
    meta = jnp.where(lane == 1, e2.astype(F32), meta)
    meta = jnp.where(lane == 2, w1, meta)
    meta = jnp.where(lane == 3, w2, meta)
    meta = jnp.where(lane == 4, rank1, meta)
    meta = jnp.where(lane == 5, rank2, meta)
    meta_ref[...] = meta


def _outproj(yr, proj, conv_w, conv_norm_w, w_out_bf, x2, mod3, norm2_w, wr, br, bsz, t_len):
    n, d = x2.shape
    dr = yr.shape[1]
    dc = conv_w.shape[1]
    tm = min(512, t_len)
    nt = t_len // tm
    nseg = dr // dc
    row = lambda off: (lambda b, t: (b * nt + t, off))
    const = lambda b, t: (0, 0)
    return pl.pallas_call(
        _outproj_kernel,
        out_shape=(jax.ShapeDtypeStruct((n, d), F32), jax.ShapeDtypeStruct((n, d // 2), jnp.uint32),
                   jax.ShapeDtypeStruct((n, LANES), F32), jax.ShapeDtypeStruct((8, LANES), F32)),
        grid=(bsz, nt),
        in_specs=[pl.BlockSpec((tm, dr), row(0)),
                  pl.BlockSpec((tm, dc), row(3 * nseg)),
                  pl.BlockSpec((tm, dc), row(4 * nseg)),
                  pl.BlockSpec((tm, dc), row(5 * nseg)),
                  pl.BlockSpec((3, dc), const),
                  pl.BlockSpec((1, dc), const),
                  pl.BlockSpec((dr + dc, d), const),
                  pl.BlockSpec((tm, d), row(0)),
                  pl.BlockSpec((1, 6, d), lambda b, t: (b, 0, 0)),
                  pl.BlockSpec((1, d), const),
                  pl.BlockSpec((d, LANES), const),
                  pl.BlockSpec((1, LANES), const)],
        out_specs=(pl.BlockSpec((tm, d), row(0)), pl.BlockSpec((tm, d // 2), row(0)),
                   pl.BlockSpec((tm, LANES), row(0)), pl.BlockSpec((8, LANES), const)),
        scratch_shapes=[pltpu.VMEM((8, dc), F32), pltpu.VMEM((1, LANES), F32)],
        compiler_params=pltpu.CompilerParams(dimension_semantics=("arbitrary", "arbitrary"),
                                             vmem_limit_bytes=VMEM_LIMIT),
        name="outproj",
    )(yr, proj, proj, proj, conv_w, conv_norm_w.reshape(1, dc), w_out_bf, x2, mod3,
      norm2_w.reshape(1, d), wr, br)


def _dispatch_kernel(zoff_ref, dest_hbm, h2_ref, xs_hbm, idx_smem, zero_ref, zblk_ref,
                     sem_idx, sem_rows, sem_zero, sem_tail):
    i = pl.program_id(0)
    nsteps = pl.num_programs(0)
    th = idx_smem.shape[1] // 2
    slot = i % 2

    def idx_copy(step, sl):
        return pltpu.make_async_copy(dest_hbm.at[step], idx_smem.at[sl], sem_idx.at[sl])

    def rows_wait():
        pltpu.make_async_copy(xs_hbm.at[pl.ds(0, 2 * th)], xs_hbm.at[pl.ds(0, 2 * th)], sem_rows).wait()

    nblk = xs_hbm.shape[0] // ROUTE_BLOCK

    def tail_copy(b):
        start = pl.multiple_of(b * ROUTE_BLOCK, ROUTE_BLOCK)
        return pltpu.make_async_copy(zblk_ref, xs_hbm.at[pl.ds(start, ROUTE_BLOCK)], sem_tail)

    @pl.when(i == 0)
    def _():
        idx_copy(0, 0).start()
        zero_ref[...] = jnp.zeros_like(zero_ref)

        def zero_expert(e, carry):
            off = zoff_ref[e]

            def zero_row(j, c):
                pltpu.make_async_copy(zero_ref, xs_hbm.at[pl.ds(off + j, 1)], sem_zero).start()
                return c

            return lax.fori_loop(0, zoff_ref[N_EXPERTS + e], zero_row, carry)

        lax.fori_loop(0, N_EXPERTS, zero_expert, 0)
        total = zoff_ref[2 * N_EXPERTS]
        bulk = pl.multiple_of((total // 8) * 8, 8)

        @pl.when(bulk > 0)
        def _():
            pltpu.make_async_copy(xs_hbm.at[pl.ds(0, bulk)], xs_hbm.at[pl.ds(0, bulk)], sem_zero).wait()

        def wait_row(j, c):
            pltpu.make_async_copy(zero_ref, xs_hbm.at[pl.ds(0, 1)], sem_zero).wait()
            return c

        lax.fori_loop(0, total - bulk, wait_row, 0)

        zblk_ref[...] = jnp.zeros_like(zblk_ref)

        def tail_start(b, c):
            tail_copy(b).start()
            return c

        lax.fori_loop(zoff_ref[2 * N_EXPERTS + 1], nblk, tail_start, 0)

    idx_copy(i, slot).wait()

    @pl.when(i + 1 < nsteps)
    def _():
        idx_copy(i + 1, 1 - slot).start()

    def body(t, carry):
        for s in range(2):
            d = idx_smem[slot, 2 * t + s]
            pltpu.make_async_copy(h2_ref.at[pl.ds(t, 1)], xs_hbm.at[pl.ds(d, 1)], sem_rows).start()
        return carry

    lax.fori_loop(0, th, body, 0, unroll=8)
    rows_wait()

    @pl.when(i == nsteps - 1)
    def _():
        def tail_wait(b, c):
            tail_copy(b).wait()
            return c

        lax.fori_loop(zoff_ref[2 * N_EXPERTS + 1], nblk, tail_wait, 0)


def _dispatch(dest, zero_off, h2, n_rows):
    n, d = h2.shape
    th = min(256, n)
    return pl.pallas_call(
        _dispatch_kernel,
        out_shape=jax.ShapeDtypeStruct((n_rows, d), h2.dtype),
        grid_spec=pltpu.PrefetchScalarGridSpec(
            num_scalar_prefetch=1,
            grid=(n // th,),
            in_specs=[pl.BlockSpec(memory_space=pl.ANY),
                      pl.BlockSpec((th, d), lambda i, zo: (i, 0))],
            out_specs=pl.BlockSpec(memory_space=pl.ANY),
            scratch_shapes=[pltpu.SMEM((2, 2 * th), jnp.int32),
                            pltpu.VMEM((1, d), h2.dtype),
                            pltpu.VMEM((ROUTE_BLOCK, d), h2.dtype),
                            pltpu.SemaphoreType.DMA((2,)), pltpu.SemaphoreType.DMA,
                            pltpu.SemaphoreType.DMA, pltpu.SemaphoreType.DMA]),
        compiler_params=pltpu.CompilerParams(dimension_semantics=("arbitrary",),
                                             vmem_limit_bytes=VMEM_LIMIT),
        name="dispatch",
    )(zero_off, dest.reshape(n // th, 2 * th), h2)


def _expert_kernel(be_ref, na_ref, xs_ref, wg_ref, wu_ref, wd_ref, ys_ref):
    del be_ref

    @pl.when(pl.program_id(0) < na_ref[0])
    def _():
        x = _unpack_bf16_pair(xs_ref[...]).astype(BF16)
        g = _dot(x, wg_ref[0].astype(BF16))
        u = _dot(x, wu_ref[0].astype(BF16))
        hid = (g * _sigmoid(g) * u).astype(BF16)
        ys_ref[...] = _pack_bf16_pair(_dot(hid, wd_ref[0].astype(BF16)))

    @pl.when(pl.program_id(0) >= na_ref[0])
    def _():
        ys_ref[...] = jnp.zeros_like(ys_ref)


def _experts(block_e, n_active, xs, exp_gate, exp_up, exp_down):
    n_rows, dp = xs.shape
    _, d, de = exp_gate.shape
    nblk = n_rows // ROUTE_BLOCK
    last = lambda i, na: jnp.minimum(i, jnp.maximum(na[0] - 1, 0))
    blk_row = lambda i, be, na: (last(i, na), 0)
    blk_exp = lambda i, be, na: (be[last(i, na)], 0, 0)
    return pl.pallas_call(
        _expert_kernel,
        out_shape=jax.ShapeDtypeStruct((n_rows, dp), jnp.uint32),
        grid_spec=pltpu.PrefetchScalarGridSpec(
            num_scalar_prefetch=2,
            grid=(nblk,),
            in_specs=[pl.BlockSpec((ROUTE_BLOCK, dp), blk_row),
                      pl.BlockSpec((1, d, de), blk_exp),
                      pl.BlockSpec((1, d, de), blk_exp),
                      pl.BlockSpec((1, de, d), blk_exp)],
            out_specs=pl.BlockSpec((ROUTE_BLOCK, dp), lambda i, be, na: (i, 0))),
        compiler_params=pltpu.CompilerParams(dimension_semantics=("arbitrary",),
                                             vmem_limit_bytes=VMEM_LIMIT),
        name="experts",
    )(block_e, n_active, xs, exp_gate, exp_up, exp_down)


def _combine_kernel(dest_hbm, ys_hbm, x1_ref, meta_ref, mod_ref, fw_ref, o_ref,
                    idx_smem, buf_ref, sem_idx, sem_rows):
    i = pl.program_id(0)
    nsteps = pl.num_programs(0)
    tj = x1_ref.shape[0]
    slot = i % 2

    def gather(step, sl):
        cp = pltpu.make_async_copy(dest_hbm.at[step], idx_smem.at[sl], sem_idx)
        cp.start()
        cp.wait()

        def body(t, carry):
            for s in range(2):
                d = idx_smem[sl, 2 * t + s]
                pltpu.make_async_copy(ys_hbm.at[pl.ds(d, 1)], buf_ref.at[sl, s, pl.ds(t, 1)],
                                      sem_rows.at[sl]).start()
            return carry

        lax.fori_loop(0, tj, body, 0, unroll=8)

    @pl.when(i == 0)
    def _():
        gather(0, 0)

    @pl.when(i + 1 < nsteps)
    def _():
        gather(i + 1, 1 - slot)

    pltpu.make_async_copy(buf_ref.at[slot], buf_ref.at[slot], sem_rows.at[slot]).wait()

    meta = meta_ref[...]
    y = (meta[:, 2:3] * _unpack_bf16_pair(buf_ref[slot, 0])
         + meta[:, 3:4] * _unpack_bf16_pair(buf_ref[slot, 1]))
    x2 = x1_ref[...] + mod_ref[0, 5:6, :] * y
    ms = jnp.mean(x2 * x2, axis=-1, keepdims=True)
    o_ref[...] = x2 * lax.rsqrt(ms + RMS_EPS) * fw_ref[...]


def _combine(dest, ys, x1, meta, mod3, final_w, t_len):
    n, d = x1.shape
    tj = min(256, t_len)
    return pl.pallas_call(
        _combine_kernel,
        out_shape=jax.ShapeDtypeStruct((n, d), F32),
        grid=(n // tj,),
        in_specs=[pl.BlockSpec(memory_space=pl.ANY),
                  pl.BlockSpec(memory_space=pl.ANY),
                  pl.BlockSpec((tj, d), lambda i: (i, 0)),
                  pl.BlockSpec((tj, LANES), lambda i: (i, 0)),
                  pl.BlockSpec((1, 6, d), lambda i: (i * tj // t_len, 0, 0)),
                  pl.BlockSpec((1, d), lambda i: (0, 0))],
        out_specs=pl.BlockSpec((tj, d), lambda i: (i, 0)),
        scratch_shapes=[pltpu.SMEM((2, 2 * tj), jnp.int32), pltpu.VMEM((2, 2, tj, d // 2), jnp.uint32),
                        pltpu.SemaphoreType.DMA, pltpu.SemaphoreType.DMA((2,))],
        compiler_params=pltpu.CompilerParams(dimension_semantics=("arbitrary",),
                                             vmem_limit_bytes=VMEM_LIMIT),
        name="combine",
    )(dest.reshape(n // tj, 2 * tj), ys, x1, meta, mod3, final_w.reshape(1, d))


def _pad_cols(w, width):
    return jnp.pad(w, ((0, 0), (0, width - w.shape[1])))


def _pad_rows(w, height):
    return jnp.pad(w, ((0, height - w.shape[0]), (0, 0)))


def _layer(x2, c, bsz, t_len, w_ada, b_ada, norm1_w, w_in, mu_r, mu_k, mu_v, mu_w, mu_a, mu_g, w0,
           w_decay1, w_decay2, a0, w_aaa1, w_aaa2, w_gate1, w_gate2, k_k, k_a, r_k, lnx_w, lnx_b,
           conv_w, conv_norm_w, w_out, norm2_w, w_grp, b_grp, w_exp, b_exp, exp_gate, exp_up,
           exp_down, final_w):
    n, d = x2.shape
    dr = w0.shape[0]
    assert w_decay1.shape[1] <= 128 and w_aaa1.shape[1] <= 128 and w_gate1.shape[1] <= 256
    assert dr % GROUP_LANES == 0 and conv_w.shape[1] == dr and w_in.shape[1] == 6 * dr
    assert t_len % WKV_CHUNK == 0 and w_grp.shape[1] == N_ROUTE_GROUPS and w_exp.shape[1] == N_EXPERTS

    mod3 = _ada(c, w_ada, b_ada).reshape(bsz, 6, d)
    h, proj = _inproj(x2, mod3, norm1_w, w_in.astype(BF16), t_len)

    w1p = jnp.concatenate([_pad_cols(w_decay1, 128), _pad_cols(w_aaa1, 128), _pad_cols(w_gate1, 256)], axis=1)
    mup = jnp.concatenate([jnp.broadcast_to(mu_w[:, None], (d, 128)),
                           jnp.broadcast_to(mu_a[:, None], (d, 128)),
                           jnp.broadcast_to(mu_g[:, None], (d, 256))], axis=1)
    lw, ag, gg = _lora(h, w1p, mup, _pad_rows(w_decay2, 128), _pad_rows(w_aaa2, 128),
                       _pad_rows(w_gate2, 256), w0, a0, bsz, t_len)

    yr = _wkv(proj, lw, ag, gg, mu_r, mu_k, mu_v, k_k, k_a, r_k.reshape(dr), lnx_w, lnx_b, bsz, t_len)

    wr = _pad_cols(jnp.concatenate([w_grp, w_exp], axis=1), LANES)
    br = _pad_cols(jnp.concatenate([b_grp, b_exp]).reshape(1, -1), LANES)
    x1, h2, meta, cnt = _outproj(yr, proj, conv_w, conv_norm_w, w_out.astype(BF16), x2, mod3,
                                 norm2_w, wr, br, bsz, t_len)

    counts = cnt[0, :N_EXPERTS].astype(jnp.int32)
    padded = (counts + ROUTE_BLOCK - 1) // ROUTE_BLOCK * ROUTE_BLOCK
    pad_ends = jnp.cumsum(padded)
    pad_starts = pad_ends - padded
    top_e = meta[:, 0:2].astype(jnp.int32)
    rank = meta[:, 4:6].astype(jnp.int32)
    dest = (pad_starts[top_e] + rank).reshape(-1)
    nblk = (2 * n) // ROUTE_BLOCK + N_EXPERTS
    n_rows = nblk * ROUTE_BLOCK
    block_start = jnp.arange(nblk, dtype=jnp.int32) * ROUTE_BLOCK
    block_e = jnp.minimum(jnp.sum(pad_ends[None, :] <= block_start[:, None], axis=1),
                          N_EXPERTS - 1).astype(jnp.int32)
    n_active = (pad_ends[-1:] // ROUTE_BLOCK).astype(jnp.int32)

    pad_len = padded - counts
    zero_off = jnp.concatenate([pad_starts + counts, pad_len, jnp.sum(pad_len, keepdims=True),
                                n_active]).astype(jnp.int32)
    xs = _dispatch(dest, zero_off, h2, n_rows)
    ys = _experts(block_e, n_active, xs, exp_gate, exp_up, exp_down)
    return _combine(dest, ys, x1, meta, mod3, final_w, t_len)


def kernel(x, c, w_ada, b_ada, norm1_w, w_in, mu_r, mu_k, mu_v, mu_w, mu_a, mu_g, w0, w_decay1, w_decay2, a0, w_aaa1, w_aaa2, w_gate1, w_gate2, k_k, k_a, r_k, lnx_w, lnx_b, conv_w, conv_norm_w, w_out, norm2_w, w_grp, b_grp, w_exp, b_exp, exp_gate, exp_up, exp_down, final_w):
    bsz, t_len, d = x.shape
    assert w_ada.shape[0] == 1, "single-layer block"
    out = _layer(x.reshape(bsz * t_len, d), c, bsz, t_len, w_ada[0], b_ada[0], norm1_w[0], w_in[0],
                 mu_r[0], mu_k[0], mu_v[0], mu_w[0], mu_a[0], mu_g[0], w0[0], w_decay1[0], w_decay2[0],
                 a0[0], w_aaa1[0], w_aaa2[0], w_gate1[0], w_gate2[0], k_k[0], k_a[0], r_k[0], lnx_w[0],
                 lnx_b[0], conv_w[0], conv_norm_w[0], w_out[0], norm2_w[0], w_grp[0], b_grp[0],
                 w_exp[0], b_exp[0], exp_gate[0], exp_up[0], exp_down[0], final_w)
    return out.reshape(bsz, t_len, d)
```

```python
import functools

import jax
import jax.numpy as jnp
from jax import lax
from jax.experimental import pallas as pl
from jax.experimental.pallas import tpu as pltpu

F32 = jnp.float32
BF16 = jnp.bfloat16

HEAD_DIM = 64
WKV_CHUNK = 64
GROUP_LANES = 256
N_ROUTE_GROUPS = 4
EXPERTS_PER_GROUP = 8
N_EXPERTS = N_ROUTE_GROUPS * EXPERTS_PER_GROUP
CONV_GROUP_DIM = 64
RMS_EPS = 1e-6
LNX_EPS = 64e-5
LANES = 128
LORA_PAD = 512
ROUTE_BLOCK = 256
VMEM_LIMIT = 56 * 1024 * 1024


def _dot(a, b, precision=None):
    return jnp.dot(a, b, preferred_element_type=F32, precision=precision)


def _dot_nt(a, b):
    return lax.dot_general(a, b, (((1,), (1,)), ((), ())), preferred_element_type=F32)


def _dot_tn(a, b):
    return lax.dot_general(a, b, (((0,), (0,)), ((), ())), preferred_element_type=F32)


def _split_bf16(x):
    hi = x.astype(BF16)
    return hi, (x - hi.astype(F32)).astype(BF16)


def _dot_x3(a, b):
    ah, al = _split_bf16(a)
    bh, bl = _split_bf16(b)
    return _dot(ah, bh) + _dot(al, bh) + _dot(ah, bl)


def _pack_bf16_pair(x):
    c = x.shape[1] // 2
    lo = pltpu.bitcast(x[:, :c].astype(BF16).astype(F32), jnp.uint32)
    hi = pltpu.bitcast(x[:, c:].astype(BF16).astype(F32), jnp.uint32)
    return (lo >> 16) | (hi & jnp.uint32(0xFFFF0000))


def _unpack_bf16_pair(w):
    lo = pltpu.bitcast(w << 16, F32)
    hi = pltpu.bitcast(w & jnp.uint32(0xFFFF0000), F32)
    return jnp.concatenate([lo, hi], axis=1)


def _sigmoid(x):
    return 1.0 / (1.0 + jnp.exp(-x))


def _shift_rows(x, carry_row):
    rolled = pltpu.roll(x, 1, axis=0)
    row = lax.broadcasted_iota(jnp.int32, x.shape, 0)
    return jnp.where(row == 0, carry_row, rolled)


def _seg_ones(n, seg):
    r = lax.broadcasted_iota(jnp.int32, (n, n), 0) // seg
    c = lax.broadcasted_iota(jnp.int32, (n, n), 1) // seg
    return jnp.where(r == c, 1.0, 0.0).astype(BF16)


def _seg_sum(x, ones_bd):
    hi = x.astype(BF16)
    lo = (x - hi.astype(F32)).astype(BF16)
    return _dot(hi, ones_bd) + _dot(lo, ones_bd)


def _ada_kernel(cb_ref, w_ref, b_ref, o_ref):
    nb = cb_ref.shape[0]
    tn = w_ref.shape[1]
    for b in range(nb):
        cv = cb_ref[b]
        s = cv * _sigmoid(cv)
        cols = [jnp.sum(s * w_ref[:, j * LANES:(j + 1) * LANES], axis=0, keepdims=True)
                for j in range(tn // LANES)]
        o_ref[b:b + 1, :] = jnp.concatenate(cols, axis=1) + b_ref[...]


def _ada(c, w_ada, b_ada):
    nb, d = c.shape
    n_out = w_ada.shape[1]
    tn = 512
    cb = jnp.broadcast_to(c[:, :, None], (nb, d, LANES))
    return pl.pallas_call(
        _ada_kernel,
        out_shape=jax.ShapeDtypeStruct((nb, n_out), F32),
        grid=(n_out // tn,),
        in_specs=[pl.BlockSpec((nb, d, LANES), lambda j: (0, 0, 0)),
                  pl.BlockSpec((d, tn), lambda j: (0, j)),
                  pl.BlockSpec((1, tn), lambda j: (0, j))],
        out_specs=pl.BlockSpec((nb, tn), lambda j: (0, j)),
        compiler_params=pltpu.CompilerParams(dimension_semantics=("arbitrary",),
                                             vmem_limit_bytes=VMEM_LIMIT),
        name="ada",
    )(cb, w_ada, b_ada.reshape(1, n_out))


def _inproj_kernel(x_ref, mod_ref, nw_ref, w_ref, h_ref, p_ref, hs_ref):
    @pl.when(pl.program_id(1) == 0)
    def _():
        x = x_ref[...]
        ms = jnp.mean(x * x, axis=-1, keepdims=True)
        y = x * lax.rsqrt(ms + RMS_EPS) * nw_ref[...]
        h = y * (1.0 + mod_ref[0, 1:2, :]) + mod_ref[0, 0:1, :]
        hb = h.astype(BF16)
        hs_ref[...] = hb
        h_ref[...] = hb

    p_ref[...] = _dot(hs_ref[...], w_ref[...])


def _inproj(x2, mod3, norm_w, w_in_bf, t_len):
    n, d = x2.shape
    n_in = w_in_bf.shape[1]
    tm = min(1024, t_len)
    tn = 1024
    return pl.pallas_call(
        _inproj_kernel,
        out_shape=(jax.ShapeDtypeStruct((n, d), BF16), jax.ShapeDtypeStruct((n, n_in), F32)),
        grid=(n // tm, n_in // tn),
        in_specs=[pl.BlockSpec((tm, d), lambda i, j: (i, 0)),
                  pl.BlockSpec((1, 6, d), lambda i, j: (i * tm // t_len, 0, 0)),
                  pl.BlockSpec((1, d), lambda i, j: (0, 0)),
                  pl.BlockSpec((d, tn), lambda i, j: (0, j))],
        out_specs=(pl.BlockSpec((tm, d), lambda i, j: (i, 0)),
                   pl.BlockSpec((tm, tn), lambda i, j: (i, j))),
        scratch_shapes=[pltpu.VMEM((tm, d), BF16)],
        compiler_params=pltpu.CompilerParams(dimension_semantics=("arbitrary", "arbitrary"),
                                             vmem_limit_bytes=VMEM_LIMIT),
        name="inproj",
    )(x2, mod3, norm_w.reshape(1, d), w_in_bf)


def _lora_kernel(h_ref, w1_ref, mu_ref, wd2_ref, wa2_ref, wg2_ref, w0_ref, a0_ref,
                 lw_ref, ag_ref, gg_ref, carry_ref):
    @pl.when(pl.program_id(1) == 0)
    def _():
        carry_ref[...] = jnp.zeros_like(carry_ref)

    h = h_ref[...]
    w1 = w1_ref[...]
    p1 = _dot(h, w1.astype(BF16))
    p2 = _dot(h, (w1 * mu_ref[...]).astype(BF16))
    prev = _shift_rows(p2, carry_ref[...])
    carry_ref[...] = p2[p2.shape[0] - 1:, :]
    lo = p1 - p2 + prev
    dec = _dot_x3(jnp.tanh(lo[:, 0:128]), wd2_ref[...])
    z = -(w0_ref[...] + dec)
    softplus = jnp.maximum(z, 0.0) + jnp.log(1.0 + jnp.exp(-jnp.abs(z)))
    lw_ref[...] = -jnp.exp(-softplus - 0.5)
    ag_ref[...] = _sigmoid(a0_ref[...] + _dot_x3(lo[:, 128:256], wa2_ref[...]))
    gg_ref[...] = _dot_x3(_sigmoid(lo[:, 256:512]), wg2_ref[...])


def _lora(h, w1p, mup, wd2p, wa2p, wg2p, w0, a0, bsz, t_len):
    n, d = h.shape
    dr = w0.shape[-1]
    tm = min(512, t_len)
    nt = t_len // tm
    row = lambda b, t: (b * nt + t, 0)
    const = lambda b, t: (0, 0)
    out = jax.ShapeDtypeStruct((n, dr), F32)
    return pl.pallas_call(
        _lora_kernel,
        out_shape=(out, out, out),
        grid=(bsz, nt),
        in_specs=[pl.BlockSpec((tm, d), row),
                  pl.BlockSpec((d, LORA_PAD), const),
                  pl.BlockSpec((d, LORA_PAD), const),
                  pl.BlockSpec((128, dr), const),
                  pl.BlockSpec((128, dr), const),
                  pl.BlockSpec((256, dr), const),
                  pl.BlockSpec((1, dr), const),
                  pl.BlockSpec((1, dr), const)],
        out_specs=(pl.BlockSpec((tm, dr), row),) * 3,
        scratch_shapes=[pltpu.VMEM((1, LORA_PAD), F32)],
        compiler_params=pltpu.CompilerParams(dimension_semantics=("arbitrary", "arbitrary"),
                                             vmem_limit_bytes=VMEM_LIMIT),
        name="lora",
    )(h, w1p, mup, wd2p, wa2p, wg2p, w0.reshape(1, dr), a0.reshape(1, dr))


def _wkv_kernel(r_ref, k_ref, v_ref, lw_ref, ag_ref, gg_ref, mur_ref, muk_ref, muv_ref,
                kk_ref, ka_ref, rk_ref, lnw_ref, lnb_ref, y_ref, s_ref, carry_ref):
    lt = r_ref.shape[0]
    gl = GROUP_LANES
    c_len = WKV_CHUNK
    heads = gl // HEAD_DIM

    @pl.when(pl.program_id(2) == 0)
    def _():
        s_ref[...] = jnp.zeros_like(s_ref)
        carry_ref[...] = jnp.zeros_like(carry_ref)

    ones_bd = _seg_ones(gl, HEAD_DIM)

    def lerp(ref, mu_ref, idx):
        z = ref[...]
        prev = _shift_rows(z, carry_ref[idx:idx + 1, :])
        carry_ref[idx:idx + 1, :] = z[lt - 1:, :]
        return z + (prev - z) * mu_ref[...]

    r1 = lerp(r_ref, mur_ref, 0)
    k1 = lerp(k_ref, muk_ref, 1)
    v1 = lerp(v_ref, muv_ref, 2)
    ag = ag_ref[...]
    kk = k1 * kk_ref[...]
    kk = kk * lax.rsqrt(jnp.maximum(_seg_sum(kk * kk, ones_bd), 1e-24))
    k2 = k1 * (1.0 + (ag - 1.0) * ka_ref[...])
    a_all = -kk
    b_all = kk * ag
    lw = lw_ref[...]

    rows = lax.broadcasted_iota(jnp.int32, (gl, gl), 0)
    cols = lax.broadcasted_iota(jnp.int32, (gl, gl), 1)
    same_head = (rows // c_len) == (cols // HEAD_DIM)
    ts = lax.broadcasted_iota(jnp.int32, (c_len, gl), 0)
    tj = lax.broadcasted_iota(jnp.int32, (c_len, gl), 1) % c_len
    tri_strict = ts > tj
    tri_incl = ts >= tj
    eye = jnp.where(ts == tj, 1.0, 0.0).astype(F32)
    cr = lax.broadcasted_iota(jnp.int32, (c_len, c_len), 0)
    cc = lax.broadcasted_iota(jnp.int32, (c_len, c_len), 1)
    cum_b = jnp.where(cr >= cc, 1.0, 0.0).astype(BF16)

    def bd(x):
        xb = x.astype(BF16)
        return jnp.where(same_head, jnp.concatenate([xb] * heads, axis=0), jnp.zeros((), BF16))

    def fold(x):
        out = x[0:c_len]
        for hh in range(1, heads):
            out = out + x[hh * c_len:(hh + 1) * c_len]
        return out

    nch = lt // c_len
    chunks = range(nch)
    sls = [slice(c * c_len, (c + 1) * c_len) for c in chunks]
    lws = [lw[sl] for sl in sls]

    def cumsum3(x):
        hi = x.astype(BF16)
        r = x - hi.astype(F32)
        mid = r.astype(BF16)
        lo = (r - mid.astype(F32)).astype(BF16)
        return _dot(cum_b, hi) + _dot(cum_b, mid) + _dot(cum_b, lo)

    cs = [cumsum3(x) for x in lws]
    e_cs = [jnp.exp(x) for x in cs]
    e_neg = [jnp.exp(-x) for x in cs]
    wc = [x[c_len - 1:, :] for x in e_cs]
    at = [a_all[sls[c]] * jnp.exp(cs[c] - lws[c]) for c in chunks]
    rt = [r1[sls[c]] * e_cs[c] for c in chunks]
    bt = [b_all[sls[c]] * e_neg[c] for c in chunks]
    kt = [k2[sls[c]] * e_neg[c] for c in chunks]
    vc = [v1[sl] for sl in sls]

    lhs = [jnp.concatenate([at[c], rt[c]], axis=0).astype(BF16) for c in chunks]
    g_b = [_dot_nt(lhs[c], bd(bt[c])) for c in chunks]
    g_k = [_dot_nt(lhs[c], bd(kt[c])) for c in chunks]
    n_ab = [jnp.where(tri_strict, g[:c_len], 0.0) for g in g_b]
    a_ak = [jnp.where(tri_strict, g[:c_len], 0.0) for g in g_k]
    a_rb = [jnp.where(tri_incl, g[c_len:], 0.0).astype(BF16) for g in g_b]
    a_rk = [jnp.where(tri_incl, g[c_len:], 0.0).astype(BF16) for g in g_k]

    p = [eye + jnp.where((ts // 2) == (tj // 2), x, 0.0) for x in n_ab]
    for half in (2, 4, 8, 16, 32):
        lower_left = (((ts // (2 * half)) == (tj // (2 * half)))
                      & ((ts % (2 * half)) >= half) & ((tj % (2 * half)) < half))
        n21_ta = [_dot(jnp.where(lower_left, n_ab[c], 0.0).astype(BF16), bd(p[c])) for c in chunks]
        p = [p[c] + _dot(p[c].astype(BF16), bd(n21_ta[c])) for c in chunks]
    tb = [x.astype(BF16) for x in p]

    z = [_dot(tb[c], jnp.concatenate([bd(a_ak[c]), bd(at[c])], axis=1)) for c in chunks]
    ap = [x[:, gl:] for x in z]
    bd_v = [bd(x) for x in vc]
    u0 = [_dot(z[c][:, :gl].astype(BF16), bd_v[c]) for c in chunks]
    w = [_dot(a_rb[c], jnp.concatenate([bd(ap[c]), bd(u0[c])], axis=1)) for c in chunks]
    rp = [rt[c] + w[c][:, :gl] for c in chunks]
    y0 = [w[c][:, gl:] + _dot(a_rk[c], bd_v[c]) for c in chunks]
    bh = [(bt[c] * wc[c]).astype(BF16) for c in chunks]
    bk = [jnp.concatenate([bh[c], (kt[c] * wc[c]).astype(BF16)], axis=0) for c in chunks]
    g_s = [jnp.where(same_head, _dot_tn(ap[c].astype(BF16), bh[c]), 0.0).astype(BF16) for c in chunks]
    n_s = [fold(jnp.where(same_head, _dot_tn(jnp.concatenate([u0[c], vc[c]], axis=0).astype(BF16), bk[c]), 0.0))
           for c in chunks]

    s = s_ref[...]
    s_in = []
    for c in chunks:
        s_in.append(s.astype(BF16))
        s = s * wc[c] + _dot(s_in[c], g_s[c]) + n_s[c]
    s_ref[...] = s
    y_chunks = [_dot_nt(rp[c].astype(BF16), bd(s_in[c])) + y0[c] for c in chunks]

    y = jnp.concatenate(y_chunks, axis=0) if len(y_chunks) > 1 else y_chunks[0]
    inv_n = 1.0 / HEAD_DIM
    mean = _seg_sum(y, ones_bd) * inv_n
    yc = y - mean
    var = _seg_sum(yc * yc, ones_bd) * inv_n
    yn = yc * lax.rsqrt(var + LNX_EPS) * lnw_ref[...] + lnb_ref[...]
    bonus = _seg_sum(r1 * k2 * rk_ref[...], ones_bd) * v1
    y_ref[...] = ((yn + bonus) * gg_ref[...]).astype(y_ref.dtype)


def _wkv(proj, lw, ag, gg, mu_r, mu_k, mu_v, k_k, k_a, r_k, lnx_w, lnx_b, bsz, t_len):
    n = proj.shape[0]
    dr = lw.shape[1]
    gl = GROUP_LANES
    ng = dr // gl
    lt = min(512, t_len)
    nt = t_len // lt
    row = lambda off: (lambda b, g, t: (b * nt + t, off + g))
    par = lambda b, g, t: (0, g)
    vec = lambda a: a.reshape(1, dr)
    return pl.pallas_call(
        _wkv_kernel,
        out_shape=jax.ShapeDtypeStruct((n, dr), BF16),
        grid=(bsz, ng, nt),
        in_specs=[pl.BlockSpec((lt, gl), row(0)),
                  pl.BlockSpec((lt, gl), row(ng)),
                  pl.BlockSpec((lt, gl), row(2 * ng)),
                  pl.BlockSpec((lt, gl), row(0)),
                  pl.BlockSpec((lt, gl), row(0)),
                  pl.BlockSpec((lt, gl), row(0))] + [pl.BlockSpec((1, gl), par)] * 8,
        out_specs=pl.BlockSpec((lt, gl), row(0)),
        scratch_shapes=[pltpu.VMEM((HEAD_DIM, gl), F32), pltpu.VMEM((8, gl), F32)],
        compiler_params=pltpu.CompilerParams(
            dimension_semantics=("arbitrary", "arbitrary", "arbitrary"),
            vmem_limit_bytes=VMEM_LIMIT),
        name="wkv",
    )(proj, proj, proj, lw, ag, gg, vec(mu_r), vec(mu_k), vec(mu_v), vec(k_k), vec(k_a),
      vec(r_k), vec(lnx_w), vec(lnx_b))


def _outproj_kernel(yr_ref, u_ref, gb_ref, gc_ref, cw_ref, cnw_ref, wo_ref, x_ref, mod_ref,
                    n2w_ref, wr_ref, br_ref, x1_ref, h2_ref, meta_ref, cnt_ref,
                    ccarry_ref, cnt_acc_ref):
    tm = x_ref.shape[0]
    dc = u_ref.shape[1]

    @pl.when(pl.program_id(1) == 0)
    def _():
        ccarry_ref[...] = jnp.zeros_like(ccarry_ref)

    @pl.when((pl.program_id(0) == 0) & (pl.program_id(1) == 0))
    def _():
        cnt_acc_ref[...] = jnp.zeros_like(cnt_acc_ref)

    pre = gc_ref[...] * u_ref[...]
    prev1 = _shift_rows(pre, ccarry_ref[0:1, :])
    prev2 = _shift_rows(prev1, ccarry_ref[1:2, :])
    ccarry_ref[0:1, :] = pre[tm - 1:, :]
    ccarry_ref[1:2, :] = pre[tm - 2:tm - 1, :]
    yc = gb_ref[...] * (prev2 * cw_ref[0:1, :] + prev1 * cw_ref[1:2, :] + pre * cw_ref[2:3, :])
    ones_bd = _seg_ones(GROUP_LANES, CONV_GROUP_DIM)
    ycs = []
    for s in range(dc // GROUP_LANES):
        ys = yc[:, s * GROUP_LANES:(s + 1) * GROUP_LANES]
        ms = _seg_sum(ys * ys, ones_bd) * (1.0 / CONV_GROUP_DIM)
        ycs.append(ys * lax.rsqrt(ms + RMS_EPS))
    y_conv = (jnp.concatenate(ycs, axis=1) * cnw_ref[...]).astype(BF16)

    dr = yr_ref.shape[1]
    y_mix = _dot(yr_ref[...], wo_ref[0:dr, :]) + _dot(y_conv, wo_ref[dr:, :])
    x1 = x_ref[...] + mod_ref[0, 2:3, :] * y_mix
    x1_ref[...] = x1
    ms = jnp.mean(x1 * x1, axis=-1, keepdims=True)
    h2 = x1 * lax.rsqrt(ms + RMS_EPS) * n2w_ref[...]
    h2 = h2 * (1.0 + mod_ref[0, 4:5, :]) + mod_ref[0, 3:4, :]
    h2_ref[...] = _pack_bf16_pair(h2)

    logits = _dot_x3(h2, wr_ref[...]) + br_ref[...]
    lane = lax.broadcasted_iota(jnp.int32, logits.shape, 1)
    neg = jnp.float32(-jnp.inf)
    big = jnp.int32(1 << 20)
    glog = jnp.where(lane < N_ROUTE_GROUPS, logits, neg)
    gmax = jnp.max(glog, axis=-1, keepdims=True)
    grp = jnp.min(jnp.where(glog == gmax, lane, big), axis=-1, keepdims=True)
    p_grp = 1.0 / jnp.sum(jnp.exp(glog - gmax), axis=-1, keepdims=True)
    e_lane = lane - N_ROUTE_GROUPS
    in_grp = (e_lane >= grp * EXPERTS_PER_GROUP) & (e_lane < (grp + 1) * EXPERTS_PER_GROUP)
    elog = jnp.where(in_grp, logits, neg)
    m1 = jnp.max(elog, axis=-1, keepdims=True)
    i1 = jnp.min(jnp.where(elog == m1, lane, big), axis=-1, keepdims=True)
    elog2 = jnp.where(lane == i1, neg, elog)
    m2 = jnp.max(elog2, axis=-1, keepdims=True)
    i2 = jnp.min(jnp.where(elog2 == m2, lane, big), axis=-1, keepdims=True)
    t2 = jnp.exp(m2 - m1)
    w1 = p_grp / (1.0 + t2)
    w2 = p_grp * t2 / (1.0 + t2)
    e1 = i1 - N_ROUTE_GROUPS
    e2 = i2 - N_ROUTE_GROUPS

    oh1 = lane == e1
    oh2 = lane == e2
    oh = jnp.where(oh1 | oh2, 1.0, 0.0)
    tr = lax.broadcasted_iota(jnp.int32, (tm, tm), 0)
    tc = lax.broadcasted_iota(jnp.int32, (tm, tm), 1)
    lower = jnp.where(tr > tc, 1.0, 0.0).astype(BF16)
    before = _dot(lower, oh.astype(BF16)) + cnt_acc_ref[...]
    rank1 = jnp.sum(jnp.where(oh1, before, 0.0), axis=-1, keepdims=True)
    rank2 = jnp.sum(jnp.where(oh2, before, 0.0), axis=-1, keepdims=True)
    cnt = cnt_acc_ref[...] + jnp.sum(oh, axis=0, keepdims=True)
    cnt_acc_ref[...] = cnt
    cnt_ref[...] = jnp.broadcast_to(cnt, cnt_ref.shape)

    meta = jnp.where(lane == 0, e1.astype(F32), 0.0)
    meta = jnp.where(lane == 1, e2.astype(F32), meta)
    meta = jnp.where(lane == 2, w1, meta)
    meta = jnp.where(lane == 3, w2, meta)
    meta = jnp.where(lane == 4, rank1, meta)
    meta = jnp.where(lane == 5, rank2, meta)
    meta_ref[...] = meta


def _outproj(yr, proj, conv_w, conv_norm_w, w_out_bf, x2, mod3, norm2_w, wr, br, bsz, t_len):
    n, d = x2.shape
    dr = yr.shape[1]
    dc = conv_w.shape[1]
    tm = min(512, t_len)
    nt = t_len // tm
    nseg = dr // dc
    row = lambda off: (lambda b, t: (b * nt + t, off))
    const = lambda b, t: (0, 0)
    return pl.pallas_call(
        _outproj_kernel,
        out_shape=(jax.ShapeDtypeStruct((n, d), F32), jax.ShapeDtypeStruct((n, d // 2), jnp.uint32),
                   jax.ShapeDtypeStruct((n, LANES), F32), jax.ShapeDtypeStruct((8, LANES), F32)),
        grid=(bsz, nt),
        in_specs=[pl.BlockSpec((tm, dr), row(0)),
                  pl.BlockSpec((tm, dc), row(3 * nseg)),
                  pl.BlockSpec((tm, dc), row(4 * nseg)),
                  pl.BlockSpec((tm, dc), row(5 * nseg)),
                  pl.BlockSpec((3, dc), const),
                  pl.BlockSpec((1, dc), const),
                  pl.BlockSpec((dr + dc, d), const),
                  pl.BlockSpec((tm, d), row(0)),
                  pl.BlockSpec((1, 6, d), lambda b, t: (b, 0, 0)),
                  pl.BlockSpec((1, d), const),
                  pl.BlockSpec((d, LANES), const),
                  pl.BlockSpec((1, LANES), const)],
        out_specs=(pl.BlockSpec((tm, d), row(0)), pl.BlockSpec((tm, d // 2), row(0)),
                   pl.BlockSpec((tm, LANES), row(0)), pl.BlockSpec((8, LANES), const)),
        scratch_shapes=[pltpu.VMEM((8, dc), F32), pltpu.VMEM((1, LANES), F32)],
        compiler_params=pltpu.CompilerParams(dimension_semantics=("arbitrary", "arbitrary"),
                                             vmem_limit_bytes=VMEM_LIMIT),
        name="outproj",
    )(yr, proj, proj, proj, conv_w, conv_norm_w.reshape(1, dc), w_out_bf, x2, mod3,
      norm2_w.reshape(1, d), wr, br)


def _dispatch_kernel(zoff_ref, dest_hbm, h2_ref, xs_hbm, idx_smem, zero_ref, zblk_ref,
                     sem_idx, sem_rows, sem_zero, sem_tail):
    i = pl.program_id(0)
    nsteps = pl.num_programs(0)
    th = idx_smem.shape[1] // 2
    slot = i % 2

    def idx_copy(step, sl):
        return pltpu.make_async_copy(dest_hbm.at[step], idx_smem.at[sl], sem_idx.at[sl])

    def rows_wait():
        pltpu.make_async_copy(xs_hbm.at[pl.ds(0, 2 * th)], xs_hbm.at[pl.ds(0, 2 * th)], sem_rows).wait()

    nblk = xs_hbm.shape[0] // ROUTE_BLOCK

    def tail_copy(b):
        start = pl.multiple_of(b * ROUTE_BLOCK, ROUTE_BLOCK)
        return pltpu.make_async_copy(zblk_ref, xs_hbm.at[pl.ds(start, ROUTE_BLOCK)], sem_tail)

    @pl.when(i == 0)
    def _():
        idx_copy(0, 0).start()
        zero_ref[...] = jnp.zeros_like(zero_ref)

        def zero_expert(e, carry):
            off = zoff_ref[e]

            def zero_row(j, c):
                pltpu.make_async_copy(zero_ref, xs_hbm.at[pl.ds(off + j, 1)], sem_zero).start()
                return c

            return lax.fori_loop(0, zoff_ref[N_EXPERTS + e], zero_row, carry)

        lax.fori_loop(0, N_EXPERTS, zero_expert, 0)
        total = zoff_ref[2 * N_EXPERTS]
        bulk = pl.multiple_of((total // 8) * 8, 8)

        @pl.when(bulk > 0)
        def _():
            pltpu.make_async_copy(xs_hbm.at[pl.ds(0, bulk)], xs_hbm.at[pl.ds(0, bulk)], sem_zero).wait()

        def wait_row(j, c):
            pltpu.make_async_copy(zero_ref, xs_hbm.at[pl.ds(0, 1)], sem_zero).wait()
            return c

        lax.fori_loop(0, total - bulk, wait_row, 0)

        zblk_ref[...] = jnp.zeros_like(zblk_ref)

        def tail_start(b, c):
            tail_copy(b).start()
            return c

        lax.fori_loop(zoff_ref[2 * N_EXPERTS + 1], nblk, tail_start, 0)

    idx_copy(i, slot).wait()

    @pl.when(i + 1 < nsteps)
    def _():
        idx_copy(i + 1, 1 - slot).start()

    def body(t, carry):
        for s in range(2):
            d = idx_smem[slot, 2 * t + s]
            pltpu.make_async_copy(h2_ref.at[pl.ds(t, 1)], xs_hbm.at[pl.ds(d, 1)], sem_rows).start()
        return carry

    lax.fori_loop(0, th, body, 0, unroll=8)
    rows_wait()

    @pl.when(i == nsteps - 1)
    def _():
        def tail_wait(b, c):
            tail_copy(b).wait()
            return c

        lax.fori_loop(zoff_ref[2 * N_EXPERTS + 1], nblk, tail_wait, 0)


def _dispatch(dest, zero_off, h2, n_rows):
    n, d = h2.shape
    th = min(256, n)
    return pl.pallas_call(
        _dispatch_kernel,
        out_shape=jax.ShapeDtypeStruct((n_rows, d), h2.dtype),
        grid_spec=pltpu.PrefetchScalarGridSpec(
            num_scalar_prefetch=1,
            grid=(n // th,),
            in_specs=[pl.BlockSpec(memory_space=pl.ANY),
                      pl.BlockSpec((th, d), lambda i, zo: (i, 0))],
            out_specs=pl.BlockSpec(memory_space=pl.ANY),
            scratch_shapes=[pltpu.SMEM((2, 2 * th), jnp.int32),
                            pltpu.VMEM((1, d), h2.dtype),
                            pltpu.VMEM((ROUTE_BLOCK, d), h2.dtype),
                            pltpu.SemaphoreType.DMA((2,)), pltpu.SemaphoreType.DMA,
                            pltpu.SemaphoreType.DMA, pltpu.SemaphoreType.DMA]),
        compiler_params=pltpu.CompilerParams(dimension_semantics=("arbitrary",),
                                             vmem_limit_bytes=VMEM_LIMIT),
        name="dispatch",
    )(zero_off, dest.reshape(n // th, 2 * th), h2)


def _expert_kernel(be_ref, na_ref, xs_ref, wg_ref, wu_ref, wd_ref, ys_ref):
    del be_ref

    @pl.when(pl.program_id(0) < na_ref[0])
    def _():
        x = _unpack_bf16_pair(xs_ref[...]).astype(BF16)
        g = _dot(x, wg_ref[0].astype(BF16))
        u = _dot(x, wu_ref[0].astype(BF16))
        hid = (g * _sigmoid(g) * u).astype(BF16)
        ys_ref[...] = _pack_bf16_pair(_dot(hid, wd_ref[0].astype(BF16)))

    @pl.when(pl.program_id(0) >= na_ref[0])
    def _():
        ys_ref[...] = jnp.zeros_like(ys_ref)


def _experts(block_e, n_active, xs, exp_gate, exp_up, exp_down):
    n_rows, dp = xs.shape
    _, d, de = exp_gate.shape
    nblk = n_rows // ROUTE_BLOCK
    last = lambda i, na: jnp.minimum(i, jnp.maximum(na[0] - 1, 0))
    blk_row = lambda i, be, na: (last(i, na), 0)
    blk_exp = lambda i, be, na: (be[last(i, na)], 0, 0)
    return pl.pallas_call(
        _expert_kernel,
        out_shape=jax.ShapeDtypeStruct((n_rows, dp), jnp.uint32),
        grid_spec=pltpu.PrefetchScalarGridSpec(
            num_scalar_prefetch=2,
            grid=(nblk,),
            in_specs=[pl.BlockSpec((ROUTE_BLOCK, dp), blk_row),
                      pl.BlockSpec((1, d, de), blk_exp),
                      pl.BlockSpec((1, d, de), blk_exp),
                      pl.BlockSpec((1, de, d), blk_exp)],
            out_specs=pl.BlockSpec((ROUTE_BLOCK, dp), lambda i, be, na: (i, 0))),
        compiler_params=pltpu.CompilerParams(dimension_semantics=("arbitrary",),
                                             vmem_limit_bytes=VMEM_LIMIT),
        name="experts",
    )(block_e, n_active, xs, exp_gate, exp_up, exp_down)


def _combine_kernel(dest_hbm, ys_hbm, x1_ref, meta_ref, mod_ref, fw_ref, o_ref,
                    idx_smem, buf_ref, sem_idx, sem_rows):
    i = pl.program_id(0)
    nsteps = pl.num_programs(0)
    tj = x1_ref.shape[0]
    slot = i % 2

    def gather(step, sl):
        cp = pltpu.make_async_copy(dest_hbm.at[step], idx_smem.at[sl], sem_idx)
        cp.start()
        cp.wait()

        def body(t, carry):
            for s in range(2):
                d = idx_smem[sl, 2 * t + s]
                pltpu.make_async_copy(ys_hbm.at[pl.ds(d, 1)], buf_ref.at[sl, s, pl.ds(t, 1)],
                                      sem_rows.at[sl]).start()
            return carry

        lax.fori_loop(0, tj, body, 0, unroll=8)

    @pl.when(i == 0)
    def _():
        gather(0, 0)

    @pl.when(i + 1 < nsteps)
    def _():
        gather(i + 1, 1 - slot)

    pltpu.make_async_copy(buf_ref.at[slot], buf_ref.at[slot], sem_rows.at[slot]).wait()

    meta = meta_ref[...]
    y = (meta[:, 2:3] * _unpack_bf16_pair(buf_ref[slot, 0])
         + meta[:, 3:4] * _unpack_bf16_pair(buf_ref[slot, 1]))
    x2 = x1_ref[...] + mod_ref[0, 5:6, :] * y
    ms = jnp.mean(x2 * x2, axis=-1, keepdims=True)
    o_ref[...] = x2 * lax.rsqrt(ms + RMS_EPS) * fw_ref[...]


def _combine(dest, ys, x1, meta, mod3, final_w, t_len):
    n, d = x1.shape
    tj = min(256, t_len)
    return pl.pallas_call(
        _combine_kernel,
        out_shape=jax.ShapeDtypeStruct((n, d), F32),
        grid=(n // tj,),
        in_specs=[pl.BlockSpec(memory_space=pl.ANY),
                  pl.BlockSpec(memory_space=pl.ANY),
                  pl.BlockSpec((tj, d), lambda i: (i, 0)),
                  pl.BlockSpec((tj, LANES), lambda i: (i, 0)),
                  pl.BlockSpec((1, 6, d), lambda i: (i * tj // t_len, 0, 0)),
                  pl.BlockSpec((1, d), lambda i: (0, 0))],
        out_specs=pl.BlockSpec((tj, d), lambda i: (i, 0)),
        scratch_shapes=[pltpu.SMEM((2, 2 * tj), jnp.int32), pltpu.VMEM((2, 2, tj, d // 2), jnp.uint32),
                        pltpu.SemaphoreType.DMA, pltpu.SemaphoreType.DMA((2,))],
        compiler_params=pltpu.CompilerParams(dimension_semantics=("arbitrary",),
                                             vmem_limit_bytes=VMEM_LIMIT),
        name="combine",
    )(dest.reshape(n // tj, 2 * tj), ys, x1, meta, mod3, final_w.reshape(1, d))


def _pad_cols(w, width):
    return jnp.pad(w, ((0, 0), (0, width - w.shape[1])))


def _pad_rows(w, height):
    return jnp.pad(w, ((0, height - w.shape[0]), (0, 0)))


def _layer(x2, c, bsz, t_len, w_ada, b_ada, norm1_w, w_in, mu_r, mu_k, mu_v, mu_w, mu_a, mu_g, w0,
           w_decay1, w_decay2, a0, w_aaa1, w_aaa2, w_gate1, w_gate2, k_k, k_a, r_k, lnx_w, lnx_b,
           conv_w, conv_norm_w, w_out, norm2_w, w_grp, b_grp, w_exp, b_exp, exp_gate, exp_up,
           exp_down, final_w):
    n, d = x2.shape
    dr = w0.shape[0]
    assert w_decay1.shape[1] <= 128 and w_aaa1.shape[1] <= 128 and w_gate1.shape[1] <= 256
    assert dr % GROUP_LANES == 0 and conv_w.shape[1] == dr and w_in.shape[1] == 6 * dr
    assert t_len % WKV_CHUNK == 0 and w_grp.shape[1] == N_ROUTE_GROUPS and w_exp.shape[1] == N_EXPERTS

    mod3 = _ada(c, w_ada, b_ada).reshape(bsz, 6, d)
    h, proj = _inproj(x2, mod3, norm1_w, w_in.astype(BF16), t_len)

    w1p = jnp.concatenate([_pad_cols(w_decay1, 128), _pad_cols(w_aaa1, 128), _pad_cols(w_gate1, 256)], axis=1)
    mup = jnp.concatenate([jnp.broadcast_to(mu_w[:, None], (d, 128)),
                           jnp.broadcast_to(mu_a[:, None], (d, 128)),
                           jnp.broadcast_to(mu_g[:, None], (d, 256))], axis=1)
    lw, ag, gg = _lora(h, w1p, mup, _pad_rows(w_decay2, 128), _pad_rows(w_aaa2, 128),
                       _pad_rows(w_gate2, 256), w0, a0, bsz, t_len)

    yr = _wkv(proj, lw, ag, gg, mu_r, mu_k, mu_v, k_k, k_a, r_k.reshape(dr), lnx_w, lnx_b, bsz, t_len)

    wr = _pad_cols(jnp.concatenate([w_grp, w_exp], axis=1), LANES)
    br = _pad_cols(jnp.concatenate([b_grp, b_exp]).reshape(1, -1), LANES)
    x1, h2, meta, cnt = _outproj(yr, proj, conv_w, conv_norm_w, w_out.astype(BF16), x2, mod3,
                                 norm2_w, wr, br, bsz, t_len)

    counts = cnt[0, :N_EXPERTS].astype(jnp.int32)
    padded = (counts + ROUTE_BLOCK - 1) // ROUTE_BLOCK * ROUTE_BLOCK
    pad_ends = jnp.cumsum(padded)
    pad_starts = pad_ends - padded
    top_e = meta[:, 0:2].astype(jnp.int32)
    rank = meta[:, 4:6].astype(jnp.int32)
    dest = (pad_starts[top_e] + rank).reshape(-1)
    nblk = (2 * n) // ROUTE_BLOCK + N_EXPERTS
    n_rows = nblk * ROUTE_BLOCK
    block_start = jnp.arange(nblk, dtype=jnp.int32) * ROUTE_BLOCK
    block_e = jnp.minimum(jnp.sum(pad_ends[None, :] <= block_start[:, None], axis=1),
                          N_EXPERTS - 1).astype(jnp.int32)
    n_active = (pad_ends[-1:] // ROUTE_BLOCK).astype(jnp.int32)

    pad_len = padded - counts
    zero_off = jnp.concatenate([pad_starts + counts, pad_len, jnp.sum(pad_len, keepdims=True),
                                n_active]).astype(jnp.int32)
    xs = _dispatch(dest, zero_off, h2, n_rows)
    ys = _experts(block_e, n_active, xs, exp_gate, exp_up, exp_down)
    return _combine(dest, ys, x1, meta, mod3, final_w, t_len)


def kernel(x, c, w_ada, b_ada, norm1_w, w_in, mu_r, mu_k, mu_v, mu_w, mu_a, mu_g, w0, w_decay1, w_decay2, a0, w_aaa1, w_aaa2, w_gate1, w_gate2, k_k, k_a, r_k, lnx_w, lnx_b, conv_w, conv_norm_w, w_out, norm2_w, w_grp, b_grp, w_exp, b_exp, exp_gate, exp_up, exp_down, final_w):
    bsz, t_len, d = x.shape
    assert w_ada.shape[0] == 1, "single-layer block"
    out = _layer(x.reshape(bsz * t_len, d), c, bsz, t_len, w_ada[0], b_ada[0], norm1_w[0], w_in[0],
                 mu_r[0], mu_k[0], mu_v[0], mu_w[0], mu_a[0], mu_g[0], w0[0], w_decay1[0], w_decay2[0],
                 a0[0], w_aaa1[0], w_aaa2[0], w_gate1[0], w_gate2[0], k_k[0], k_a[0], r_k[0], lnx_w[0],
                 lnx_b[0], conv_w[0], conv_norm_w[0], w_out[0], norm2_w[0], w_grp[0], b_grp[0],
                 w_exp[0], b_exp[0], exp_gate[0], exp_up[0], exp_down[0], final_w)
    return out.reshape(bsz, t_len, d)
```

```python
import functools

import jax
import jax.numpy as jnp
from jax import lax
from jax.experimental import pallas as pl
from jax.experimental.pallas import tpu as pltpu

F32 = jnp.float32
BF16 = jnp.bfloat16

HEAD_DIM = 64
WKV_CHUNK = 64
GROUP_LANES = 256
N_ROUTE_GROUPS = 4
EXPERTS_PER_GROUP = 8
N_EXPERTS = N_ROUTE_GROUPS * EXPERTS_PER_GROUP
CONV_GROUP_DIM = 64
RMS_EPS = 1e-6
LNX_EPS = 64e-5
LANES = 128
LORA_PAD = 512
ROUTE_BLOCK = 256
VMEM_LIMIT = 56 * 1024 * 1024


def _dot(a, b, precision=None):
    return jnp.dot(a, b, preferred_element_type=F32, precision=precision)


def _dot_nt(a, b):
    return lax.dot_general(a, b, (((1,), (1,)), ((), ())), preferred_element_type=F32)


def _dot_tn(a, b):
    return lax.dot_general(a, b, (((0,), (0,)), ((), ())), preferred_element_type=F32)


def _split_bf16(x):
    hi = x.astype(BF16)
    return hi, (x - hi.astype(F32)).astype(BF16)


def _dot_x3(a, b):
    ah, al = _split_bf16(a)
    bh, bl = _split_bf16(b)
    return _dot(ah, bh) + _dot(al, bh) + _dot(ah, bl)


def _pack_bf16_pair(x):
    c = x.shape[1] // 2
    lo = pltpu.bitcast(x[:, :c].astype(BF16).astype(F32), jnp.uint32)
    hi = pltpu.bitcast(x[:, c:].astype(BF16).astype(F32), jnp.uint32)
    return (lo >> 16) | (hi & jnp.uint32(0xFFFF0000))


def _unpack_bf16_pair(w):
    lo = pltpu.bitcast(w << 16, F32)
    hi = pltpu.bitcast(w & jnp.uint32(0xFFFF0000), F32)
    return jnp.concatenate([lo, hi], axis=1)


def _sigmoid(x):
    return 1.0 / (1.0 + jnp.exp(-x))


def _shift_rows(x, carry_row):
    rolled = pltpu.roll(x, 1, axis=0)
    row = lax.broadcasted_iota(jnp.int32, x.shape, 0)
    return jnp.where(row == 0, carry_row, rolled)


def _seg_ones(n, seg):
    r = lax.broadcasted_iota(jnp.int32, (n, n), 0) // seg
    c = lax.broadcasted_iota(jnp.int32, (n, n), 1) // seg
    return jnp.where(r == c, 1.0, 0.0).astype(BF16)


def _seg_sum(x, ones_bd):
    hi = x.astype(BF16)
    lo = (x - hi.astype(F32)).astype(BF16)
    return _dot(hi, ones_bd) + _dot(lo, ones_bd)


def _seg_sum1(x, ones_bd):
    return _dot(x.astype(BF16), ones_bd)


def _ada_kernel(cb_ref, w_ref, b_ref, o_ref):
    nb = cb_ref.shape[0]
    tn = w_ref.shape[1]
    for b in range(nb):
        cv = cb_ref[b]
        s = cv * _sigmoid(cv)
        cols = [jnp.sum(s * w_ref[:, j * LANES:(j + 1) * LANES], axis=0, keepdims=True)
                for j in range(tn // LANES)]
        o_ref[b:b + 1, :] = jnp.concatenate(cols, axis=1) + b_ref[...]


def _ada(c, w_ada, b_ada):
    nb, d = c.shape
    n_out = w_ada.shape[1]
    tn = 512
    cb = jnp.broadcast_to(c[:, :, None], (nb, d, LANES))
    return pl.pallas_call(
        _ada_kernel,
        out_shape=jax.ShapeDtypeStruct((nb, n_out), F32),
        grid=(n_out // tn,),
        in_specs=[pl.BlockSpec((nb, d, LANES), lambda j: (0, 0, 0)),
                  pl.BlockSpec((d, tn), lambda j: (0, j)),
                  pl.BlockSpec((1, tn), lambda j: (0, j))],
        out_specs=pl.BlockSpec((nb, tn), lambda j: (0, j)),
        compiler_params=pltpu.CompilerParams(dimension_semantics=("arbitrary",),
                                             vmem_limit_bytes=VMEM_LIMIT),
        name="ada",
    )(cb, w_ada, b_ada.reshape(1, n_out))


def _inproj_kernel(x_ref, mod_ref, nw_ref, w_ref, h_ref, p_ref, hs_ref):
    @pl.when(pl.program_id(1) == 0)
    def _():
        x = x_ref[...]
        ms = jnp.mean(x * x, axis=-1, keepdims=True)
        y = x * lax.rsqrt(ms + RMS_EPS) * nw_ref[...]
        h = y * (1.0 + mod_ref[0, 1:2, :]) + mod_ref[0, 0:1, :]
        hb = h.astype(BF16)
        hs_ref[...] = hb
        h_ref[...] = hb

    p_ref[...] = _dot(hs_ref[...], w_ref[...])


def _inproj(x2, mod3, norm_w, w_in_bf, t_len):
    n, d = x2.shape
    n_in = w_in_bf.shape[1]
    tm = min(1024, t_len)
    tn = 1024
    return pl.pallas_call(
        _inproj_kernel,
        out_shape=(jax.ShapeDtypeStruct((n, d), BF16), jax.ShapeDtypeStruct((n, n_in), F32)),
        grid=(n // tm, n_in // tn),
        in_specs=[pl.BlockSpec((tm, d), lambda i, j: (i, 0)),
                  pl.BlockSpec((1, 6, d), lambda i, j: (i * tm // t_len, 0, 0)),
                  pl.BlockSpec((1, d), lambda i, j: (0, 0)),
                  pl.BlockSpec((d, tn), lambda i, j: (0, j))],
        out_specs=(pl.BlockSpec((tm, d), lambda i, j: (i, 0)),
                   pl.BlockSpec((tm, tn), lambda i, j: (i, j))),
        scratch_shapes=[pltpu.VMEM((tm, d), BF16)],
        compiler_params=pltpu.CompilerParams(dimension_semantics=("arbitrary", "arbitrary"),
                                             vmem_limit_bytes=VMEM_LIMIT),
        name="inproj",
    )(x2, mod3, norm_w.reshape(1, d), w_in_bf)


def _lora_kernel(h_ref, w1_ref, mu_ref, wd2_ref, wa2_ref, wg2_ref, w0_ref, a0_ref,
                 lw_ref, ag_ref, gg_ref, carry_ref):
    @pl.when(pl.program_id(1) == 0)
    def _():
        carry_ref[...] = jnp.zeros_like(carry_ref)

    h = h_ref[...]
    w1 = w1_ref[...]
    p1 = _dot(h, w1.astype(BF16))
    p2 = _dot(h, (w1 * mu_ref[...]).astype(BF16))
    prev = _shift_rows(p2, carry_ref[...])
    carry_ref[...] = p2[p2.shape[0] - 1:, :]
    lo = p1 - p2 + prev
    dec = _dot(jnp.tanh(lo[:, 0:128]).astype(BF16), wd2_ref[...].astype(BF16))
    z = -(w0_ref[...] + dec)
    softplus = jnp.maximum(z, 0.0) + jnp.log(1.0 + jnp.exp(-jnp.abs(z)))
    lw_ref[...] = -jnp.exp(-softplus - 0.5)
    ag_ref[...] = _sigmoid(a0_ref[...] + _dot(lo[:, 128:256].astype(BF16), wa2_ref[...].astype(BF16)))
    gg_ref[...] = _dot(_sigmoid(lo[:, 256:512]).astype(BF16), wg2_ref[...].astype(BF16))


def _lora(h, w1p, mup, wd2p, wa2p, wg2p, w0, a0, bsz, t_len):
    n, d = h.shape
    dr = w0.shape[-1]
    tm = min(512, t_len)
    nt = t_len // tm
    row = lambda b, t: (b * nt + t, 0)
    const = lambda b, t: (0, 0)
    out = jax.ShapeDtypeStruct((n, dr), F32)
    return pl.pallas_call(
        _lora_kernel,
        out_shape=(out, out, out),
        grid=(bsz, nt),
        in_specs=[pl.BlockSpec((tm, d), row),
                  pl.BlockSpec((d, LORA_PAD), const),
                  pl.BlockSpec((d, LORA_PAD), const),
                  pl.BlockSpec((128, dr), const),
                  pl.BlockSpec((128, dr), const),
                  pl.BlockSpec((256, dr), const),
                  pl.BlockSpec((1, dr), const),
                  pl.BlockSpec((1, dr), const)],
        out_specs=(pl.BlockSpec((tm, dr), row),) * 3,
        scratch_shapes=[pltpu.VMEM((1, LORA_PAD), F32)],
        compiler_params=pltpu.CompilerParams(dimension_semantics=("arbitrary", "arbitrary"),
                                             vmem_limit_bytes=VMEM_LIMIT),
        name="lora",
    )(h, w1p, mup, wd2p, wa2p, wg2p, w0.reshape(1, dr), a0.reshape(1, dr))


def _wkv_kernel(r_ref, k_ref, v_ref, lw_ref, ag_ref, gg_ref, mur_ref, muk_ref, muv_ref,
                kk_ref, ka_ref, rk_ref, lnw_ref, lnb_ref, y_ref, s_ref, carry_ref):
    lt = r_ref.shape[0]
    gl = GROUP_LANES
    c_len = WKV_CHUNK
    heads = gl // HEAD_DIM

    @pl.when(pl.program_id(2) == 0)
    def _():
        s_ref[...] = jnp.zeros_like(s_ref)
        carry_ref[...] = jnp.zeros_like(carry_ref)

    ones_bd = _seg_ones(gl, HEAD_DIM)

    def lerp(ref, mu_ref, idx):
        z = ref[...]
        prev = _shift_rows(z, carry_ref[idx:idx + 1, :])
        carry_ref[idx:idx + 1, :] = z[lt - 1:, :]
        return z + (prev - z) * mu_ref[...]

    r1 = lerp(r_ref, mur_ref, 0)
    k1 = lerp(k_ref, muk_ref, 1)
    v1 = lerp(v_ref, muv_ref, 2)
    ag = ag_ref[...]
    kk = k1 * kk_ref[...]
    kk = kk * lax.rsqrt(jnp.maximum(_seg_sum1(kk * kk, ones_bd), 1e-24))
    k2 = k1 * (1.0 + (ag - 1.0) * ka_ref[...])
    a_all = -kk
    b_all = kk * ag
    lw = lw_ref[...]

    rows = lax.broadcasted_iota(jnp.int32, (gl, gl), 0)
    cols = lax.broadcasted_iota(jnp.int32, (gl, gl), 1)
    same_head = (rows // c_len) == (cols // HEAD_DIM)
    ts = lax.broadcasted_iota(jnp.int32, (c_len, gl), 0)
    tj = lax.broadcasted_iota(jnp.int32, (c_len, gl), 1) % c_len
    tri_strict = ts > tj
    tri_incl = ts >= tj
    eye = jnp.where(ts == tj, 1.0, 0.0).astype(F32)
    cr = lax.broadcasted_iota(jnp.int32, (c_len, c_len), 0)
    cc = lax.broadcasted_iota(jnp.int32, (c_len, c_len), 1)
    cum_b = jnp.where(cr >= cc, 1.0, 0.0).astype(BF16)

    def bd(x):
        xb = x.astype(BF16)
        return jnp.where(same_head, jnp.concatenate([xb] * heads, axis=0), jnp.zeros((), BF16))

    def fold(x):
        out = x[0:c_len]
        for hh in range(1, heads):
            out = out + x[hh * c_len:(hh + 1) * c_len]
        return out

    nch = lt // c_len
    chunks = range(nch)
    sls = [slice(c * c_len, (c + 1) * c_len) for c in chunks]
    lws = [lw[sl] for sl in sls]

    def cumsum3(x):
        hi = x.astype(BF16)
        r = x - hi.astype(F32)
        mid = r.astype(BF16)
        lo = (r - mid.astype(F32)).astype(BF16)
        return _dot(cum_b, hi) + _dot(cum_b, mid) + _dot(cum_b, lo)

    cs = [cumsum3(x) for x in lws]
    e_cs = [jnp.exp(x) for x in cs]
    e_neg = [jnp.exp(-x) for x in cs]
    wc = [x[c_len - 1:, :] for x in e_cs]
    at = [a_all[sls[c]] * jnp.exp(cs[c] - lws[c]) for c in chunks]
    rt = [r1[sls[c]] * e_cs[c] for c in chunks]
    bt = [b_all[sls[c]] * e_neg[c] for c in chunks]
    kt = [k2[sls[c]] * e_neg[c] for c in chunks]
    vc = [v1[sl] for sl in sls]

    lhs = [jnp.concatenate([at[c], rt[c]], axis=0).astype(BF16) for c in chunks]
    g_b = [_dot_nt(lhs[c], bd(bt[c])) for c in chunks]
    g_k = [_dot_nt(lhs[c], bd(kt[c])) for c in chunks]
    n_ab = [jnp.where(tri_strict, g[:c_len], 0.0) for g in g_b]
    a_ak = [jnp.where(tri_strict, g[:c_len], 0.0) for g in g_k]
    a_rb = [jnp.where(tri_incl, g[c_len:], 0.0).astype(BF16) for g in g_b]
    a_rk = [jnp.where(tri_incl, g[c_len:], 0.0).astype(BF16) for g in g_k]

    p = [eye + jnp.where((ts // 2) == (tj // 2), x, 0.0) for x in n_ab]
    for half in (2, 4, 8, 16, 32):
        lower_left = (((ts // (2 * half)) == (tj // (2 * half)))
                      & ((ts % (2 * half)) >= half) & ((tj % (2 * half)) < half))
        n21_ta = [_dot(jnp.where(lower_left, n_ab[c], 0.0).astype(BF16), bd(p[c])) for c in chunks]
        p = [p[c] + _dot(p[c].astype(BF16), bd(n21_ta[c])) for c in chunks]
    tb = [x.astype(BF16) for x in p]

    z = [_dot(tb[c], jnp.concatenate([bd(a_ak[c]), bd(at[c])], axis=1)) for c in chunks]
    ap = [x[:, gl:] for x in z]
    bd_v = [bd(x) for x in vc]
    u0 = [_dot(z[c][:, :gl].astype(BF16), bd_v[c]) for c in chunks]
    w = [_dot(a_rb[c], jnp.concatenate([bd(ap[c]), bd(u0[c])], axis=1)) for c in chunks]
    rp = [rt[c] + w[c][:, :gl] for c in chunks]
    y0 = [w[c][:, gl:] + _dot(a_rk[c], bd_v[c]) for c in chunks]
    bh = [(bt[c] * wc[c]).astype(BF16) for c in chunks]
    bk = [jnp.concatenate([bh[c], (kt[c] * wc[c]).astype(BF16)], axis=0) for c in chunks]
    g_s = [jnp.where(same_head, _dot_tn(ap[c].astype(BF16), bh[c]), 0.0).astype(BF16) for c in chunks]
    n_s = [fold(jnp.where(same_head, _dot_tn(jnp.concatenate([u0[c], vc[c]], axis=0).astype(BF16), bk[c]), 0.0))
           for c in chunks]

    s = s_ref[...]
    s_in = []
    for c in chunks:
        s_in.append(s.astype(BF16))
        s = s * wc[c] + _dot(s_in[c], g_s[c]) + n_s[c]
    s_ref[...] = s
    y_chunks = [_dot_nt(rp[c].astype(BF16), bd(s_in[c])) + y0[c] for c in chunks]

    y = jnp.concatenate(y_chunks, axis=0) if len(y_chunks) > 1 else y_chunks[0]
    inv_n = 1.0 / HEAD_DIM
    mean = _seg_sum(y, ones_bd) * inv_n
    yc = y - mean
    var = _seg_sum1(yc * yc, ones_bd) * inv_n
    yn = yc * lax.rsqrt(var + LNX_EPS) * lnw_ref[...] + lnb_ref[...]
    bonus = _seg_sum(r1 * k2 * rk_ref[...], ones_bd) * v1
    y_ref[...] = ((yn + bonus) * gg_ref[...]).astype(y_ref.dtype)


def _wkv(proj, lw, ag, gg, mu_r, mu_k, mu_v, k_k, k_a, r_k, lnx_w, lnx_b, bsz, t_len):
    n = proj.shape[0]
    dr = lw.shape[1]
    gl = GROUP_LANES
    ng = dr // gl
    lt = min(512, t_len)
    nt = t_len // lt
    row = lambda off: (lambda b, g, t: (b * nt + t, off + g))
    par = lambda b, g, t: (0, g)
    vec = lambda a: a.reshape(1, dr)
    return pl.pallas_call(
        _wkv_kernel,
        out_shape=jax.ShapeDtypeStruct((n, dr), BF16),
        grid=(bsz, ng, nt),
        in_specs=[pl.BlockSpec((lt, gl), row(0)),
                  pl.BlockSpec((lt, gl), row(ng)),
                  pl.BlockSpec((lt, gl), row(2 * ng)),
                  pl.BlockSpec((lt, gl), row(0)),
                  pl.BlockSpec((lt, gl), row(0)),
                  pl.BlockSpec((lt, gl), row(0))] + [pl.BlockSpec((1, gl), par)] * 8,
        out_specs=pl.BlockSpec((lt, gl), row(0)),
        scratch_shapes=[pltpu.VMEM((HEAD_DIM, gl), F32), pltpu.VMEM((8, gl), F32)],
        compiler_params=pltpu.CompilerParams(
            dimension_semantics=("arbitrary", "arbitrary", "arbitrary"),
            vmem_limit_bytes=VMEM_LIMIT),
        name="wkv",
    )(proj, proj, proj, lw, ag, gg, vec(mu_r), vec(mu_k), vec(mu_v), vec(k_k), vec(k_a),
      vec(r_k), vec(lnx_w), vec(lnx_b))


def _outproj_kernel(yr_ref, u_ref, gb_ref, gc_ref, cw_ref, cnw_ref, wo_ref, x_ref, mod_ref,
                    n2w_ref, wr_ref, br_ref, x1_ref, h2_ref, meta_ref, cnt_ref,
                    ccarry_ref, cnt_acc_ref):
    tm = x_ref.shape[0]
    dc = u_ref.shape[1]

    @pl.when(pl.program_id(1) == 0)
    def _():
        ccarry_ref[...] = jnp.zeros_like(ccarry_ref)

    @pl.when((pl.program_id(0) == 0) & (pl.program_id(1) == 0))
    def _():
        cnt_acc_ref[...] = jnp.zeros_like(cnt_acc_ref)

    pre = gc_ref[...] * u_ref[...]
    prev1 = _shift_rows(pre, ccarry_ref[0:1, :])
    prev2 = _shift_rows(prev1, ccarry_ref[1:2, :])
    ccarry_ref[0:1, :] = pre[tm - 1:, :]
    ccarry_ref[1:2, :] = pre[tm - 2:tm - 1, :]
    yc = gb_ref[...] * (prev2 * cw_ref[0:1, :] + prev1 * cw_ref[1:2, :] + pre * cw_ref[2:3, :])
    ones_bd = _seg_ones(GROUP_LANES, CONV_GROUP_DIM)
    ycs = []
    for s in range(dc // GROUP_LANES):
        ys = yc[:, s * GROUP_LANES:(s + 1) * GROUP_LANES]
        ms = _seg_sum1(ys * ys, ones_bd) * (1.0 / CONV_GROUP_DIM)
        ycs.append(ys * lax.rsqrt(ms + RMS_EPS))
    y_conv = (jnp.concatenate(ycs, axis=1) * cnw_ref[...]).astype(BF16)

    dr = yr_ref.shape[1]
    y_mix = _dot(yr_ref[...], wo_ref[0:dr, :]) + _dot(y_conv, wo_ref[dr:, :])
    x1 = x_ref[...] + mod_ref[0, 2:3, :] * y_mix
    x1_ref[...] = x1
    ms = jnp.mean(x1 * x1, axis=-1, keepdims=True)
    h2 = x1 * lax.rsqrt(ms + RMS_EPS) * n2w_ref[...]
    h2 = h2 * (1.0 + mod_ref[0, 4:5, :]) + mod_ref[0, 3:4, :]
    h2_ref[...] = _pack_bf16_pair(h2)

    logits = _dot_x3(h2, wr_ref[...]) + br_ref[...]
    lane = lax.broadcasted_iota(jnp.int32, logits.shape, 1)
    neg = jnp.float32(-jnp.inf)
    big = jnp.int32(1 << 20)
    glog = jnp.where(lane < N_ROUTE_GROUPS, logits, neg)
    gmax = jnp.max(glog, axis=-1, keepdims=True)
    grp = jnp.min(jnp.where(glog == gmax, lane, big), axis=-1, keepdims=True)
    p_grp = 1.0 / jnp.sum(jnp.exp(glog - gmax), axis=-1, keepdims=True)
    e_lane = lane - N_ROUTE_GROUPS
    in_grp = (e_lane >= grp * EXPERTS_PER_GROUP) & (e_lane < (grp + 1) * EXPERTS_PER_GROUP)
    elog = jnp.where(in_grp, logits, neg)
    m1 = jnp.max(elog, axis=-1, keepdims=True)
    i1 = jnp.min(jnp.where(elog == m1, lane, big), axis=-1, keepdims=True)
    elog2 = jnp.where(lane == i1, neg, elog)
    m2 = jnp.max(elog2, axis=-1, keepdims=True)
    i2 = jnp.min(jnp.where(elog2 == m2, lane, big), axis=-1, keepdims=True)
    t2 = jnp.exp(m2 - m1)
    w1 = p_grp / (1.0 + t2)
    w2 = p_grp * t2 / (1.0 + t2)
    e1 = i1 - N_ROUTE_GROUPS
    e2 = i2 - N_ROUTE_GROUPS

    oh1 = lane == e1
    oh2 = lane == e2
    oh = jnp.where(oh1 | oh2, 1.0, 0.0)
    tr = lax.broadcasted_iota(jnp.int32, (tm, tm), 0)
    tc = lax.broadcasted_iota(jnp.int32, (tm, tm), 1)
    lower = jnp.where(tr > tc, 1.0, 0.0).astype(BF16)
    before = _dot(lower, oh.astype(BF16)) + cnt_acc_ref[...]
    rank1 = jnp.sum(jnp.where(oh1, before, 0.0), axis=-1, keepdims=True)
    rank2 = jnp.sum(jnp.where(oh2, before, 0.0), axis=-1, keepdims=True)
    cnt = cnt_acc_ref[...] + jnp.sum(oh, axis=0, keepdims=True)
    cnt_acc_ref[...] = cnt
    cnt_ref[...] = jnp.broadcast_to(cnt, cnt_ref.shape)

    meta = jnp.where(lane == 0, e1.astype(F32), 0.0)
    meta = jnp.where(lane == 1, e2.astype(F32), meta)
    meta = jnp.where(lane == 2, w1, meta)
    meta = jnp.where(lane == 3, w2, meta)
    meta = jnp.where(lane == 4, rank1, meta)
    meta = jnp.where(lane == 5, rank2, meta)
    meta_ref[...] = meta


def _outproj(yr, proj, conv_w, conv_norm_w, w_out_bf, x2, mod3, norm2_w, wr, br, bsz, t_len):
    n, d = x2.shape
    dr = yr.shape[1]
    dc = conv_w.shape[1]
    tm = min(512, t_len)
    nt = t_len // tm
    nseg = dr // dc
    row = lambda off: (lambda b, t: (b * nt + t, off))
    const = lambda b, t: (0, 0)
    return pl.pallas_call(
        _outproj_kernel,
        out_shape=(jax.ShapeDtypeStruct((n, d), F32), jax.ShapeDtypeStruct((n, d // 2), jnp.uint32),
                   jax.ShapeDtypeStruct((n, LANES), F32), jax.ShapeDtypeStruct((8, LANES), F32)),
        grid=(bsz, nt),
        in_specs=[pl.BlockSpec((tm, dr), row(0)),
                  pl.BlockSpec((tm, dc), row(3 * nseg)),
                  pl.BlockSpec((tm, dc), row(4 * nseg)),
                  pl.BlockSpec((tm, dc), row(5 * nseg)),
                  pl.BlockSpec((3, dc), const),
                  pl.BlockSpec((1, dc), const),
                  pl.BlockSpec((dr + dc, d), const),
                  pl.BlockSpec((tm, d), row(0)),
                  pl.BlockSpec((1, 6, d), lambda b, t: (b, 0, 0)),
                  pl.BlockSpec((1, d), const),
                  pl.BlockSpec((d, LANES), const),
                  pl.BlockSpec((1, LANES), const)],
        out_specs=(pl.BlockSpec((tm, d), row(0)), pl.BlockSpec((tm, d // 2), row(0)),
                   pl.BlockSpec((tm, LANES), row(0)), pl.BlockSpec((8, LANES), const)),
        scratch_shapes=[pltpu.VMEM((8, dc), F32), pltpu.VMEM((1, LANES), F32)],
        compiler_params=pltpu.CompilerParams(dimension_semantics=("arbitrary", "arbitrary"),
                                             vmem_limit_bytes=VMEM_LIMIT),
        name="outproj",
    )(yr, proj, proj, proj, conv_w, conv_norm_w.reshape(1, dc), w_out_bf, x2, mod3,
      norm2_w.reshape(1, d), wr, br)


def _dispatch_kernel(zoff_ref, dest_hbm, h2_ref, xs_hbm, idx_smem, zero_ref, zblk_ref,
                     sem_idx, sem_rows, sem_zero, sem_tail):
    i = pl.program_id(0)
    nsteps = pl.num_programs(0)
    th = idx_smem.shape[1] // 2
    slot = i % 2

    def idx_copy(step, sl):
        return pltpu.make_async_copy(dest_hbm.at[step], idx_smem.at[sl], sem_idx.at[sl])

    def rows_wait():
        pltpu.make_async_copy(xs_hbm.at[pl.ds(0, 2 * th)], xs_hbm.at[pl.ds(0, 2 * th)], sem_rows).wait()

    nblk = xs_hbm.shape[0] // ROUTE_BLOCK

    def tail_copy(b):
        start = pl.multiple_of(b * ROUTE_BLOCK, ROUTE_BLOCK)
        return pltpu.make_async_copy(zblk_ref, xs_hbm.at[pl.ds(start, ROUTE_BLOCK)], sem_tail)

    @pl.when(i == 0)
    def _():
        idx_copy(0, 0).start()
        zero_ref[...] = jnp.zeros_like(zero_ref)

        def zero_expert(e, carry):
            off = zoff_ref[e]

            def zero_row(j, c):
                pltpu.make_async_copy(zero_ref, xs_hbm.at[pl.ds(off + j, 1)], sem_zero).start()
                return c

            return lax.fori_loop(0, zoff_ref[N_EXPERTS + e], zero_row, carry)

        lax.fori_loop(0, N_EXPERTS, zero_expert, 0)
        total = zoff_ref[2 * N_EXPERTS]
        bulk = pl.multiple_of((total // 8) * 8, 8)

        @pl.when(bulk > 0)
        def _():
            pltpu.make_async_copy(xs_hbm.at[pl.ds(0, bulk)], xs_hbm.at[pl.ds(0, bulk)], sem_zero).wait()

        def wait_row(j, c):
            pltpu.make_async_copy(zero_ref, xs_hbm.at[pl.ds(0, 1)], sem_zero).wait()
            return c

        lax.fori_loop(0, total - bulk, wait_row, 0)

        zblk_ref[...] = jnp.zeros_like(zblk_ref)

        def tail_start(b, c):
            tail_copy(b).start()
            return c

        lax.fori_loop(zoff_ref[2 * N_EXPERTS + 1], nblk, tail_start, 0)

    idx_copy(i, slot).wait()

    @pl.when(i + 1 < nsteps)
    def _():
        idx_copy(i + 1, 1 - slot).start()

    def body(t, carry):
        for s in range(2):
            d = idx_smem[slot, 2 * t + s]
            pltpu.make_async_copy(h2_ref.at[pl.ds(t, 1)], xs_hbm.at[pl.ds(d, 1)], sem_rows).start()
        return carry

    lax.fori_loop(0, th, body, 0, unroll=8)
    rows_wait()

    @pl.when(i == nsteps - 1)
    def _():
        def tail_wait(b, c):
            tail_copy(b).wait()
            return c

        lax.fori_loop(zoff_ref[2 * N_EXPERTS + 1], nblk, tail_wait, 0)


def _dispatch(dest, zero_off, h2, n_rows):
    n, d = h2.shape
    th = min(256, n)
    return pl.pallas_call(
        _dispatch_kernel,
        out_shape=jax.ShapeDtypeStruct((n_rows, d), h2.dtype),
        grid_spec=pltpu.PrefetchScalarGridSpec(
            num_scalar_prefetch=1,
            grid=(n // th,),
            in_specs=[pl.BlockSpec(memory_space=pl.ANY),
                      pl.BlockSpec((th, d), lambda i, zo: (i, 0))],
            out_specs=pl.BlockSpec(memory_space=pl.ANY),
            scratch_shapes=[pltpu.SMEM((2, 2 * th), jnp.int32),
                            pltpu.VMEM((1, d), h2.dtype),
                            pltpu.VMEM((ROUTE_BLOCK, d), h2.dtype),
                            pltpu.SemaphoreType.DMA((2,)), pltpu.SemaphoreType.DMA,
                            pltpu.SemaphoreType.DMA, pltpu.SemaphoreType.DMA]),
        compiler_params=pltpu.CompilerParams(dimension_semantics=("arbitrary",),
                                             vmem_limit_bytes=VMEM_LIMIT),
        name="dispatch",
    )(zero_off, dest.reshape(n // th, 2 * th), h2)


def _expert_kernel(sched_ref, na_ref, xs_ref, wg_hbm, wu_hbm, wd_hbm, ys_ref,
                   wg_buf, wu_buf, wd_buf, sem):
    i = pl.program_id(0)

    def weight_copies(e, slot):
        return (pltpu.make_async_copy(wg_hbm.at[e], wg_buf.at[slot], sem.at[slot, 0]),
                pltpu.make_async_copy(wu_hbm.at[e], wu_buf.at[slot], sem.at[slot, 1]),
                pltpu.make_async_copy(wd_hbm.at[e], wd_buf.at[slot], sem.at[slot, 2]))

    @pl.when(i < na_ref[0])
    def _():
        e = sched_ref[0, i]
        slot = sched_ref[1, i]

        @pl.when(i == 0)
        def _():
            for cp in weight_copies(e, slot):
                cp.start()

        @pl.when(sched_ref[2, i] == 1)
        def _():
            for cp in weight_copies(e, slot):
                cp.wait()
            nxt = sched_ref[3, i]

            @pl.when(nxt >= 0)
            def _():
                for cp in weight_copies(nxt, 1 - slot):
                    cp.start()

        x = _unpack_bf16_pair(xs_ref[...]).astype(BF16)
        g = _dot(x, wg_buf[slot].astype(BF16))
        u = _dot(x, wu_buf[slot].astype(BF16))
        hid = (g * _sigmoid(g) * u).astype(BF16)
        ys_ref[...] = _pack_bf16_pair(_dot(hid, wd_buf[slot].astype(BF16)))

    @pl.when(i >= na_ref[0])
    def _():
        ys_ref[...] = jnp.zeros_like(ys_ref)


def _experts(block_e, n_active, xs, exp_gate, exp_up, exp_down):
    n_rows, dp = xs.shape
    _, d, de = exp_gate.shape
    nblk = n_rows // ROUTE_BLOCK
    idx = jnp.arange(nblk, dtype=jnp.int32)
    valid = idx < n_active[0]
    first = valid & ((idx == 0) | (block_e != jnp.roll(block_e, 1)))
    slot = jnp.maximum(jnp.cumsum(first.astype(jnp.int32)) - 1, 0) % 2
    first_at = jnp.where(first, idx, nblk)
    next_first = jnp.concatenate([lax.cummin(first_at, reverse=True)[1:], jnp.full((1,), nblk, jnp.int32)])
    nxt = jnp.where(next_first < nblk, block_e[jnp.minimum(next_first, nblk - 1)], -1)
    sched = jnp.stack([block_e, slot, first.astype(jnp.int32), nxt]).astype(jnp.int32)

    blk_row = lambda i, sc, na: (jnp.minimum(i, jnp.maximum(na[0] - 1, 0)), 0)
    return pl.pallas_call(
        _expert_kernel,
        out_shape=jax.ShapeDtypeStruct((n_rows, dp), jnp.uint32),
        grid_spec=pltpu.PrefetchScalarGridSpec(
            num_scalar_prefetch=2,
            grid=(nblk,),
            in_specs=[pl.BlockSpec((ROUTE_BLOCK, dp), blk_row),
                      pl.BlockSpec(memory_space=pl.ANY),
                      pl.BlockSpec(memory_space=pl.ANY),
                      pl.BlockSpec(memory_space=pl.ANY)],
            out_specs=pl.BlockSpec((ROUTE_BLOCK, dp), lambda i, sc, na: (i, 0)),
            scratch_shapes=[pltpu.VMEM((2, d, de), exp_gate.dtype),
                            pltpu.VMEM((2, d, de), exp_up.dtype),
                            pltpu.VMEM((2, de, d), exp_down.dtype),
                            pltpu.SemaphoreType.DMA((2, 3))]),
        compiler_params=pltpu.CompilerParams(dimension_semantics=("arbitrary",),
                                             vmem_limit_bytes=VMEM_LIMIT),
        name="experts",
    )(sched, n_active, xs, exp_gate, exp_up, exp_down)


def _combine_kernel(dest_hbm, ys_hbm, x1_ref, meta_ref, mod_ref, fw_ref, o_ref,
                    idx_smem, buf_ref, sem_idx, sem_rows):
    i = pl.program_id(0)
    nsteps = pl.num_programs(0)
    tj = x1_ref.shape[0]
    slot = i % 2

    def gather(step, sl):
        cp = pltpu.make_async_copy(dest_hbm.at[step], idx_smem.at[sl], sem_idx)
        cp.start()
        cp.wait()

        def body(t, carry):
            for s in range(2):
                d = idx_smem[sl, 2 * t + s]
                pltpu.make_async_copy(ys_hbm.at[pl.ds(d, 1)], buf_ref.at[sl, s, pl.ds(t, 1)],
                                      sem_rows.at[sl]).start()
            return carry

        lax.fori_loop(0, tj, body, 0, unroll=8)

    @pl.when(i == 0)
    def _():
        gather(0, 0)

    @pl.when(i + 1 < nsteps)
    def _():
        gather(i + 1, 1 - slot)

    pltpu.make_async_copy(buf_ref.at[slot], buf_ref.at[slot], sem_rows.at[slot]).wait()

    meta = meta_ref[...]
    y = (meta[:, 2:3] * _unpack_bf16_pair(buf_ref[slot, 0])
         + meta[:, 3:4] * _unpack_bf16_pair(buf_ref[slot, 1]))
    x2 = x1_ref[...] + mod_ref[0, 5:6, :] * y
    ms = jnp.mean(x2 * x2, axis=-1, keepdims=True)
    o_ref[...] = x2 * lax.rsqrt(ms + RMS_EPS) * fw_ref[...]


def _combine(dest, ys, x1, meta, mod3, final_w, t_len):
    n, d = x1.shape
    tj = min(256, t_len)
    return pl.pallas_call(
        _combine_kernel,
        out_shape=jax.ShapeDtypeStruct((n, d), F32),
        grid=(n // tj,),
        in_specs=[pl.BlockSpec(memory_space=pl.ANY),
                  pl.BlockSpec(memory_space=pl.ANY),
                  pl.BlockSpec((tj, d), lambda i: (i, 0)),
                  pl.BlockSpec((tj, LANES), lambda i: (i, 0)),
                  pl.BlockSpec((1, 6, d), lambda i: (i * tj // t_len, 0, 0)),
                  pl.BlockSpec((1, d), lambda i: (0, 0))],
        out_specs=pl.BlockSpec((tj, d), lambda i: (i, 0)),
        scratch_shapes=[pltpu.SMEM((2, 2 * tj), jnp.int32), pltpu.VMEM((2, 2, tj, d // 2), jnp.uint32),
                        pltpu.SemaphoreType.DMA, pltpu.SemaphoreType.DMA((2,))],
        compiler_params=pltpu.CompilerParams(dimension_semantics=("arbitrary",),
                                             vmem_limit_bytes=VMEM_LIMIT),
        name="combine",
    )(dest.reshape(n // tj, 2 * tj), ys, x1, meta, mod3, final_w.reshape(1, d))


def _pad_cols(w, width):
    return jnp.pad(w, ((0, 0), (0, width - w.shape[1])))


def _pad_rows(w, height):
    return jnp.pad(w, ((0, height - w.shape[0]), (0, 0)))


def _layer(x2, c, bsz, t_len, w_ada, b_ada, norm1_w, w_in, mu_r, mu_k, mu_v, mu_w, mu_a, mu_g, w0,
           w_decay1, w_decay2, a0, w_aaa1, w_aaa2, w_gate1, w_gate2, k_k, k_a, r_k, lnx_w, lnx_b,
           conv_w, conv_norm_w, w_out, norm2_w, w_grp, b_grp, w_exp, b_exp, exp_gate, exp_up,
           exp_down, final_w):
    n, d = x2.shape
    dr = w0.shape[0]
    assert w_decay1.shape[1] <= 128 and w_aaa1.shape[1] <= 128 and w_gate1.shape[1] <= 256
    assert dr % GROUP_LANES == 0 and conv_w.shape[1] == dr and w_in.shape[1] == 6 * dr
    assert t_len % WKV_CHUNK == 0 and w_grp.shape[1] == N_ROUTE_GROUPS and w_exp.shape[1] == N_EXPERTS

    mod3 = _ada(c, w_ada, b_ada).reshape(bsz, 6, d)
    h, proj = _inproj(x2, mod3, norm1_w, w_in.astype(BF16), t_len)

    w1p = jnp.concatenate([_pad_cols(w_decay1, 128), _pad_cols(w_aaa1, 128), _pad_cols(w_gate1, 256)], axis=1)
    mup = jnp.concatenate([jnp.broadcast_to(mu_w[:, None], (d, 128)),
                           jnp.broadcast_to(mu_a[:, None], (d, 128)),
                           jnp.broadcast_to(mu_g[:, None], (d, 256))], axis=1)
    lw, ag, gg = _lora(h, w1p, mup, _pad_rows(w_decay2, 128), _pad_rows(w_aaa2, 128),
                       _pad_rows(w_gate2, 256), w0, a0, bsz, t_len)

    yr = _wkv(proj, lw, ag, gg, mu_r, mu_k, mu_v, k_k, k_a, r_k.reshape(dr), lnx_w, lnx_b, bsz, t_len)

    wr = _pad_cols(jnp.concatenate([w_grp, w_exp], axis=1), LANES)
    br = _pad_cols(jnp.concatenate([b_grp, b_exp]).reshape(1, -1), LANES)
    x1, h2, meta, cnt = _outproj(yr, proj, conv_w, conv_norm_w, w_out.astype(BF16), x2, mod3,
                                 norm2_w, wr, br, bsz, t_len)

    counts = cnt[0, :N_EXPERTS].astype(jnp.int32)
    padded = (counts + ROUTE_BLOCK - 1) // ROUTE_BLOCK * ROUTE_BLOCK
    pad_ends = jnp.cumsum(padded)
    pad_starts = pad_ends - padded
    top_e = meta[:, 0:2].astype(jnp.int32)
    rank = meta[:, 4:6].astype(jnp.int32)
    dest = (pad_starts[top_e] + rank).reshape(-1)
    nblk = (2 * n) // ROUTE_BLOCK + N_EXPERTS
    n_rows = nblk * ROUTE_BLOCK
    block_start = jnp.arange(nblk, dtype=jnp.int32) * ROUTE_BLOCK
    block_e = jnp.minimum(jnp.sum(pad_ends[None, :] <= block_start[:, None], axis=1),
                          N_EXPERTS - 1).astype(jnp.int32)
    n_active = (pad_ends[-1:] // ROUTE_BLOCK).astype(jnp.int32)

    pad_len = padded - counts
    zero_off = jnp.concatenate([pad_starts + counts, pad_len, jnp.sum(pad_len, keepdims=True),
                                n_active]).astype(jnp.int32)
    xs = _dispatch(dest, zero_off, h2, n_rows)
    ys = _experts(block_e, n_active, xs, exp_gate, exp_up, exp_down)
    return _combine(dest, ys, x1, meta, mod3, final_w, t_len)


def kernel(x, c, w_ada, b_ada, norm1_w, w_in, mu_r, mu_k, mu_v, mu_w, mu_a, mu_g, w0, w_decay1, w_decay2, a0, w_aaa1, w_aaa2, w_gate1, w_gate2, k_k, k_a, r_k, lnx_w, lnx_b, conv_w, conv_norm_w, w_out, norm2_w, w_grp, b_grp, w_exp, b_exp, exp_gate, exp_up, exp_down, final_w):
    bsz, t_len, d = x.shape
    assert w_ada.shape[0] == 1, "single-layer block"
    out = _layer(x.reshape(bsz * t_len, d), c, bsz, t_len, w_ada[0], b_ada[0], norm1_w[0], w_in[0],
                 mu_r[0], mu_k[0], mu_v[0], mu_w[0], mu_a[0], mu_g[0], w0[0], w_decay1[0], w_decay2[0],
                 a0[0], w_aaa1[0], w_aaa2[0], w_gate1[0], w_gate2[0], k_k[0], k_a[0], r_k[0], lnx_w[0],
                 lnx_b[0], conv_w[0], conv_norm_w[0], w_out[0], norm2_w[0], w_grp[0], b_grp[0],
                 w_exp[0], b_exp[0], exp_gate[0], exp_up[0], exp_down[0], final_w)
    return out.reshape(bsz, t_len, d)
```

```python
import functools

import jax
import jax.numpy as jnp
from jax import lax
from jax.experimental import pallas as pl
from jax.experimental.pallas import tpu as pltpu

F32 = jnp.float32
BF16 = jnp.bfloat16

HEAD_DIM = 64
WKV_CHUNK = 64
GROUP_LANES = 256
N_ROUTE_GROUPS = 4
EXPERTS_PER_GROUP = 8
N_EXPERTS = N_ROUTE_GROUPS * EXPERTS_PER_GROUP
CONV_GROUP_DIM = 64
RMS_EPS = 1e-6
LNX_EPS = 64e-5
LANES = 128
LORA_PAD = 512
ROUTE_BLOCK = 256
VMEM_LIMIT = 56 * 1024 * 1024


def _dot(a, b, precision=None):
    return jnp.dot(a, b, preferred_element_type=F32, precision=precision)


def _dot_nt(a, b):
    return lax.dot_general(a, b, (((1,), (1,)), ((), ())), preferred_element_type=F32)


def _dot_tn(a, b):
    return lax.dot_general(a, b, (((0,), (0,)), ((), ())), preferred_element_type=F32)


def _split_bf16(x):
    hi = x.astype(BF16)
    return hi, (x - hi.astype(F32)).astype(BF16)


def _dot_x3(a, b):
    ah, al = _split_bf16(a)
    bh, bl = _split_bf16(b)
    return _dot(ah, bh) + _dot(al, bh) + _dot(ah, bl)


def _pack_bf16_pair(x):
    c = x.shape[1] // 2
    lo = pltpu.bitcast(x[:, :c].astype(BF16).astype(F32), jnp.uint32)
    hi = pltpu.bitcast(x[:, c:].astype(BF16).astype(F32), jnp.uint32)
    return (lo >> 16) | (hi & jnp.uint32(0xFFFF0000))


def _unpack_bf16_pair(w):
    lo = pltpu.bitcast(w << 16, F32)
    hi = pltpu.bitcast(w & jnp.uint32(0xFFFF0000), F32)
    return jnp.concatenate([lo, hi], axis=1)


def _sigmoid(x):
    return 1.0 / (1.0 + jnp.exp(-x))


def _shift_rows(x, carry_row):
    rolled = pltpu.roll(x, 1, axis=0)
    row = lax.broadcasted_iota(jnp.int32, x.shape, 0)
    return jnp.where(row == 0, carry_row, rolled)


def _seg_ones(n, seg):
    r = lax.broadcasted_iota(jnp.int32, (n, n), 0) // seg
    c = lax.broadcasted_iota(jnp.int32, (n, n), 1) // seg
    return jnp.where(r == c, 1.0, 0.0).astype(BF16)


def _seg_sum(x, ones_bd):
    hi = x.astype(BF16)
    lo = (x - hi.astype(F32)).astype(BF16)
    return _dot(hi, ones_bd) + _dot(lo, ones_bd)


def _seg_sum1(x, ones_bd):
    return _dot(x.astype(BF16), ones_bd)


def _ada_kernel(cb_ref, w_ref, b_ref, o_ref):
    nb = cb_ref.shape[0]
    tn = w_ref.shape[1]
    for b in range(nb):
        cv = cb_ref[b]
        s = cv * _sigmoid(cv)
        cols = [jnp.sum(s * w_ref[:, j * LANES:(j + 1) * LANES], axis=0, keepdims=True)
                for j in range(tn // LANES)]
        o_ref[b:b + 1, :] = jnp.concatenate(cols, axis=1) + b_ref[...]


def _ada(c, w_ada, b_ada):
    nb, d = c.shape
    n_out = w_ada.shape[1]
    tn = 512
    cb = jnp.broadcast_to(c[:, :, None], (nb, d, LANES))
    return pl.pallas_call(
        _ada_kernel,
        out_shape=jax.ShapeDtypeStruct((nb, n_out), F32),
        grid=(n_out // tn,),
        in_specs=[pl.BlockSpec((nb, d, LANES), lambda j: (0, 0, 0)),
                  pl.BlockSpec((d, tn), lambda j: (0, j)),
                  pl.BlockSpec((1, tn), lambda j: (0, j))],
        out_specs=pl.BlockSpec((nb, tn), lambda j: (0, j)),
        compiler_params=pltpu.CompilerParams(dimension_semantics=("arbitrary",),
                                             vmem_limit_bytes=VMEM_LIMIT),
        name="ada",
    )(cb, w_ada, b_ada.reshape(1, n_out))


def _inproj_kernel(x_ref, mod_ref, nw_ref, w_ref, h_ref, p_ref, hs_ref):
    @pl.when(pl.program_id(1) == 0)
    def _():
        x = x_ref[...]
        ms = jnp.mean(x * x, axis=-1, keepdims=True)
        y = x * lax.rsqrt(ms + RMS_EPS) * nw_ref[...]
        h = y * (1.0 + mod_ref[0, 1:2, :]) + mod_ref[0, 0:1, :]
        hb = h.astype(BF16)
        hs_ref[...] = hb
        h_ref[...] = hb

    p_ref[...] = _dot(hs_ref[...], w_ref[...])


def _inproj(x2, mod3, norm_w, w_in_bf, t_len):
    n, d = x2.shape
    n_in = w_in_bf.shape[1]
    tm = min(1024, t_len)
    tn = 1024
    return pl.pallas_call(
        _inproj_kernel,
        out_shape=(jax.ShapeDtypeStruct((n, d), BF16), jax.ShapeDtypeStruct((n, n_in), F32)),
        grid=(n // tm, n_in // tn),
        in_specs=[pl.BlockSpec((tm, d), lambda i, j: (i, 0)),
                  pl.BlockSpec((1, 6, d), lambda i, j: (i * tm // t_len, 0, 0)),
                  pl.BlockSpec((1, d), lambda i, j: (0, 0)),
                  pl.BlockSpec((d, tn), lambda i, j: (0, j))],
        out_specs=(pl.BlockSpec((tm, d), lambda i, j: (i, 0)),
                   pl.BlockSpec((tm, tn), lambda i, j: (i, j))),
        scratch_shapes=[pltpu.VMEM((tm, d), BF16)],
        compiler_params=pltpu.CompilerParams(dimension_semantics=("arbitrary", "arbitrary"),
                                             vmem_limit_bytes=VMEM_LIMIT),
        name="inproj",
    )(x2, mod3, norm_w.reshape(1, d), w_in_bf)


def _lora_kernel(h_ref, w1_ref, mu_ref, wd2_ref, wa2_ref, wg2_ref, w0_ref, a0_ref,
                 lw_ref, ag_ref, gg_ref, carry_ref):
    @pl.when(pl.program_id(1) == 0)
    def _():
        carry_ref[...] = jnp.zeros_like(carry_ref)

    h = h_ref[...]
    w1 = w1_ref[...]
    p1 = _dot(h, w1.astype(BF16))
    p2 = _dot(h, (w1 * mu_ref[...]).astype(BF16))
    prev = _shift_rows(p2, carry_ref[...])
    carry_ref[...] = p2[p2.shape[0] - 1:, :]
    lo = p1 - p2 + prev
    dec = _dot(jnp.tanh(lo[:, 0:128]).astype(BF16), wd2_ref[...].astype(BF16))
    z = -(w0_ref[...] + dec)
    softplus = jnp.maximum(z, 0.0) + jnp.log(1.0 + jnp.exp(-jnp.abs(z)))
    lw_ref[...] = -jnp.exp(-softplus - 0.5)
    ag_ref[...] = _sigmoid(a0_ref[...] + _dot(lo[:, 128:256].astype(BF16), wa2_ref[...].astype(BF16)))
    gg_ref[...] = _dot(_sigmoid(lo[:, 256:512]).astype(BF16), wg2_ref[...].astype(BF16))


def _lora(h, w1p, mup, wd2p, wa2p, wg2p, w0, a0, bsz, t_len):
    n, d = h.shape
    dr = w0.shape[-1]
    tm = min(512, t_len)
    nt = t_len // tm
    row = lambda b, t: (b * nt + t, 0)
    const = lambda b, t: (0, 0)
    out = jax.ShapeDtypeStruct((n, dr), F32)
    return pl.pallas_call(
        _lora_kernel,
        out_shape=(out, out, out),
        grid=(bsz, nt),
        in_specs=[pl.BlockSpec((tm, d), row),
                  pl.BlockSpec((d, LORA_PAD), const),
                  pl.BlockSpec((d, LORA_PAD), const),
                  pl.BlockSpec((128, dr), const),
                  pl.BlockSpec((128, dr), const),
                  pl.BlockSpec((256, dr), const),
                  pl.BlockSpec((1, dr), const),
                  pl.BlockSpec((1, dr), const)],
        out_specs=(pl.BlockSpec((tm, dr), row),) * 3,
        scratch_shapes=[pltpu.VMEM((1, LORA_PAD), F32)],
        compiler_params=pltpu.CompilerParams(dimension_semantics=("arbitrary", "arbitrary"),
                                             vmem_limit_bytes=VMEM_LIMIT),
        name="lora",
    )(h, w1p, mup, wd2p, wa2p, wg2p, w0.reshape(1, dr), a0.reshape(1, dr))


def _wkv_kernel(r_ref, k_ref, v_ref, lw_ref, ag_ref, gg_ref, mur_ref, muk_ref, muv_ref,
                kk_ref, ka_ref, rk_ref, lnw_ref, lnb_ref, y_ref, s_ref, carry_ref):
    lt = r_ref.shape[0]
    gl = GROUP_LANES
    c_len = WKV_CHUNK
    heads = gl // HEAD_DIM

    @pl.when(pl.program_id(2) == 0)
    def _():
        s_ref[...] = jnp.zeros_like(s_ref)
        carry_ref[...] = jnp.zeros_like(carry_ref)

    ones_bd = _seg_ones(gl, HEAD_DIM)

    def lerp(ref, mu_ref, idx):
        z = ref[...]
        prev = _shift_rows(z, carry_ref[idx:idx + 1, :])
        carry_ref[idx:idx + 1, :] = z[lt - 1:, :]
        return z + (prev - z) * mu_ref[...]

    r1 = lerp(r_ref, mur_ref, 0)
    k1 = lerp(k_ref, muk_ref, 1)
    v1 = lerp(v_ref, muv_ref, 2)
    ag = ag_ref[...]
    kk = k1 * kk_ref[...]
    kk = kk * lax.rsqrt(jnp.maximum(_seg_sum1(kk * kk, ones_bd), 1e-24))
    k2 = k1 * (1.0 + (ag - 1.0) * ka_ref[...])
    a_all = -kk
    b_all = kk * ag
    lw = lw_ref[...]

    rows = lax.broadcasted_iota(jnp.int32, (gl, gl), 0)
    cols = lax.broadcasted_iota(jnp.int32, (gl, gl), 1)
    same_head = (rows // c_len) == (cols // HEAD_DIM)
    ts = lax.broadcasted_iota(jnp.int32, (c_len, gl), 0)
    tj = lax.broadcasted_iota(jnp.int32, (c_len, gl), 1) % c_len
    tri_strict = ts > tj
    tri_incl = ts >= tj
    eye = jnp.where(ts == tj, 1.0, 0.0).astype(F32)
    cr = lax.broadcasted_iota(jnp.int32, (c_len, c_len), 0)
    cc = lax.broadcasted_iota(jnp.int32, (c_len, c_len), 1)
    cum_b = jnp.where(cr >= cc, 1.0, 0.0).astype(BF16)

    def bd(x):
        xb = x.astype(BF16)
        return jnp.where(same_head, jnp.concatenate([xb] * heads, axis=0), jnp.zeros((), BF16))

    def fold(x):
        out = x[0:c_len]
        for hh in range(1, heads):
            out = out + x[hh * c_len:(hh + 1) * c_len]
        return out

    nch = lt // c_len
    chunks = range(nch)
    sls = [slice(c * c_len, (c + 1) * c_len) for c in chunks]
    lws = [lw[sl] for sl in sls]

    def cumsum3(x):
        hi = x.astype(BF16)
        r = x - hi.astype(F32)
        mid = r.astype(BF16)
        lo = (r - mid.astype(F32)).astype(BF16)
        return _dot(cum_b, hi) + _dot(cum_b, mid) + _dot(cum_b, lo)

    cs = [cumsum3(x) for x in lws]
    e_cs = [jnp.exp(x) for x in cs]
    e_neg = [jnp.exp(-x) for x in cs]
    wc = [x[c_len - 1:, :] for x in e_cs]
    at = [a_all[sls[c]] * jnp.exp(cs[c] - lws[c]) for c in chunks]
    rt = [r1[sls[c]] * e_cs[c] for c in chunks]
    bt = [b_all[sls[c]] * e_neg[c] for c in chunks]
    kt = [k2[sls[c]] * e_neg[c] for c in chunks]
    vc = [v1[sl] for sl in sls]

    lhs = [jnp.concatenate([at[c], rt[c]], axis=0).astype(BF16) for c in chunks]
    g_b = [_dot_nt(lhs[c], bd(bt[c])) for c in chunks]
    g_k = [_dot_nt(lhs[c], bd(kt[c])) for c in chunks]
    n_ab = [jnp.where(tri_strict, g[:c_len], 0.0) for g in g_b]
    a_ak = [jnp.where(tri_strict, g[:c_len], 0.0) for g in g_k]
    a_rb = [jnp.where(tri_incl, g[c_len:], 0.0).astype(BF16) for g in g_b]
    a_rk = [jnp.where(tri_incl, g[c_len:], 0.0).astype(BF16) for g in g_k]

    p = [eye + jnp.where((ts // 2) == (tj // 2), x, 0.0) for x in n_ab]
    for half in (2, 4, 8, 16, 32):
        lower_left = (((ts // (2 * half)) == (tj // (2 * half)))
                      & ((ts % (2 * half)) >= half) & ((tj % (2 * half)) < half))
        n21_ta = [_dot(jnp.where(lower_left, n_ab[c], 0.0).astype(BF16), bd(p[c])) for c in chunks]
        p = [p[c] + _dot(p[c].astype(BF16), bd(n21_ta[c])) for c in chunks]
    tb = [x.astype(BF16) for x in p]

    z = [_dot(tb[c], jnp.concatenate([bd(a_ak[c]), bd(at[c])], axis=1)) for c in chunks]
    ap = [x[:, gl:] for x in z]
    bd_v = [bd(x) for x in vc]
    u0 = [_dot(z[c][:, :gl].astype(BF16), bd_v[c]) for c in chunks]
    w = [_dot(a_rb[c], jnp.concatenate([bd(ap[c]), bd(u0[c])], axis=1)) for c in chunks]
    rp = [rt[c] + w[c][:, :gl] for c in chunks]
    y0 = [w[c][:, gl:] + _dot(a_rk[c], bd_v[c]) for c in chunks]
    bh = [(bt[c] * wc[c]).astype(BF16) for c in chunks]
    bk = [jnp.concatenate([bh[c], (kt[c] * wc[c]).astype(BF16)], axis=0) for c in chunks]
    g_s = [jnp.where(same_head, _dot_tn(ap[c].astype(BF16), bh[c]), 0.0).astype(BF16) for c in chunks]
    n_s = [fold(jnp.where(same_head, _dot_tn(jnp.concatenate([u0[c], vc[c]], axis=0).astype(BF16), bk[c]), 0.0))
           for c in chunks]

    s = s_ref[...]
    s_in = []
    for c in chunks:
        s_in.append(s.astype(BF16))
        s = s * wc[c] + _dot(s_in[c], g_s[c]) + n_s[c]
    s_ref[...] = s
    y_chunks = [_dot_nt(rp[c].astype(BF16), bd(s_in[c])) + y0[c] for c in chunks]

    y = jnp.concatenate(y_chunks, axis=0) if len(y_chunks) > 1 else y_chunks[0]
    inv_n = 1.0 / HEAD_DIM
    mean = _seg_sum(y, ones_bd) * inv_n
    yc = y - mean
    var = _seg_sum1(yc * yc, ones_bd) * inv_n
    yn = yc * lax.rsqrt(var + LNX_EPS) * lnw_ref[...] + lnb_ref[...]
    bonus = _seg_sum(r1 * k2 * rk_ref[...], ones_bd) * v1
    y_ref[...] = ((yn + bonus) * gg_ref[...]).astype(y_ref.dtype)


def _wkv(proj, lw, ag, gg, mu_r, mu_k, mu_v, k_k, k_a, r_k, lnx_w, lnx_b, bsz, t_len):
    n = proj.shape[0]
    dr = lw.shape[1]
    gl = GROUP_LANES
    ng = dr // gl
    lt = min(512, t_len)
    nt = t_len // lt
    row = lambda off: (lambda b, g, t: (b * nt + t, off + g))
    par = lambda b, g, t: (0, g)
    vec = lambda a: a.reshape(1, dr)
    return pl.pallas_call(
        _wkv_kernel,
        out_shape=jax.ShapeDtypeStruct((n, dr), BF16),
        grid=(bsz, ng, nt),
        in_specs=[pl.BlockSpec((lt, gl), row(0)),
                  pl.BlockSpec((lt, gl), row(ng)),
                  pl.BlockSpec((lt, gl), row(2 * ng)),
                  pl.BlockSpec((lt, gl), row(0)),
                  pl.BlockSpec((lt, gl), row(0)),
                  pl.BlockSpec((lt, gl), row(0))] + [pl.BlockSpec((1, gl), par)] * 8,
        out_specs=pl.BlockSpec((lt, gl), row(0)),
        scratch_shapes=[pltpu.VMEM((HEAD_DIM, gl), F32), pltpu.VMEM((8, gl), F32)],
        compiler_params=pltpu.CompilerParams(
            dimension_semantics=("arbitrary", "arbitrary", "arbitrary"),
            vmem_limit_bytes=VMEM_LIMIT),
        name="wkv",
    )(proj, proj, proj, lw, ag, gg, vec(mu_r), vec(mu_k), vec(mu_v), vec(k_k), vec(k_a),
      vec(r_k), vec(lnx_w), vec(lnx_b))


def _outproj_kernel(yr_ref, u_ref, gb_ref, gc_ref, cw_ref, cnw_ref, wo_ref, x_ref, mod_ref,
                    n2w_ref, wr_ref, br_ref, x1_ref, h2_ref, meta_ref, cnt_ref,
                    ccarry_ref, cnt_acc_ref):
    tm = x_ref.shape[0]
    dc = u_ref.shape[1]

    @pl.when(pl.program_id(1) == 0)
    def _():
        ccarry_ref[...] = jnp.zeros_like(ccarry_ref)

    @pl.when((pl.program_id(0) == 0) & (pl.program_id(1) == 0))
    def _():
        cnt_acc_ref[...] = jnp.zeros_like(cnt_acc_ref)

    pre = gc_ref[...] * u_ref[...]
    prev1 = _shift_rows(pre, ccarry_ref[0:1, :])
    prev2 = _shift_rows(prev1, ccarry_ref[1:2, :])
    ccarry_ref[0:1, :] = pre[tm - 1:, :]
    ccarry_ref[1:2, :] = pre[tm - 2:tm - 1, :]
    yc = gb_ref[...] * (prev2 * cw_ref[0:1, :] + prev1 * cw_ref[1:2, :] + pre * cw_ref[2:3, :])
    ones_bd = _seg_ones(GROUP_LANES, CONV_GROUP_DIM)
    ycs = []
    for s in range(dc // GROUP_LANES):
        ys = yc[:, s * GROUP_LANES:(s + 1) * GROUP_LANES]
        ms = _seg_sum1(ys * ys, ones_bd) * (1.0 / CONV_GROUP_DIM)
        ycs.append(ys * lax.rsqrt(ms + RMS_EPS))
    y_conv = (jnp.concatenate(ycs, axis=1) * cnw_ref[...]).astype(BF16)

    dr = yr_ref.shape[1]
    y_mix = _dot(yr_ref[...], wo_ref[0:dr, :]) + _dot(y_conv, wo_ref[dr:, :])
    x1 = x_ref[...] + mod_ref[0, 2:3, :] * y_mix
    x1_ref[...] = x1
    ms = jnp.mean(x1 * x1, axis=-1, keepdims=True)
    h2 = x1 * lax.rsqrt(ms + RMS_EPS) * n2w_ref[...]
    h2 = h2 * (1.0 + mod_ref[0, 4:5, :]) + mod_ref[0, 3:4, :]
    h2_ref[...] = _pack_bf16_pair(h2)

    logits = _dot_x3(h2, wr_ref[...]) + br_ref[...]
    lane = lax.broadcasted_iota(jnp.int32, logits.shape, 1)
    neg = jnp.float32(-jnp.inf)
    big = jnp.int32(1 << 20)
    glog = jnp.where(lane < N_ROUTE_GROUPS, logits, neg)
    gmax = jnp.max(glog, axis=-1, keepdims=True)
    grp = jnp.min(jnp.where(glog == gmax, lane, big), axis=-1, keepdims=True)
    p_grp = 1.0 / jnp.sum(jnp.exp(glog - gmax), axis=-1, keepdims=True)
    e_lane = lane - N_ROUTE_GROUPS
    in_grp = (e_lane >= grp * EXPERTS_PER_GROUP) & (e_lane < (grp + 1) * EXPERTS_PER_GROUP)
    elog = jnp.where(in_grp, logits, neg)
    m1 = jnp.max(elog, axis=-1, keepdims=True)
    i1 = jnp.min(jnp.where(elog == m1, lane, big), axis=-1, keepdims=True)
    elog2 = jnp.where(lane == i1, neg, elog)
    m2 = jnp.max(elog2, axis=-1, keepdims=True)
    i2 = jnp.min(jnp.where(elog2 == m2, lane, big), axis=-1, keepdims=True)
    t2 = jnp.exp(m2 - m1)
    w1 = p_grp / (1.0 + t2)
    w2 = p_grp * t2 / (1.0 + t2)
    e1 = i1 - N_ROUTE_GROUPS
    e2 = i2 - N_ROUTE_GROUPS

    oh1 = lane == e1
    oh2 = lane == e2
    oh = jnp.where(oh1 | oh2, 1.0, 0.0)
    tr = lax.broadcasted_iota(jnp.int32, (tm, tm), 0)
    tc = lax.broadcasted_iota(jnp.int32, (tm, tm), 1)
    lower = jnp.where(tr > tc, 1.0, 0.0).astype(BF16)
    before = _dot(lower, oh.astype(BF16)) + cnt_acc_ref[...]
    rank1 = jnp.sum(jnp.where(oh1, before, 0.0), axis=-1, keepdims=True)
    rank2 = jnp.sum(jnp.where(oh2, before, 0.0), axis=-1, keepdims=True)
    cnt = cnt_acc_ref[...] + jnp.sum(oh, axis=0, keepdims=True)
    cnt_acc_ref[...] = cnt
    cnt_ref[...] = jnp.broadcast_to(cnt, cnt_ref.shape)

    meta = jnp.where(lane == 0, e1.astype(F32), 0.0)
    meta = jnp.where(lane == 1, e2.astype(F32), meta)
    meta = jnp.where(lane == 2, w1, meta)
    meta = jnp.where(lane == 3, w2, meta)
    meta = jnp.where(lane == 4, rank1, meta)
    meta = jnp.where(lane == 5, rank2, meta)
    meta_ref[...] = meta


def _outproj(yr, proj, conv_w, conv_norm_w, w_out_bf, x2, mod3, norm2_w, wr, br, bsz, t_len):
    n, d = x2.shape
    dr = yr.shape[1]
    dc = conv_w.shape[1]
    tm = min(512, t_len)
    nt = t_len // tm
    nseg = dr // dc
    row = lambda off: (lambda b, t: (b * nt + t, off))
    const = lambda b, t: (0, 0)
    return pl.pallas_call(
        _outproj_kernel,
        out_shape=(jax.ShapeDtypeStruct((n, d), F32), jax.ShapeDtypeStruct((n, d // 2), jnp.uint32),
                   jax.ShapeDtypeStruct((n, LANES), F32), jax.ShapeDtypeStruct((8, LANES), F32)),
        grid=(bsz, nt),
        in_specs=[pl.BlockSpec((tm, dr), row(0)),
                  pl.BlockSpec((tm, dc), row(3 * nseg)),
                  pl.BlockSpec((tm, dc), row(4 * nseg)),
                  pl.BlockSpec((tm, dc), row(5 * nseg)),
                  pl.BlockSpec((3, dc), const),
                  pl.BlockSpec((1, dc), const),
                  pl.BlockSpec((dr + dc, d), const),
                  pl.BlockSpec((tm, d), row(0)),
                  pl.BlockSpec((1, 6, d), lambda b, t: (b, 0, 0)),
                  pl.BlockSpec((1, d), const),
                  pl.BlockSpec((d, LANES), const),
                  pl.BlockSpec((1, LANES), const)],
        out_specs=(pl.BlockSpec((tm, d), row(0)), pl.BlockSpec((tm, d // 2), row(0)),
                   pl.BlockSpec((tm, LANES), row(0)), pl.BlockSpec((8, LANES), const)),
        scratch_shapes=[pltpu.VMEM((8, dc), F32), pltpu.VMEM((1, LANES), F32)],
        compiler_params=pltpu.CompilerParams(dimension_semantics=("arbitrary", "arbitrary"),
                                             vmem_limit_bytes=VMEM_LIMIT),
        name="outproj",
    )(yr, proj, proj, proj, conv_w, conv_norm_w.reshape(1, dc), w_out_bf, x2, mod3,
      norm2_w.reshape(1, d), wr, br)


def _dispatch_kernel(zoff_ref, dest_hbm, h2_ref, xs_hbm, idx_smem, zero_ref, zblk_ref,
                     sem_idx, sem_rows, sem_zero, sem_tail):
    i = pl.program_id(0)
    nsteps = pl.num_programs(0)
    th = idx_smem.shape[1] // 2
    slot = i % 2

    def idx_copy(step, sl):
        return pltpu.make_async_copy(dest_hbm.at[step], idx_smem.at[sl], sem_idx.at[sl])

    def rows_wait():
        pltpu.make_async_copy(xs_hbm.at[pl.ds(0, 2 * th)], xs_hbm.at[pl.ds(0, 2 * th)], sem_rows).wait()

    nblk = xs_hbm.shape[0] // ROUTE_BLOCK

    def tail_copy(b):
        start = pl.multiple_of(b * ROUTE_BLOCK, ROUTE_BLOCK)
        return pltpu.make_async_copy(zblk_ref, xs_hbm.at[pl.ds(start, ROUTE_BLOCK)], sem_tail)

    @pl.when(i == 0)
    def _():
        idx_copy(0, 0).start()
        zero_ref[...] = jnp.zeros_like(zero_ref)

        def zero_expert(e, carry):
            off = zoff_ref[e]

            def zero_row(j, c):
                pltpu.make_async_copy(zero_ref, xs_hbm.at[pl.ds(off + j, 1)], sem_zero).start()
                return c

            return lax.fori_loop(0, zoff_ref[N_EXPERTS + e], zero_row, carry)

        lax.fori_loop(0, N_EXPERTS, zero_expert, 0)
        total = zoff_ref[2 * N_EXPERTS]
        bulk = pl.multiple_of((total // 8) * 8, 8)

        @pl.when(bulk > 0)
        def _():
            pltpu.make_async_copy(xs_hbm.at[pl.ds(0, bulk)], xs_hbm.at[pl.ds(0, bulk)], sem_zero).wait()

        def wait_row(j, c):
            pltpu.make_async_copy(zero_ref, xs_hbm.at[pl.ds(0, 1)], sem_zero).wait()
            return c

        lax.fori_loop(0, total - bulk, wait_row, 0)

        zblk_ref[...] = jnp.zeros_like(zblk_ref)

        def tail_start(b, c):
            tail_copy(b).start()
            return c

        lax.fori_loop(zoff_ref[2 * N_EXPERTS + 1], nblk, tail_start, 0)

    idx_copy(i, slot).wait()

    @pl.when(i + 1 < nsteps)
    def _():
        idx_copy(i + 1, 1 - slot).start()

    def body(t, carry):
        for s in range(2):
            d = idx_smem[slot, 2 * t + s]
            pltpu.make_async_copy(h2_ref.at[pl.ds(t, 1)], xs_hbm.at[pl.ds(d, 1)], sem_rows).start()
        return carry

    lax.fori_loop(0, th, body, 0, unroll=8)
    rows_wait()

    @pl.when(i == nsteps - 1)
    def _():
        def tail_wait(b, c):
            tail_copy(b).wait()
            return c

        lax.fori_loop(zoff_ref[2 * N_EXPERTS + 1], nblk, tail_wait, 0)


def _dispatch(dest, zero_off, h2, n_rows):
    n, d = h2.shape
    th = min(256, n)
    return pl.pallas_call(
        _dispatch_kernel,
        out_shape=jax.ShapeDtypeStruct((n_rows, d), h2.dtype),
        grid_spec=pltpu.PrefetchScalarGridSpec(
            num_scalar_prefetch=1,
            grid=(n // th,),
            in_specs=[pl.BlockSpec(memory_space=pl.ANY),
                      pl.BlockSpec((th, d), lambda i, zo: (i, 0))],
            out_specs=pl.BlockSpec(memory_space=pl.ANY),
            scratch_shapes=[pltpu.SMEM((2, 2 * th), jnp.int32),
                            pltpu.VMEM((1, d), h2.dtype),
                            pltpu.VMEM((ROUTE_BLOCK, d), h2.dtype),
                            pltpu.SemaphoreType.DMA((2,)), pltpu.SemaphoreType.DMA,
                            pltpu.SemaphoreType.DMA, pltpu.SemaphoreType.DMA]),
        compiler_params=pltpu.CompilerParams(dimension_semantics=("arbitrary",),
                                             vmem_limit_bytes=VMEM_LIMIT),
        name="dispatch",
    )(zero_off, dest.reshape(n // th, 2 * th), h2)


def _expert_kernel(sched_ref, na_ref, rowtok_hbm, h2_hbm, wg_hbm, wu_hbm, wd_hbm, ys_ref,
                   idx_smem, x_buf0, x_buf1, wg_buf, wu_buf, wd_buf, sem_w, sem_idx, sem_x):
    i = pl.program_id(0)
    na = na_ref[0]
    x_bufs = (x_buf0, x_buf1)

    def weight_copies(e, slot):
        return (pltpu.make_async_copy(wg_hbm.at[e], wg_buf.at[slot], sem_w.at[slot, 0]),
                pltpu.make_async_copy(wu_hbm.at[e], wu_buf.at[slot], sem_w.at[slot, 1]),
                pltpu.make_async_copy(wd_hbm.at[e], wd_buf.at[slot], sem_w.at[slot, 2]))

    def idx_copy(blk, p):
        return pltpu.make_async_copy(rowtok_hbm.at[blk], idx_smem.at[p], sem_idx.at[p])

    def gather_start(p):
        for t in range(ROUTE_BLOCK):
            tok = idx_smem[p, t]
            pltpu.make_async_copy(h2_hbm.at[pl.ds(tok, 1)], x_bufs[p].at[pl.ds(t, 1)], sem_x.at[p]).start()

    def gather_wait(p):
        pltpu.make_async_copy(x_bufs[p], x_bufs[p], sem_x.at[p]).wait()

    last = jnp.maximum(na - 1, 0)

    @pl.when(i == 0)
    def _():
        first = idx_copy(0, 0)
        first.start()
        first.wait()
        gather_start(0)
        idx_copy(jnp.minimum(1, last), 1).start()
        for cp in weight_copies(sched_ref[0, 0], sched_ref[1, 0]):
            cp.start()

    def active_step(p):
        e = sched_ref[0, i]
        slot = sched_ref[1, i]

        @pl.when(sched_ref[2, i] == 1)
        def _():
            for cp in weight_copies(e, slot):
                cp.wait()
            nxt = sched_ref[3, i]

            @pl.when(nxt >= 0)
            def _():
                for cp in weight_copies(nxt, 1 - slot):
                    cp.start()

        idx_copy(jnp.minimum(i + 1, last), 1 - p).wait()
        idx_copy(jnp.minimum(i + 2, last), p).start()
        gather_wait(p)
        gather_start(1 - p)
        x = _unpack_bf16_pair(x_bufs[p][...]).astype(BF16)
        g = _dot(x, wg_buf[slot].astype(BF16))
        u = _dot(x, wu_buf[slot].astype(BF16))
        hid = (g * _sigmoid(g) * u).astype(BF16)
        ys_ref[...] = _pack_bf16_pair(_dot(hid, wd_buf[slot].astype(BF16)))

    for p in range(2):
        @pl.when((i < na) & (i % 2 == p))
        def _(p=p):
            active_step(p)

        @pl.when((i == na) & (i % 2 == p))
        def _(p=p):
            idx_copy(last, 1 - p).wait()
            gather_wait(p)

    @pl.when(i >= na)
    def _():
        ys_ref[...] = jnp.zeros_like(ys_ref)


def _experts(block_e, n_active, row_tok, h2p, exp_gate, exp_up, exp_down):
    n_rows = row_tok.shape[0]
    dp = h2p.shape[1]
    _, d, de = exp_gate.shape
    nblk = n_rows // ROUTE_BLOCK
    idx = jnp.arange(nblk, dtype=jnp.int32)
    valid = idx < n_active[0]
    first = valid & ((idx == 0) | (block_e != jnp.roll(block_e, 1)))
    slot = jnp.maximum(jnp.cumsum(first.astype(jnp.int32)) - 1, 0) % 2
    first_at = jnp.where(first, idx, nblk)
    next_first = jnp.concatenate([lax.cummin(first_at, reverse=True)[1:], jnp.full((1,), nblk, jnp.int32)])
    nxt = jnp.where(next_first < nblk, block_e[jnp.minimum(next_first, nblk - 1)], -1)
    sched = jnp.stack([block_e, slot, first.astype(jnp.int32), nxt]).astype(jnp.int32)

    return pl.pallas_call(
        _expert_kernel,
        out_shape=jax.ShapeDtypeStruct((n_rows, dp), jnp.uint32),
        grid_spec=pltpu.PrefetchScalarGridSpec(
            num_scalar_prefetch=2,
            grid=(nblk,),
            in_specs=[pl.BlockSpec(memory_space=pl.ANY)] * 5,
            out_specs=pl.BlockSpec((ROUTE_BLOCK, dp), lambda i, sc, na: (i, 0)),
            scratch_shapes=[pltpu.SMEM((2, ROUTE_BLOCK), jnp.int32),
                            pltpu.VMEM((ROUTE_BLOCK, dp), jnp.uint32),
                            pltpu.VMEM((ROUTE_BLOCK, dp), jnp.uint32),
                            pltpu.VMEM((2, d, de), exp_gate.dtype),
                            pltpu.VMEM((2, d, de), exp_up.dtype),
                            pltpu.VMEM((2, de, d), exp_down.dtype),
                            pltpu.SemaphoreType.DMA((2, 3)), pltpu.SemaphoreType.DMA((2,)),
                            pltpu.SemaphoreType.DMA((2,))]),
        compiler_params=pltpu.CompilerParams(dimension_semantics=("arbitrary",),
                                             vmem_limit_bytes=VMEM_LIMIT),
        name="experts",
    )(sched, n_active, row_tok.reshape(nblk, ROUTE_BLOCK), h2p, exp_gate, exp_up, exp_down)


def _combine_kernel(dest_hbm, ys_hbm, x1_ref, meta_ref, mod_ref, fw_ref, o_ref,
                    idx_smem, buf_ref, sem_idx, sem_rows):
    i = pl.program_id(0)
    nsteps = pl.num_programs(0)
    tj = x1_ref.shape[0]
    slot = i % 2

    def gather(step, sl):
        cp = pltpu.make_async_copy(dest_hbm.at[step], idx_smem.at[sl], sem_idx)
        cp.start()
        cp.wait()

        def body(t, carry):
            for s in range(2):
                d = idx_smem[sl, 2 * t + s]
                pltpu.make_async_copy(ys_hbm.at[pl.ds(d, 1)], buf_ref.at[sl, s, pl.ds(t, 1)],
                                      sem_rows.at[sl]).start()
            return carry

        lax.fori_loop(0, tj, body, 0, unroll=8)

    @pl.when(i == 0)
    def _():
        gather(0, 0)

    @pl.when(i + 1 < nsteps)
    def _():
        gather(i + 1, 1 - slot)

    pltpu.make_async_copy(buf_ref.at[slot], buf_ref.at[slot], sem_rows.at[slot]).wait()

    meta = meta_ref[...]
    y = (meta[:, 2:3] * _unpack_bf16_pair(buf_ref[slot, 0])
         + meta[:, 3:4] * _unpack_bf16_pair(buf_ref[slot, 1]))
    x2 = x1_ref[...] + mod_ref[0, 5:6, :] * y
    ms = jnp.mean(x2 * x2, axis=-1, keepdims=True)
    o_ref[...] = x2 * lax.rsqrt(ms + RMS_EPS) * fw_ref[...]


def _combine(dest, ys, x1, meta, mod3, final_w, t_len):
    n, d = x1.shape
    tj = min(256, t_len)
    return pl.pallas_call(
        _combine_kernel,
        out_shape=jax.ShapeDtypeStruct((n, d), F32),
        grid=(n // tj,),
        in_specs=[pl.BlockSpec(memory_space=pl.ANY),
                  pl.BlockSpec(memory_space=pl.ANY),
                  pl.BlockSpec((tj, d), lambda i: (i, 0)),
                  pl.BlockSpec((tj, LANES), lambda i: (i, 0)),
                  pl.BlockSpec((1, 6, d), lambda i: (i * tj // t_len, 0, 0)),
                  pl.BlockSpec((1, d), lambda i: (0, 0))],
        out_specs=pl.BlockSpec((tj, d), lambda i: (i, 0)),
        scratch_shapes=[pltpu.SMEM((2, 2 * tj), jnp.int32), pltpu.VMEM((2, 2, tj, d // 2), jnp.uint32),
                        pltpu.SemaphoreType.DMA, pltpu.SemaphoreType.DMA((2,))],
        compiler_params=pltpu.CompilerParams(dimension_semantics=("arbitrary",),
                                             vmem_limit_bytes=VMEM_LIMIT),
        name="combine",
    )(dest.reshape(n // tj, 2 * tj), ys, x1, meta, mod3, final_w.reshape(1, d))


def _pad_cols(w, width):
    return jnp.pad(w, ((0, 0), (0, width - w.shape[1])))


def _pad_rows(w, height):
    return jnp.pad(w, ((0, height - w.shape[0]), (0, 0)))


def _layer(x2, c, bsz, t_len, w_ada, b_ada, norm1_w, w_in, mu_r, mu_k, mu_v, mu_w, mu_a, mu_g, w0,
           w_decay1, w_decay2, a0, w_aaa1, w_aaa2, w_gate1, w_gate2, k_k, k_a, r_k, lnx_w, lnx_b,
           conv_w, conv_norm_w, w_out, norm2_w, w_grp, b_grp, w_exp, b_exp, exp_gate, exp_up,
           exp_down, final_w):
    n, d = x2.shape
    dr = w0.shape[0]
    assert w_decay1.shape[1] <= 128 and w_aaa1.shape[1] <= 128 and w_gate1.shape[1] <= 256
    assert dr % GROUP_LANES == 0 and conv_w.shape[1] == dr and w_in.shape[1] == 6 * dr
    assert t_len % WKV_CHUNK == 0 and w_grp.shape[1] == N_ROUTE_GROUPS and w_exp.shape[1] == N_EXPERTS

    mod3 = _ada(c, w_ada, b_ada).reshape(bsz, 6, d)
    h, proj = _inproj(x2, mod3, norm1_w, w_in.astype(BF16), t_len)

    w1p = jnp.concatenate([_pad_cols(w_decay1, 128), _pad_cols(w_aaa1, 128), _pad_cols(w_gate1, 256)], axis=1)
    mup = jnp.concatenate([jnp.broadcast_to(mu_w[:, None], (d, 128)),
                           jnp.broadcast_to(mu_a[:, None], (d, 128)),
                           jnp.broadcast_to(mu_g[:, None], (d, 256))], axis=1)
    lw, ag, gg = _lora(h, w1p, mup, _pad_rows(w_decay2, 128), _pad_rows(w_aaa2, 128),
                       _pad_rows(w_gate2, 256), w0, a0, bsz, t_len)

    yr = _wkv(proj, lw, ag, gg, mu_r, mu_k, mu_v, k_k, k_a, r_k.reshape(dr), lnx_w, lnx_b, bsz, t_len)

    wr = _pad_cols(jnp.concatenate([w_grp, w_exp], axis=1), LANES)
    br = _pad_cols(jnp.concatenate([b_grp, b_exp]).reshape(1, -1), LANES)
    x1, h2, meta, cnt = _outproj(yr, proj, conv_w, conv_norm_w, w_out.astype(BF16), x2, mod3,
                                 norm2_w, wr, br, bsz, t_len)

    counts = cnt[0, :N_EXPERTS].astype(jnp.int32)
    padded = (counts + ROUTE_BLOCK - 1) // ROUTE_BLOCK * ROUTE_BLOCK
    pad_ends = jnp.cumsum(padded)
    pad_starts = pad_ends - padded
    top_e = meta[:, 0:2].astype(jnp.int32)
    rank = meta[:, 4:6].astype(jnp.int32)
    one_hot = top_e[:, :, None] == jnp.arange(N_EXPERTS, dtype=jnp.int32)
    dest = (jnp.sum(jnp.where(one_hot, pad_starts, 0), axis=-1) + rank).reshape(-1)
    nblk = (2 * n) // ROUTE_BLOCK + N_EXPERTS
    n_rows = nblk * ROUTE_BLOCK
    block_start = jnp.arange(nblk, dtype=jnp.int32) * ROUTE_BLOCK
    block_e = jnp.minimum(jnp.sum(pad_ends[None, :] <= block_start[:, None], axis=1),
                          N_EXPERTS - 1).astype(jnp.int32)
    n_active = (pad_ends[-1:] // ROUTE_BLOCK).astype(jnp.int32)

    tok_of = jnp.repeat(jnp.arange(n, dtype=jnp.int32), 2)
    row_tok = jnp.zeros((n_rows,), jnp.int32).at[dest].set(tok_of)
    ys = _experts(block_e, n_active, row_tok, h2, exp_gate, exp_up, exp_down)
    return _combine(dest, ys, x1, meta, mod3, final_w, t_len)


def kernel(x, c, w_ada, b_ada, norm1_w, w_in, mu_r, mu_k, mu_v, mu_w, mu_a, mu_g, w0, w_decay1, w_decay2, a0, w_aaa1, w_aaa2, w_gate1, w_gate2, k_k, k_a, r_k, lnx_w, lnx_b, conv_w, conv_norm_w, w_out, norm2_w, w_grp, b_grp, w_exp, b_exp, exp_gate, exp_up, exp_down, final_w):
    bsz, t_len, d = x.shape
    assert w_ada.shape[0] == 1, "single-layer block"
    out = _layer(x.reshape(bsz * t_len, d), c, bsz, t_len, w_ada[0], b_ada[0], norm1_w[0], w_in[0],
                 mu_r[0], mu_k[0], mu_v[0], mu_w[0], mu_a[0], mu_g[0], w0[0], w_decay1[0], w_decay2[0],
                 a0[0], w_aaa1[0], w_aaa2[0], w_gate1[0], w_gate2[0], k_k[0], k_a[0], r_k[0], lnx_w[0],
                 lnx_b[0], conv_w[0], conv_norm_w[0], w_out[0], norm2_w[0], w_grp[0], b_grp[0],
                 w_exp[0], b_exp[0], exp_gate[0], exp_up[0], exp_down[0], final_w)
    return out.reshape(bsz, t_len, d)
```

```python
import functools

import jax
import jax.numpy as jnp
from jax import lax
from jax.experimental import pallas as pl
from jax.experimental.pallas import tpu as pltpu

F32 = jnp.float32
BF16 = jnp.bfloat16

HEAD_DIM = 64
WKV_CHUNK = 64
GROUP_LANES = 256
N_ROUTE_GROUPS = 4
EXPERTS_PER_GROUP = 8
N_EXPERTS = N_ROUTE_GROUPS * EXPERTS_PER_GROUP
CONV_GROUP_DIM = 64
RMS_EPS = 1e-6
LNX_EPS = 64e-5
LANES = 128
LORA_PAD = 512
ROUTE_BLOCK = 256
VMEM_LIMIT = 56 * 1024 * 1024


def _dot(a, b, precision=None):
    return jnp.dot(a, b, preferred_element_type=F32, precision=precision)


def _dot_nt(a, b):
    return lax.dot_general(a, b, (((1,), (1,)), ((), ())), preferred_element_type=F32)


def _dot_tn(a, b):
    return lax.dot_general(a, b, (((0,), (0,)), ((), ())), preferred_element_type=F32)


def _split_bf16(x):
    hi = x.astype(BF16)
    return hi, (x - hi.astype(F32)).astype(BF16)


def _dot_x3(a, b):
    ah, al = _split_bf16(a)
    bh, bl = _split_bf16(b)
    return _dot(ah, bh) + _dot(al, bh) + _dot(ah, bl)


def _pack_bf16_pair(x):
    c = x.shape[1] // 2
    lo = pltpu.bitcast(x[:, :c].astype(BF16).astype(F32), jnp.uint32)
    hi = pltpu.bitcast(x[:, c:].astype(BF16).astype(F32), jnp.uint32)
    return (lo >> 16) | (hi & jnp.uint32(0xFFFF0000))


def _unpack_bf16_pair(w):
    lo = pltpu.bitcast(w << 16, F32)
    hi = pltpu.bitcast(w & jnp.uint32(0xFFFF0000), F32)
    return jnp.concatenate([lo, hi], axis=1)


def _sigmoid(x):
    return 1.0 / (1.0 + jnp.exp(-x))


def _shift_rows(x, carry_row):
    rolled = pltpu.roll(x, 1, axis=0)
    row = lax.broadcasted_iota(jnp.int32, x.shape, 0)
    return jnp.where(row == 0, carry_row, rolled)


def _seg_ones(n, seg):
    r = lax.broadcasted_iota(jnp.int32, (n, n), 0) // seg
    c = lax.broadcasted_iota(jnp.int32, (n, n), 1) // seg
    return jnp.where(r == c, 1.0, 0.0).astype(BF16)


def _seg_sum(x, ones_bd):
    hi = x.astype(BF16)
    lo = (x - hi.astype(F32)).astype(BF16)
    return _dot(hi, ones_bd) + _dot(lo, ones_bd)


def _seg_sum1(x, ones_bd):
    return _dot(x.astype(BF16), ones_bd)


def _ada_kernel(cb_ref, w_ref, b_ref, o_ref):
    nb = cb_ref.shape[0]
    tn = w_ref.shape[1]
    for b in range(nb):
        cv = cb_ref[b]
        s = cv * _sigmoid(cv)
        cols = [jnp.sum(s * w_ref[:, j * LANES:(j + 1) * LANES], axis=0, keepdims=True)
                for j in range(tn // LANES)]
        o_ref[b:b + 1, :] = jnp.concatenate(cols, axis=1) + b_ref[...]


def _ada(c, w_ada, b_ada):
    nb, d = c.shape
    n_out = w_ada.shape[1]
    tn = 512
    cb = jnp.broadcast_to(c[:, :, None], (nb, d, LANES))
    return pl.pallas_call(
        _ada_kernel,
        out_shape=jax.ShapeDtypeStruct((nb, n_out), F32),
        grid=(n_out // tn,),
        in_specs=[pl.BlockSpec((nb, d, LANES), lambda j: (0, 0, 0)),
                  pl.BlockSpec((d, tn), lambda j: (0, j)),
                  pl.BlockSpec((1, tn), lambda j: (0, j))],
        out_specs=pl.BlockSpec((nb, tn), lambda j: (0, j)),
        compiler_params=pltpu.CompilerParams(dimension_semantics=("arbitrary",),
                                             vmem_limit_bytes=VMEM_LIMIT),
        name="ada",
    )(cb, w_ada, b_ada.reshape(1, n_out))


def _inproj_kernel(x_ref, mod_ref, nw_ref, w_ref, h_ref, p_ref, hs_ref):
    @pl.when(pl.program_id(1) == 0)
    def _():
        x = x_ref[...]
        ms = jnp.mean(x * x, axis=-1, keepdims=True)
        y = x * lax.rsqrt(ms + RMS_EPS) * nw_ref[...]
        h = y * (1.0 + mod_ref[0, 1:2, :]) + mod_ref[0, 0:1, :]
        hb = h.astype(BF16)
        hs_ref[...] = hb
        h_ref[...] = hb

    p_ref[...] = _dot(hs_ref[...], w_ref[...])


def _inproj(x2, mod3, norm_w, w_in_bf, t_len):
    n, d = x2.shape
    n_in = w_in_bf.shape[1]
    tm = min(1024, t_len)
    tn = 1024
    return pl.pallas_call(
        _inproj_kernel,
        out_shape=(jax.ShapeDtypeStruct((n, d), BF16), jax.ShapeDtypeStruct((n, n_in), F32)),
        grid=(n // tm, n_in // tn),
        in_specs=[pl.BlockSpec((tm, d), lambda i, j: (i, 0)),
                  pl.BlockSpec((1, 6, d), lambda i, j: (i * tm // t_len, 0, 0)),
                  pl.BlockSpec((1, d), lambda i, j: (0, 0)),
                  pl.BlockSpec((d, tn), lambda i, j: (0, j))],
        out_specs=(pl.BlockSpec((tm, d), lambda i, j: (i, 0)),
                   pl.BlockSpec((tm, tn), lambda i, j: (i, j))),
        scratch_shapes=[pltpu.VMEM((tm, d), BF16)],
        compiler_params=pltpu.CompilerParams(dimension_semantics=("arbitrary", "arbitrary"),
                                             vmem_limit_bytes=VMEM_LIMIT),
        name="inproj",
    )(x2, mod3, norm_w.reshape(1, d), w_in_bf)


def _lora_kernel(h_ref, w1_ref, mu_ref, wd2_ref, wa2_ref, wg2_ref, w0_ref, a0_ref,
                 lw_ref, ag_ref, gg_ref, carry_ref):
    @pl.when(pl.program_id(1) == 0)
    def _():
        carry_ref[...] = jnp.zeros_like(carry_ref)

    h = h_ref[...]
    w1 = w1_ref[...]
    p1 = _dot(h, w1.astype(BF16))
    p2 = _dot(h, (w1 * mu_ref[...]).astype(BF16))
    prev = _shift_rows(p2, carry_ref[...])
    carry_ref[...] = p2[p2.shape[0] - 1:, :]
    lo = p1 - p2 + prev
    dec = _dot(jnp.tanh(lo[:, 0:128]).astype(BF16), wd2_ref[...].astype(BF16))
    z = -(w0_ref[...] + dec)
    softplus = jnp.maximum(z, 0.0) + jnp.log(1.0 + jnp.exp(-jnp.abs(z)))
    lw_ref[...] = -jnp.exp(-softplus - 0.5)
    ag_ref[...] = _sigmoid(a0_ref[...] + _dot(lo[:, 128:256].astype(BF16), wa2_ref[...].astype(BF16)))
    gg_ref[...] = _dot(_sigmoid(lo[:, 256:512]).astype(BF16), wg2_ref[...].astype(BF16))


def _lora(h, w1p, mup, wd2p, wa2p, wg2p, w0, a0, bsz, t_len):
    n, d = h.shape
    dr = w0.shape[-1]
    tm = min(512, t_len)
    nt = t_len // tm
    row = lambda b, t: (b * nt + t, 0)
    const = lambda b, t: (0, 0)
    out = jax.ShapeDtypeStruct((n, dr), F32)
    return pl.pallas_call(
        _lora_kernel,
        out_shape=(out, out, out),
        grid=(bsz, nt),
        in_specs=[pl.BlockSpec((tm, d), row),
                  pl.BlockSpec((d, LORA_PAD), const),
                  pl.BlockSpec((d, LORA_PAD), const),
                  pl.BlockSpec((128, dr), const),
                  pl.BlockSpec((128, dr), const),
                  pl.BlockSpec((256, dr), const),
                  pl.BlockSpec((1, dr), const),
                  pl.BlockSpec((1, dr), const)],
        out_specs=(pl.BlockSpec((tm, dr), row),) * 3,
        scratch_shapes=[pltpu.VMEM((1, LORA_PAD), F32)],
        compiler_params=pltpu.CompilerParams(dimension_semantics=("arbitrary", "arbitrary"),
                                             vmem_limit_bytes=VMEM_LIMIT),
        name="lora",
    )(h, w1p, mup, wd2p, wa2p, wg2p, w0.reshape(1, dr), a0.reshape(1, dr))


def _wkv_kernel(r_ref, k_ref, v_ref, lw_ref, ag_ref, gg_ref, mur_ref, muk_ref, muv_ref,
                kk_ref, ka_ref, rk_ref, lnw_ref, lnb_ref, y_ref, s_ref, carry_ref):
    lt = r_ref.shape[0]
    gl = GROUP_LANES
    c_len = WKV_CHUNK
    heads = gl // HEAD_DIM

    @pl.when(pl.program_id(2) == 0)
    def _():
        s_ref[...] = jnp.zeros_like(s_ref)
        carry_ref[...] = jnp.zeros_like(carry_ref)

    ones_bd = _seg_ones(gl, HEAD_DIM)

    def lerp(ref, mu_ref, idx):
        z = ref[...]
        prev = _shift_rows(z, carry_ref[idx:idx + 1, :])
        carry_ref[idx:idx + 1, :] = z[lt - 1:, :]
        return z + (prev - z) * mu_ref[...]

    r1 = lerp(r_ref, mur_ref, 0)
    k1 = lerp(k_ref, muk_ref, 1)
    v1 = lerp(v_ref, muv_ref, 2)
    ag = ag_ref[...]
    kk = k1 * kk_ref[...]
    kk = kk * lax.rsqrt(jnp.maximum(_seg_sum1(kk * kk, ones_bd), 1e-24))
    k2 = k1 * (1.0 + (ag - 1.0) * ka_ref[...])
    a_all = -kk
    b_all = kk * ag
    lw = lw_ref[...]

    rows = lax.broadcasted_iota(jnp.int32, (gl, gl), 0)
    cols = lax.broadcasted_iota(jnp.int32, (gl, gl), 1)
    same_head = (rows // c_len) == (cols // HEAD_DIM)
    ts = lax.broadcasted_iota(jnp.int32, (c_len, gl), 0)
    tj = lax.broadcasted_iota(jnp.int32, (c_len, gl), 1) % c_len
    tri_strict = ts > tj
    tri_incl = ts >= tj
    eye = jnp.where(ts == tj, 1.0, 0.0).astype(F32)
    cr = lax.broadcasted_iota(jnp.int32, (c_len, c_len), 0)
    cc = lax.broadcasted_iota(jnp.int32, (c_len, c_len), 1)
    cum_b = jnp.where(cr >= cc, 1.0, 0.0).astype(BF16)

    def bd(x):
        xb = x.astype(BF16)
        return jnp.where(same_head, jnp.concatenate([xb] * heads, axis=0), jnp.zeros((), BF16))

    def fold(x):
        out = x[0:c_len]
        for hh in range(1, heads):
            out = out + x[hh * c_len:(hh + 1) * c_len]
        return out

    nch = lt // c_len
    chunks = range(nch)
    sls = [slice(c * c_len, (c + 1) * c_len) for c in chunks]
    lws = [lw[sl] for sl in sls]

    def cumsum3(x):
        hi = x.astype(BF16)
        r = x - hi.astype(F32)
        mid = r.astype(BF16)
        lo = (r - mid.astype(F32)).astype(BF16)
        return _dot(cum_b, hi) + _dot(cum_b, mid) + _dot(cum_b, lo)

    cs = [cumsum3(x) for x in lws]
    e_cs = [jnp.exp(x) for x in cs]
    e_neg = [jnp.exp(-x) for x in cs]
    wc = [x[c_len - 1:, :] for x in e_cs]
    at = [a_all[sls[c]] * jnp.exp(cs[c] - lws[c]) for c in chunks]
    rt = [r1[sls[c]] * e_cs[c] for c in chunks]
    bt = [b_all[sls[c]] * e_neg[c] for c in chunks]
    kt = [k2[sls[c]] * e_neg[c] for c in chunks]
    vc = [v1[sl] for sl in sls]

    lhs = [jnp.concatenate([at[c], rt[c]], axis=0).astype(BF16) for c in chunks]
    g_b = [_dot_nt(lhs[c], bd(bt[c])) for c in chunks]
    g_k = [_dot_nt(lhs[c], bd(kt[c])) for c in chunks]
    n_ab = [jnp.where(tri_strict, g[:c_len], 0.0) for g in g_b]
    a_ak = [jnp.where(tri_strict, g[:c_len], 0.0) for g in g_k]
    a_rb = [jnp.where(tri_incl, g[c_len:], 0.0).astype(BF16) for g in g_b]
    a_rk = [jnp.where(tri_incl, g[c_len:], 0.0).astype(BF16) for g in g_k]

    p = [eye + jnp.where((ts // 2) == (tj // 2), x, 0.0) for x in n_ab]
    for half in (2, 4, 8, 16, 32):
        lower_left = (((ts // (2 * half)) == (tj // (2 * half)))
                      & ((ts % (2 * half)) >= half) & ((tj % (2 * half)) < half))
        n21_ta = [_dot(jnp.where(lower_left, n_ab[c], 0.0).astype(BF16), bd(p[c])) for c in chunks]
        p = [p[c] + _dot(p[c].astype(BF16), bd(n21_ta[c])) for c in chunks]
    tb = [x.astype(BF16) for x in p]

    z = [_dot(tb[c], jnp.concatenate([bd(a_ak[c]), bd(at[c])], axis=1)) for c in chunks]
    ap = [x[:, gl:] for x in z]
    bd_v = [bd(x) for x in vc]
    u0 = [_dot(z[c][:, :gl].astype(BF16), bd_v[c]) for c in chunks]
    w = [_dot(a_rb[c], jnp.concatenate([bd(ap[c]), bd(u0[c])], axis=1)) for c in chunks]
    rp = [rt[c] + w[c][:, :gl] for c in chunks]
    y0 = [w[c][:, gl:] + _dot(a_rk[c], bd_v[c]) for c in chunks]
    bh = [(bt[c] * wc[c]).astype(BF16) for c in chunks]
    bk = [jnp.concatenate([bh[c], (kt[c] * wc[c]).astype(BF16)], axis=0) for c in chunks]
    g_s = [jnp.where(same_head, _dot_tn(ap[c].astype(BF16), bh[c]), 0.0).astype(BF16) for c in chunks]
    n_s = [fold(jnp.where(same_head, _dot_tn(jnp.concatenate([u0[c], vc[c]], axis=0).astype(BF16), bk[c]), 0.0))
           for c in chunks]

    s = s_ref[...]
    s_in = []
    for c in chunks:
        s_in.append(s.astype(BF16))
        s = s * wc[c] + _dot(s_in[c], g_s[c]) + n_s[c]
    s_ref[...] = s
    y_chunks = [_dot_nt(rp[c].astype(BF16), bd(s_in[c])) + y0[c] for c in chunks]

    y = jnp.concatenate(y_chunks, axis=0) if len(y_chunks) > 1 else y_chunks[0]
    inv_n = 1.0 / HEAD_DIM
    mean = _seg_sum(y, ones_bd) * inv_n
    yc = y - mean
    var = _seg_sum1(yc * yc, ones_bd) * inv_n
    yn = yc * lax.rsqrt(var + LNX_EPS) * lnw_ref[...] + lnb_ref[...]
    bonus = _seg_sum(r1 * k2 * rk_ref[...], ones_bd) * v1
    y_ref[...] = ((yn + bonus) * gg_ref[...]).astype(y_ref.dtype)


def _wkv(proj, lw, ag, gg, mu_r, mu_k, mu_v, k_k, k_a, r_k, lnx_w, lnx_b, bsz, t_len):
    n = proj.shape[0]
    dr = lw.shape[1]
    gl = GROUP_LANES
    ng = dr // gl
    lt = min(512, t_len)
    nt = t_len // lt
    row = lambda off: (lambda b, g, t: (b * nt + t, off + g))
    par = lambda b, g, t: (0, g)
    vec = lambda a: a.reshape(1, dr)
    return pl.pallas_call(
        _wkv_kernel,
        out_shape=jax.ShapeDtypeStruct((n, dr), BF16),
        grid=(bsz, ng, nt),
        in_specs=[pl.BlockSpec((lt, gl), row(0)),
                  pl.BlockSpec((lt, gl), row(ng)),
                  pl.BlockSpec((lt, gl), row(2 * ng)),
                  pl.BlockSpec((lt, gl), row(0)),
                  pl.BlockSpec((lt, gl), row(0)),
                  pl.BlockSpec((lt, gl), row(0))] + [pl.BlockSpec((1, gl), par)] * 8,
        out_specs=pl.BlockSpec((lt, gl), row(0)),
        scratch_shapes=[pltpu.VMEM((HEAD_DIM, gl), F32), pltpu.VMEM((8, gl), F32)],
        compiler_params=pltpu.CompilerParams(
            dimension_semantics=("arbitrary", "arbitrary", "arbitrary"),
            vmem_limit_bytes=VMEM_LIMIT),
        name="wkv",
    )(proj, proj, proj, lw, ag, gg, vec(mu_r), vec(mu_k), vec(mu_v), vec(k_k), vec(k_a),
      vec(r_k), vec(lnx_w), vec(lnx_b))


def _outproj_kernel(yr_ref, u_ref, gb_ref, gc_ref, cw_ref, cnw_ref, wo_ref, x_ref, mod_ref,
                    n2w_ref, wr_ref, br_ref, x1_ref, h2_ref, meta_ref, cnt_ref,
                    ccarry_ref, cnt_acc_ref):
    tm = x_ref.shape[0]
    dc = u_ref.shape[1]

    @pl.when(pl.program_id(1) == 0)
    def _():
        ccarry_ref[...] = jnp.zeros_like(ccarry_ref)

    @pl.when((pl.program_id(0) == 0) & (pl.program_id(1) == 0))
    def _():
        cnt_acc_ref[...] = jnp.zeros_like(cnt_acc_ref)

    pre = gc_ref[...] * u_ref[...]
    prev1 = _shift_rows(pre, ccarry_ref[0:1, :])
    prev2 = _shift_rows(prev1, ccarry_ref[1:2, :])
    ccarry_ref[0:1, :] = pre[tm - 1:, :]
    ccarry_ref[1:2, :] = pre[tm - 2:tm - 1, :]
    yc = gb_ref[...] * (prev2 * cw_ref[0:1, :] + prev1 * cw_ref[1:2, :] + pre * cw_ref[2:3, :])
    ones_bd = _seg_ones(GROUP_LANES, CONV_GROUP_DIM)
    ycs = []
    for s in range(dc // GROUP_LANES):
        ys = yc[:, s * GROUP_LANES:(s + 1) * GROUP_LANES]
        ms = _seg_sum1(ys * ys, ones_bd) * (1.0 / CONV_GROUP_DIM)
        ycs.append(ys * lax.rsqrt(ms + RMS_EPS))
    y_conv = (jnp.concatenate(ycs, axis=1) * cnw_ref[...]).astype(BF16)

    dr = yr_ref.shape[1]
    y_mix = _dot(yr_ref[...], wo_ref[0:dr, :]) + _dot(y_conv, wo_ref[dr:, :])
    x1 = x_ref[...] + mod_ref[0, 2:3, :] * y_mix
    x1_ref[...] = x1
    ms = jnp.mean(x1 * x1, axis=-1, keepdims=True)
    h2 = x1 * lax.rsqrt(ms + RMS_EPS) * n2w_ref[...]
    h2 = h2 * (1.0 + mod_ref[0, 4:5, :]) + mod_ref[0, 3:4, :]
    h2_ref[...] = _pack_bf16_pair(h2)

    logits = _dot_x3(h2, wr_ref[...]) + br_ref[...]
    lane = lax.broadcasted_iota(jnp.int32, logits.shape, 1)
    neg = jnp.float32(-jnp.inf)
    big = jnp.int32(1 << 20)
    glog = jnp.where(lane < N_ROUTE_GROUPS, logits, neg)
    gmax = jnp.max(glog, axis=-1, keepdims=True)
    grp = jnp.min(jnp.where(glog == gmax, lane, big), axis=-1, keepdims=True)
    p_grp = 1.0 / jnp.sum(jnp.exp(glog - gmax), axis=-1, keepdims=True)
    e_lane = lane - N_ROUTE_GROUPS
    in_grp = (e_lane >= grp * EXPERTS_PER_GROUP) & (e_lane < (grp + 1) * EXPERTS_PER_GROUP)
    elog = jnp.where(in_grp, logits, neg)
    m1 = jnp.max(elog, axis=-1, keepdims=True)
    i1 = jnp.min(jnp.where(elog == m1, lane, big), axis=-1, keepdims=True)
    elog2 = jnp.where(lane == i1, neg, elog)
    m2 = jnp.max(elog2, axis=-1, keepdims=True)
    i2 = jnp.min(jnp.where(elog2 == m2, lane, big), axis=-1, keepdims=True)
    t2 = jnp.exp(m2 - m1)
    w1 = p_grp / (1.0 + t2)
    w2 = p_grp * t2 / (1.0 + t2)
    e1 = i1 - N_ROUTE_GROUPS
    e2 = i2 - N_ROUTE_GROUPS

    oh1 = lane == e1
    oh2 = lane == e2
    oh = jnp.where(oh1 | oh2, 1.0, 0.0)
    tr = lax.broadcasted_iota(jnp.int32, (tm, tm), 0)
    tc = lax.broadcasted_iota(jnp.int32, (tm, tm), 1)
    lower = jnp.where(tr > tc, 1.0, 0.0).astype(BF16)
    before = _dot(lower, oh.astype(BF16)) + cnt_acc_ref[...]
    rank1 = jnp.sum(jnp.where(oh1, before, 0.0), axis=-1, keepdims=True)
    rank2 = jnp.sum(jnp.where(oh2, before, 0.0), axis=-1, keepdims=True)
    cnt = cnt_acc_ref[...] + jnp.sum(oh, axis=0, keepdims=True)
    cnt_acc_ref[...] = cnt
    cnt_ref[...] = jnp.broadcast_to(cnt, cnt_ref.shape)

    meta = jnp.where(lane == 0, e1.astype(F32), 0.0)
    meta = jnp.where(lane == 1, e2.astype(F32), meta)
    meta = jnp.where(lane == 2, w1, meta)
    meta = jnp.where(lane == 3, w2, meta)
    meta = jnp.where(lane == 4, rank1, meta)
    meta = jnp.where(lane == 5, rank2, meta)
    meta_ref[...] = meta


def _outproj(yr, proj, conv_w, conv_norm_w, w_out_bf, x2, mod3, norm2_w, wr, br, bsz, t_len):
    n, d = x2.shape
    dr = yr.shape[1]
    dc = conv_w.shape[1]
    tm = min(512, t_len)
    nt = t_len // tm
    nseg = dr // dc
    row = lambda off: (lambda b, t: (b * nt + t, off))
    const = lambda b, t: (0, 0)
    return pl.pallas_call(
        _outproj_kernel,
        out_shape=(jax.ShapeDtypeStruct((n, d), F32), jax.ShapeDtypeStruct((n, d // 2), jnp.uint32),
                   jax.ShapeDtypeStruct((n, LANES), F32), jax.ShapeDtypeStruct((8, LANES), F32)),
        grid=(bsz, nt),
        in_specs=[pl.BlockSpec((tm, dr), row(0)),
                  pl.BlockSpec((tm, dc), row(3 * nseg)),
                  pl.BlockSpec((tm, dc), row(4 * nseg)),
                  pl.BlockSpec((tm, dc), row(5 * nseg)),
                  pl.BlockSpec((3, dc), const),
                  pl.BlockSpec((1, dc), const),
                  pl.BlockSpec((dr + dc, d), const),
                  pl.BlockSpec((tm, d), row(0)),
                  pl.BlockSpec((1, 6, d), lambda b, t: (b, 0, 0)),
                  pl.BlockSpec((1, d), const),
                  pl.BlockSpec((d, LANES), const),
                  pl.BlockSpec((1, LANES), const)],
        out_specs=(pl.BlockSpec((tm, d), row(0)), pl.BlockSpec((tm, d // 2), row(0)),
                   pl.BlockSpec((tm, LANES), row(0)), pl.BlockSpec((8, LANES), const)),
        scratch_shapes=[pltpu.VMEM((8, dc), F32), pltpu.VMEM((1, LANES), F32)],
        compiler_params=pltpu.CompilerParams(dimension_semantics=("arbitrary", "arbitrary"),
                                             vmem_limit_bytes=VMEM_LIMIT),
        name="outproj",
    )(yr, proj, proj, proj, conv_w, conv_norm_w.reshape(1, dc), w_out_bf, x2, mod3,
      norm2_w.reshape(1, d), wr, br)


def _dispatch_kernel(zoff_ref, dest_hbm, h2_ref, xs_hbm, idx_smem, zero_ref, zblk_ref,
                     sem_idx, sem_rows, sem_zero, sem_tail):
    i = pl.program_id(0)
    nsteps = pl.num_programs(0)
    th = idx_smem.shape[1] // 2
    slot = i % 2

    def idx_copy(step, sl):
        return pltpu.make_async_copy(dest_hbm.at[step], idx_smem.at[sl], sem_idx.at[sl])

    def rows_wait():
        pltpu.make_async_copy(xs_hbm.at[pl.ds(0, 2 * th)], xs_hbm.at[pl.ds(0, 2 * th)], sem_rows).wait()

    nblk = xs_hbm.shape[0] // ROUTE_BLOCK

    def tail_copy(b):
        start = pl.multiple_of(b * ROUTE_BLOCK, ROUTE_BLOCK)
        return pltpu.make_async_copy(zblk_ref, xs_hbm.at[pl.ds(start, ROUTE_BLOCK)], sem_tail)

    @pl.when(i == 0)
    def _():
        idx_copy(0, 0).start()
        zero_ref[...] = jnp.zeros_like(zero_ref)

        def zero_expert(e, carry):
            off = zoff_ref[e]

            def zero_row(j, c):
                pltpu.make_async_copy(zero_ref, xs_hbm.at[pl.ds(off + j, 1)], sem_zero).start()
                return c

            return lax.fori_loop(0, zoff_ref[N_EXPERTS + e], zero_row, carry)

        lax.fori_loop(0, N_EXPERTS, zero_expert, 0)
        total = zoff_ref[2 * N_EXPERTS]
        bulk = pl.multiple_of((total // 8) * 8, 8)

        @pl.when(bulk > 0)
        def _():
            pltpu.make_async_copy(xs_hbm.at[pl.ds(0, bulk)], xs_hbm.at[pl.ds(0, bulk)], sem_zero).wait()

        def wait_row(j, c):
            pltpu.make_async_copy(zero_ref, xs_hbm.at[pl.ds(0, 1)], sem_zero).wait()
            return c

        lax.fori_loop(0, total - bulk, wait_row, 0)

        zblk_ref[...] = jnp.zeros_like(zblk_ref)

        def tail_start(b, c):
            tail_copy(b).start()
            return c

        lax.fori_loop(zoff_ref[2 * N_EXPERTS + 1], nblk, tail_start, 0)

    idx_copy(i, slot).wait()

    @pl.when(i + 1 < nsteps)
    def _():
        idx_copy(i + 1, 1 - slot).start()

    for p in range(2):
        @pl.when(slot == p)
        def _(p=p):
            for t in range(th):
                for s in range(2):
                    d = idx_smem[p, 2 * t + s]
                    pltpu.make_async_copy(h2_ref.at[pl.ds(t, 1)], xs_hbm.at[pl.ds(d, 1)], sem_rows).start()

    rows_wait()

    @pl.when(i == nsteps - 1)
    def _():
        def tail_wait(b, c):
            tail_copy(b).wait()
            return c

        lax.fori_loop(zoff_ref[2 * N_EXPERTS + 1], nblk, tail_wait, 0)


def _dispatch(dest, zero_off, h2, n_rows):
    n, d = h2.shape
    th = min(256, n)
    return pl.pallas_call(
        _dispatch_kernel,
        out_shape=jax.ShapeDtypeStruct((n_rows, d), h2.dtype),
        grid_spec=pltpu.PrefetchScalarGridSpec(
            num_scalar_prefetch=1,
            grid=(n // th,),
            in_specs=[pl.BlockSpec(memory_space=pl.ANY),
                      pl.BlockSpec((th, d), lambda i, zo: (i, 0))],
            out_specs=pl.BlockSpec(memory_space=pl.ANY),
            scratch_shapes=[pltpu.SMEM((2, 2 * th), jnp.int32),
                            pltpu.VMEM((1, d), h2.dtype),
                            pltpu.VMEM((ROUTE_BLOCK, d), h2.dtype),
                            pltpu.SemaphoreType.DMA((2,)), pltpu.SemaphoreType.DMA,
                            pltpu.SemaphoreType.DMA, pltpu.SemaphoreType.DMA]),
        compiler_params=pltpu.CompilerParams(dimension_semantics=("arbitrary",),
                                             vmem_limit_bytes=VMEM_LIMIT),
        name="dispatch",
    )(zero_off, dest.reshape(n // th, 2 * th), h2)


def _expert_kernel(sched_ref, na_ref, xs_ref, wg_hbm, wu_hbm, wd_hbm, ys_ref,
                   wg_buf, wu_buf, wd_buf, sem):
    i = pl.program_id(0)

    def weight_copies(e, slot):
        return (pltpu.make_async_copy(wg_hbm.at[e], wg_buf.at[slot], sem.at[slot, 0]),
                pltpu.make_async_copy(wu_hbm.at[e], wu_buf.at[slot], sem.at[slot, 1]),
                pltpu.make_async_copy(wd_hbm.at[e], wd_buf.at[slot], sem.at[slot, 2]))

    @pl.when(i < na_ref[0])
    def _():
        e = sched_ref[0, i]
        slot = sched_ref[1, i]

        @pl.when(i == 0)
        def _():
            for cp in weight_copies(e, slot):
                cp.start()

        @pl.when(sched_ref[2, i] == 1)
        def _():
            for cp in weight_copies(e, slot):
                cp.wait()
            nxt = sched_ref[3, i]

            @pl.when(nxt >= 0)
            def _():
                for cp in weight_copies(nxt, 1 - slot):
                    cp.start()

        x = _unpack_bf16_pair(xs_ref[...]).astype(BF16)
        g = _dot(x, wg_buf[slot].astype(BF16))
        u = _dot(x, wu_buf[slot].astype(BF16))
        hid = (g * _sigmoid(g) * u).astype(BF16)
        ys_ref[...] = _pack_bf16_pair(_dot(hid, wd_buf[slot].astype(BF16)))

    @pl.when(i >= na_ref[0])
    def _():
        ys_ref[...] = jnp.zeros_like(ys_ref)


def _experts(block_e, n_active, xs, exp_gate, exp_up, exp_down):
    n_rows, dp = xs.shape
    _, d, de = exp_gate.shape
    nblk = n_rows // ROUTE_BLOCK
    idx = jnp.arange(nblk, dtype=jnp.int32)
    valid = idx < n_active[0]
    first = valid & ((idx == 0) | (block_e != jnp.roll(block_e, 1)))
    slot = jnp.maximum(jnp.cumsum(first.astype(jnp.int32)) - 1, 0) % 2
    first_at = jnp.where(first, idx, nblk)
    next_first = jnp.concatenate([lax.cummin(first_at, reverse=True)[1:], jnp.full((1,), nblk, jnp.int32)])
    nxt = jnp.where(next_first < nblk, block_e[jnp.minimum(next_first, nblk - 1)], -1)
    sched = jnp.stack([block_e, slot, first.astype(jnp.int32), nxt]).astype(jnp.int32)

    blk_row = lambda i, sc, na: (jnp.minimum(i, jnp.maximum(na[0] - 1, 0)), 0)
    return pl.pallas_call(
        _expert_kernel,
        out_shape=jax.ShapeDtypeStruct((n_rows, dp), jnp.uint32),
        grid_spec=pltpu.PrefetchScalarGridSpec(
            num_scalar_prefetch=2,
            grid=(nblk,),
            in_specs=[pl.BlockSpec((ROUTE_BLOCK, dp), blk_row),
                      pl.BlockSpec(memory_space=pl.ANY),
                      pl.BlockSpec(memory_space=pl.ANY),
                      pl.BlockSpec(memory_space=pl.ANY)],
            out_specs=pl.BlockSpec((ROUTE_BLOCK, dp), lambda i, sc, na: (i, 0)),
            scratch_shapes=[pltpu.VMEM((2, d, de), exp_gate.dtype),
                            pltpu.VMEM((2, d, de), exp_up.dtype),
                            pltpu.VMEM((2, de, d), exp_down.dtype),
                            pltpu.SemaphoreType.DMA((2, 3))]),
        compiler_params=pltpu.CompilerParams(dimension_semantics=("arbitrary",),
                                             vmem_limit_bytes=VMEM_LIMIT),
        name="experts",
    )(sched, n_active, xs, exp_gate, exp_up, exp_down)


def _combine_kernel(dest_hbm, ys_hbm, x1_ref, meta_ref, mod_ref, fw_ref, o_ref,
                    idx_smem, buf_ref, sem_idx, sem_rows):
    i = pl.program_id(0)
    nsteps = pl.num_programs(0)
    tj = x1_ref.shape[0]
    slot = i % 2

    def gather(step, sl):
        cp = pltpu.make_async_copy(dest_hbm.at[step], idx_smem.at[sl], sem_idx)
        cp.start()
        cp.wait()
        for t in range(tj):
            for s in range(2):
                d = idx_smem[sl, 2 * t + s]
                pltpu.make_async_copy(ys_hbm.at[pl.ds(d, 1)], buf_ref.at[sl, s, pl.ds(t, 1)],
                                      sem_rows.at[sl]).start()

    @pl.when(i == 0)
    def _():
        gather(0, 0)

    for p in range(2):
        @pl.when((i + 1 < nsteps) & (slot == 1 - p))
        def _(p=p):
            gather(i + 1, p)

    pltpu.make_async_copy(buf_ref.at[slot], buf_ref.at[slot], sem_rows.at[slot]).wait()

    meta = meta_ref[...]
    y = (meta[:, 2:3] * _unpack_bf16_pair(buf_ref[slot, 0])
         + meta[:, 3:4] * _unpack_bf16_pair(buf_ref[slot, 1]))
    x2 = x1_ref[...] + mod_ref[0, 5:6, :] * y
    ms = jnp.mean(x2 * x2, axis=-1, keepdims=True)
    o_ref[...] = x2 * lax.rsqrt(ms + RMS_EPS) * fw_ref[...]


def _combine(dest, ys, x1, meta, mod3, final_w, t_len):
    n, d = x1.shape
    tj = min(256, t_len)
    return pl.pallas_call(
        _combine_kernel,
        out_shape=jax.ShapeDtypeStruct((n, d), F32),
        grid=(n // tj,),
        in_specs=[pl.BlockSpec(memory_space=pl.ANY),
                  pl.BlockSpec(memory_space=pl.ANY),
                  pl.BlockSpec((tj, d), lambda i: (i, 0)),
                  pl.BlockSpec((tj, LANES), lambda i: (i, 0)),
                  pl.BlockSpec((1, 6, d), lambda i: (i * tj // t_len, 0, 0)),
                  pl.BlockSpec((1, d), lambda i: (0, 0))],
        out_specs=pl.BlockSpec((tj, d), lambda i: (i, 0)),
        scratch_shapes=[pltpu.SMEM((2, 2 * tj), jnp.int32), pltpu.VMEM((2, 2, tj, d // 2), jnp.uint32),
                        pltpu.SemaphoreType.DMA, pltpu.SemaphoreType.DMA((2,))],
        compiler_params=pltpu.CompilerParams(dimension_semantics=("arbitrary",),
                                             vmem_limit_bytes=VMEM_LIMIT),
        name="combine",
    )(dest.reshape(n // tj, 2 * tj), ys, x1, meta, mod3, final_w.reshape(1, d))


def _pad_cols(w, width):
    return jnp.pad(w, ((0, 0), (0, width - w.shape[1])))


def _pad_rows(w, height):
    return jnp.pad(w, ((0, height - w.shape[0]), (0, 0)))


def _layer(x2, c, bsz, t_len, w_ada, b_ada, norm1_w, w_in, mu_r, mu_k, mu_v, mu_w, mu_a, mu_g, w0,
           w_decay1, w_decay2, a0, w_aaa1, w_aaa2, w_gate1, w_gate2, k_k, k_a, r_k, lnx_w, lnx_b,
           conv_w, conv_norm_w, w_out, norm2_w, w_grp, b_grp, w_exp, b_exp, exp_gate, exp_up,
           exp_down, final_w):
    n, d = x2.shape
    dr = w0.shape[0]
    assert w_decay1.shape[1] <= 128 and w_aaa1.shape[1] <= 128 and w_gate1.shape[1] <= 256
    assert dr % GROUP_LANES == 0 and conv_w.shape[1] == dr and w_in.shape[1] == 6 * dr
    assert t_len % WKV_CHUNK == 0 and w_grp.shape[1] == N_ROUTE_GROUPS and w_exp.shape[1] == N_EXPERTS

    mod3 = _ada(c, w_ada, b_ada).reshape(bsz, 6, d)
    h, proj = _inproj(x2, mod3, norm1_w, w_in.astype(BF16), t_len)

    w1p = jnp.concatenate([_pad_cols(w_decay1, 128), _pad_cols(w_aaa1, 128), _pad_cols(w_gate1, 256)], axis=1)
    mup = jnp.concatenate([jnp.broadcast_to(mu_w[:, None], (d, 128)),
                           jnp.broadcast_to(mu_a[:, None], (d, 128)),
                           jnp.broadcast_to(mu_g[:, None], (d, 256))], axis=1)
    lw, ag, gg = _lora(h, w1p, mup, _pad_rows(w_decay2, 128), _pad_rows(w_aaa2, 128),
                       _pad_rows(w_gate2, 256), w0, a0, bsz, t_len)

    yr = _wkv(proj, lw, ag, gg, mu_r, mu_k, mu_v, k_k, k_a, r_k.reshape(dr), lnx_w, lnx_b, bsz, t_len)

    wr = _pad_cols(jnp.concatenate([w_grp, w_exp], axis=1), LANES)
    br = _pad_cols(jnp.concatenate([b_grp, b_exp]).reshape(1, -1), LANES)
    x1, h2, meta, cnt = _outproj(yr, proj, conv_w, conv_norm_w, w_out.astype(BF16), x2, mod3,
                                 norm2_w, wr, br, bsz, t_len)

    counts = cnt[0, :N_EXPERTS].astype(jnp.int32)
    padded = (counts + ROUTE_BLOCK - 1) // ROUTE_BLOCK * ROUTE_BLOCK
    pad_ends = jnp.cumsum(padded)
    pad_starts = pad_ends - padded
    top_e = meta[:, 0:2].astype(jnp.int32)
    rank = meta[:, 4:6].astype(jnp.int32)
    one_hot = top_e[:, :, None] == jnp.arange(N_EXPERTS, dtype=jnp.int32)
    dest = (jnp.sum(jnp.where(one_hot, pad_starts, 0), axis=-1) + rank).reshape(-1)
    nblk = (2 * n) // ROUTE_BLOCK + N_EXPERTS
    n_rows = nblk * ROUTE_BLOCK
    block_start = jnp.arange(nblk, dtype=jnp.int32) * ROUTE_BLOCK
    block_e = jnp.minimum(jnp.sum(pad_ends[None, :] <= block_start[:, None], axis=1),
                          N_EXPERTS - 1).astype(jnp.int32)
    n_active = (pad_ends[-1:] // ROUTE_BLOCK).astype(jnp.int32)

    pad_len = padded - counts
    zero_off = jnp.concatenate([pad_starts + counts, pad_len, jnp.sum(pad_len, keepdims=True),
                                n_active]).astype(jnp.int32)
    xs = _dispatch(dest, zero_off, h2, n_rows)
    ys = _experts(block_e, n_active, xs, exp_gate, exp_up, exp_down)
    return _combine(dest, ys, x1, meta, mod3, final_w, t_len)


def kernel(x, c, w_ada, b_ada, norm1_w, w_in, mu_r, mu_k, mu_v, mu_w, mu_a, mu_g, w0, w_decay1, w_decay2, a0, w_aaa1, w_aaa2, w_gate1, w_gate2, k_k, k_a, r_k, lnx_w, lnx_b, conv_w, conv_norm_w, w_out, norm2_w, w_grp, b_grp, w_exp, b_exp, exp_gate, exp_up, exp_down, final_w):
    bsz, t_len, d = x.shape
    assert w_ada.shape[0] == 1, "single-layer block"
    out = _layer(x.reshape(bsz * t_len, d), c, bsz, t_len, w_ada[0], b_ada[0], norm1_w[0], w_in[0],
                 mu_r[0], mu_k[0], mu_v[0], mu_w[0], mu_a[0], mu_g[0], w0[0], w_decay1[0], w_decay2[0],
                 a0[0], w_aaa1[0], w_aaa2[0], w_gate1[0], w_gate2[0], k_k[0], k_a[0], r_k[0], lnx_w[0],
                 lnx_b[0], conv_w[0], conv_norm_w[0], w_out[0], norm2_w[0], w_grp[0], b_grp[0],
                 w_exp[0], b_exp[0], exp_gate[0], exp_up[0], exp_down[0], final_w)
    return out.reshape(bsz, t_len, d)
```

```python
import itertools

import jax
import jax.numpy as jnp
from jax import lax
from jax.experimental import pallas as pl
from jax.experimental.pallas import tpu as pltpu

F32 = jnp.float32
BF16 = jnp.bfloat16

HEAD_DIM = 64
WKV_CHUNK = 64
GROUP_LANES = 256
N_ROUTE_GROUPS = 4
EXPERTS_PER_GROUP = 8
N_EXPERTS = N_ROUTE_GROUPS * EXPERTS_PER_GROUP
CONV_GROUP_DIM = 64
RMS_EPS = 1e-6
LNX_EPS = 64e-5
LANES = 128
LORA_PAD = 512
ROUTE_BLOCK = 256
VMEM_LIMIT = 56 * 1024 * 1024


def _dot(a, b, precision=None):
    return jnp.dot(a, b, preferred_element_type=F32, precision=precision)


def _dot_nt(a, b):
    return lax.dot_general(a, b, (((1,), (1,)), ((), ())), preferred_element_type=F32)


def _dot_tn(a, b):
    return lax.dot_general(a, b, (((0,), (0,)), ((), ())), preferred_element_type=F32)


def _split_bf16(x):
    hi = x.astype(BF16)
    return hi, (x - hi.astype(F32)).astype(BF16)


def _dot_x3(a, b):
    ah, al = _split_bf16(a)
    bh, bl = _split_bf16(b)
    return _dot(ah, bh) + _dot(al, bh) + _dot(ah, bl)


def _pack_bf16_pair(x):
    c = x.shape[1] // 2
    lo = pltpu.bitcast(x[:, :c].astype(BF16).astype(F32), jnp.uint32)
    hi = pltpu.bitcast(x[:, c:].astype(BF16).astype(F32), jnp.uint32)
    return (lo >> 16) | (hi & jnp.uint32(0xFFFF0000))


def _unpack_bf16_pair(w):
    lo = pltpu.bitcast(w << 16, F32)
    hi = pltpu.bitcast(w & jnp.uint32(0xFFFF0000), F32)
    return jnp.concatenate([lo, hi], axis=1)


def _sigmoid(x):
    return 1.0 / (1.0 + jnp.exp(-x))


def _shift_rows(x, carry_row):
    rolled = pltpu.roll(x, 1, axis=0)
    row = lax.broadcasted_iota(jnp.int32, x.shape, 0)
    return jnp.where(row == 0, carry_row, rolled)


def _seg_ones(n, seg):
    r = lax.broadcasted_iota(jnp.int32, (n, n), 0) // seg
    c = lax.broadcasted_iota(jnp.int32, (n, n), 1) // seg
    return jnp.where(r == c, 1.0, 0.0).astype(BF16)


def _seg_sum(x, ones_bd):
    hi = x.astype(BF16)
    lo = (x - hi.astype(F32)).astype(BF16)
    return _dot(hi, ones_bd) + _dot(lo, ones_bd)


def _seg_sum1(x, ones_bd):
    return _dot(x.astype(BF16), ones_bd)


def _ada_kernel(cb_ref, w_ref, b_ref, o_ref):
    nb = cb_ref.shape[0]
    tn = w_ref.shape[1]
    for b in range(nb):
        cv = cb_ref[b]
        s = cv * _sigmoid(cv)
        cols = [jnp.sum(s * w_ref[:, j * LANES:(j + 1) * LANES], axis=0, keepdims=True)
                for j in range(tn // LANES)]
        o_ref[b:b + 1, :] = jnp.concatenate(cols, axis=1) + b_ref[...]


def _ada(c, w_ada, b_ada):
    nb, d = c.shape
    n_out = w_ada.shape[1]
    tn = 512
    cb = jnp.broadcast_to(c[:, :, None], (nb, d, LANES))
    return pl.pallas_call(
        _ada_kernel,
        out_shape=jax.ShapeDtypeStruct((nb, n_out), F32),
        grid=(n_out // tn,),
        in_specs=[pl.BlockSpec((nb, d, LANES), lambda j: (0, 0, 0)),
                  pl.BlockSpec((d, tn), lambda j: (0, j)),
                  pl.BlockSpec((1, tn), lambda j: (0, j))],
        out_specs=pl.BlockSpec((nb, tn), lambda j: (0, j)),
        compiler_params=pltpu.CompilerParams(dimension_semantics=("arbitrary",),
                                             vmem_limit_bytes=VMEM_LIMIT),
        name="ada",
    )(cb, w_ada, b_ada.reshape(1, n_out))


def _inproj_kernel(x_ref, mod_ref, nw_ref, w_ref, h_ref, p_ref, hs_ref):
    @pl.when(pl.program_id(1) == 0)
    def _():
        x = x_ref[...]
        ms = jnp.mean(x * x, axis=-1, keepdims=True)
        y = x * lax.rsqrt(ms + RMS_EPS) * nw_ref[...]
        h = y * (1.0 + mod_ref[0, 1:2, :]) + mod_ref[0, 0:1, :]
        hb = h.astype(BF16)
        hs_ref[...] = hb
        h_ref[...] = hb

    p_ref[...] = _dot(hs_ref[...], w_ref[...])


def _inproj(x2, mod3, norm_w, w_in_bf, t_len):
    n, d = x2.shape
    n_in = w_in_bf.shape[1]
    tm = min(1024, t_len)
    tn = 1024
    return pl.pallas_call(
        _inproj_kernel,
        out_shape=(jax.ShapeDtypeStruct((n, d), BF16), jax.ShapeDtypeStruct((n, n_in), F32)),
        grid=(n // tm, n_in // tn),
        in_specs=[pl.BlockSpec((tm, d), lambda i, j: (i, 0)),
                  pl.BlockSpec((1, 6, d), lambda i, j: (i * tm // t_len, 0, 0)),
                  pl.BlockSpec((1, d), lambda i, j: (0, 0)),
                  pl.BlockSpec((d, tn), lambda i, j: (0, j))],
        out_specs=(pl.BlockSpec((tm, d), lambda i, j: (i, 0)),
                   pl.BlockSpec((tm, tn), lambda i, j: (i, j))),
        scratch_shapes=[pltpu.VMEM((tm, d), BF16)],
        compiler_params=pltpu.CompilerParams(dimension_semantics=("arbitrary", "arbitrary"),
                                             vmem_limit_bytes=VMEM_LIMIT),
        name="inproj",
    )(x2, mod3, norm_w.reshape(1, d), w_in_bf)


def _lora_kernel(h_ref, w1_ref, mu_ref, wd2_ref, wa2_ref, wg2_ref, w0_ref, a0_ref,
                 lw_ref, ag_ref, gg_ref, carry_ref):
    @pl.when(pl.program_id(1) == 0)
    def _():
        carry_ref[...] = jnp.zeros_like(carry_ref)

    h = h_ref[...]
    w1 = w1_ref[...]
    p1 = _dot(h, w1.astype(BF16))
    p2 = _dot(h, (w1 * mu_ref[...]).astype(BF16))
    prev = _shift_rows(p2, carry_ref[...])
    carry_ref[...] = p2[p2.shape[0] - 1:, :]
    lo = p1 - p2 + prev
    dec = _dot(jnp.tanh(lo[:, 0:128]).astype(BF16), wd2_ref[...].astype(BF16))
    z = -(w0_ref[...] + dec)
    softplus = jnp.maximum(z, 0.0) + jnp.log(1.0 + jnp.exp(-jnp.abs(z)))
    lw_ref[...] = -jnp.exp(-softplus - 0.5)
    ag_ref[...] = _sigmoid(a0_ref[...] + _dot(lo[:, 128:256].astype(BF16), wa2_ref[...].astype(BF16)))
    gg_ref[...] = _dot(_sigmoid(lo[:, 256:512]).astype(BF16), wg2_ref[...].astype(BF16))


def _lora(h, w1p, mup, wd2p, wa2p, wg2p, w0, a0, bsz, t_len):
    n, d = h.shape
    dr = w0.shape[-1]
    tm = min(512, t_len)
    nt = t_len // tm
    row = lambda b, t: (b * nt + t, 0)
    const = lambda b, t: (0, 0)
    out = jax.ShapeDtypeStruct((n, dr), F32)
    return pl.pallas_call(
        _lora_kernel,
        out_shape=(out, out, out),
        grid=(bsz, nt),
        in_specs=[pl.BlockSpec((tm, d), row),
                  pl.BlockSpec((d, LORA_PAD), const),
                  pl.BlockSpec((d, LORA_PAD), const),
                  pl.BlockSpec((128, dr), const),
                  pl.BlockSpec((128, dr), const),
                  pl.BlockSpec((256, dr), const),
                  pl.BlockSpec((1, dr), const),
                  pl.BlockSpec((1, dr), const)],
        out_specs=(pl.BlockSpec((tm, dr), row),) * 3,
        scratch_shapes=[pltpu.VMEM((1, LORA_PAD), F32)],
        compiler_params=pltpu.CompilerParams(dimension_semantics=("arbitrary", "arbitrary"),
                                             vmem_limit_bytes=VMEM_LIMIT),
        name="lora",
    )(h, w1p, mup, wd2p, wa2p, wg2p, w0.reshape(1, dr), a0.reshape(1, dr))


def _wkv_kernel(r_ref, k_ref, v_ref, lw_ref, ag_ref, gg_ref, mur_ref, muk_ref, muv_ref,
                kk_ref, ka_ref, rk_ref, lnw_ref, lnb_ref, y_ref, s_ref, carry_ref):
    lt = r_ref.shape[0]
    gl = GROUP_LANES
    c_len = WKV_CHUNK
    heads = gl // HEAD_DIM

    @pl.when(pl.program_id(2) == 0)
    def _():
        s_ref[...] = jnp.zeros_like(s_ref)
        carry_ref[...] = jnp.zeros_like(carry_ref)

    ones_bd = _seg_ones(gl, HEAD_DIM)

    def lerp(ref, mu_ref, idx):
        z = ref[...]
        prev = _shift_rows(z, carry_ref[idx:idx + 1, :])
        carry_ref[idx:idx + 1, :] = z[lt - 1:, :]
        return z + (prev - z) * mu_ref[...]

    r1 = lerp(r_ref, mur_ref, 0)
    k1 = lerp(k_ref, muk_ref, 1)
    v1 = lerp(v_ref, muv_ref, 2)
    ag = ag_ref[...]
    kk = k1 * kk_ref[...]
    kk = kk * lax.rsqrt(jnp.maximum(_seg_sum1(kk * kk, ones_bd), 1e-24))
    k2 = k1 * (1.0 + (ag - 1.0) * ka_ref[...])
    a_all = -kk
    b_all = kk * ag
    lw = lw_ref[...]

    rows = lax.broadcasted_iota(jnp.int32, (gl, gl), 0)
    cols = lax.broadcasted_iota(jnp.int32, (gl, gl), 1)
    same_head = (rows // c_len) == (cols // HEAD_DIM)
    ts = lax.broadcasted_iota(jnp.int32, (c_len, gl), 0)
    tj = lax.broadcasted_iota(jnp.int32, (c_len, gl), 1) % c_len
    tri_strict = ts > tj
    tri_incl = ts >= tj
    eye = jnp.where(ts == tj, 1.0, 0.0).astype(F32)
    cr = lax.broadcasted_iota(jnp.int32, (c_len, c_len), 0)
    cc = lax.broadcasted_iota(jnp.int32, (c_len, c_len), 1)
    cum_b = jnp.where(cr >= cc, 1.0, 0.0).astype(BF16)

    def bd(x):
        xb = x.astype(BF16)
        return jnp.where(same_head, jnp.concatenate([xb] * heads, axis=0), jnp.zeros((), BF16))

    def fold(x):
        out = x[0:c_len]
        for hh in range(1, heads):
            out = out + x[hh * c_len:(hh + 1) * c_len]
        return out

    nch = lt // c_len
    sls = [slice(c * c_len, (c + 1) * c_len) for c in range(nch)]

    def cumsum3(x):
        hi = x.astype(BF16)
        r = x - hi.astype(F32)
        mid = r.astype(BF16)
        lo = (r - mid.astype(F32)).astype(BF16)
        return _dot(cum_b, hi) + _dot(cum_b, mid) + _dot(cum_b, lo)

    wc, g_s, n_s, rp, y0, s_in = {}, {}, {}, {}, {}, {}
    state = [s_ref[...]]

    def chunk_parallel(chunks):
        lws = {c: lw[sls[c]] for c in chunks}
        cs = {c: cumsum3(lws[c]) for c in chunks}
        yield
        e_cs = {c: jnp.exp(cs[c]) for c in chunks}
        e_neg = {c: jnp.exp(-cs[c]) for c in chunks}
        for c in chunks:
            wc[c] = e_cs[c][c_len - 1:, :]
        at = {c: a_all[sls[c]] * jnp.exp(cs[c] - lws[c]) for c in chunks}
        rt = {c: r1[sls[c]] * e_cs[c] for c in chunks}
        bt = {c: b_all[sls[c]] * e_neg[c] for c in chunks}
        kt = {c: k2[sls[c]] * e_neg[c] for c in chunks}
        vc = {c: v1[sls[c]] for c in chunks}
        yield
        lhs = {c: jnp.concatenate([at[c], rt[c]], axis=0).astype(BF16) for c in chunks}
        g_b = {c: _dot_nt(lhs[c], bd(bt[c])) for c in chunks}
        yield
        g_k = {c: _dot_nt(lhs[c], bd(kt[c])) for c in chunks}
        yield
        n_ab = {c: jnp.where(tri_strict, g_b[c][:c_len], 0.0) for c in chunks}
        a_ak = {c: jnp.where(tri_strict, g_k[c][:c_len], 0.0) for c in chunks}
        a_rb = {c: jnp.where(tri_incl, g_b[c][c_len:], 0.0).astype(BF16) for c in chunks}
        a_rk = {c: jnp.where(tri_incl, g_k[c][c_len:], 0.0).astype(BF16) for c in chunks}

        p = {c: eye + jnp.where((ts // 2) == (tj // 2), n_ab[c], 0.0) for c in chunks}
        for half in (2, 4, 8, 16, 32):
            lower_left = (((ts // (2 * half)) == (tj // (2 * half)))
                          & ((ts % (2 * half)) >= half) & ((tj % (2 * half)) < half))
            n21_ta = {c: _dot(jnp.where(lower_left, n_ab[c], 0.0).astype(BF16), bd(p[c])) for c in chunks}
            yield
            p = {c: p[c] + _dot(p[c].astype(BF16), bd(n21_ta[c])) for c in chunks}
            yield

        z = {c: _dot(p[c].astype(BF16), jnp.concatenate([bd(a_ak[c]), bd(at[c])], axis=1))
             for c in chunks}
        yield
        ap = {c: z[c][:, gl:] for c in chunks}
        bd_v = {c: bd(vc[c]) for c in chunks}
        u0 = {c: _dot(z[c][:, :gl].astype(BF16), bd_v[c]) for c in chunks}
        yield
        w = {c: _dot(a_rb[c], jnp.concatenate([bd(ap[c]), bd(u0[c])], axis=1)) for c in chunks}
        yield
        for c in chunks:
            rp[c] = rt[c] + w[c][:, :gl]
            y0[c] = w[c][:, gl:] + _dot(a_rk[c], bd_v[c])
        yield
        bh = {c: (bt[c] * wc[c]).astype(BF16) for c in chunks}
        bk = {c: jnp.concatenate([bh[c], (kt[c] * wc[c]).astype(BF16)], axis=0) for c in chunks}
        for c in chunks:
            g_s[c] = jnp.where(same_head, _dot_tn(ap[c].astype(BF16), bh[c]), 0.0).astype(BF16)
        yield
        for c in chunks:
            n_s[c] = fold(jnp.where(
                same_head, _dot_tn(jnp.concatenate([u0[c], vc[c]], axis=0).astype(BF16), bk[c]), 0.0))
        yield

    def state_update(chunks):
        for c in chunks:
            s_in[c] = state[0].astype(BF16)
            state[0] = state[0] * wc[c] + _dot(s_in[c], g_s[c]) + n_s[c]
            yield

    def epilogue(chunks):
        rows = slice(chunks[0] * c_len, (chunks[-1] + 1) * c_len)
        y = jnp.concatenate([_dot_nt(rp[c].astype(BF16), bd(s_in[c])) + y0[c] for c in chunks], axis=0)
        yield
        inv_n = 1.0 / HEAD_DIM
        mean = _seg_sum(y, ones_bd) * inv_n
        yield
        yc = y - mean
        var = _seg_sum1(yc * yc, ones_bd) * inv_n
        yield
        yn = yc * lax.rsqrt(var + LNX_EPS) * lnw_ref[...] + lnb_ref[...]
        bonus = _seg_sum(r1[rows] * k2[rows] * rk_ref[...], ones_bd) * v1[rows]
        y_ref[rows, :] = ((yn + bonus) * gg_ref[rows, :]).astype(y_ref.dtype)
        yield

    def drain(gen):
        for _ in gen:
            pass

    def interleave(serial, other, every):
        for k, _ in enumerate(other, 1):
            if k % every == 0:
                next(serial, None)
        drain(serial)

    wave = max(nch // 2, 1)
    waves = [list(range(w0, min(w0 + wave, nch))) for w0 in range(0, nch, wave)]
    par_stages, epi_stages = 20, 4
    drain(chunk_parallel(waves[0]))
    for wi, chunks in enumerate(waves):
        others, n_other = [], 0
        if wi + 1 < len(waves):
            others.append(chunk_parallel(waves[wi + 1]))
            n_other += par_stages
        if wi >= 1:
            others.append(epilogue(waves[wi - 1]))
            n_other += epi_stages
        interleave(state_update(chunks), itertools.chain(*others), max(n_other // len(chunks), 1))
    drain(epilogue(waves[-1]))
    s_ref[...] = state[0]


def _wkv(proj, lw, ag, gg, mu_r, mu_k, mu_v, k_k, k_a, r_k, lnx_w, lnx_b, bsz, t_len):
    n = proj.shape[0]
    dr = lw.shape[1]
    gl = GROUP_LANES
    ng = dr // gl
    lt = min(1024, t_len)
    nt = t_len // lt
    row = lambda off: (lambda b, g, t: (b * nt + t, off + g))
    par = lambda b, g, t: (0, g)
    vec = lambda a: a.reshape(1, dr)
    return pl.pallas_call(
        _wkv_kernel,
        out_shape=jax.ShapeDtypeStruct((n, dr), BF16),
        grid=(bsz, ng, nt),
        in_specs=[pl.BlockSpec((lt, gl), row(0)),
                  pl.BlockSpec((lt, gl), row(ng)),
                  pl.BlockSpec((lt, gl), row(2 * ng)),
                  pl.BlockSpec((lt, gl), row(0)),
                  pl.BlockSpec((lt, gl), row(0)),
                  pl.BlockSpec((lt, gl), row(0))] + [pl.BlockSpec((1, gl), par)] * 8,
        out_specs=pl.BlockSpec((lt, gl), row(0)),
        scratch_shapes=[pltpu.VMEM((HEAD_DIM, gl), F32), pltpu.VMEM((8, gl), F32)],
        compiler_params=pltpu.CompilerParams(
            dimension_semantics=("arbitrary", "arbitrary", "arbitrary"),
            vmem_limit_bytes=VMEM_LIMIT),
        name="wkv",
    )(proj, proj, proj, lw, ag, gg, vec(mu_r), vec(mu_k), vec(mu_v), vec(k_k), vec(k_a),
      vec(r_k), vec(lnx_w), vec(lnx_b))


def _outproj_kernel(yr_ref, u_ref, gb_ref, gc_ref, cw_ref, cnw_ref, wo_ref, x_ref, mod_ref,
                    n2w_ref, wr_ref, br_ref, x1_ref, h2_ref, meta_ref, cnt_ref,
                    ccarry_ref, cnt_acc_ref):
    tm = x_ref.shape[0]
    dc = u_ref.shape[1]

    @pl.when(pl.program_id(1) == 0)
    def _():
        ccarry_ref[...] = jnp.zeros_like(ccarry_ref)

    @pl.when((pl.program_id(0) == 0) & (pl.program_id(1) == 0))
    def _():
        cnt_acc_ref[...] = jnp.zeros_like(cnt_acc_ref)

    pre = gc_ref[...] * u_ref[...]
    prev1 = _shift_rows(pre, ccarry_ref[0:1, :])
    prev2 = _shift_rows(prev1, ccarry_ref[1:2, :])
    ccarry_ref[0:1, :] = pre[tm - 1:, :]
    ccarry_ref[1:2, :] = pre[tm - 2:tm - 1, :]
    yc = gb_ref[...] * (prev2 * cw_ref[0:1, :] + prev1 * cw_ref[1:2, :] + pre * cw_ref[2:3, :])
    ones_bd = _seg_ones(GROUP_LANES, CONV_GROUP_DIM)
    ycs = []
    for s in range(dc // GROUP_LANES):
        ys = yc[:, s * GROUP_LANES:(s + 1) * GROUP_LANES]
        ms = _seg_sum1(ys * ys, ones_bd) * (1.0 / CONV_GROUP_DIM)
        ycs.append(ys * lax.rsqrt(ms + RMS_EPS))
    y_conv = (jnp.concatenate(ycs, axis=1) * cnw_ref[...]).astype(BF16)

    dr = yr_ref.shape[1]
    y_mix = _dot(yr_ref[...], wo_ref[0:dr, :]) + _dot(y_conv, wo_ref[dr:, :])
    x1 = x_ref[...] + mod_ref[0, 2:3, :] * y_mix
    x1_ref[...] = x1
    ms = jnp.mean(x1 * x1, axis=-1, keepdims=True)
    h2 = x1 * lax.rsqrt(ms + RMS_EPS) * n2w_ref[...]
    h2 = h2 * (1.0 + mod_ref[0, 4:5, :]) + mod_ref[0, 3:4, :]
    h2_ref[...] = _pack_bf16_pair(h2)

    logits = _dot_x3(h2, wr_ref[...]) + br_ref[...]
    lane = lax.broadcasted_iota(jnp.int32, logits.shape, 1)
    neg = jnp.float32(-jnp.inf)
    big = jnp.int32(1 << 20)
    glog = jnp.where(lane < N_ROUTE_GROUPS, logits, neg)
    gmax = jnp.max(glog, axis=-1, keepdims=True)
    grp = jnp.min(jnp.where(glog == gmax, lane, big), axis=-1, keepdims=True)
    p_grp = 1.0 / jnp.sum(jnp.exp(glog - gmax), axis=-1, keepdims=True)
    e_lane = lane - N_ROUTE_GROUPS
    in_grp = (e_lane >= grp * EXPERTS_PER_GROUP) & (e_lane < (grp + 1) * EXPERTS_PER_GROUP)
    elog = jnp.where(in_grp, logits, neg)
    m1 = jnp.max(elog, axis=-1, keepdims=True)
    i1 = jnp.min(jnp.where(elog == m1, lane, big), axis=-1, keepdims=True)
    elog2 = jnp.where(lane == i1, neg, elog)
    m2 = jnp.max(elog2, axis=-1, keepdims=True)
    i2 = jnp.min(jnp.where(elog2 == m2, lane, big), axis=-1, keepdims=True)
    t2 = jnp.exp(m2 - m1)
    w1 = p_grp / (1.0 + t2)
    w2 = p_grp * t2 / (1.0 + t2)
    e1 = i1 - N_ROUTE_GROUPS
    e2 = i2 - N_ROUTE_GROUPS

    oh1 = lane == e1
    oh2 = lane == e2
    oh = jnp.where(oh1 | oh2, 1.0, 0.0)
    tr = lax.broadcasted_iota(jnp.int32, (tm, tm), 0)
    tc = lax.broadcasted_iota(jnp.int32, (tm, tm), 1)
    lower = jnp.where(tr > tc, 1.0, 0.0).astype(BF16)
    before = _dot(lower, oh.astype(BF16)) + cnt_acc_ref[...]
    rank1 = jnp.sum(jnp.where(oh1, before, 0.0), axis=-1, keepdims=True)
    rank2 = jnp.sum(jnp.where(oh2, before, 0.0), axis=-1, keepdims=True)
    cnt = cnt_acc_ref[...] + jnp.sum(oh, axis=0, keepdims=True)
    cnt_acc_ref[...] = cnt
    cnt_ref[...] = jnp.broadcast_to(cnt, cnt_ref.shape)

    meta = jnp.where(lane == 0, e1.astype(F32), 0.0)
    meta = jnp.where(lane == 1, e2.astype(F32), meta)
    meta = jnp.where(lane == 2, w1, meta)
    meta = jnp.where(lane == 3, w2, meta)
    meta = jnp.where(lane == 4, rank1, meta)
    meta = jnp.where(lane == 5, rank2, meta)
    meta_ref[...] = meta


def _outproj(yr, proj, conv_w, conv_norm_w, w_out_bf, x2, mod3, norm2_w, wr, br, bsz, t_len):
    n, d = x2.shape
    dr = yr.shape[1]
    dc = conv_w.shape[1]
    tm = min(512, t_len)
    nt = t_len // tm
    nseg = dr // dc
    row = lambda off: (lambda b, t: (b * nt + t, off))
    const = lambda b, t: (0, 0)
    return pl.pallas_call(
        _outproj_kernel,
        out_shape=(jax.ShapeDtypeStruct((n, d), F32), jax.ShapeDtypeStruct((n, d // 2), jnp.uint32),
                   jax.ShapeDtypeStruct((n, LANES), F32), jax.ShapeDtypeStruct((8, LANES), F32)),
        grid=(bsz, nt),
        in_specs=[pl.BlockSpec((tm, dr), row(0)),
                  pl.BlockSpec((tm, dc), row(3 * nseg)),
                  pl.BlockSpec((tm, dc), row(4 * nseg)),
                  pl.BlockSpec((tm, dc), row(5 * nseg)),
                  pl.BlockSpec((3, dc), const),
                  pl.BlockSpec((1, dc), const),
                  pl.BlockSpec((dr + dc, d), const),
                  pl.BlockSpec((tm, d), row(0)),
                  pl.BlockSpec((1, 6, d), lambda b, t: (b, 0, 0)),
                  pl.BlockSpec((1, d), const),
                  pl.BlockSpec((d, LANES), const),
                  pl.BlockSpec((1, LANES), const)],
        out_specs=(pl.BlockSpec((tm, d), row(0)), pl.BlockSpec((tm, d // 2), row(0)),
                   pl.BlockSpec((tm, LANES), row(0)), pl.BlockSpec((8, LANES), const)),
        scratch_shapes=[pltpu.VMEM((8, dc), F32), pltpu.VMEM((1, LANES), F32)],
        compiler_params=pltpu.CompilerParams(dimension_semantics=("arbitrary", "arbitrary"),
                                             vmem_limit_bytes=VMEM_LIMIT),
        name="outproj",
    )(yr, proj, proj, proj, conv_w, conv_norm_w.reshape(1, dc), w_out_bf, x2, mod3,
      norm2_w.reshape(1, d), wr, br)


def _dispatch_kernel(zoff_ref, dest_hbm, h2_ref, xs_hbm, idx_smem, zero_ref, zblk_ref,
                     sem_idx, sem_rows, sem_zero, sem_tail):
    i = pl.program_id(0)
    nsteps = pl.num_programs(0)
    th = idx_smem.shape[1] // 2
    slot = i % 2

    def idx_copy(step, sl):
        return pltpu.make_async_copy(dest_hbm.at[step], idx_smem.at[sl], sem_idx.at[sl])

    def rows_wait():
        pltpu.make_async_copy(xs_hbm.at[pl.ds(0, 2 * th)], xs_hbm.at[pl.ds(0, 2 * th)], sem_rows).wait()

    nblk = xs_hbm.shape[0] // ROUTE_BLOCK

    def tail_copy(b):
        start = pl.multiple_of(b * ROUTE_BLOCK, ROUTE_BLOCK)
        return pltpu.make_async_copy(zblk_ref, xs_hbm.at[pl.ds(start, ROUTE_BLOCK)], sem_tail)

    @pl.when(i == 0)
    def _():
        idx_copy(0, 0).start()
        zero_ref[...] = jnp.zeros_like(zero_ref)

        def zero_expert(e, carry):
            off = zoff_ref[e]

            def zero_row(j, c):
                pltpu.make_async_copy(zero_ref, xs_hbm.at[pl.ds(off + j, 1)], sem_zero).start()
                return c

            return lax.fori_loop(0, zoff_ref[N_EXPERTS + e], zero_row, carry)

        lax.fori_loop(0, N_EXPERTS, zero_expert, 0)
        total = zoff_ref[2 * N_EXPERTS]
        bulk = pl.multiple_of((total // 8) * 8, 8)

        @pl.when(bulk > 0)
        def _():
            pltpu.make_async_copy(xs_hbm.at[pl.ds(0, bulk)], xs_hbm.at[pl.ds(0, bulk)], sem_zero).wait()

        def wait_row(j, c):
            pltpu.make_async_copy(zero_ref, xs_hbm.at[pl.ds(0, 1)], sem_zero).wait()
            return c

        lax.fori_loop(0, total - bulk, wait_row, 0)

        zblk_ref[...] = jnp.zeros_like(zblk_ref)

        def tail_start(b, c):
            tail_copy(b).start()
            return c

        lax.fori_loop(zoff_ref[2 * N_EXPERTS + 1], nblk, tail_start, 0)

    idx_copy(i, slot).wait()

    @pl.when(i + 1 < nsteps)
    def _():
        idx_copy(i + 1, 1 - slot).start()

    for p in range(2):
        @pl.when(slot == p)
        def _(p=p):
            for t in range(th):
                for s in range(2):
                    d = idx_smem[p, 2 * t + s]
                    pltpu.make_async_copy(h2_ref.at[pl.ds(t, 1)], xs_hbm.at[pl.ds(d, 1)], sem_rows).start()

    rows_wait()

    @pl.when(i == nsteps - 1)
    def _():
        def tail_wait(b, c):
            tail_copy(b).wait()
            return c

        lax.fori_loop(zoff_ref[2 * N_EXPERTS + 1], nblk, tail_wait, 0)


def _dispatch(dest, zero_off, h2, n_rows):
    n, d = h2.shape
    th = min(256, n)
    return pl.pallas_call(
        _dispatch_kernel,
        out_shape=jax.ShapeDtypeStruct((n_rows, d), h2.dtype),
        grid_spec=pltpu.PrefetchScalarGridSpec(
            num_scalar_prefetch=1,
            grid=(n // th,),
            in_specs=[pl.BlockSpec(memory_space=pl.ANY),
                      pl.BlockSpec((th, d), lambda i, zo: (i, 0))],
            out_specs=pl.BlockSpec(memory_space=pl.ANY),
            scratch_shapes=[pltpu.SMEM((2, 2 * th), jnp.int32),
                            pltpu.VMEM((1, d), h2.dtype),
                            pltpu.VMEM((ROUTE_BLOCK, d), h2.dtype),
                            pltpu.SemaphoreType.DMA((2,)), pltpu.SemaphoreType.DMA,
                            pltpu.SemaphoreType.DMA, pltpu.SemaphoreType.DMA]),
        compiler_params=pltpu.CompilerParams(dimension_semantics=("arbitrary",),
                                             vmem_limit_bytes=VMEM_LIMIT),
        name="dispatch",
    )(zero_off, dest.reshape(n // th, 2 * th), h2)


def _expert_kernel(sched_ref, na_ref, xs_ref, wg_hbm, wu_hbm, wd_hbm, ys_ref,
                   wg_buf, wu_buf, wd_buf, sem):
    i = pl.program_id(0)

    def weight_copies(e, slot):
        return (pltpu.make_async_copy(wg_hbm.at[e], wg_buf.at[slot], sem.at[slot, 0]),
                pltpu.make_async_copy(wu_hbm.at[e], wu_buf.at[slot], sem.at[slot, 1]),
                pltpu.make_async_copy(wd_hbm.at[e], wd_buf.at[slot], sem.at[slot, 2]))

    @pl.when(i < na_ref[0])
    def _():
        e = sched_ref[0, i]
        slot = sched_ref[1, i]

        @pl.when(i == 0)
        def _():
            for cp in weight_copies(e, slot):
                cp.start()

        @pl.when(sched_ref[2, i] == 1)
        def _():
            for cp in weight_copies(e, slot):
                cp.wait()
            nxt = sched_ref[3, i]

            @pl.when(nxt >= 0)
            def _():
                for cp in weight_copies(nxt, 1 - slot):
                    cp.start()

        x = _unpack_bf16_pair(xs_ref[...]).astype(BF16)
        g = _dot(x, wg_buf[slot].astype(BF16))
        u = _dot(x, wu_buf[slot].astype(BF16))
        hid = (g * _sigmoid(g) * u).astype(BF16)
        ys_ref[...] = _pack_bf16_pair(_dot(hid, wd_buf[slot].astype(BF16)))

    @pl.when(i >= na_ref[0])
    def _():
        ys_ref[...] = jnp.zeros_like(ys_ref)


def _experts(block_e, n_active, xs, exp_gate, exp_up, exp_down):
    n_rows, dp = xs.shape
    _, d, de = exp_gate.shape
    nblk = n_rows // ROUTE_BLOCK
    idx = jnp.arange(nblk, dtype=jnp.int32)
    valid = idx < n_active[0]
    first = valid & ((idx == 0) | (block_e != jnp.roll(block_e, 1)))
    slot = jnp.maximum(jnp.cumsum(first.astype(jnp.int32)) - 1, 0) % 2
    first_at = jnp.where(first, idx, nblk)
    next_first = jnp.concatenate([lax.cummin(first_at, reverse=True)[1:], jnp.full((1,), nblk, jnp.int32)])
    nxt = jnp.where(next_first < nblk, block_e[jnp.minimum(next_first, nblk - 1)], -1)
    sched = jnp.stack([block_e, slot, first.astype(jnp.int32), nxt]).astype(jnp.int32)

    blk_row = lambda i, sc, na: (jnp.minimum(i, jnp.maximum(na[0] - 1, 0)), 0)
    return pl.pallas_call(
        _expert_kernel,
        out_shape=jax.ShapeDtypeStruct((n_rows, dp), jnp.uint32),
        grid_spec=pltpu.PrefetchScalarGridSpec(
            num_scalar_prefetch=2,
            grid=(nblk,),
            in_specs=[pl.BlockSpec((ROUTE_BLOCK, dp), blk_row),
                      pl.BlockSpec(memory_space=pl.ANY),
                      pl.BlockSpec(memory_space=pl.ANY),
                      pl.BlockSpec(memory_space=pl.ANY)],
            out_specs=pl.BlockSpec((ROUTE_BLOCK, dp), lambda i, sc, na: (i, 0)),
            scratch_shapes=[pltpu.VMEM((2, d, de), exp_gate.dtype),
                            pltpu.VMEM((2, d, de), exp_up.dtype),
                            pltpu.VMEM((2, de, d), exp_down.dtype),
                            pltpu.SemaphoreType.DMA((2, 3))]),
        compiler_params=pltpu.CompilerParams(dimension_semantics=("arbitrary",),
                                             vmem_limit_bytes=VMEM_LIMIT),
        name="experts",
    )(sched, n_active, xs, exp_gate, exp_up, exp_down)


def _combine_kernel(dest_hbm, ys_hbm, x1_ref, meta_ref, mod_ref, fw_ref, o_ref,
                    idx_smem, buf_ref, sem_idx, sem_rows):
    i = pl.program_id(0)
    nsteps = pl.num_programs(0)
    tj = x1_ref.shape[0]
    slot = i % 2

    def gather(step, sl):
        cp = pltpu.make_async_copy(dest_hbm.at[step], idx_smem.at[sl], sem_idx)
        cp.start()
        cp.wait()
        for t in range(tj):
            for s in range(2):
                d = idx_smem[sl, 2 * t + s]
                pltpu.make_async_copy(ys_hbm.at[pl.ds(d, 1)], buf_ref.at[sl, s, pl.ds(t, 1)],
                                      sem_rows.at[sl]).start()

    @pl.when(i == 0)
    def _():
        gather(0, 0)

    for p in range(2):
        @pl.when((i + 1 < nsteps) & (slot == 1 - p))
        def _(p=p):
            gather(i + 1, p)

    pltpu.make_async_copy(buf_ref.at[slot], buf_ref.at[slot], sem_rows.at[slot]).wait()

    meta = meta_ref[...]
    y = (meta[:, 2:3] * _unpack_bf16_pair(buf_ref[slot, 0])
         + meta[:, 3:4] * _unpack_bf16_pair(buf_ref[slot, 1]))
    x2 = x1_ref[...] + mod_ref[0, 5:6, :] * y
    ms = jnp.mean(x2 * x2, axis=-1, keepdims=True)
    o_ref[...] = x2 * lax.rsqrt(ms + RMS_EPS) * fw_ref[...]


def _combine(dest, ys, x1, meta, mod3, final_w, t_len):
    n, d = x1.shape
    tj = min(256, t_len)
    return pl.pallas_call(
        _combine_kernel,
        out_shape=jax.ShapeDtypeStruct((n, d), F32),
        grid=(n // tj,),
        in_specs=[pl.BlockSpec(memory_space=pl.ANY),
                  pl.BlockSpec(memory_space=pl.ANY),
                  pl.BlockSpec((tj, d), lambda i: (i, 0)),
                  pl.BlockSpec((tj, LANES), lambda i: (i, 0)),
                  pl.BlockSpec((1, 6, d), lambda i: (i * tj // t_len, 0, 0)),
                  pl.BlockSpec((1, d), lambda i: (0, 0))],
        out_specs=pl.BlockSpec((tj, d), lambda i: (i, 0)),
        scratch_shapes=[pltpu.SMEM((2, 2 * tj), jnp.int32), pltpu.VMEM((2, 2, tj, d // 2), jnp.uint32),
                        pltpu.SemaphoreType.DMA, pltpu.SemaphoreType.DMA((2,))],
        compiler_params=pltpu.CompilerParams(dimension_semantics=("arbitrary",),
                                             vmem_limit_bytes=VMEM_LIMIT),
        name="combine",
    )(dest.reshape(n // tj, 2 * tj), ys, x1, meta, mod3, final_w.reshape(1, d))


def _pad_cols(w, width):
    return jnp.pad(w, ((0, 0), (0, width - w.shape[1])))


def _pad_rows(w, height):
    return jnp.pad(w, ((0, height - w.shape[0]), (0, 0)))


def _layer(x2, c, bsz, t_len, w_ada, b_ada, norm1_w, w_in, mu_r, mu_k, mu_v, mu_w, mu_a, mu_g, w0,
           w_decay1, w_decay2, a0, w_aaa1, w_aaa2, w_gate1, w_gate2, k_k, k_a, r_k, lnx_w, lnx_b,
           conv_w, conv_norm_w, w_out, norm2_w, w_grp, b_grp, w_exp, b_exp, exp_gate, exp_up,
           exp_down, final_w):
    n, d = x2.shape
    dr = w0.shape[0]
    assert w_decay1.shape[1] <= 128 and w_aaa1.shape[1] <= 128 and w_gate1.shape[1] <= 256
    assert dr % GROUP_LANES == 0 and conv_w.shape[1] == dr and w_in.shape[1] == 6 * dr
    assert t_len % WKV_CHUNK == 0 and w_grp.shape[1] == N_ROUTE_GROUPS and w_exp.shape[1] == N_EXPERTS

    mod3 = _ada(c, w_ada, b_ada).reshape(bsz, 6, d)
    h, proj = _inproj(x2, mod3, norm1_w, w_in.astype(BF16), t_len)

    w1p = jnp.concatenate([_pad_cols(w_decay1, 128), _pad_cols(w_aaa1, 128), _pad_cols(w_gate1, 256)], axis=1)
    mup = jnp.concatenate([jnp.broadcast_to(mu_w[:, None], (d, 128)),
                           jnp.broadcast_to(mu_a[:, None], (d, 128)),
                           jnp.broadcast_to(mu_g[:, None], (d, 256))], axis=1)
    lw, ag, gg = _lora(h, w1p, mup, _pad_rows(w_decay2, 128), _pad_rows(w_aaa2, 128),
                       _pad_rows(w_gate2, 256), w0, a0, bsz, t_len)

    yr = _wkv(proj, lw, ag, gg, mu_r, mu_k, mu_v, k_k, k_a, r_k.reshape(dr), lnx_w, lnx_b, bsz, t_len)

    wr = _pad_cols(jnp.concatenate([w_grp, w_exp], axis=1), LANES)
    br = _pad_cols(jnp.concatenate([b_grp, b_exp]).reshape(1, -1), LANES)
    x1, h2, meta, cnt = _outproj(yr, proj, conv_w, conv_norm_w, w_out.astype(BF16), x2, mod3,
                                 norm2_w, wr, br, bsz, t_len)

    counts = cnt[0, :N_EXPERTS].astype(jnp.int32)
    padded = (counts + ROUTE_BLOCK - 1) // ROUTE_BLOCK * ROUTE_BLOCK
    pad_ends = jnp.cumsum(padded)
    pad_starts = pad_ends - padded
    top_e = meta[:, 0:2].astype(jnp.int32)
    rank = meta[:, 4:6].astype(jnp.int32)
    one_hot = top_e[:, :, None] == jnp.arange(N_EXPERTS, dtype=jnp.int32)
    dest = (jnp.sum(jnp.where(one_hot, pad_starts, 0), axis=-1) + rank).reshape(-1)
    nblk = (2 * n) // ROUTE_BLOCK + N_EXPERTS
    n_rows = nblk * ROUTE_BLOCK
    block_start = jnp.arange(nblk, dtype=jnp.int32) * ROUTE_BLOCK
    block_e = jnp.minimum(jnp.sum(pad_ends[None, :] <= block_start[:, None], axis=1),
                          N_EXPERTS - 1).astype(jnp.int32)
    n_active = (pad_ends[-1:] // ROUTE_BLOCK).astype(jnp.int32)

    pad_len = padded - counts
    zero_off = jnp.concatenate([pad_starts + counts, pad_len, jnp.sum(pad_len, keepdims=True),
                                n_active]).astype(jnp.int32)
    xs = _dispatch(dest, zero_off, h2, n_rows)
    ys = _experts(block_e, n_active, xs, exp_gate, exp_up, exp_down)
    return _combine(dest, ys, x1, meta, mod3, final_w, t_len)


def kernel(x, c, w_ada, b_ada, norm1_w, w_in, mu_r, mu_k, mu_v, mu_w, mu_a, mu_g, w0, w_decay1, w_decay2, a0, w_aaa1, w_aaa2, w_gate1, w_gate2, k_k, k_a, r_k, lnx_w, lnx_b, conv_w, conv_norm_w, w_out, norm2_w, w_grp, b_grp, w_exp, b_exp, exp_gate, exp_up, exp_down, final_w):
    bsz, t_len, d = x.shape
    assert w_ada.shape[0] == 1, "single-layer block"
    out = _layer(x.reshape(bsz * t_len, d), c, bsz, t_len, w_ada[0], b_ada[0], norm1_w[0], w_in[0],
                 mu_r[0], mu_k[0], mu_v[0], mu_w[0], mu_a[0], mu_g[0], w0[0], w_decay1[0], w_decay2[0],
                 a0[0], w_aaa1[0], w_aaa2[0], w_gate1[0], w_gate2[0], k_k[0], k_a[0], r_k[0], lnx_w[0],
                 lnx_b[0], conv_w[0], conv_norm_w[0], w_out[0], norm2_w[0], w_grp[0], b_grp[0],
                 w_exp[0], b_exp[0], exp_gate[0], exp_up[0], exp_down[0], final_w)
    return out.reshape(bsz, t_len, d)
```

```python
import itertools

import jax
import jax.numpy as jnp
from jax import lax
from jax.experimental import pallas as pl
from jax.experimental.pallas import tpu as pltpu

F32 = jnp.float32
BF16 = jnp.bfloat16

HEAD_DIM = 64
WKV_CHUNK = 64
GROUP_LANES = 256
N_ROUTE_GROUPS = 4
EXPERTS_PER_GROUP = 8
N_EXPERTS = N_ROUTE_GROUPS * EXPERTS_PER_GROUP
CONV_GROUP_DIM = 64
RMS_EPS = 1e-6
LNX_EPS = 64e-5
LANES = 128
LORA_PAD = 512
ROUTE_BLOCK = 256
NORM_ROWS = 32
VMEM_LIMIT = 56 * 1024 * 1024


def _dot(a, b, precision=None):
    return jnp.dot(a, b, preferred_element_type=F32, precision=precision)


def _dot_nt(a, b):
    return lax.dot_general(a, b, (((1,), (1,)), ((), ())), preferred_element_type=F32)


def _dot_tn(a, b):
    return lax.dot_general(a, b, (((0,), (0,)), ((), ())), preferred_element_type=F32)


def _split_bf16(x):
    hi = x.astype(BF16)
    return hi, (x - hi.astype(F32)).astype(BF16)


def _dot_x3(a, b):
    ah, al = _split_bf16(a)
    bh, bl = _split_bf16(b)
    return _dot(ah, bh) + _dot(al, bh) + _dot(ah, bl)


def _pack_bf16_pair(x):
    c = x.shape[1] // 2
    lo = pltpu.bitcast(x[:, :c].astype(BF16).astype(F32), jnp.uint32)
    hi = pltpu.bitcast(x[:, c:].astype(BF16).astype(F32), jnp.uint32)
    return (lo >> 16) | (hi & jnp.uint32(0xFFFF0000))


def _unpack_bf16_pair(w):
    lo = pltpu.bitcast(w << 16, F32)
    hi = pltpu.bitcast(w & jnp.uint32(0xFFFF0000), F32)
    return jnp.concatenate([lo, hi], axis=1)


def _sigmoid(x):
    return 1.0 / (1.0 + jnp.exp(-x))


def _shift_rows(x, carry_row):
    rolled = pltpu.roll(x, 1, axis=0)
    row = lax.broadcasted_iota(jnp.int32, x.shape, 0)
    return jnp.where(row == 0, carry_row, rolled)


def _seg_ones(n, seg):
    r = lax.broadcasted_iota(jnp.int32, (n, n), 0) // seg
    c = lax.broadcasted_iota(jnp.int32, (n, n), 1) // seg
    return jnp.where(r == c, 1.0, 0.0).astype(BF16)


def _seg_sum(x, ones_bd):
    hi = x.astype(BF16)
    lo = (x - hi.astype(F32)).astype(BF16)
    return _dot(hi, ones_bd) + _dot(lo, ones_bd)


def _seg_sum1(x, ones_bd):
    return _dot(x.astype(BF16), ones_bd)


def _ada_kernel(cb_ref, w_ref, b_ref, o_ref):
    nb = cb_ref.shape[0]
    tn = w_ref.shape[1]
    for b in range(nb):
        cv = cb_ref[b]
        s = cv * _sigmoid(cv)
        cols = [jnp.sum(s * w_ref[:, j * LANES:(j + 1) * LANES], axis=0, keepdims=True)
                for j in range(tn // LANES)]
        o_ref[b:b + 1, :] = jnp.concatenate(cols, axis=1) + b_ref[...]


def _ada(c, w_ada, b_ada):
    nb, d = c.shape
    n_out = w_ada.shape[1]
    tn = 512
    cb = jnp.broadcast_to(c[:, :, None], (nb, d, LANES))
    return pl.pallas_call(
        _ada_kernel,
        out_shape=jax.ShapeDtypeStruct((nb, n_out), F32),
        grid=(n_out // tn,),
        in_specs=[pl.BlockSpec((nb, d, LANES), lambda j: (0, 0, 0)),
                  pl.BlockSpec((d, tn), lambda j: (0, j)),
                  pl.BlockSpec((1, tn), lambda j: (0, j))],
        out_specs=pl.BlockSpec((nb, tn), lambda j: (0, j)),
        compiler_params=pltpu.CompilerParams(dimension_semantics=("arbitrary",),
                                             vmem_limit_bytes=VMEM_LIMIT),
        name="ada",
    )(cb, w_ada, b_ada.reshape(1, n_out))


def _inproj_kernel(x_ref, mod_ref, nw_ref, w_ref, h_ref, p_ref, hs_ref):
    @pl.when(pl.program_id(1) == 0)
    def _():
        rows = min(NORM_ROWS, x_ref.shape[0])
        gain = nw_ref[...] * (1.0 + mod_ref[0, 1:2, :])
        shift = mod_ref[0, 0:1, :]

        def norm_rows(r, carry):
            sl = pl.ds(pl.multiple_of(r * rows, rows), rows)
            x = x_ref[sl, :]
            ms = jnp.mean(x * x, axis=-1, keepdims=True)
            h = x * lax.rsqrt(ms + RMS_EPS) * gain + shift
            hb = h.astype(BF16)
            hs_ref[sl, :] = hb
            h_ref[sl, :] = hb
            return carry

        lax.fori_loop(0, x_ref.shape[0] // rows, norm_rows, 0, unroll=4)

    p_ref[...] = _dot(hs_ref[...], w_ref[...])


def _inproj(x2, mod3, norm_w, w_in_bf, t_len):
    n, d = x2.shape
    n_in = w_in_bf.shape[1]
    tm = min(1024, t_len)
    tn = 1024
    return pl.pallas_call(
        _inproj_kernel,
        out_shape=(jax.ShapeDtypeStruct((n, d), BF16), jax.ShapeDtypeStruct((n, n_in), F32)),
        grid=(n // tm, n_in // tn),
        in_specs=[pl.BlockSpec((tm, d), lambda i, j: (i, 0)),
                  pl.BlockSpec((1, 6, d), lambda i, j: (i * tm // t_len, 0, 0)),
                  pl.BlockSpec((1, d), lambda i, j: (0, 0)),
                  pl.BlockSpec((d, tn), lambda i, j: (0, j))],
        out_specs=(pl.BlockSpec((tm, d), lambda i, j: (i, 0)),
                   pl.BlockSpec((tm, tn), lambda i, j: (i, j))),
        scratch_shapes=[pltpu.VMEM((tm, d), BF16)],
        compiler_params=pltpu.CompilerParams(dimension_semantics=("arbitrary", "arbitrary"),
                                             vmem_limit_bytes=VMEM_LIMIT),
        name="inproj",
    )(x2, mod3, norm_w.reshape(1, d), w_in_bf)


def _lora_kernel(h_ref, w1_ref, mu_ref, wd2_ref, wa2_ref, wg2_ref, w0_ref, a0_ref,
                 lw_ref, ag_ref, gg_ref, carry_ref):
    @pl.when(pl.program_id(1) == 0)
    def _():
        carry_ref[...] = jnp.zeros_like(carry_ref)

    h = h_ref[...]
    w1 = w1_ref[...]
    p1 = _dot(h, w1.astype(BF16))
    p2 = _dot(h, (w1 * mu_ref[...]).astype(BF16))
    prev = _shift_rows(p2, carry_ref[...])
    carry_ref[...] = p2[p2.shape[0] - 1:, :]
    lo = p1 - p2 + prev
    dec = _dot(jnp.tanh(lo[:, 0:128]).astype(BF16), wd2_ref[...].astype(BF16))
    z = -(w0_ref[...] + dec)
    softplus = jnp.maximum(z, 0.0) + jnp.log(1.0 + jnp.exp(-jnp.abs(z)))
    lw_ref[...] = -jnp.exp(-softplus - 0.5)
    ag_ref[...] = _sigmoid(a0_ref[...] + _dot(lo[:, 128:256].astype(BF16), wa2_ref[...].astype(BF16)))
    gg_ref[...] = _dot(_sigmoid(lo[:, 256:512]).astype(BF16), wg2_ref[...].astype(BF16))


def _lora(h, w1p, mup, wd2p, wa2p, wg2p, w0, a0, bsz, t_len):
    n, d = h.shape
    dr = w0.shape[-1]
    tm = min(512, t_len)
    nt = t_len // tm
    row = lambda b, t: (b * nt + t, 0)
    const = lambda b, t: (0, 0)
    out = jax.ShapeDtypeStruct((n, dr), F32)
    return pl.pallas_call(
        _lora_kernel,
        out_shape=(out, out, out),
        grid=(bsz, nt),
        in_specs=[pl.BlockSpec((tm, d), row),
                  pl.BlockSpec((d, LORA_PAD), const),
                  pl.BlockSpec((d, LORA_PAD), const),
                  pl.BlockSpec((128, dr), const),
                  pl.BlockSpec((128, dr), const),
                  pl.BlockSpec((256, dr), const),
                  pl.BlockSpec((1, dr), const),
                  pl.BlockSpec((1, dr), const)],
        out_specs=(pl.BlockSpec((tm, dr), row),) * 3,
        scratch_shapes=[pltpu.VMEM((1, LORA_PAD), F32)],
        compiler_params=pltpu.CompilerParams(dimension_semantics=("arbitrary", "arbitrary"),
                                             vmem_limit_bytes=VMEM_LIMIT),
        name="lora",
    )(h, w1p, mup, wd2p, wa2p, wg2p, w0.reshape(1, dr), a0.reshape(1, dr))


def _wkv_kernel(r_ref, k_ref, v_ref, lw_ref, ag_ref, gg_ref, mur_ref, muk_ref, muv_ref,
                kk_ref, ka_ref, rk_ref, lnw_ref, lnb_ref, y_ref, s_ref, carry_ref):
    lt = r_ref.shape[0]
    gl = GROUP_LANES
    c_len = WKV_CHUNK
    heads = gl // HEAD_DIM

    @pl.when(pl.program_id(2) == 0)
    def _():
        s_ref[...] = jnp.zeros_like(s_ref)
        carry_ref[...] = jnp.zeros_like(carry_ref)

    ones_bd = _seg_ones(gl, HEAD_DIM)

    def lerp(ref, mu_ref, idx):
        z = ref[...]
        prev = _shift_rows(z, carry_ref[idx:idx + 1, :])
        carry_ref[idx:idx + 1, :] = z[lt - 1:, :]
        return z + (prev - z) * mu_ref[...]

    r1 = lerp(r_ref, mur_ref, 0)
    k1 = lerp(k_ref, muk_ref, 1)
    v1 = lerp(v_ref, muv_ref, 2)
    ag = ag_ref[...]
    kk = k1 * kk_ref[...]
    kk = kk * lax.rsqrt(jnp.maximum(_seg_sum1(kk * kk, ones_bd), 1e-24))
    k2 = k1 * (1.0 + (ag - 1.0) * ka_ref[...])
    a_all = -kk
    b_all = kk * ag
    lw = lw_ref[...]

    rows = lax.broadcasted_iota(jnp.int32, (gl, gl), 0)
    cols = lax.broadcasted_iota(jnp.int32, (gl, gl), 1)
    same_head = (rows // c_len) == (cols // HEAD_DIM)
    ts = lax.broadcasted_iota(jnp.int32, (c_len, gl), 0)
    tj = lax.broadcasted_iota(jnp.int32, (c_len, gl), 1) % c_len
    tri_strict = ts > tj
    tri_incl = ts >= tj
    eye = jnp.where(ts == tj, 1.0, 0.0).astype(F32)
    cr = lax.broadcasted_iota(jnp.int32, (c_len, c_len), 0)
    cc = lax.broadcasted_iota(jnp.int32, (c_len, c_len), 1)
    cum_b = jnp.where(cr >= cc, 1.0, 0.0).astype(BF16)

    def bd(x):
        xb = x.astype(BF16)
        return jnp.where(same_head, jnp.concatenate([xb] * heads, axis=0), jnp.zeros((), BF16))

    def fold(x):
        out = x[0:c_len]
        for hh in range(1, heads):
            out = out + x[hh * c_len:(hh + 1) * c_len]
        return out

    nch = lt // c_len
    sls = [slice(c * c_len, (c + 1) * c_len) for c in range(nch)]

    def cumsum3(x):
        hi = x.astype(BF16)
        r = x - hi.astype(F32)
        mid = r.astype(BF16)
        lo = (r - mid.astype(F32)).astype(BF16)
        return _dot(cum_b, hi) + _dot(cum_b, mid) + _dot(cum_b, lo)

    wc, g_s, n_s, rp, y0, s_in = {}, {}, {}, {}, {}, {}
    state = [s_ref[...]]

    def chunk_parallel(chunks):
        lws = {c: lw[sls[c]] for c in chunks}
        cs = {c: cumsum3(lws[c]) for c in chunks}
        yield
        e_cs = {c: jnp.exp(cs[c]) for c in chunks}
        e_neg = {c: jnp.exp(-cs[c]) for c in chunks}
        for c in chunks:
            wc[c] = e_cs[c][c_len - 1:, :]
        at = {c: a_all[sls[c]] * jnp.exp(cs[c] - lws[c]) for c in chunks}
        rt = {c: r1[sls[c]] * e_cs[c] for c in chunks}
        bt = {c: b_all[sls[c]] * e_neg[c] for c in chunks}
        kt = {c: k2[sls[c]] * e_neg[c] for c in chunks}
        vc = {c: v1[sls[c]] for c in chunks}
        yield
        lhs = {c: jnp.concatenate([at[c], rt[c]], axis=0).astype(BF16) for c in chunks}
        g_b = {c: _dot_nt(lhs[c], bd(bt[c])) for c in chunks}
        yield
        g_k = {c: _dot_nt(lhs[c], bd(kt[c])) for c in chunks}
        yield
        n_ab = {c: jnp.where(tri_strict, g_b[c][:c_len], 0.0) for c in chunks}
        a_ak = {c: jnp.where(tri_strict, g_k[c][:c_len], 0.0) for c in chunks}
        a_rb = {c: jnp.where(tri_incl, g_b[c][c_len:], 0.0).astype(BF16) for c in chunks}
        a_rk = {c: jnp.where(tri_incl, g_k[c][c_len:], 0.0).astype(BF16) for c in chunks}

        p = {c: eye + jnp.where((ts // 2) == (tj // 2), n_ab[c], 0.0) for c in chunks}
        for half in (2, 4, 8, 16, 32):
            lower_left = (((ts // (2 * half)) == (tj // (2 * half)))
                          & ((ts % (2 * half)) >= half) & ((tj % (2 * half)) < half))
            n21_ta = {c: _dot(jnp.where(lower_left, n_ab[c], 0.0).astype(BF16), bd(p[c])) for c in chunks}
            yield
            p = {c: p[c] + _dot(p[c].astype(BF16), bd(n21_ta[c])) for c in chunks}
            yield

        z = {c: _dot(p[c].astype(BF16), jnp.concatenate([bd(a_ak[c]), bd(at[c])], axis=1))
             for c in chunks}
        yield
        ap = {c: z[c][:, gl:] for c in chunks}
        bd_v = {c: bd(vc[c]) for c in chunks}
        u0 = {c: _dot(z[c][:, :gl].astype(BF16), bd_v[c]) for c in chunks}
        yield
        w = {c: _dot(a_rb[c], jnp.concatenate([bd(ap[c]), bd(u0[c])], axis=1)) for c in chunks}
        yield
        for c in chunks:
            rp[c] = rt[c] + w[c][:, :gl]
            y0[c] = w[c][:, gl:] + _dot(a_rk[c], bd_v[c])
        yield
        bh = {c: (bt[c] * wc[c]).astype(BF16) for c in chunks}
        bk = {c: jnp.concatenate([bh[c], (kt[c] * wc[c]).astype(BF16)], axis=0) for c in chunks}
        for c in chunks:
            g_s[c] = jnp.where(same_head, _dot_tn(ap[c].astype(BF16), bh[c]), 0.0).astype(BF16)
        yield
        for c in chunks:
            n_s[c] = fold(jnp.where(
                same_head, _dot_tn(jnp.concatenate([u0[c], vc[c]], axis=0).astype(BF16), bk[c]), 0.0))
        yield

    def state_update(chunks):
        for c in chunks:
            s_in[c] = state[0].astype(BF16)
            state[0] = state[0] * wc[c] + _dot(s_in[c], g_s[c]) + n_s[c]
            yield

    def epilogue(chunks):
        rows = slice(chunks[0] * c_len, (chunks[-1] + 1) * c_len)
        y = jnp.concatenate([_dot_nt(rp[c].astype(BF16), bd(s_in[c])) + y0[c] for c in chunks], axis=0)
        yield
        inv_n = 1.0 / HEAD_DIM
        mean = _seg_sum(y, ones_bd) * inv_n
        yield
        yc = y - mean
        var = _seg_sum1(yc * yc, ones_bd) * inv_n
        yield
        yn = yc * lax.rsqrt(var + LNX_EPS) * lnw_ref[...] + lnb_ref[...]
        bonus = _seg_sum(r1[rows] * k2[rows] * rk_ref[...], ones_bd) * v1[rows]
        y_ref[rows, :] = ((yn + bonus) * gg_ref[rows, :]).astype(y_ref.dtype)
        yield

    def drain(gen):
        for _ in gen:
            pass

    def interleave(serial, other, every):
        for k, _ in enumerate(other, 1):
            if k % every == 0:
                next(serial, None)
        drain(serial)

    wave = max(nch // 2, 1)
    waves = [list(range(w0, min(w0 + wave, nch))) for w0 in range(0, nch, wave)]
    par_stages, epi_stages = 20, 4
    drain(chunk_parallel(waves[0]))
    for wi, chunks in enumerate(waves):
        others, n_other = [], 0
        if wi + 1 < len(waves):
            others.append(chunk_parallel(waves[wi + 1]))
            n_other += par_stages
        if wi >= 1:
            others.append(epilogue(waves[wi - 1]))
            n_other += epi_stages
        interleave(state_update(chunks), itertools.chain(*others), max(n_other // len(chunks), 1))
    drain(epilogue(waves[-1]))
    s_ref[...] = state[0]


def _wkv(proj, lw, ag, gg, mu_r, mu_k, mu_v, k_k, k_a, r_k, lnx_w, lnx_b, bsz, t_len):
    n = proj.shape[0]
    dr = lw.shape[1]
    gl = GROUP_LANES
    ng = dr // gl
    lt = min(1024, t_len)
    nt = t_len // lt
    row = lambda off: (lambda b, g, t: (b * nt + t, off + g))
    par = lambda b, g, t: (0, g)
    vec = lambda a: a.reshape(1, dr)
    return pl.pallas_call(
        _wkv_kernel,
        out_shape=jax.ShapeDtypeStruct((n, dr), BF16),
        grid=(bsz, ng, nt),
        in_specs=[pl.BlockSpec((lt, gl), row(0)),
                  pl.BlockSpec((lt, gl), row(ng)),
                  pl.BlockSpec((lt, gl), row(2 * ng)),
                  pl.BlockSpec((lt, gl), row(0)),
                  pl.BlockSpec((lt, gl), row(0)),
                  pl.BlockSpec((lt, gl), row(0))] + [pl.BlockSpec((1, gl), par)] * 8,
        out_specs=pl.BlockSpec((lt, gl), row(0)),
        scratch_shapes=[pltpu.VMEM((HEAD_DIM, gl), F32), pltpu.VMEM((8, gl), F32)],
        compiler_params=pltpu.CompilerParams(
            dimension_semantics=("arbitrary", "arbitrary", "arbitrary"),
            vmem_limit_bytes=VMEM_LIMIT),
        name="wkv",
    )(proj, proj, proj, lw, ag, gg, vec(mu_r), vec(mu_k), vec(mu_v), vec(k_k), vec(k_a),
      vec(r_k), vec(lnx_w), vec(lnx_b))


def _outproj_kernel(yr_ref, u_ref, gb_ref, gc_ref, cw_ref, cnw_ref, wo_ref, x_ref, mod_ref,
                    n2w_ref, wr_ref, br_ref, x1_ref, h2_ref, meta_ref, cnt_ref,
                    ccarry_ref, cnt_acc_ref):
    tm = x_ref.shape[0]
    dc = u_ref.shape[1]

    @pl.when(pl.program_id(1) == 0)
    def _():
        ccarry_ref[...] = jnp.zeros_like(ccarry_ref)

    @pl.when((pl.program_id(0) == 0) & (pl.program_id(1) == 0))
    def _():
        cnt_acc_ref[...] = jnp.zeros_like(cnt_acc_ref)

    pre = gc_ref[...] * u_ref[...]
    prev1 = _shift_rows(pre, ccarry_ref[0:1, :])
    prev2 = _shift_rows(prev1, ccarry_ref[1:2, :])
    ccarry_ref[0:1, :] = pre[tm - 1:, :]
    ccarry_ref[1:2, :] = pre[tm - 2:tm - 1, :]
    yc = gb_ref[...] * (prev2 * cw_ref[0:1, :] + prev1 * cw_ref[1:2, :] + pre * cw_ref[2:3, :])
    ones_bd = _seg_ones(GROUP_LANES, CONV_GROUP_DIM)
    ycs = []
    for s in range(dc // GROUP_LANES):
        ys = yc[:, s * GROUP_LANES:(s + 1) * GROUP_LANES]
        ms = _seg_sum1(ys * ys, ones_bd) * (1.0 / CONV_GROUP_DIM)
        ycs.append(ys * lax.rsqrt(ms + RMS_EPS))
    y_conv = (jnp.concatenate(ycs, axis=1) * cnw_ref[...]).astype(BF16)

    dr = yr_ref.shape[1]
    y_mix = _dot(yr_ref[...], wo_ref[0:dr, :]) + _dot(y_conv, wo_ref[dr:, :])
    x1 = x_ref[...] + mod_ref[0, 2:3, :] * y_mix
    x1_ref[...] = x1
    ms = jnp.mean(x1 * x1, axis=-1, keepdims=True)
    h2 = x1 * lax.rsqrt(ms + RMS_EPS) * n2w_ref[...]
    h2 = h2 * (1.0 + mod_ref[0, 4:5, :]) + mod_ref[0, 3:4, :]
    h2_ref[...] = _pack_bf16_pair(h2)

    logits = _dot_x3(h2, wr_ref[...]) + br_ref[...]
    lane = lax.broadcasted_iota(jnp.int32, logits.shape, 1)
    neg = jnp.float32(-jnp.inf)
    big = jnp.int32(1 << 20)
    glog = jnp.where(lane < N_ROUTE_GROUPS, logits, neg)
    gmax = jnp.max(glog, axis=-1, keepdims=True)
    grp = jnp.min(jnp.where(glog == gmax, lane, big), axis=-1, keepdims=True)
    p_grp = 1.0 / jnp.sum(jnp.exp(glog - gmax), axis=-1, keepdims=True)
    e_lane = lane - N_ROUTE_GROUPS
    in_grp = (e_lane >= grp * EXPERTS_PER_GROUP) & (e_lane < (grp + 1) * EXPERTS_PER_GROUP)
    elog = jnp.where(in_grp, logits, neg)
    m1 = jnp.max(elog, axis=-1, keepdims=True)
    i1 = jnp.min(jnp.where(elog == m1, lane, big), axis=-1, keepdims=True)
    elog2 = jnp.where(lane == i1, neg, elog)
    m2 = jnp.max(elog2, axis=-1, keepdims=True)
    i2 = jnp.min(jnp.where(elog2 == m2, lane, big), axis=-1, keepdims=True)
    t2 = jnp.exp(m2 - m1)
    w1 = p_grp / (1.0 + t2)
    w2 = p_grp * t2 / (1.0 + t2)
    e1 = i1 - N_ROUTE_GROUPS
    e2 = i2 - N_ROUTE_GROUPS

    oh1 = lane == e1
    oh2 = lane == e2
    oh = jnp.where(oh1 | oh2, 1.0, 0.0)
    tr = lax.broadcasted_iota(jnp.int32, (tm, tm), 0)
    tc = lax.broadcasted_iota(jnp.int32, (tm, tm), 1)
    lower = jnp.where(tr > tc, 1.0, 0.0).astype(BF16)
    before = _dot(lower, oh.astype(BF16)) + cnt_acc_ref[...]
    rank1 = jnp.sum(jnp.where(oh1, before, 0.0), axis=-1, keepdims=True)
    rank2 = jnp.sum(jnp.where(oh2, before, 0.0), axis=-1, keepdims=True)
    cnt = cnt_acc_ref[...] + jnp.sum(oh, axis=0, keepdims=True)
    cnt_acc_ref[...] = cnt
    cnt_ref[...] = jnp.broadcast_to(cnt, cnt_ref.shape)

    meta = jnp.where(lane == 0, e1.astype(F32), 0.0)
    meta = jnp.where(lane == 1, e2.astype(F32), meta)
    meta = jnp.where(lane == 2, w1, meta)
    meta = jnp.where(lane == 3, w2, meta)
    meta = jnp.where(lane == 4, rank1, meta)
    meta = jnp.where(lane == 5, rank2, meta)
    meta_ref[...] = meta


def _outproj(yr, proj, conv_w, conv_norm_w, w_out_bf, x2, mod3, norm2_w, wr, br, bsz, t_len):
    n, d = x2.shape
    dr = yr.shape[1]
    dc = conv_w.shape[1]
    tm = min(512, t_len)
    nt = t_len // tm
    nseg = dr // dc
    row = lambda off: (lambda b, t: (b * nt + t, off))
    const = lambda b, t: (0, 0)
    return pl.pallas_call(
        _outproj_kernel,
        out_shape=(jax.ShapeDtypeStruct((n, d), F32), jax.ShapeDtypeStruct((n, d // 2), jnp.uint32),
                   jax.ShapeDtypeStruct((n, LANES), F32), jax.ShapeDtypeStruct((8, LANES), F32)),
        grid=(bsz, nt),
        in_specs=[pl.BlockSpec((tm, dr), row(0)),
                  pl.BlockSpec((tm, dc), row(3 * nseg)),
                  pl.BlockSpec((tm, dc), row(4 * nseg)),
                  pl.BlockSpec((tm, dc), row(5 * nseg)),
                  pl.BlockSpec((3, dc), const),
                  pl.BlockSpec((1, dc), const),
                  pl.BlockSpec((dr + dc, d), const),
                  pl.BlockSpec((tm, d), row(0)),
                  pl.BlockSpec((1, 6, d), lambda b, t: (b, 0, 0)),
                  pl.BlockSpec((1, d), const),
                  pl.BlockSpec((d, LANES), const),
                  pl.BlockSpec((1, LANES), const)],
        out_specs=(pl.BlockSpec((tm, d), row(0)), pl.BlockSpec((tm, d // 2), row(0)),
                   pl.BlockSpec((tm, LANES), row(0)), pl.BlockSpec((8, LANES), const)),
        scratch_shapes=[pltpu.VMEM((8, dc), F32), pltpu.VMEM((1, LANES), F32)],
        compiler_params=pltpu.CompilerParams(dimension_semantics=("arbitrary", "arbitrary"),
                                             vmem_limit_bytes=VMEM_LIMIT),
        name="outproj",
    )(yr, proj, proj, proj, conv_w, conv_norm_w.reshape(1, dc), w_out_bf, x2, mod3,
      norm2_w.reshape(1, d), wr, br)


def _dispatch_kernel(zoff_ref, dest_hbm, h2_ref, xs_hbm, idx_smem, zero_ref, zblk_ref,
                     sem_idx, sem_rows, sem_zero, sem_tail):
    i = pl.program_id(0)
    nsteps = pl.num_programs(0)
    th = idx_smem.shape[1] // 2
    slot = i % 2

    def idx_copy(step, sl):
        return pltpu.make_async_copy(dest_hbm.at[step], idx_smem.at[sl], sem_idx.at[sl])

    def rows_wait():
        pltpu.make_async_copy(xs_hbm.at[pl.ds(0, 2 * th)], xs_hbm.at[pl.ds(0, 2 * th)], sem_rows).wait()

    nblk = xs_hbm.shape[0] // ROUTE_BLOCK

    def tail_copy(b):
        start = pl.multiple_of(b * ROUTE_BLOCK, ROUTE_BLOCK)
        return pltpu.make_async_copy(zblk_ref, xs_hbm.at[pl.ds(start, ROUTE_BLOCK)], sem_tail)

    @pl.when(i == 0)
    def _():
        idx_copy(0, 0).start()
        zero_ref[...] = jnp.zeros_like(zero_ref)

        def zero_expert(e, carry):
            off = zoff_ref[e]

            def zero_row(j, c):
                pltpu.make_async_copy(zero_ref, xs_hbm.at[pl.ds(off + j, 1)], sem_zero).start()
                return c

            return lax.fori_loop(0, zoff_ref[N_EXPERTS + e], zero_row, carry)

        lax.fori_loop(0, N_EXPERTS, zero_expert, 0)
        total = zoff_ref[2 * N_EXPERTS]
        bulk = pl.multiple_of((total // 8) * 8, 8)

        @pl.when(bulk > 0)
        def _():
            pltpu.make_async_copy(xs_hbm.at[pl.ds(0, bulk)], xs_hbm.at[pl.ds(0, bulk)], sem_zero).wait()

        def wait_row(j, c):
            pltpu.make_async_copy(zero_ref, xs_hbm.at[pl.ds(0, 1)], sem_zero).wait()
            return c

        lax.fori_loop(0, total - bulk, wait_row, 0)

        zblk_ref[...] = jnp.zeros_like(zblk_ref)

        def tail_start(b, c):
            tail_copy(b).start()
            return c

        lax.fori_loop(zoff_ref[2 * N_EXPERTS + 1], nblk, tail_start, 0)

    idx_copy(i, slot).wait()

    @pl.when(i + 1 < nsteps)
    def _():
        idx_copy(i + 1, 1 - slot).start()

    for p in range(2):
        @pl.when(slot == p)
        def _(p=p):
            for t in range(th):
                for s in range(2):
                    d = idx_smem[p, 2 * t + s]
                    pltpu.make_async_copy(h2_ref.at[pl.ds(t, 1)], xs_hbm.at[pl.ds(d, 1)],
                                          sem_rows).start(priority=s)

    rows_wait()

    @pl.when(i == nsteps - 1)
    def _():
        def tail_wait(b, c):
            tail_copy(b).wait()
            return c

        lax.fori_loop(zoff_ref[2 * N_EXPERTS + 1], nblk, tail_wait, 0)


def _dispatch(dest, zero_off, h2, n_rows):
    n, d = h2.shape
    th = min(256, n)
    return pl.pallas_call(
        _dispatch_kernel,
        out_shape=jax.ShapeDtypeStruct((n_rows, d), h2.dtype),
        grid_spec=pltpu.PrefetchScalarGridSpec(
            num_scalar_prefetch=1,
            grid=(n // th,),
            in_specs=[pl.BlockSpec(memory_space=pl.ANY),
                      pl.BlockSpec((th, d), lambda i, zo: (i, 0))],
            out_specs=pl.BlockSpec(memory_space=pl.ANY),
            scratch_shapes=[pltpu.SMEM((2, 2 * th), jnp.int32),
                            pltpu.VMEM((1, d), h2.dtype),
                            pltpu.VMEM((ROUTE_BLOCK, d), h2.dtype),
                            pltpu.SemaphoreType.DMA((2,)), pltpu.SemaphoreType.DMA,
                            pltpu.SemaphoreType.DMA, pltpu.SemaphoreType.DMA]),
        compiler_params=pltpu.CompilerParams(dimension_semantics=("arbitrary",),
                                             vmem_limit_bytes=VMEM_LIMIT),
        name="dispatch",
    )(zero_off, dest.reshape(n // th, 2 * th), h2)


def _expert_kernel(sched_ref, na_ref, xs_ref, wg_hbm, wu_hbm, wd_hbm, ys_ref,
                   wg_buf, wu_buf, wd_buf, sem):
    i = pl.program_id(0)

    def weight_copies(e, slot):
        return (pltpu.make_async_copy(wg_hbm.at[e], wg_buf.at[slot], sem.at[slot, 0]),
                pltpu.make_async_copy(wu_hbm.at[e], wu_buf.at[slot], sem.at[slot, 1]),
                pltpu.make_async_copy(wd_hbm.at[e], wd_buf.at[slot], sem.at[slot, 2]))

    @pl.when(i < na_ref[0])
    def _():
        e = sched_ref[0, i]
        slot = sched_ref[1, i]

        @pl.when(i == 0)
        def _():
            for cp in weight_copies(e, slot):
                cp.start()

        @pl.when(sched_ref[2, i] == 1)
        def _():
            for cp in weight_copies(e, slot):
                cp.wait()
            nxt = sched_ref[3, i]

            @pl.when(nxt >= 0)
            def _():
                for cp in weight_copies(nxt, 1 - slot):
                    cp.start()

        x = _unpack_bf16_pair(xs_ref[...]).astype(BF16)
        g = _dot(x, wg_buf[slot].astype(BF16))
        u = _dot(x, wu_buf[slot].astype(BF16))
        hid = (g * _sigmoid(g) * u).astype(BF16)
        ys_ref[...] = _pack_bf16_pair(_dot(hid, wd_buf[slot].astype(BF16)))

    @pl.when(i >= na_ref[0])
    def _():
        ys_ref[...] = jnp.zeros_like(ys_ref)


def _experts(block_e, n_active, xs, exp_gate, exp_up, exp_down):
    n_rows, dp = xs.shape
    _, d, de = exp_gate.shape
    nblk = n_rows // ROUTE_BLOCK
    idx = jnp.arange(nblk, dtype=jnp.int32)
    valid = idx < n_active[0]
    first = valid & ((idx == 0) | (block_e != jnp.roll(block_e, 1)))
    slot = jnp.maximum(jnp.cumsum(first.astype(jnp.int32)) - 1, 0) % 2
    first_at = jnp.where(first, idx, nblk)
    next_first = jnp.concatenate([lax.cummin(first_at, reverse=True)[1:], jnp.full((1,), nblk, jnp.int32)])
    nxt = jnp.where(next_first < nblk, block_e[jnp.minimum(next_first, nblk - 1)], -1)
    sched = jnp.stack([block_e, slot, first.astype(jnp.int32), nxt]).astype(jnp.int32)

    blk_row = lambda i, sc, na: (jnp.minimum(i, jnp.maximum(na[0] - 1, 0)), 0)
    return pl.pallas_call(
        _expert_kernel,
        out_shape=jax.ShapeDtypeStruct((n_rows, dp), jnp.uint32),
        grid_spec=pltpu.PrefetchScalarGridSpec(
            num_scalar_prefetch=2,
            grid=(nblk,),
            in_specs=[pl.BlockSpec((ROUTE_BLOCK, dp), blk_row),
                      pl.BlockSpec(memory_space=pl.ANY),
                      pl.BlockSpec(memory_space=pl.ANY),
                      pl.BlockSpec(memory_space=pl.ANY)],
            out_specs=pl.BlockSpec((ROUTE_BLOCK, dp), lambda i, sc, na: (i, 0)),
            scratch_shapes=[pltpu.VMEM((2, d, de), exp_gate.dtype),
                            pltpu.VMEM((2, d, de), exp_up.dtype),
                            pltpu.VMEM((2, de, d), exp_down.dtype),
                            pltpu.SemaphoreType.DMA((2, 3))]),
        compiler_params=pltpu.CompilerParams(dimension_semantics=("arbitrary",),
                                             vmem_limit_bytes=VMEM_LIMIT),
        name="experts",
    )(sched, n_active, xs, exp_gate, exp_up, exp_down)


def _combine_kernel(dest_hbm, ys_hbm, x1_ref, meta_ref, mod_ref, fw_ref, o_ref,
                    idx_smem, buf_ref, sem_idx, sem_rows):
    i = pl.program_id(0)
    nsteps = pl.num_programs(0)
    tj = x1_ref.shape[0]
    slot = i % 2

    def gather(step, sl):
        cp = pltpu.make_async_copy(dest_hbm.at[step], idx_smem.at[sl], sem_idx)
        cp.start()
        cp.wait()
        for t in range(tj):
            for s in range(2):
                d = idx_smem[sl, 2 * t + s]
                pltpu.make_async_copy(ys_hbm.at[pl.ds(d, 1)], buf_ref.at[sl, s, pl.ds(t, 1)],
                                      sem_rows.at[sl]).start(priority=s)

    @pl.when(i == 0)
    def _():
        gather(0, 0)

    for p in range(2):
        @pl.when((i + 1 < nsteps) & (slot == 1 - p))
        def _(p=p):
            gather(i + 1, p)

    pltpu.make_async_copy(buf_ref.at[slot], buf_ref.at[slot], sem_rows.at[slot]).wait()

    meta = meta_ref[...]
    y = (meta[:, 2:3] * _unpack_bf16_pair(buf_ref[slot, 0])
         + meta[:, 3:4] * _unpack_bf16_pair(buf_ref[slot, 1]))
    x2 = x1_ref[...] + mod_ref[0, 5:6, :] * y
    ms = jnp.mean(x2 * x2, axis=-1, keepdims=True)
    o_ref[...] = x2 * lax.rsqrt(ms + RMS_EPS) * fw_ref[...]


def _combine(dest, ys, x1, meta, mod3, final_w, t_len):
    n, d = x1.shape
    tj = min(256, t_len)
    return pl.pallas_call(
        _combine_kernel,
        out_shape=jax.ShapeDtypeStruct((n, d), F32),
        grid=(n // tj,),
        in_specs=[pl.BlockSpec(memory_space=pl.ANY),
                  pl.BlockSpec(memory_space=pl.ANY),
                  pl.BlockSpec((tj, d), lambda i: (i, 0)),
                  pl.BlockSpec((tj, LANES), lambda i: (i, 0)),
                  pl.BlockSpec((1, 6, d), lambda i: (i * tj // t_len, 0, 0)),
                  pl.BlockSpec((1, d), lambda i: (0, 0))],
        out_specs=pl.BlockSpec((tj, d), lambda i: (i, 0)),
        scratch_shapes=[pltpu.SMEM((2, 2 * tj), jnp.int32), pltpu.VMEM((2, 2, tj, d // 2), jnp.uint32),
                        pltpu.SemaphoreType.DMA, pltpu.SemaphoreType.DMA((2,))],
        compiler_params=pltpu.CompilerParams(dimension_semantics=("arbitrary",),
                                             vmem_limit_bytes=VMEM_LIMIT),
        name="combine",
    )(dest.reshape(n // tj, 2 * tj), ys, x1, meta, mod3, final_w.reshape(1, d))


def _pad_cols(w, width):
    return jnp.pad(w, ((0, 0), (0, width - w.shape[1])))


def _pad_rows(w, height):
    return jnp.pad(w, ((0, height - w.shape[0]), (0, 0)))


def _layer(x2, c, bsz, t_len, w_ada, b_ada, norm1_w, w_in, mu_r, mu_k, mu_v, mu_w, mu_a, mu_g, w0,
           w_decay1, w_decay2, a0, w_aaa1, w_aaa2, w_gate1, w_gate2, k_k, k_a, r_k, lnx_w, lnx_b,
           conv_w, conv_norm_w, w_out, norm2_w, w_grp, b_grp, w_exp, b_exp, exp_gate, exp_up,
           exp_down, final_w):
    n, d = x2.shape
    dr = w0.shape[0]
    assert w_decay1.shape[1] <= 128 and w_aaa1.shape[1] <= 128 and w_gate1.shape[1] <= 256
    assert dr % GROUP_LANES == 0 and conv_w.shape[1] == dr and w_in.shape[1] == 6 * dr
    assert t_len % WKV_CHUNK == 0 and w_grp.shape[1] == N_ROUTE_GROUPS and w_exp.shape[1] == N_EXPERTS

    mod3 = _ada(c, w_ada, b_ada).reshape(bsz, 6, d)
    h, proj = _inproj(x2, mod3, norm1_w, w_in.astype(BF16), t_len)

    w1p = jnp.concatenate([_pad_cols(w_decay1, 128), _pad_cols(w_aaa1, 128), _pad_cols(w_gate1, 256)], axis=1)
    mup = jnp.concatenate([jnp.broadcast_to(mu_w[:, None], (d, 128)),
                           jnp.broadcast_to(mu_a[:, None], (d, 128)),
                           jnp.broadcast_to(mu_g[:, None], (d, 256))], axis=1)
    lw, ag, gg = _lora(h, w1p, mup, _pad_rows(w_decay2, 128), _pad_rows(w_aaa2, 128),
                       _pad_rows(w_gate2, 256), w0, a0, bsz, t_len)

    yr = _wkv(proj, lw, ag, gg, mu_r, mu_k, mu_v, k_k, k_a, r_k.reshape(dr), lnx_w, lnx_b, bsz, t_len)

    wr = _pad_cols(jnp.concatenate([w_grp, w_exp], axis=1), LANES)
    br = _pad_cols(jnp.concatenate([b_grp, b_exp]).reshape(1, -1), LANES)
    x1, h2, meta, cnt = _outproj(yr, proj, conv_w, conv_norm_w, w_out.astype(BF16), x2, mod3,
                                 norm2_w, wr, br, bsz, t_len)

    counts = cnt[0, :N_EXPERTS].astype(jnp.int32)
    padded = (counts + ROUTE_BLOCK - 1) // ROUTE_BLOCK * ROUTE_BLOCK
    pad_ends = jnp.cumsum(padded)
    pad_starts = pad_ends - padded
    top_e = meta[:, 0:2].astype(jnp.int32)
    rank = meta[:, 4:6].astype(jnp.int32)
    one_hot = top_e[:, :, None] == jnp.arange(N_EXPERTS, dtype=jnp.int32)
    dest = (jnp.sum(jnp.where(one_hot, pad_starts, 0), axis=-1) + rank).reshape(-1)
    nblk = (2 * n) // ROUTE_BLOCK + N_EXPERTS
    n_rows = nblk * ROUTE_BLOCK
    block_start = jnp.arange(nblk, dtype=jnp.int32) * ROUTE_BLOCK
    block_e = jnp.minimum(jnp.sum(pad_ends[None, :] <= block_start[:, None], axis=1),
                          N_EXPERTS - 1).astype(jnp.int32)
    n_active = (pad_ends[-1:] // ROUTE_BLOCK).astype(jnp.int32)

    pad_len = padded - counts
    zero_off = jnp.concatenate([pad_starts + counts, pad_len, jnp.sum(pad_len, keepdims=True),
                                n_active]).astype(jnp.int32)
    xs = _dispatch(dest, zero_off, h2, n_rows)
    ys = _experts(block_e, n_active, xs, exp_gate, exp_up, exp_down)
    return _combine(dest, ys, x1, meta, mod3, final_w, t_len)


def kernel(x, c, w_ada, b_ada, norm1_w, w_in, mu_r, mu_k, mu_v, mu_w, mu_a, mu_g, w0, w_decay1, w_decay2, a0, w_aaa1, w_aaa2, w_gate1, w_gate2, k_k, k_a, r_k, lnx_w, lnx_b, conv_w, conv_norm_w, w_out, norm2_w, w_grp, b_grp, w_exp, b_exp, exp_gate, exp_up, exp_down, final_w):
    bsz, t_len, d = x.shape
    assert w_ada.shape[0] == 1, "single-layer block"
    out = _layer(x.reshape(bsz * t_len, d), c, bsz, t_len, w_ada[0], b_ada[0], norm1_w[0], w_in[0],
                 mu_r[0], mu_k[0], mu_v[0], mu_w[0], mu_a[0], mu_g[0], w0[0], w_decay1[0], w_decay2[0],
                 a0[0], w_aaa1[0], w_aaa2[0], w_gate1[0], w_gate2[0], k_k[0], k_a[0], r_k[0], lnx_w[0],
                 lnx_b[0], conv_w[0], conv_norm_w[0], w_out[0], norm2_w[0], w_grp[0], b_grp[0],
                 w_exp[0], b_exp[0], exp_gate[0], exp_up[0], exp_down[0], final_w)
    return out.reshape(bsz, t_len, d)
```

```python
import itertools

import jax
import jax.numpy as jnp
from jax import lax
from jax.experimental import pallas as pl
from jax.experimental.pallas import tpu as pltpu

F32 = jnp.float32
BF16 = jnp.bfloat16

HEAD_DIM = 64
WKV_CHUNK = 64
GROUP_LANES = 256
N_ROUTE_GROUPS = 4
EXPERTS_PER_GROUP = 8
N_EXPERTS = N_ROUTE_GROUPS * EXPERTS_PER_GROUP
CONV_GROUP_DIM = 64
RMS_EPS = 1e-6
LNX_EPS = 64e-5
LANES = 128
LORA_PAD = 512
ROUTE_BLOCK = 256
NORM_ROWS = 32
VMEM_LIMIT = 56 * 1024 * 1024


def _dot(a, b, precision=None):
    return jnp.dot(a, b, preferred_element_type=F32, precision=precision)


def _dot_nt(a, b):
    return lax.dot_general(a, b, (((1,), (1,)), ((), ())), preferred_element_type=F32)


def _dot_tn(a, b):
    return lax.dot_general(a, b, (((0,), (0,)), ((), ())), preferred_element_type=F32)


def _split_bf16(x):
    hi = x.astype(BF16)
    return hi, (x - hi.astype(F32)).astype(BF16)


def _dot_x3(a, b):
    ah, al = _split_bf16(a)
    bh, bl = _split_bf16(b)
    return _dot(ah, bh) + _dot(al, bh) + _dot(ah, bl)


def _pack_bf16_pair(x):
    c = x.shape[1] // 2
    lo = pltpu.bitcast(x[:, :c].astype(BF16).astype(F32), jnp.uint32)
    hi = pltpu.bitcast(x[:, c:].astype(BF16).astype(F32), jnp.uint32)
    return (lo >> 16) | (hi & jnp.uint32(0xFFFF0000))


def _unpack_bf16_pair(w):
    lo = pltpu.bitcast(w << 16, F32)
    hi = pltpu.bitcast(w & jnp.uint32(0xFFFF0000), F32)
    return jnp.concatenate([lo, hi], axis=1)


def _sigmoid(x):
    return 1.0 / (1.0 + jnp.exp(-x))


def _shift_rows(x, carry_row):
    rolled = pltpu.roll(x, 1, axis=0)
    row = lax.broadcasted_iota(jnp.int32, x.shape, 0)
    return jnp.where(row == 0, carry_row, rolled)


def _seg_ones(n, seg):
    r = lax.broadcasted_iota(jnp.int32, (n, n), 0) // seg
    c = lax.broadcasted_iota(jnp.int32, (n, n), 1) // seg
    return jnp.where(r == c, 1.0, 0.0).astype(BF16)


def _seg_sum(x, ones_bd):
    hi = x.astype(BF16)
    lo = (x - hi.astype(F32)).astype(BF16)
    return _dot(hi, ones_bd) + _dot(lo, ones_bd)


def _seg_sum1(x, ones_bd):
    return _dot(x.astype(BF16), ones_bd)


def _ada_kernel(cb_ref, w_ref, b_ref, o_ref):
    nb = cb_ref.shape[0]
    tn = w_ref.shape[1]
    for b in range(nb):
        cv = cb_ref[b]
        s = cv * _sigmoid(cv)
        cols = [jnp.sum(s * w_ref[:, j * LANES:(j + 1) * LANES], axis=0, keepdims=True)
                for j in range(tn // LANES)]
        o_ref[b:b + 1, :] = jnp.concatenate(cols, axis=1) + b_ref[...]


def _ada(c, w_ada, b_ada):
    nb, d = c.shape
    n_out = w_ada.shape[1]
    tn = 1024
    cb = jnp.broadcast_to(c[:, :, None], (nb, d, LANES))
    return pl.pallas_call(
        _ada_kernel,
        out_shape=jax.ShapeDtypeStruct((nb, n_out), F32),
        grid=(n_out // tn,),
        in_specs=[pl.BlockSpec((nb, d, LANES), lambda j: (0, 0, 0)),
                  pl.BlockSpec((d, tn), lambda j: (0, j)),
                  pl.BlockSpec((1, tn), lambda j: (0, j))],
        out_specs=pl.BlockSpec((nb, tn), lambda j: (0, j)),
        compiler_params=pltpu.CompilerParams(dimension_semantics=("arbitrary",),
                                             vmem_limit_bytes=VMEM_LIMIT),
        name="ada",
    )(cb, w_ada, b_ada.reshape(1, n_out))


def _inproj_kernel(x_ref, mod_ref, nw_ref, w_ref, h_ref, p_ref, hs_ref):
    @pl.when(pl.program_id(1) == 0)
    def _():
        rows = min(NORM_ROWS, x_ref.shape[0])
        gain = nw_ref[...] * (1.0 + mod_ref[0, 1:2, :])
        shift = mod_ref[0, 0:1, :]

        def norm_rows(r, carry):
            sl = pl.ds(pl.multiple_of(r * rows, rows), rows)
            x = x_ref[sl, :]
            ms = jnp.mean(x * x, axis=-1, keepdims=True)
            h = x * lax.rsqrt(ms + RMS_EPS) * gain + shift
            hb = h.astype(BF16)
            hs_ref[sl, :] = hb
            h_ref[sl, :] = hb
            return carry

        lax.fori_loop(0, x_ref.shape[0] // rows, norm_rows, 0, unroll=4)

    p_ref[...] = _dot(hs_ref[...], w_ref[...])


def _inproj(x2, mod3, norm_w, w_in_bf, t_len):
    n, d = x2.shape
    n_in = w_in_bf.shape[1]
    tm = min(1024, t_len)
    tn = 1024
    return pl.pallas_call(
        _inproj_kernel,
        out_shape=(jax.ShapeDtypeStruct((n, d), BF16), jax.ShapeDtypeStruct((n, n_in), F32)),
        grid=(n // tm, n_in // tn),
        in_specs=[pl.BlockSpec((tm, d), lambda i, j: (i, 0)),
                  pl.BlockSpec((1, 6, d), lambda i, j: (i * tm // t_len, 0, 0)),
                  pl.BlockSpec((1, d), lambda i, j: (0, 0)),
                  pl.BlockSpec((d, tn), lambda i, j: (0, j))],
        out_specs=(pl.BlockSpec((tm, d), lambda i, j: (i, 0)),
                   pl.BlockSpec((tm, tn), lambda i, j: (i, j))),
        scratch_shapes=[pltpu.VMEM((tm, d), BF16)],
        compiler_params=pltpu.CompilerParams(dimension_semantics=("arbitrary", "arbitrary"),
                                             vmem_limit_bytes=VMEM_LIMIT),
        name="inproj",
    )(x2, mod3, norm_w.reshape(1, d), w_in_bf)


def _lora_kernel(h_ref, w1_ref, mu_ref, wd2_ref, wa2_ref, wg2_ref, w0_ref, a0_ref,
                 lw_ref, ag_ref, gg_ref, carry_ref):
    @pl.when(pl.program_id(1) == 0)
    def _():
        carry_ref[...] = jnp.zeros_like(carry_ref)

    h = h_ref[...]
    w1 = w1_ref[...]
    p1 = _dot(h, w1.astype(BF16))
    p2 = _dot(h, (w1 * mu_ref[...]).astype(BF16))
    prev = _shift_rows(p2, carry_ref[...])
    carry_ref[...] = p2[p2.shape[0] - 1:, :]
    lo = p1 - p2 + prev
    dec = _dot(jnp.tanh(lo[:, 0:128]).astype(BF16), wd2_ref[...].astype(BF16))
    z = -(w0_ref[...] + dec)
    softplus = jnp.maximum(z, 0.0) + jnp.log(1.0 + jnp.exp(-jnp.abs(z)))
    lw_ref[...] = -jnp.exp(-softplus - 0.5)
    ag_ref[...] = _sigmoid(a0_ref[...] + _dot(lo[:, 128:256].astype(BF16), wa2_ref[...].astype(BF16)))
    gg_ref[...] = _dot(_sigmoid(lo[:, 256:512]).astype(BF16), wg2_ref[...].astype(BF16))


def _lora(h, w1p, mup, wd2p, wa2p, wg2p, w0, a0, bsz, t_len):
    n, d = h.shape
    dr = w0.shape[-1]
    tm = min(512, t_len)
    nt = t_len // tm
    row = lambda b, t: (b * nt + t, 0)
    const = lambda b, t: (0, 0)
    out = jax.ShapeDtypeStruct((n, dr), F32)
    return pl.pallas_call(
        _lora_kernel,
        out_shape=(out, out, out),
        grid=(bsz, nt),
        in_specs=[pl.BlockSpec((tm, d), row),
                  pl.BlockSpec((d, LORA_PAD), const),
                  pl.BlockSpec((d, LORA_PAD), const),
                  pl.BlockSpec((128, dr), const),
                  pl.BlockSpec((128, dr), const),
                  pl.BlockSpec((256, dr), const),
                  pl.BlockSpec((1, dr), const),
                  pl.BlockSpec((1, dr), const)],
        out_specs=(pl.BlockSpec((tm, dr), row),) * 3,
        scratch_shapes=[pltpu.VMEM((1, LORA_PAD), F32)],
        compiler_params=pltpu.CompilerParams(dimension_semantics=("arbitrary", "arbitrary"),
                                             vmem_limit_bytes=VMEM_LIMIT),
        name="lora",
    )(h, w1p, mup, wd2p, wa2p, wg2p, w0.reshape(1, dr), a0.reshape(1, dr))


def _wkv_kernel(r_ref, k_ref, v_ref, lw_ref, ag_ref, gg_ref, mur_ref, muk_ref, muv_ref,
                kk_ref, ka_ref, rk_ref, lnw_ref, lnb_ref, y_ref, s_ref, carry_ref):
    lt = r_ref.shape[0]
    gl = GROUP_LANES
    c_len = WKV_CHUNK
    heads = gl // HEAD_DIM

    @pl.when(pl.program_id(2) == 0)
    def _():
        s_ref[...] = jnp.zeros_like(s_ref)
        carry_ref[...] = jnp.zeros_like(carry_ref)

    ones_bd = _seg_ones(gl, HEAD_DIM)

    def lerp(ref, mu_ref, idx):
        z = ref[...]
        prev = _shift_rows(z, carry_ref[idx:idx + 1, :])
        carry_ref[idx:idx + 1, :] = z[lt - 1:, :]
        return z + (prev - z) * mu_ref[...]

    r1 = lerp(r_ref, mur_ref, 0)
    k1 = lerp(k_ref, muk_ref, 1)
    v1 = lerp(v_ref, muv_ref, 2)
    ag = ag_ref[...]
    kk = k1 * kk_ref[...]
    kk = kk * lax.rsqrt(jnp.maximum(_seg_sum1(kk * kk, ones_bd), 1e-24))
    k2 = k1 * (1.0 + (ag - 1.0) * ka_ref[...])
    a_all = -kk
    b_all = kk * ag
    lw = lw_ref[...]

    rows = lax.broadcasted_iota(jnp.int32, (gl, gl), 0)
    cols = lax.broadcasted_iota(jnp.int32, (gl, gl), 1)
    same_head = (rows // c_len) == (cols // HEAD_DIM)
    ts = lax.broadcasted_iota(jnp.int32, (c_len, gl), 0)
    tj = lax.broadcasted_iota(jnp.int32, (c_len, gl), 1) % c_len
    tri_strict = ts > tj
    tri_incl = ts >= tj
    eye = jnp.where(ts == tj, 1.0, 0.0).astype(F32)
    cr = lax.broadcasted_iota(jnp.int32, (c_len, c_len), 0)
    cc = lax.broadcasted_iota(jnp.int32, (c_len, c_len), 1)
    cum_b = jnp.where(cr >= cc, 1.0, 0.0).astype(BF16)

    def bd(x):
        xb = x.astype(BF16)
        return jnp.where(same_head, jnp.concatenate([xb] * heads, axis=0), jnp.zeros((), BF16))

    def fold(x):
        out = x[0:c_len]
        for hh in range(1, heads):
            out = out + x[hh * c_len:(hh + 1) * c_len]
        return out

    nch = lt // c_len
    sls = [slice(c * c_len, (c + 1) * c_len) for c in range(nch)]

    def cumsum3(x):
        hi = x.astype(BF16)
        r = x - hi.astype(F32)
        mid = r.astype(BF16)
        lo = (r - mid.astype(F32)).astype(BF16)
        return _dot(cum_b, hi) + _dot(cum_b, mid) + _dot(cum_b, lo)

    wc, g_s, n_s, rp, y0, s_in = {}, {}, {}, {}, {}, {}
    state = [s_ref[...]]

    def chunk_parallel(chunks):
        lws = {c: lw[sls[c]] for c in chunks}
        cs = {c: cumsum3(lws[c]) for c in chunks}
        yield
        e_cs = {c: jnp.exp(cs[c]) for c in chunks}
        e_neg = {c: jnp.exp(-cs[c]) for c in chunks}
        for c in chunks:
            wc[c] = e_cs[c][c_len - 1:, :]
        at = {c: a_all[sls[c]] * jnp.exp(cs[c] - lws[c]) for c in chunks}
        rt = {c: r1[sls[c]] * e_cs[c] for c in chunks}
        bt = {c: b_all[sls[c]] * e_neg[c] for c in chunks}
        kt = {c: k2[sls[c]] * e_neg[c] for c in chunks}
        vc = {c: v1[sls[c]] for c in chunks}
        yield
        lhs = {c: jnp.concatenate([at[c], rt[c]], axis=0).astype(BF16) for c in chunks}
        g_b = {c: _dot_nt(lhs[c], bd(bt[c])) for c in chunks}
        yield
        g_k = {c: _dot_nt(lhs[c], bd(kt[c])) for c in chunks}
        yield
        n_ab = {c: jnp.where(tri_strict, g_b[c][:c_len], 0.0) for c in chunks}
        a_ak = {c: jnp.where(tri_strict, g_k[c][:c_len], 0.0) for c in chunks}
        a_rb = {c: jnp.where(tri_incl, g_b[c][c_len:], 0.0).astype(BF16) for c in chunks}
        a_rk = {c: jnp.where(tri_incl, g_k[c][c_len:], 0.0).astype(BF16) for c in chunks}

        p = {c: eye + jnp.where((ts // 2) == (tj // 2), n_ab[c], 0.0) for c in chunks}
        for half in (2, 4, 8, 16, 32):
            lower_left = (((ts // (2 * half)) == (tj // (2 * half)))
                          & ((ts % (2 * half)) >= half) & ((tj % (2 * half)) < half))
            n21_ta = {c: _dot(jnp.where(lower_left, n_ab[c], 0.0).astype(BF16), bd(p[c])) for c in chunks}
            yield
            p = {c: p[c] + _dot(p[c].astype(BF16), bd(n21_ta[c])) for c in chunks}
            yield

        z = {c: _dot(p[c].astype(BF16), jnp.concatenate([bd(a_ak[c]), bd(at[c])], axis=1))
             for c in chunks}
        yield
        ap = {c: z[c][:, gl:] for c in chunks}
        bd_v = {c: bd(vc[c]) for c in chunks}
        u0 = {c: _dot(z[c][:, :gl].astype(BF16), bd_v[c]) for c in chunks}
        yield
        w = {c: _dot(a_rb[c], jnp.concatenate([bd(ap[c]), bd(u0[c])], axis=1)) for c in chunks}
        yield
        for c in chunks:
            rp[c] = rt[c] + w[c][:, :gl]
            y0[c] = w[c][:, gl:] + _dot(a_rk[c], bd_v[c])
        yield
        bh = {c: (bt[c] * wc[c]).astype(BF16) for c in chunks}
        bk = {c: jnp.concatenate([bh[c], (kt[c] * wc[c]).astype(BF16)], axis=0) for c in chunks}
        for c in chunks:
            g_s[c] = jnp.where(same_head, _dot_tn(ap[c].astype(BF16), bh[c]), 0.0).astype(BF16)
        yield
        for c in chunks:
            n_s[c] = fold(jnp.where(
                same_head, _dot_tn(jnp.concatenate([u0[c], vc[c]], axis=0).astype(BF16), bk[c]), 0.0))
        yield

    def state_update(chunks):
        for c in chunks:
            s_in[c] = state[0].astype(BF16)
            state[0] = state[0] * wc[c] + _dot(s_in[c], g_s[c]) + n_s[c]
            yield

    def epilogue(chunks):
        rows = slice(chunks[0] * c_len, (chunks[-1] + 1) * c_len)
        y = jnp.concatenate([_dot_nt(rp[c].astype(BF16), bd(s_in[c])) + y0[c] for c in chunks], axis=0)
        yield
        inv_n = 1.0 / HEAD_DIM
        mean = _seg_sum(y, ones_bd) * inv_n
        yield
        yc = y - mean
        var = _seg_sum1(yc * yc, ones_bd) * inv_n
        yield
        yn = yc * lax.rsqrt(var + LNX_EPS) * lnw_ref[...] + lnb_ref[...]
        bonus = _seg_sum(r1[rows] * k2[rows] * rk_ref[...], ones_bd) * v1[rows]
        y_ref[rows, :] = ((yn + bonus) * gg_ref[rows, :]).astype(y_ref.dtype)
        yield

    def drain(gen):
        for _ in gen:
            pass

    def interleave(serial, other, every):
        for k, _ in enumerate(other, 1):
            if k % every == 0:
                next(serial, None)
        drain(serial)

    wave = max(nch // 2, 1)
    waves = [list(range(w0, min(w0 + wave, nch))) for w0 in range(0, nch, wave)]
    par_stages, epi_stages = 20, 4
    drain(chunk_parallel(waves[0]))
    for wi, chunks in enumerate(waves):
        others, n_other = [], 0
        if wi + 1 < len(waves):
            others.append(chunk_parallel(waves[wi + 1]))
            n_other += par_stages
        if wi >= 1:
            others.append(epilogue(waves[wi - 1]))
            n_other += epi_stages
        interleave(state_update(chunks), itertools.chain(*others), max(n_other // len(chunks), 1))
    drain(epilogue(waves[-1]))
    s_ref[...] = state[0]


def _wkv(proj, lw, ag, gg, mu_r, mu_k, mu_v, k_k, k_a, r_k, lnx_w, lnx_b, bsz, t_len):
    n = proj.shape[0]
    dr = lw.shape[1]
    gl = GROUP_LANES
    ng = dr // gl
    lt = min(1024, t_len)
    nt = t_len // lt
    row = lambda off: (lambda b, g, t: (b * nt + t, off + g))
    par = lambda b, g, t: (0, g)
    vec = lambda a: a.reshape(1, dr)
    return pl.pallas_call(
        _wkv_kernel,
        out_shape=jax.ShapeDtypeStruct((n, dr), BF16),
        grid=(bsz, ng, nt),
        in_specs=[pl.BlockSpec((lt, gl), row(0)),
                  pl.BlockSpec((lt, gl), row(ng)),
                  pl.BlockSpec((lt, gl), row(2 * ng)),
                  pl.BlockSpec((lt, gl), row(0)),
                  pl.BlockSpec((lt, gl), row(0)),
                  pl.BlockSpec((lt, gl), row(0))] + [pl.BlockSpec((1, gl), par)] * 8,
        out_specs=pl.BlockSpec((lt, gl), row(0)),
        scratch_shapes=[pltpu.VMEM((HEAD_DIM, gl), F32), pltpu.VMEM((8, gl), F32)],
        compiler_params=pltpu.CompilerParams(
            dimension_semantics=("arbitrary", "arbitrary", "arbitrary"),
            vmem_limit_bytes=VMEM_LIMIT),
        name="wkv",
    )(proj, proj, proj, lw, ag, gg, vec(mu_r), vec(mu_k), vec(mu_v), vec(k_k), vec(k_a),
      vec(r_k), vec(lnx_w), vec(lnx_b))


def _outproj_kernel(yr_ref, u_ref, gb_ref, gc_ref, cw_ref, cnw_ref, wo_ref, x_ref, mod_ref,
                    n2w_ref, wr_ref, br_ref, x1_ref, h2_ref, meta_ref, cnt_ref,
                    ccarry_ref, cnt_acc_ref):
    tm = x_ref.shape[0]
    dc = u_ref.shape[1]

    @pl.when(pl.program_id(1) == 0)
    def _():
        ccarry_ref[...] = jnp.zeros_like(ccarry_ref)

    @pl.when((pl.program_id(0) == 0) & (pl.program_id(1) == 0))
    def _():
        cnt_acc_ref[...] = jnp.zeros_like(cnt_acc_ref)

    pre = gc_ref[...] * u_ref[...]
    prev1 = _shift_rows(pre, ccarry_ref[0:1, :])
    prev2 = _shift_rows(prev1, ccarry_ref[1:2, :])
    ccarry_ref[0:1, :] = pre[tm - 1:, :]
    ccarry_ref[1:2, :] = pre[tm - 2:tm - 1, :]
    yc = gb_ref[...] * (prev2 * cw_ref[0:1, :] + prev1 * cw_ref[1:2, :] + pre * cw_ref[2:3, :])
    ones_bd = _seg_ones(GROUP_LANES, CONV_GROUP_DIM)
    ycs = []
    for s in range(dc // GROUP_LANES):
        ys = yc[:, s * GROUP_LANES:(s + 1) * GROUP_LANES]
        ms = _seg_sum1(ys * ys, ones_bd) * (1.0 / CONV_GROUP_DIM)
        ycs.append(ys * lax.rsqrt(ms + RMS_EPS))
    y_conv = (jnp.concatenate(ycs, axis=1) * cnw_ref[...]).astype(BF16)

    dr = yr_ref.shape[1]
    y_mix = _dot(yr_ref[...], wo_ref[0:dr, :]) + _dot(y_conv, wo_ref[dr:, :])
    x1 = x_ref[...] + mod_ref[0, 2:3, :] * y_mix
    x1_ref[...] = x1
    ms = jnp.mean(x1 * x1, axis=-1, keepdims=True)
    h2 = x1 * lax.rsqrt(ms + RMS_EPS) * n2w_ref[...]
    h2 = h2 * (1.0 + mod_ref[0, 4:5, :]) + mod_ref[0, 3:4, :]
    h2_ref[...] = _pack_bf16_pair(h2)

    logits = _dot_x3(h2, wr_ref[...]) + br_ref[...]
    lane = lax.broadcasted_iota(jnp.int32, logits.shape, 1)
    neg = jnp.float32(-jnp.inf)
    big = jnp.int32(1 << 20)
    glog = jnp.where(lane < N_ROUTE_GROUPS, logits, neg)
    gmax = jnp.max(glog, axis=-1, keepdims=True)
    grp = jnp.min(jnp.where(glog == gmax, lane, big), axis=-1, keepdims=True)
    p_grp = 1.0 / jnp.sum(jnp.exp(glog - gmax), axis=-1, keepdims=True)
    e_lane = lane - N_ROUTE_GROUPS
    in_grp = (e_lane >= grp * EXPERTS_PER_GROUP) & (e_lane < (grp + 1) * EXPERTS_PER_GROUP)
    elog = jnp.where(in_grp, logits, neg)
    m1 = jnp.max(elog, axis=-1, keepdims=True)
    i1 = jnp.min(jnp.where(elog == m1, lane, big), axis=-1, keepdims=True)
    elog2 = jnp.where(lane == i1, neg, elog)
    m2 = jnp.max(elog2, axis=-1, keepdims=True)
    i2 = jnp.min(jnp.where(elog2 == m2, lane, big), axis=-1, keepdims=True)
    t2 = jnp.exp(m2 - m1)
    w1 = p_grp / (1.0 + t2)
    w2 = p_grp * t2 / (1.0 + t2)
    e1 = i1 - N_ROUTE_GROUPS
    e2 = i2 - N_ROUTE_GROUPS

    oh1 = lane == e1
    oh2 = lane == e2
    oh = jnp.where(oh1 | oh2, 1.0, 0.0)
    tr = lax.broadcasted_iota(jnp.int32, (tm, tm), 0)
    tc = lax.broadcasted_iota(jnp.int32, (tm, tm), 1)
    lower = jnp.where(tr > tc, 1.0, 0.0).astype(BF16)
    before = _dot(lower, oh.astype(BF16)) + cnt_acc_ref[...]
    rank1 = jnp.sum(jnp.where(oh1, before, 0.0), axis=-1, keepdims=True)
    rank2 = jnp.sum(jnp.where(oh2, before, 0.0), axis=-1, keepdims=True)
    cnt = cnt_acc_ref[...] + jnp.sum(oh, axis=0, keepdims=True)
    cnt_acc_ref[...] = cnt
    cnt_ref[...] = jnp.broadcast_to(cnt, cnt_ref.shape)

    meta = jnp.where(lane == 0, e1.astype(F32), 0.0)
    meta = jnp.where(lane == 1, e2.astype(F32), meta)
    meta = jnp.where(lane == 2, w1, meta)
    meta = jnp.where(lane == 3, w2, meta)
    meta = jnp.where(lane == 4, rank1, meta)
    meta = jnp.where(lane == 5, rank2, meta)
    meta_ref[...] = meta


def _outproj(yr, proj, conv_w, conv_norm_w, w_out_bf, x2, mod3, norm2_w, wr, br, bsz, t_len):
    n, d = x2.shape
    dr = yr.shape[1]
    dc = conv_w.shape[1]
    tm = min(512, t_len)
    nt = t_len // tm
    nseg = dr // dc
    row = lambda off: (lambda b, t: (b * nt + t, off))
    const = lambda b, t: (0, 0)
    return pl.pallas_call(
        _outproj_kernel,
        out_shape=(jax.ShapeDtypeStruct((n, d), F32), jax.ShapeDtypeStruct((n, d // 2), jnp.uint32),
                   jax.ShapeDtypeStruct((n, LANES), F32), jax.ShapeDtypeStruct((8, LANES), F32)),
        grid=(bsz, nt),
        in_specs=[pl.BlockSpec((tm, dr), row(0)),
                  pl.BlockSpec((tm, dc), row(3 * nseg)),
                  pl.BlockSpec((tm, dc), row(4 * nseg)),
                  pl.BlockSpec((tm, dc), row(5 * nseg)),
                  pl.BlockSpec((3, dc), const),
                  pl.BlockSpec((1, dc), const),
                  pl.BlockSpec((dr + dc, d), const),
                  pl.BlockSpec((tm, d), row(0)),
                  pl.BlockSpec((1, 6, d), lambda b, t: (b, 0, 0)),
                  pl.BlockSpec((1, d), const),
                  pl.BlockSpec((d, LANES), const),
                  pl.BlockSpec((1, LANES), const)],
        out_specs=(pl.BlockSpec((tm, d), row(0)), pl.BlockSpec((tm, d // 2), row(0)),
                   pl.BlockSpec((tm, LANES), row(0)), pl.BlockSpec((8, LANES), const)),
        scratch_shapes=[pltpu.VMEM((8, dc), F32), pltpu.VMEM((1, LANES), F32)],
        compiler_params=pltpu.CompilerParams(dimension_semantics=("arbitrary", "arbitrary"),
                                             vmem_limit_bytes=VMEM_LIMIT),
        name="outproj",
    )(yr, proj, proj, proj, conv_w, conv_norm_w.reshape(1, dc), w_out_bf, x2, mod3,
      norm2_w.reshape(1, d), wr, br)


def _dispatch_kernel(zoff_ref, dest_hbm, h2_hbm, xs_hbm, idx_smem, tile_ref, zero_ref, zblk_ref,
                     sem_idx, sem_tile, sem_rows, sem_zero, sem_tail):
    i = pl.program_id(0)
    nsteps = pl.num_programs(0)
    th = idx_smem.shape[1] // 2
    slot = i % 2

    def idx_copy(step, sl):
        return pltpu.make_async_copy(dest_hbm.at[step], idx_smem.at[sl], sem_idx.at[sl])

    def tile_copy(step, sl):
        rows = pl.ds(pl.multiple_of(step * th, th), th)
        return pltpu.make_async_copy(h2_hbm.at[rows], tile_ref.at[sl], sem_tile.at[sl])

    def rows_wait(sl):
        pltpu.make_async_copy(xs_hbm.at[pl.ds(0, 2 * th)], xs_hbm.at[pl.ds(0, 2 * th)],
                              sem_rows.at[sl]).wait()

    nblk = xs_hbm.shape[0] // ROUTE_BLOCK

    def tail_copy(b):
        start = pl.multiple_of(b * ROUTE_BLOCK, ROUTE_BLOCK)
        return pltpu.make_async_copy(zblk_ref, xs_hbm.at[pl.ds(start, ROUTE_BLOCK)], sem_tail)

    @pl.when(i == 0)
    def _():
        idx_copy(0, 0).start()
        tile_copy(0, 0).start()
        zero_ref[...] = jnp.zeros_like(zero_ref)

        def zero_expert(e, carry):
            off = zoff_ref[e]

            def zero_row(j, c):
                pltpu.make_async_copy(zero_ref, xs_hbm.at[pl.ds(off + j, 1)], sem_zero).start()
                return c

            return lax.fori_loop(0, zoff_ref[N_EXPERTS + e], zero_row, carry)

        lax.fori_loop(0, N_EXPERTS, zero_expert, 0)
        total = zoff_ref[2 * N_EXPERTS]
        bulk = pl.multiple_of((total // 8) * 8, 8)

        @pl.when(bulk > 0)
        def _():
            pltpu.make_async_copy(xs_hbm.at[pl.ds(0, bulk)], xs_hbm.at[pl.ds(0, bulk)], sem_zero).wait()

        def wait_row(j, c):
            pltpu.make_async_copy(zero_ref, xs_hbm.at[pl.ds(0, 1)], sem_zero).wait()
            return c

        lax.fori_loop(0, total - bulk, wait_row, 0)

        zblk_ref[...] = jnp.zeros_like(zblk_ref)

        def tail_start(b, c):
            tail_copy(b).start()
            return c

        lax.fori_loop(zoff_ref[2 * N_EXPERTS + 1], nblk, tail_start, 0)

    idx_copy(i, slot).wait()
    tile_copy(i, slot).wait()

    @pl.when(i > 0)
    def _():
        rows_wait(1 - slot)

    @pl.when(i + 1 < nsteps)
    def _():
        idx_copy(i + 1, 1 - slot).start()
        tile_copy(i + 1, 1 - slot).start()

    for p in range(2):
        @pl.when(slot == p)
        def _(p=p):
            for t in range(th):
                for s in range(2):
                    d = idx_smem[p, 2 * t + s]
                    pltpu.make_async_copy(tile_ref.at[p, pl.ds(t, 1)], xs_hbm.at[pl.ds(d, 1)],
                                          sem_rows.at[p]).start(priority=s)

    @pl.when(i == nsteps - 1)
    def _():
        rows_wait(slot)

        def tail_wait(b, c):
            tail_copy(b).wait()
            return c

        lax.fori_loop(zoff_ref[2 * N_EXPERTS + 1], nblk, tail_wait, 0)


def _dispatch(dest, zero_off, h2, n_rows):
    n, d = h2.shape
    th = min(256, n)
    return pl.pallas_call(
        _dispatch_kernel,
        out_shape=jax.ShapeDtypeStruct((n_rows, d), h2.dtype),
        grid_spec=pltpu.PrefetchScalarGridSpec(
            num_scalar_prefetch=1,
            grid=(n // th,),
            in_specs=[pl.BlockSpec(memory_space=pl.ANY), pl.BlockSpec(memory_space=pl.ANY)],
            out_specs=pl.BlockSpec(memory_space=pl.ANY),
            scratch_shapes=[pltpu.SMEM((2, 2 * th), jnp.int32),
                            pltpu.VMEM((2, th, d), h2.dtype),
                            pltpu.VMEM((1, d), h2.dtype),
                            pltpu.VMEM((ROUTE_BLOCK, d), h2.dtype),
                            pltpu.SemaphoreType.DMA((2,)), pltpu.SemaphoreType.DMA((2,)),
                            pltpu.SemaphoreType.DMA((2,)),
                            pltpu.SemaphoreType.DMA, pltpu.SemaphoreType.DMA]),
        compiler_params=pltpu.CompilerParams(dimension_semantics=("arbitrary",),
                                             vmem_limit_bytes=VMEM_LIMIT),
        name="dispatch",
    )(zero_off, dest.reshape(n // th, 2 * th), h2)


def _expert_kernel(sched_ref, na_ref, xs_ref, wg_hbm, wu_hbm, wd_hbm, ys_ref,
                   wg_buf, wu_buf, wd_buf, sem):
    i = pl.program_id(0)

    def weight_copies(e, slot):
        return (pltpu.make_async_copy(wg_hbm.at[e], wg_buf.at[slot], sem.at[slot, 0]),
                pltpu.make_async_copy(wu_hbm.at[e], wu_buf.at[slot], sem.at[slot, 1]),
                pltpu.make_async_copy(wd_hbm.at[e], wd_buf.at[slot], sem.at[slot, 2]))

    @pl.when(i < na_ref[0])
    def _():
        e = sched_ref[0, i]
        slot = sched_ref[1, i]

        @pl.when(i == 0)
        def _():
            for cp in weight_copies(e, slot):
                cp.start()

        @pl.when(sched_ref[2, i] == 1)
        def _():
            for cp in weight_copies(e, slot):
                cp.wait()
            nxt = sched_ref[3, i]

            @pl.when(nxt >= 0)
            def _():
                for cp in weight_copies(nxt, 1 - slot):
                    cp.start()

        x = _unpack_bf16_pair(xs_ref[...]).astype(BF16)
        g = _dot(x, wg_buf[slot].astype(BF16))
        u = _dot(x, wu_buf[slot].astype(BF16))
        hid = (g * _sigmoid(g) * u).astype(BF16)
        ys_ref[...] = _pack_bf16_pair(_dot(hid, wd_buf[slot].astype(BF16)))

    @pl.when(i >= na_ref[0])
    def _():
        ys_ref[...] = jnp.zeros_like(ys_ref)


def _experts(block_e, n_active, xs, exp_gate, exp_up, exp_down):
    n_rows, dp = xs.shape
    _, d, de = exp_gate.shape
    nblk = n_rows // ROUTE_BLOCK
    idx = jnp.arange(nblk, dtype=jnp.int32)
    valid = idx < n_active[0]
    first = valid & ((idx == 0) | (block_e != jnp.roll(block_e, 1)))
    slot = jnp.maximum(jnp.cumsum(first.astype(jnp.int32)) - 1, 0) % 2
    first_at = jnp.where(first, idx, nblk)
    next_first = jnp.concatenate([lax.cummin(first_at, reverse=True)[1:], jnp.full((1,), nblk, jnp.int32)])
    nxt = jnp.where(next_first < nblk, block_e[jnp.minimum(next_first, nblk - 1)], -1)
    sched = jnp.stack([block_e, slot, first.astype(jnp.int32), nxt]).astype(jnp.int32)

    blk_row = lambda i, sc, na: (jnp.minimum(i, jnp.maximum(na[0] - 1, 0)), 0)
    return pl.pallas_call(
        _expert_kernel,
        out_shape=jax.ShapeDtypeStruct((n_rows, dp), jnp.uint32),
        grid_spec=pltpu.PrefetchScalarGridSpec(
            num_scalar_prefetch=2,
            grid=(nblk,),
            in_specs=[pl.BlockSpec((ROUTE_BLOCK, dp), blk_row),
                      pl.BlockSpec(memory_space=pl.ANY),
                      pl.BlockSpec(memory_space=pl.ANY),
                      pl.BlockSpec(memory_space=pl.ANY)],
            out_specs=pl.BlockSpec((ROUTE_BLOCK, dp), lambda i, sc, na: (i, 0)),
            scratch_shapes=[pltpu.VMEM((2, d, de), exp_gate.dtype),
                            pltpu.VMEM((2, d, de), exp_up.dtype),
                            pltpu.VMEM((2, de, d), exp_down.dtype),
                            pltpu.SemaphoreType.DMA((2, 3))]),
        compiler_params=pltpu.CompilerParams(dimension_semantics=("arbitrary",),
                                             vmem_limit_bytes=VMEM_LIMIT),
        name="experts",
    )(sched, n_active, xs, exp_gate, exp_up, exp_down)


def _combine_kernel(dest_hbm, ys_hbm, x1_ref, meta_ref, mod_ref, fw_ref, o_ref,
                    idx_smem, buf_ref, sem_idx, sem_rows):
    i = pl.program_id(0)
    nsteps = pl.num_programs(0)
    tj = x1_ref.shape[0]
    slot = i % 2

    def idx_copy(step, sl):
        return pltpu.make_async_copy(dest_hbm.at[step], idx_smem.at[sl], sem_idx.at[sl])

    def gather(sl):
        for t in range(tj):
            for s in range(2):
                d = idx_smem[sl, 2 * t + s]
                pltpu.make_async_copy(ys_hbm.at[pl.ds(d, 1)], buf_ref.at[sl, s, pl.ds(t, 1)],
                                      sem_rows.at[sl]).start(priority=s)

    @pl.when(i == 0)
    def _():
        first = idx_copy(0, 0)
        first.start()
        first.wait()
        gather(0)

        @pl.when(nsteps > 1)
        def _():
            idx_copy(1, 1).start()

    for p in range(2):
        @pl.when((i + 1 < nsteps) & (slot == 1 - p))
        def _(p=p):
            idx_copy(i + 1, p).wait()

            @pl.when(i + 2 < nsteps)
            def _():
                idx_copy(i + 2, 1 - p).start()

            gather(p)

    pltpu.make_async_copy(buf_ref.at[slot], buf_ref.at[slot], sem_rows.at[slot]).wait()

    meta = meta_ref[...]
    y = (meta[:, 2:3] * _unpack_bf16_pair(buf_ref[slot, 0])
         + meta[:, 3:4] * _unpack_bf16_pair(buf_ref[slot, 1]))
    x2 = x1_ref[...] + mod_ref[0, 5:6, :] * y
    ms = jnp.mean(x2 * x2, axis=-1, keepdims=True)
    o_ref[...] = x2 * lax.rsqrt(ms + RMS_EPS) * fw_ref[...]


def _combine(dest, ys, x1, meta, mod3, final_w, t_len):
    n, d = x1.shape
    tj = min(256, t_len)
    return pl.pallas_call(
        _combine_kernel,
        out_shape=jax.ShapeDtypeStruct((n, d), F32),
        grid=(n // tj,),
        in_specs=[pl.BlockSpec(memory_space=pl.ANY),
                  pl.BlockSpec(memory_space=pl.ANY),
                  pl.BlockSpec((tj, d), lambda i: (i, 0)),
                  pl.BlockSpec((tj, LANES), lambda i: (i, 0)),
                  pl.BlockSpec((1, 6, d), lambda i: (i * tj // t_len, 0, 0)),
                  pl.BlockSpec((1, d), lambda i: (0, 0))],
        out_specs=pl.BlockSpec((tj, d), lambda i: (i, 0)),
        scratch_shapes=[pltpu.SMEM((2, 2 * tj), jnp.int32), pltpu.VMEM((2, 2, tj, d // 2), jnp.uint32),
                        pltpu.SemaphoreType.DMA((2,)), pltpu.SemaphoreType.DMA((2,))],
        compiler_params=pltpu.CompilerParams(dimension_semantics=("arbitrary",),
                                             vmem_limit_bytes=VMEM_LIMIT),
        name="combine",
    )(dest.reshape(n // tj, 2 * tj), ys, x1, meta, mod3, final_w.reshape(1, d))


def _pad_cols(w, width):
    return jnp.pad(w, ((0, 0), (0, width - w.shape[1])))


def _pad_rows(w, height):
    return jnp.pad(w, ((0, height - w.shape[0]), (0, 0)))


def _layer(x2, c, bsz, t_len, w_ada, b_ada, norm1_w, w_in, mu_r, mu_k, mu_v, mu_w, mu_a, mu_g, w0,
           w_decay1, w_decay2, a0, w_aaa1, w_aaa2, w_gate1, w_gate2, k_k, k_a, r_k, lnx_w, lnx_b,
           conv_w, conv_norm_w, w_out, norm2_w, w_grp, b_grp, w_exp, b_exp, exp_gate, exp_up,
           exp_down, final_w):
    n, d = x2.shape
    dr = w0.shape[0]
    assert w_decay1.shape[1] <= 128 and w_aaa1.shape[1] <= 128 and w_gate1.shape[1] <= 256
    assert dr % GROUP_LANES == 0 and conv_w.shape[1] == dr and w_in.shape[1] == 6 * dr
    assert t_len % WKV_CHUNK == 0 and w_grp.shape[1] == N_ROUTE_GROUPS and w_exp.shape[1] == N_EXPERTS

    mod3 = _ada(c, w_ada, b_ada).reshape(bsz, 6, d)
    h, proj = _inproj(x2, mod3, norm1_w, w_in.astype(BF16), t_len)

    w1p = jnp.concatenate([_pad_cols(w_decay1, 128), _pad_cols(w_aaa1, 128), _pad_cols(w_gate1, 256)], axis=1)
    mup = jnp.concatenate([jnp.broadcast_to(mu_w[:, None], (d, 128)),
                           jnp.broadcast_to(mu_a[:, None], (d, 128)),
                           jnp.broadcast_to(mu_g[:, None], (d, 256))], axis=1)
    lw, ag, gg = _lora(h, w1p, mup, _pad_rows(w_decay2, 128), _pad_rows(w_aaa2, 128),
                       _pad_rows(w_gate2, 256), w0, a0, bsz, t_len)

    yr = _wkv(proj, lw, ag, gg, mu_r, mu_k, mu_v, k_k, k_a, r_k.reshape(dr), lnx_w, lnx_b, bsz, t_len)

    wr = _pad_cols(jnp.concatenate([w_grp, w_exp], axis=1), LANES)
    br = _pad_cols(jnp.concatenate([b_grp, b_exp]).reshape(1, -1), LANES)
    x1, h2, meta, cnt = _outproj(yr, proj, conv_w, conv_norm_w, w_out.astype(BF16), x2, mod3,
                                 norm2_w, wr, br, bsz, t_len)

    counts = cnt[0, :N_EXPERTS].astype(jnp.int32)
    padded = (counts + ROUTE_BLOCK - 1) // ROUTE_BLOCK * ROUTE_BLOCK
    pad_ends = jnp.cumsum(padded)
    pad_starts = pad_ends - padded
    top_e = meta[:, 0:2].astype(jnp.int32)
    rank = meta[:, 4:6].astype(jnp.int32)
    one_hot = top_e[:, :, None] == jnp.arange(N_EXPERTS, dtype=jnp.int32)
    dest = (jnp.sum(jnp.where(one_hot, pad_starts, 0), axis=-1) + rank).reshape(-1)
    nblk = (2 * n) // ROUTE_BLOCK + N_EXPERTS
    n_rows = nblk * ROUTE_BLOCK
    block_start = jnp.arange(nblk, dtype=jnp.int32) * ROUTE_BLOCK
    block_e = jnp.minimum(jnp.sum(pad_ends[None, :] <= block_start[:, None], axis=1),
                          N_EXPERTS - 1).astype(jnp.int32)
    n_active = (pad_ends[-1:] // ROUTE_BLOCK).astype(jnp.int32)

    pad_len = padded - counts
    zero_off = jnp.concatenate([pad_starts + counts, pad_len, jnp.sum(pad_len, keepdims=True),
                                n_active]).astype(jnp.int32)
    xs = _dispatch(dest, zero_off, h2, n_rows)
    ys = _experts(block_e, n_active, xs, exp_gate, exp_up, exp_down)
    return _combine(dest, ys, x1, meta, mod3, final_w, t_len)


def kernel(x, c, w_ada, b_ada, norm1_w, w_in, mu_r, mu_k, mu_v, mu_w, mu_a, mu_g, w0, w_decay1, w_decay2, a0, w_aaa1, w_aaa2, w_gate1, w_gate2, k_k, k_a, r_k, lnx_w, lnx_b, conv_w, conv_norm_w, w_out, norm2_w, w_grp, b_grp, w_exp, b_exp, exp_gate, exp_up, exp_down, final_w):
    bsz, t_len, d = x.shape
    assert w_ada.shape[0] == 1, "single-layer block"
    out = _layer(x.reshape(bsz * t_len, d), c, bsz, t_len, w_ada[0], b_ada[0], norm1_w[0], w_in[0],
                 mu_r[0], mu_k[0], mu_v[0], mu_w[0], mu_a[0], mu_g[0], w0[0], w_decay1[0], w_decay2[0],
                 a0[0], w_aaa1[0], w_aaa2[0], w_gate1[0], w_gate2[0], k_k[0], k_a[0], r_k[0], lnx_w[0],
                 lnx_b[0], conv_w[0], conv_norm_w[0], w_out[0], norm2_w[0], w_grp[0], b_grp[0],
                 w_exp[0], b_exp[0], exp_gate[0], exp_up[0], exp_down[0], final_w)
    return out.reshape(bsz, t_len, d)
```

```python
import itertools

import jax
import jax.numpy as jnp
from jax import lax
from jax.experimental import pallas as pl
from jax.experimental.pallas import tpu as pltpu

F32 = jnp.float32
BF16 = jnp.bfloat16

HEAD_DIM = 64
WKV_CHUNK = 64
GROUP_LANES = 256
N_ROUTE_GROUPS = 4
EXPERTS_PER_GROUP = 8
N_EXPERTS = N_ROUTE_GROUPS * EXPERTS_PER_GROUP
CONV_GROUP_DIM = 64
RMS_EPS = 1e-6
LNX_EPS = 64e-5
LANES = 128
LORA_DECAY, LORA_AAA, LORA_GATE = (0, 128), (128, 256), (256, 512)
LORA_PAD = LORA_GATE[1]
ADA_COLS = 1024
INPROJ_ROWS, INPROJ_COLS = 1024, 1024
LORA_ROWS = 512
WKV_ROWS = 1024
OUTPROJ_ROWS = 512
DISPATCH_ROWS = 256
COMBINE_ROWS = 256
ROUTE_BLOCK = 256
NORM_ROWS = 32
VMEM_LIMIT = 56 * 1024 * 1024


def _dot(a, b, precision=None):
    return jnp.dot(a, b, preferred_element_type=F32, precision=precision)


def _dot_nt(a, b):
    return lax.dot_general(a, b, (((1,), (1,)), ((), ())), preferred_element_type=F32)


def _dot_tn(a, b):
    return lax.dot_general(a, b, (((0,), (0,)), ((), ())), preferred_element_type=F32)


def _split_bf16(x):
    hi = x.astype(BF16)
    return hi, (x - hi.astype(F32)).astype(BF16)


def _dot_x3(a, b):
    ah, al = _split_bf16(a)
    bh, bl = _split_bf16(b)
    return _dot(ah, bh) + _dot(al, bh) + _dot(ah, bl)


def _pack_bf16_pair(x):
    c = x.shape[1] // 2
    lo = pltpu.bitcast(x[:, :c].astype(BF16).astype(F32), jnp.uint32)
    hi = pltpu.bitcast(x[:, c:].astype(BF16).astype(F32), jnp.uint32)
    return (lo >> 16) | (hi & jnp.uint32(0xFFFF0000))


def _unpack_bf16_pair(w):
    lo = pltpu.bitcast(w << 16, F32)
    hi = pltpu.bitcast(w & jnp.uint32(0xFFFF0000), F32)
    return jnp.concatenate([lo, hi], axis=1)


def _sigmoid(x):
    return 1.0 / (1.0 + jnp.exp(-x))


def _shift_rows(x, carry_row):
    rolled = pltpu.roll(x, 1, axis=0)
    row = lax.broadcasted_iota(jnp.int32, x.shape, 0)
    return jnp.where(row == 0, carry_row, rolled)


def _seg_ones(n, seg):
    r = lax.broadcasted_iota(jnp.int32, (n, n), 0) // seg
    c = lax.broadcasted_iota(jnp.int32, (n, n), 1) // seg
    return jnp.where(r == c, 1.0, 0.0).astype(BF16)


def _seg_sum(x, ones_bd):
    hi = x.astype(BF16)
    lo = (x - hi.astype(F32)).astype(BF16)
    return _dot(hi, ones_bd) + _dot(lo, ones_bd)


def _seg_sum1(x, ones_bd):
    return _dot(x.astype(BF16), ones_bd)


def _ada_kernel(cb_ref, w_ref, b_ref, o_ref):
    nb = cb_ref.shape[0]
    tn = w_ref.shape[1]
    for b in range(nb):
        cv = cb_ref[b]
        s = cv * _sigmoid(cv)
        cols = [jnp.sum(s * w_ref[:, j * LANES:(j + 1) * LANES], axis=0, keepdims=True)
                for j in range(tn // LANES)]
        o_ref[b:b + 1, :] = jnp.concatenate(cols, axis=1) + b_ref[...]


def _ada(c, w_ada, b_ada):
    nb, d = c.shape
    n_out = w_ada.shape[1]
    tn = ADA_COLS
    cb = jnp.broadcast_to(c[:, :, None], (nb, d, LANES))
    return pl.pallas_call(
        _ada_kernel,
        out_shape=jax.ShapeDtypeStruct((nb, n_out), F32),
        grid=(n_out // tn,),
        in_specs=[pl.BlockSpec((nb, d, LANES), lambda j: (0, 0, 0)),
                  pl.BlockSpec((d, tn), lambda j: (0, j)),
                  pl.BlockSpec((1, tn), lambda j: (0, j))],
        out_specs=pl.BlockSpec((nb, tn), lambda j: (0, j)),
        compiler_params=pltpu.CompilerParams(dimension_semantics=("arbitrary",),
                                             vmem_limit_bytes=VMEM_LIMIT),
        name="ada",
    )(cb, w_ada, b_ada.reshape(1, n_out))


def _inproj_kernel(x_ref, mod_ref, nw_ref, w_ref, h_ref, p_ref, hs_ref):
    @pl.when(pl.program_id(1) == 0)
    def _():
        rows = min(NORM_ROWS, x_ref.shape[0])
        gain = nw_ref[...] * (1.0 + mod_ref[0, 1:2, :])
        shift = mod_ref[0, 0:1, :]

        def norm_rows(r, carry):
            sl = pl.ds(pl.multiple_of(r * rows, rows), rows)
            x = x_ref[sl, :]
            ms = jnp.mean(x * x, axis=-1, keepdims=True)
            h = x * lax.rsqrt(ms + RMS_EPS) * gain + shift
            hb = h.astype(BF16)
            hs_ref[sl, :] = hb
            h_ref[sl, :] = hb
            return carry

        lax.fori_loop(0, x_ref.shape[0] // rows, norm_rows, 0, unroll=4)

    p_ref[...] = _dot(hs_ref[...], w_ref[...])


def _inproj(x2, mod3, norm_w, w_in_bf, t_len):
    n, d = x2.shape
    n_in = w_in_bf.shape[1]
    tm = min(INPROJ_ROWS, t_len)
    tn = INPROJ_COLS
    return pl.pallas_call(
        _inproj_kernel,
        out_shape=(jax.ShapeDtypeStruct((n, d), BF16), jax.ShapeDtypeStruct((n, n_in), F32)),
        grid=(n // tm, n_in // tn),
        in_specs=[pl.BlockSpec((tm, d), lambda i, j: (i, 0)),
                  pl.BlockSpec((1, 6, d), lambda i, j: (i * tm // t_len, 0, 0)),
                  pl.BlockSpec((1, d), lambda i, j: (0, 0)),
                  pl.BlockSpec((d, tn), lambda i, j: (0, j))],
        out_specs=(pl.BlockSpec((tm, d), lambda i, j: (i, 0)),
                   pl.BlockSpec((tm, tn), lambda i, j: (i, j))),
        scratch_shapes=[pltpu.VMEM((tm, d), BF16)],
        compiler_params=pltpu.CompilerParams(dimension_semantics=("arbitrary", "arbitrary"),
                                             vmem_limit_bytes=VMEM_LIMIT),
        name="inproj",
    )(x2, mod3, norm_w.reshape(1, d), w_in_bf)


def _lora_kernel(h_ref, w1_ref, mu_ref, wd2_ref, wa2_ref, wg2_ref, w0_ref, a0_ref,
                 lw_ref, ag_ref, gg_ref, carry_ref):
    @pl.when(pl.program_id(1) == 0)
    def _():
        carry_ref[...] = jnp.zeros_like(carry_ref)

    h = h_ref[...]
    w1 = w1_ref[...]
    p1 = _dot(h, w1.astype(BF16))
    p2 = _dot(h, (w1 * mu_ref[...]).astype(BF16))
    prev = _shift_rows(p2, carry_ref[...])
    carry_ref[...] = p2[p2.shape[0] - 1:, :]
    lo = p1 - p2 + prev
    dec = _dot(jnp.tanh(lo[:, slice(*LORA_DECAY)]).astype(BF16), wd2_ref[...].astype(BF16))
    z = -(w0_ref[...] + dec)
    softplus = jnp.maximum(z, 0.0) + jnp.log(1.0 + jnp.exp(-jnp.abs(z)))
    lw_ref[...] = -jnp.exp(-softplus - 0.5)
    ag_ref[...] = _sigmoid(a0_ref[...] + _dot(lo[:, slice(*LORA_AAA)].astype(BF16), wa2_ref[...].astype(BF16)))
    gg_ref[...] = _dot(_sigmoid(lo[:, slice(*LORA_GATE)]).astype(BF16), wg2_ref[...].astype(BF16))


def _lora(h, w1p, mup, wd2p, wa2p, wg2p, w0, a0, bsz, t_len):
    n, d = h.shape
    dr = w0.shape[-1]
    tm = min(LORA_ROWS, t_len)
    nt = t_len // tm
    row = lambda b, t: (b * nt + t, 0)
    const = lambda b, t: (0, 0)
    out = jax.ShapeDtypeStruct((n, dr), F32)
    return pl.pallas_call(
        _lora_kernel,
        out_shape=(out, out, out),
        grid=(bsz, nt),
        in_specs=[pl.BlockSpec((tm, d), row),
                  pl.BlockSpec((d, LORA_PAD), const),
                  pl.BlockSpec((d, LORA_PAD), const),
                  pl.BlockSpec(wd2p.shape, const),
                  pl.BlockSpec(wa2p.shape, const),
                  pl.BlockSpec(wg2p.shape, const),
                  pl.BlockSpec((1, dr), const),
                  pl.BlockSpec((1, dr), const)],
        out_specs=(pl.BlockSpec((tm, dr), row),) * 3,
        scratch_shapes=[pltpu.VMEM((1, LORA_PAD), F32)],
        compiler_params=pltpu.CompilerParams(dimension_semantics=("arbitrary", "arbitrary"),
                                             vmem_limit_bytes=VMEM_LIMIT),
        name="lora",
    )(h, w1p, mup, wd2p, wa2p, wg2p, w0.reshape(1, dr), a0.reshape(1, dr))


def _wkv_kernel(r_ref, k_ref, v_ref, lw_ref, ag_ref, gg_ref, mur_ref, muk_ref, muv_ref,
                kk_ref, ka_ref, rk_ref, lnw_ref, lnb_ref, y_ref, s_ref, carry_ref):
    lt = r_ref.shape[0]
    gl = GROUP_LANES
    c_len = WKV_CHUNK
    heads = gl // HEAD_DIM

    @pl.when(pl.program_id(2) == 0)
    def _():
        s_ref[...] = jnp.zeros_like(s_ref)
        carry_ref[...] = jnp.zeros_like(carry_ref)

    ones_bd = _seg_ones(gl, HEAD_DIM)

    def lerp(ref, mu_ref, idx):
        z = ref[...]
        prev = _shift_rows(z, carry_ref[idx:idx + 1, :])
        carry_ref[idx:idx + 1, :] = z[lt - 1:, :]
        return z + (prev - z) * mu_ref[...]

    r1 = lerp(r_ref, mur_ref, 0)
    k1 = lerp(k_ref, muk_ref, 1)
    v1 = lerp(v_ref, muv_ref, 2)
    ag = ag_ref[...]
    kk = k1 * kk_ref[...]
    kk = kk * lax.rsqrt(jnp.maximum(_seg_sum1(kk * kk, ones_bd), 1e-24))
    k2 = k1 * (1.0 + (ag - 1.0) * ka_ref[...])
    a_all = -kk
    b_all = kk * ag
    lw = lw_ref[...]

    rows = lax.broadcasted_iota(jnp.int32, (gl, gl), 0)
    cols = lax.broadcasted_iota(jnp.int32, (gl, gl), 1)
    same_head = (rows // c_len) == (cols // HEAD_DIM)
    ts = lax.broadcasted_iota(jnp.int32, (c_len, gl), 0)
    tj = lax.broadcasted_iota(jnp.int32, (c_len, gl), 1) % c_len
    tri_strict = ts > tj
    tri_incl = ts >= tj
    eye = jnp.where(ts == tj, 1.0, 0.0).astype(F32)
    cr = lax.broadcasted_iota(jnp.int32, (c_len, c_len), 0)
    cc = lax.broadcasted_iota(jnp.int32, (c_len, c_len), 1)
    cum_b = jnp.where(cr >= cc, 1.0, 0.0).astype(BF16)

    def bd(x):
        xb = x.astype(BF16)
        return jnp.where(same_head, jnp.concatenate([xb] * heads, axis=0), jnp.zeros((), BF16))

    def fold(x):
        out = x[0:c_len]
        for hh in range(1, heads):
            out = out + x[hh * c_len:(hh + 1) * c_len]
        return out

    nch = lt // c_len
    sls = [slice(c * c_len, (c + 1) * c_len) for c in range(nch)]

    def cumsum3(x):
        hi = x.astype(BF16)
        r = x - hi.astype(F32)
        mid = r.astype(BF16)
        lo = (r - mid.astype(F32)).astype(BF16)
        return _dot(cum_b, hi) + _dot(cum_b, mid) + _dot(cum_b, lo)

    wc, g_s, n_s, rp, y0, s_in = {}, {}, {}, {}, {}, {}
    state = [s_ref[...]]

    def chunk_parallel(chunks):
        lws = {c: lw[sls[c]] for c in chunks}
        cs = {c: cumsum3(lws[c]) for c in chunks}
        yield
        e_cs = {c: jnp.exp(cs[c]) for c in chunks}
        e_neg = {c: jnp.exp(-cs[c]) for c in chunks}
        for c in chunks:
            wc[c] = e_cs[c][c_len - 1:, :]
        at = {c: a_all[sls[c]] * jnp.exp(cs[c] - lws[c]) for c in chunks}
        rt = {c: r1[sls[c]] * e_cs[c] for c in chunks}
        bt = {c: b_all[sls[c]] * e_neg[c] for c in chunks}
        kt = {c: k2[sls[c]] * e_neg[c] for c in chunks}
        vc = {c: v1[sls[c]] for c in chunks}
        yield
        lhs = {c: jnp.concatenate([at[c], rt[c]], axis=0).astype(BF16) for c in chunks}
        g_b = {c: _dot_nt(lhs[c], bd(bt[c])) for c in chunks}
        yield
        g_k = {c: _dot_nt(lhs[c], bd(kt[c])) for c in chunks}
        yield
        n_ab = {c: jnp.where(tri_strict, g_b[c][:c_len], 0.0) for c in chunks}
        a_ak = {c: jnp.where(tri_strict, g_k[c][:c_len], 0.0) for c in chunks}
        a_rb = {c: jnp.where(tri_incl, g_b[c][c_len:], 0.0).astype(BF16) for c in chunks}
        a_rk = {c: jnp.where(tri_incl, g_k[c][c_len:], 0.0).astype(BF16) for c in chunks}

        p = {c: eye + jnp.where((ts // 2) == (tj // 2), n_ab[c], 0.0) for c in chunks}
        for half in (2, 4, 8, 16, 32):
            lower_left = (((ts // (2 * half)) == (tj // (2 * half)))
                          & ((ts % (2 * half)) >= half) & ((tj % (2 * half)) < half))
            n21_ta = {c: _dot(jnp.where(lower_left, n_ab[c], 0.0).astype(BF16), bd(p[c])) for c in chunks}
            yield
            p = {c: p[c] + _dot(p[c].astype(BF16), bd(n21_ta[c])) for c in chunks}
            yield

        z = {c: _dot(p[c].astype(BF16), jnp.concatenate([bd(a_ak[c]), bd(at[c])], axis=1))
             for c in chunks}
        yield
        ap = {c: z[c][:, gl:] for c in chunks}
        bd_v = {c: bd(vc[c]) for c in chunks}
        u0 = {c: _dot(z[c][:, :gl].astype(BF16), bd_v[c]) for c in chunks}
        yield
        w = {c: _dot(a_rb[c], jnp.concatenate([bd(ap[c]), bd(u0[c])], axis=1)) for c in chunks}
        yield
        for c in chunks:
            rp[c] = rt[c] + w[c][:, :gl]
            y0[c] = w[c][:, gl:] + _dot(a_rk[c], bd_v[c])
        yield
        bh = {c: (bt[c] * wc[c]).astype(BF16) for c in chunks}
        bk = {c: jnp.concatenate([bh[c], (kt[c] * wc[c]).astype(BF16)], axis=0) for c in chunks}
        for c in chunks:
            g_s[c] = jnp.where(same_head, _dot_tn(ap[c].astype(BF16), bh[c]), 0.0).astype(BF16)
        yield
        for c in chunks:
            n_s[c] = fold(jnp.where(
                same_head, _dot_tn(jnp.concatenate([u0[c], vc[c]], axis=0).astype(BF16), bk[c]), 0.0))
        yield

    def state_update(chunks):
        for c in chunks:
            s_in[c] = state[0].astype(BF16)
            state[0] = state[0] * wc[c] + _dot(s_in[c], g_s[c]) + n_s[c]
            yield

    def epilogue(chunks):
        rows = slice(chunks[0] * c_len, (chunks[-1] + 1) * c_len)
        y = jnp.concatenate([_dot_nt(rp[c].astype(BF16), bd(s_in[c])) + y0[c] for c in chunks], axis=0)
        yield
        inv_n = 1.0 / HEAD_DIM
        mean = _seg_sum(y, ones_bd) * inv_n
        yield
        yc = y - mean
        var = _seg_sum1(yc * yc, ones_bd) * inv_n
        yield
        yn = yc * lax.rsqrt(var + LNX_EPS) * lnw_ref[...] + lnb_ref[...]
        bonus = _seg_sum(r1[rows] * k2[rows] * rk_ref[...], ones_bd) * v1[rows]
        y_ref[rows, :] = ((yn + bonus) * gg_ref[rows, :]).astype(y_ref.dtype)
        yield

    def drain(gen):
        for _ in gen:
            pass

    def interleave(serial, other, every):
        for k, _ in enumerate(other, 1):
            if k % every == 0:
                next(serial, None)
        drain(serial)

    wave = max(nch // 2, 1)
    waves = [list(range(w0, min(w0 + wave, nch))) for w0 in range(0, nch, wave)]
    par_stages, epi_stages = 20, 4
    drain(chunk_parallel(waves[0]))
    for wi, chunks in enumerate(waves):
        others, n_other = [], 0
        if wi + 1 < len(waves):
            others.append(chunk_parallel(waves[wi + 1]))
            n_other += par_stages
        if wi >= 1:
            others.append(epilogue(waves[wi - 1]))
            n_other += epi_stages
        interleave(state_update(chunks), itertools.chain(*others), max(n_other // len(chunks), 1))
    drain(epilogue(waves[-1]))
    s_ref[...] = state[0]


def _wkv(proj, lw, ag, gg, mu_r, mu_k, mu_v, k_k, k_a, r_k, lnx_w, lnx_b, bsz, t_len):
    n = proj.shape[0]
    dr = lw.shape[1]
    gl = GROUP_LANES
    ng = dr // gl
    lt = min(WKV_ROWS, t_len)
    nt = t_len // lt
    row = lambda off: (lambda b, g, t: (b * nt + t, off + g))
    par = lambda b, g, t: (0, g)
    vec = lambda a: a.reshape(1, dr)
    return pl.pallas_call(
        _wkv_kernel,
        out_shape=jax.ShapeDtypeStruct((n, dr), BF16),
        grid=(bsz, ng, nt),
        in_specs=[pl.BlockSpec((lt, gl), row(0)),
                  pl.BlockSpec((lt, gl), row(ng)),
                  pl.BlockSpec((lt, gl), row(2 * ng)),
                  pl.BlockSpec((lt, gl), row(0)),
                  pl.BlockSpec((lt, gl), row(0)),
                  pl.BlockSpec((lt, gl), row(0))] + [pl.BlockSpec((1, gl), par)] * 8,
        out_specs=pl.BlockSpec((lt, gl), row(0)),
        scratch_shapes=[pltpu.VMEM((HEAD_DIM, gl), F32), pltpu.VMEM((8, gl), F32)],
        compiler_params=pltpu.CompilerParams(
            dimension_semantics=("arbitrary", "arbitrary", "arbitrary"),
            vmem_limit_bytes=VMEM_LIMIT),
        name="wkv",
    )(proj, proj, proj, lw, ag, gg, vec(mu_r), vec(mu_k), vec(mu_v), vec(k_k), vec(k_a),
      vec(r_k), vec(lnx_w), vec(lnx_b))


def _outproj_kernel(yr_ref, u_ref, gb_ref, gc_ref, cw_ref, cnw_ref, wo_ref, x_ref, mod_ref,
                    n2w_ref, wr_ref, br_ref, x1_ref, h2_ref, meta_ref, cnt_ref,
                    ccarry_ref, cnt_acc_ref):
    tm = x_ref.shape[0]
    dc = u_ref.shape[1]

    @pl.when(pl.program_id(1) == 0)
    def _():
        ccarry_ref[...] = jnp.zeros_like(ccarry_ref)

    @pl.when((pl.program_id(0) == 0) & (pl.program_id(1) == 0))
    def _():
        cnt_acc_ref[...] = jnp.zeros_like(cnt_acc_ref)

    pre = gc_ref[...] * u_ref[...]
    prev1 = _shift_rows(pre, ccarry_ref[0:1, :])
    prev2 = _shift_rows(prev1, ccarry_ref[1:2, :])
    ccarry_ref[0:1, :] = pre[tm - 1:, :]
    ccarry_ref[1:2, :] = pre[tm - 2:tm - 1, :]
    yc = gb_ref[...] * (prev2 * cw_ref[0:1, :] + prev1 * cw_ref[1:2, :] + pre * cw_ref[2:3, :])
    ones_bd = _seg_ones(GROUP_LANES, CONV_GROUP_DIM)
    ycs = []
    for s in range(dc // GROUP_LANES):
        ys = yc[:, s * GROUP_LANES:(s + 1) * GROUP_LANES]
        ms = _seg_sum1(ys * ys, ones_bd) * (1.0 / CONV_GROUP_DIM)
        ycs.append(ys * lax.rsqrt(ms + RMS_EPS))
    y_conv = (jnp.concatenate(ycs, axis=1) * cnw_ref[...]).astype(BF16)

    dr = yr_ref.shape[1]
    y_mix = _dot(yr_ref[...], wo_ref[0:dr, :]) + _dot(y_conv, wo_ref[dr:, :])
    x1 = x_ref[...] + mod_ref[0, 2:3, :] * y_mix
    x1_ref[...] = x1
    ms = jnp.mean(x1 * x1, axis=-1, keepdims=True)
    h2 = x1 * lax.rsqrt(ms + RMS_EPS) * n2w_ref[...]
    h2 = h2 * (1.0 + mod_ref[0, 4:5, :]) + mod_ref[0, 3:4, :]
    h2_ref[...] = _pack_bf16_pair(h2)

    logits = _dot_x3(h2, wr_ref[...]) + br_ref[...]
    lane = lax.broadcasted_iota(jnp.int32, logits.shape, 1)
    neg = jnp.float32(-jnp.inf)
    big = jnp.int32(1 << 20)
    glog = jnp.where(lane < N_ROUTE_GROUPS, logits, neg)
    gmax = jnp.max(glog, axis=-1, keepdims=True)
    grp = jnp.min(jnp.where(glog == gmax, lane, big), axis=-1, keepdims=True)
    p_grp = 1.0 / jnp.sum(jnp.exp(glog - gmax), axis=-1, keepdims=True)
    e_lane = lane - N_ROUTE_GROUPS
    in_grp = (e_lane >= grp * EXPERTS_PER_GROUP) & (e_lane < (grp + 1) * EXPERTS_PER_GROUP)
    elog = jnp.where(in_grp, logits, neg)
    m1 = jnp.max(elog, axis=-1, keepdims=True)
    i1 = jnp.min(jnp.where(elog == m1, lane, big), axis=-1, keepdims=True)
    elog2 = jnp.where(lane == i1, neg, elog)
    m2 = jnp.max(elog2, axis=-1, keepdims=True)
    i2 = jnp.min(jnp.where(elog2 == m2, lane, big), axis=-1, keepdims=True)
    t2 = jnp.exp(m2 - m1)
    w1 = p_grp / (1.0 + t2)
    w2 = p_grp * t2 / (1.0 + t2)
    e1 = i1 - N_ROUTE_GROUPS
    e2 = i2 - N_ROUTE_GROUPS

    oh1 = lane == e1
    oh2 = lane == e2
    oh = jnp.where(oh1 | oh2, 1.0, 0.0)
    tr = lax.broadcasted_iota(jnp.int32, (tm, tm), 0)
    tc = lax.broadcasted_iota(jnp.int32, (tm, tm), 1)
    lower = jnp.where(tr > tc, 1.0, 0.0).astype(BF16)
    before = _dot(lower, oh.astype(BF16)) + cnt_acc_ref[...]
    rank1 = jnp.sum(jnp.where(oh1, before, 0.0), axis=-1, keepdims=True)
    rank2 = jnp.sum(jnp.where(oh2, before, 0.0), axis=-1, keepdims=True)
    cnt = cnt_acc_ref[...] + jnp.sum(oh, axis=0, keepdims=True)
    cnt_acc_ref[...] = cnt
    cnt_ref[...] = jnp.broadcast_to(cnt, cnt_ref.shape)

    meta = jnp.where(lane == 0, e1.astype(F32), 0.0)
    meta = jnp.where(lane == 1, e2.astype(F32), meta)
    meta = jnp.where(lane == 2, w1, meta)
    meta = jnp.where(lane == 3, w2, meta)
    meta = jnp.where(lane == 4, rank1, meta)
    meta = jnp.where(lane == 5, rank2, meta)
    meta_ref[...] = meta


def _outproj(yr, proj, conv_w, conv_norm_w, w_out_bf, x2, mod3, norm2_w, wr, br, bsz, t_len):
    n, d = x2.shape
    dr = yr.shape[1]
    dc = conv_w.shape[1]
    tm = min(OUTPROJ_ROWS, t_len)
    nt = t_len // tm
    nseg = dr // dc
    row = lambda off: (lambda b, t: (b * nt + t, off))
    const = lambda b, t: (0, 0)
    return pl.pallas_call(
        _outproj_kernel,
        out_shape=(jax.ShapeDtypeStruct((n, d), F32), jax.ShapeDtypeStruct((n, d // 2), jnp.uint32),
                   jax.ShapeDtypeStruct((n, LANES), F32), jax.ShapeDtypeStruct((8, LANES), F32)),
        grid=(bsz, nt),
        in_specs=[pl.BlockSpec((tm, dr), row(0)),
                  pl.BlockSpec((tm, dc), row(3 * nseg)),
                  pl.BlockSpec((tm, dc), row(4 * nseg)),
                  pl.BlockSpec((tm, dc), row(5 * nseg)),
                  pl.BlockSpec((3, dc), const),
                  pl.BlockSpec((1, dc), const),
                  pl.BlockSpec((dr + dc, d), const),
                  pl.BlockSpec((tm, d), row(0)),
                  pl.BlockSpec((1, 6, d), lambda b, t: (b, 0, 0)),
                  pl.BlockSpec((1, d), const),
                  pl.BlockSpec((d, LANES), const),
                  pl.BlockSpec((1, LANES), const)],
        out_specs=(pl.BlockSpec((tm, d), row(0)), pl.BlockSpec((tm, d // 2), row(0)),
                   pl.BlockSpec((tm, LANES), row(0)), pl.BlockSpec((8, LANES), const)),
        scratch_shapes=[pltpu.VMEM((8, dc), F32), pltpu.VMEM((1, LANES), F32)],
        compiler_params=pltpu.CompilerParams(dimension_semantics=("arbitrary", "arbitrary"),
                                             vmem_limit_bytes=VMEM_LIMIT),
        name="outproj",
    )(yr, proj, proj, proj, conv_w, conv_norm_w.reshape(1, dc), w_out_bf, x2, mod3,
      norm2_w.reshape(1, d), wr, br)


def _dispatch_kernel(zoff_ref, dest_hbm, h2_hbm, xs_hbm, idx_smem, tile_ref, zero_ref, zblk_ref,
                     sem_idx, sem_tile, sem_rows, sem_zero, sem_tail):
    i = pl.program_id(0)
    nsteps = pl.num_programs(0)
    th = idx_smem.shape[1] // 2
    slot = i % 2

    def idx_copy(step, sl):
        return pltpu.make_async_copy(dest_hbm.at[step], idx_smem.at[sl], sem_idx.at[sl])

    def tile_copy(step, sl):
        rows = pl.ds(pl.multiple_of(step * th, th), th)
        return pltpu.make_async_copy(h2_hbm.at[rows], tile_ref.at[sl], sem_tile.at[sl])

    def rows_wait(sl):
        pltpu.make_async_copy(xs_hbm.at[pl.ds(0, 2 * th)], xs_hbm.at[pl.ds(0, 2 * th)],
                              sem_rows.at[sl]).wait()

    nblk = xs_hbm.shape[0] // ROUTE_BLOCK

    def tail_copy(b):
        start = pl.multiple_of(b * ROUTE_BLOCK, ROUTE_BLOCK)
        return pltpu.make_async_copy(zblk_ref, xs_hbm.at[pl.ds(start, ROUTE_BLOCK)], sem_tail)

    @pl.when(i == 0)
    def _():
        idx_copy(0, 0).start()
        tile_copy(0, 0).start()
        zero_ref[...] = jnp.zeros_like(zero_ref)

        def zero_expert(e, carry):
            off = zoff_ref[e]

            def zero_row(j, c):
                pltpu.make_async_copy(zero_ref, xs_hbm.at[pl.ds(off + j, 1)], sem_zero).start()
                return c

            return lax.fori_loop(0, zoff_ref[N_EXPERTS + e], zero_row, carry)

        lax.fori_loop(0, N_EXPERTS, zero_expert, 0)
        total = zoff_ref[2 * N_EXPERTS]
        bulk = pl.multiple_of((total // 8) * 8, 8)

        @pl.when(bulk > 0)
        def _():
            pltpu.make_async_copy(xs_hbm.at[pl.ds(0, bulk)], xs_hbm.at[pl.ds(0, bulk)], sem_zero).wait()

        def wait_row(j, c):
            pltpu.make_async_copy(zero_ref, xs_hbm.at[pl.ds(0, 1)], sem_zero).wait()
            return c

        lax.fori_loop(0, total - bulk, wait_row, 0)

        zblk_ref[...] = jnp.zeros_like(zblk_ref)

        def tail_start(b, c):
            tail_copy(b).start()
            return c

        lax.fori_loop(zoff_ref[2 * N_EXPERTS + 1], nblk, tail_start, 0)

    idx_copy(i, slot).wait()
    tile_copy(i, slot).wait()

    @pl.when(i > 0)
    def _():
        rows_wait(1 - slot)

    @pl.when(i + 1 < nsteps)
    def _():
        idx_copy(i + 1, 1 - slot).start()
        tile_copy(i + 1, 1 - slot).start()

    for p in range(2):
        @pl.when(slot == p)
        def _(p=p):
            for t in range(th):
                for s in range(2):
                    d = idx_smem[p, 2 * t + s]
                    pltpu.make_async_copy(tile_ref.at[p, pl.ds(t, 1)], xs_hbm.at[pl.ds(d, 1)],
                                          sem_rows.at[p]).start(priority=s)

    @pl.when(i == nsteps - 1)
    def _():
        rows_wait(slot)

        def tail_wait(b, c):
            tail_copy(b).wait()
            return c

        lax.fori_loop(zoff_ref[2 * N_EXPERTS + 1], nblk, tail_wait, 0)


def _dispatch(dest, zero_off, h2, n_rows):
    n, d = h2.shape
    th = min(DISPATCH_ROWS, n)
    return pl.pallas_call(
        _dispatch_kernel,
        out_shape=jax.ShapeDtypeStruct((n_rows, d), h2.dtype),
        grid_spec=pltpu.PrefetchScalarGridSpec(
            num_scalar_prefetch=1,
            grid=(n // th,),
            in_specs=[pl.BlockSpec(memory_space=pl.ANY), pl.BlockSpec(memory_space=pl.ANY)],
            out_specs=pl.BlockSpec(memory_space=pl.ANY),
            scratch_shapes=[pltpu.SMEM((2, 2 * th), jnp.int32),
                            pltpu.VMEM((2, th, d), h2.dtype),
                            pltpu.VMEM((1, d), h2.dtype),
                            pltpu.VMEM((ROUTE_BLOCK, d), h2.dtype),
                            pltpu.SemaphoreType.DMA((2,)), pltpu.SemaphoreType.DMA((2,)),
                            pltpu.SemaphoreType.DMA((2,)),
                            pltpu.SemaphoreType.DMA, pltpu.SemaphoreType.DMA]),
        compiler_params=pltpu.CompilerParams(dimension_semantics=("arbitrary",),
                                             vmem_limit_bytes=VMEM_LIMIT),
        name="dispatch",
    )(zero_off, dest.reshape(n // th, 2 * th), h2)


def _expert_kernel(sched_ref, na_ref, xs_ref, wg_hbm, wu_hbm, wd_hbm, ys_ref,
                   wg_buf, wu_buf, wd_buf, sem):
    i = pl.program_id(0)

    def weight_copies(e, slot):
        return (pltpu.make_async_copy(wg_hbm.at[e], wg_buf.at[slot], sem.at[slot, 0]),
                pltpu.make_async_copy(wu_hbm.at[e], wu_buf.at[slot], sem.at[slot, 1]),
                pltpu.make_async_copy(wd_hbm.at[e], wd_buf.at[slot], sem.at[slot, 2]))

    @pl.when(i < na_ref[0])
    def _():
        e = sched_ref[0, i]
        slot = sched_ref[1, i]

        @pl.when(i == 0)
        def _():
            for cp in weight_copies(e, slot):
                cp.start()

        @pl.when(sched_ref[2, i] == 1)
        def _():
            for cp in weight_copies(e, slot):
                cp.wait()
            nxt = sched_ref[3, i]

            @pl.when(nxt >= 0)
            def _():
                for cp in weight_copies(nxt, 1 - slot):
                    cp.start()

        x = _unpack_bf16_pair(xs_ref[...]).astype(BF16)
        g = _dot(x, wg_buf[slot].astype(BF16))
        u = _dot(x, wu_buf[slot].astype(BF16))
        hid = (g * _sigmoid(g) * u).astype(BF16)
        ys_ref[...] = _pack_bf16_pair(_dot(hid, wd_buf[slot].astype(BF16)))

    @pl.when(i >= na_ref[0])
    def _():
        ys_ref[...] = jnp.zeros_like(ys_ref)


def _experts(block_e, n_active, xs, exp_gate, exp_up, exp_down):
    n_rows, dp = xs.shape
    _, d, de = exp_gate.shape
    nblk = n_rows // ROUTE_BLOCK
    idx = jnp.arange(nblk, dtype=jnp.int32)
    valid = idx < n_active[0]
    first = valid & ((idx == 0) | (block_e != jnp.roll(block_e, 1)))
    slot = jnp.maximum(jnp.cumsum(first.astype(jnp.int32)) - 1, 0) % 2
    first_at = jnp.where(first, idx, nblk)
    next_first = jnp.concatenate([lax.cummin(first_at, reverse=True)[1:], jnp.full((1,), nblk, jnp.int32)])
    nxt = jnp.where(next_first < nblk, block_e[jnp.minimum(next_first, nblk - 1)], -1)
    sched = jnp.stack([block_e, slot, first.astype(jnp.int32), nxt]).astype(jnp.int32)

    blk_row = lambda i, sc, na: (jnp.minimum(i, jnp.maximum(na[0] - 1, 0)), 0)
    return pl.pallas_call(
        _expert_kernel,
        out_shape=jax.ShapeDtypeStruct((n_rows, dp), jnp.uint32),
        grid_spec=pltpu.PrefetchScalarGridSpec(
            num_scalar_prefetch=2,
            grid=(nblk,),
            in_specs=[pl.BlockSpec((ROUTE_BLOCK, dp), blk_row),
                      pl.BlockSpec(memory_space=pl.ANY),
                      pl.BlockSpec(memory_space=pl.ANY),
                      pl.BlockSpec(memory_space=pl.ANY)],
            out_specs=pl.BlockSpec((ROUTE_BLOCK, dp), lambda i, sc, na: (i, 0)),
            scratch_shapes=[pltpu.VMEM((2, d, de), exp_gate.dtype),
                            pltpu.VMEM((2, d, de), exp_up.dtype),
                            pltpu.VMEM((2, de, d), exp_down.dtype),
                            pltpu.SemaphoreType.DMA((2, 3))]),
        compiler_params=pltpu.CompilerParams(dimension_semantics=("arbitrary",),
                                             vmem_limit_bytes=VMEM_LIMIT),
        name="experts",
    )(sched, n_active, xs, exp_gate, exp_up, exp_down)


def _combine_kernel(dest_hbm, ys_hbm, x1_ref, meta_ref, mod_ref, fw_ref, o_ref,
                    idx_smem, buf_ref, sem_idx, sem_rows):
    i = pl.program_id(0)
    nsteps = pl.num_programs(0)
    tj = x1_ref.shape[0]
    slot = i % 2

    def idx_copy(step, sl):
        return pltpu.make_async_copy(dest_hbm.at[step], idx_smem.at[sl], sem_idx.at[sl])

    def gather(sl):
        for t in range(tj):
            for s in range(2):
                d = idx_smem[sl, 2 * t + s]
                pltpu.make_async_copy(ys_hbm.at[pl.ds(d, 1)], buf_ref.at[sl, s, pl.ds(t, 1)],
                                      sem_rows.at[sl]).start(priority=s)

    @pl.when(i == 0)
    def _():
        first = idx_copy(0, 0)
        first.start()
        first.wait()
        gather(0)

        @pl.when(nsteps > 1)
        def _():
            idx_copy(1, 1).start()

    for p in range(2):
        @pl.when((i + 1 < nsteps) & (slot == 1 - p))
        def _(p=p):
            idx_copy(i + 1, p).wait()

            @pl.when(i + 2 < nsteps)
            def _():
                idx_copy(i + 2, 1 - p).start()

            gather(p)

    pltpu.make_async_copy(buf_ref.at[slot], buf_ref.at[slot], sem_rows.at[slot]).wait()

    meta = meta_ref[...]
    y = (meta[:, 2:3] * _unpack_bf16_pair(buf_ref[slot, 0])
         + meta[:, 3:4] * _unpack_bf16_pair(buf_ref[slot, 1]))
    x2 = x1_ref[...] + mod_ref[0, 5:6, :] * y
    ms = jnp.mean(x2 * x2, axis=-1, keepdims=True)
    o_ref[...] = x2 * lax.rsqrt(ms + RMS_EPS) * fw_ref[...]


def _combine(dest, ys, x1, meta, mod3, final_w, t_len):
    n, d = x1.shape
    tj = min(COMBINE_ROWS, t_len)
    return pl.pallas_call(
        _combine_kernel,
        out_shape=jax.ShapeDtypeStruct((n, d), F32),
        grid=(n // tj,),
        in_specs=[pl.BlockSpec(memory_space=pl.ANY),
                  pl.BlockSpec(memory_space=pl.ANY),
                  pl.BlockSpec((tj, d), lambda i: (i, 0)),
                  pl.BlockSpec((tj, LANES), lambda i: (i, 0)),
                  pl.BlockSpec((1, 6, d), lambda i: (i * tj // t_len, 0, 0)),
                  pl.BlockSpec((1, d), lambda i: (0, 0))],
        out_specs=pl.BlockSpec((tj, d), lambda i: (i, 0)),
        scratch_shapes=[pltpu.SMEM((2, 2 * tj), jnp.int32), pltpu.VMEM((2, 2, tj, d // 2), jnp.uint32),
                        pltpu.SemaphoreType.DMA((2,)), pltpu.SemaphoreType.DMA((2,))],
        compiler_params=pltpu.CompilerParams(dimension_semantics=("arbitrary",),
                                             vmem_limit_bytes=VMEM_LIMIT),
        name="combine",
    )(dest.reshape(n // tj, 2 * tj), ys, x1, meta, mod3, final_w.reshape(1, d))


def _pad_cols(w, width):
    return jnp.pad(w, ((0, 0), (0, width - w.shape[1])))


def _pad_rows(w, height):
    return jnp.pad(w, ((0, height - w.shape[0]), (0, 0)))


def _layer(x2, c, bsz, t_len, w_ada, b_ada, norm1_w, w_in, mu_r, mu_k, mu_v, mu_w, mu_a, mu_g, w0,
           w_decay1, w_decay2, a0, w_aaa1, w_aaa2, w_gate1, w_gate2, k_k, k_a, r_k, lnx_w, lnx_b,
           conv_w, conv_norm_w, w_out, norm2_w, w_grp, b_grp, w_exp, b_exp, exp_gate, exp_up,
           exp_down, final_w):
    n, d = x2.shape
    dr = w0.shape[0]
    widths = [hi - lo for lo, hi in (LORA_DECAY, LORA_AAA, LORA_GATE)]
    assert all(w.shape[1] <= n_ for w, n_ in zip((w_decay1, w_aaa1, w_gate1), widths))
    assert dr % GROUP_LANES == 0 and conv_w.shape[1] == dr and w_in.shape[1] == 6 * dr
    assert t_len % WKV_CHUNK == 0 and w_grp.shape[1] == N_ROUTE_GROUPS and w_exp.shape[1] == N_EXPERTS

    mod3 = _ada(c, w_ada, b_ada).reshape(bsz, 6, d)
    h, proj = _inproj(x2, mod3, norm1_w, w_in.astype(BF16), t_len)

    w1p = jnp.concatenate([_pad_cols(w, n_) for w, n_ in zip((w_decay1, w_aaa1, w_gate1), widths)], axis=1)
    mup = jnp.concatenate([jnp.broadcast_to(mu[:, None], (d, n_)) for mu, n_ in zip((mu_w, mu_a, mu_g), widths)],
                          axis=1)
    wd2p, wa2p, wg2p = (_pad_rows(w, n_) for w, n_ in zip((w_decay2, w_aaa2, w_gate2), widths))
    lw, ag, gg = _lora(h, w1p, mup, wd2p, wa2p, wg2p, w0, a0, bsz, t_len)

    yr = _wkv(proj, lw, ag, gg, mu_r, mu_k, mu_v, k_k, k_a, r_k.reshape(dr), lnx_w, lnx_b, bsz, t_len)

    wr = _pad_cols(jnp.concatenate([w_grp, w_exp], axis=1), LANES)
    br = _pad_cols(jnp.concatenate([b_grp, b_exp]).reshape(1, -1), LANES)
    x1, h2, meta, cnt = _outproj(yr, proj, conv_w, conv_norm_w, w_out.astype(BF16), x2, mod3,
                                 norm2_w, wr, br, bsz, t_len)

    counts = cnt[0, :N_EXPERTS].astype(jnp.int32)
    padded = (counts + ROUTE_BLOCK - 1) // ROUTE_BLOCK * ROUTE_BLOCK
    pad_ends = jnp.cumsum(padded)
    pad_starts = pad_ends - padded
    top_e = meta[:, 0:2].astype(jnp.int32)
    rank = meta[:, 4:6].astype(jnp.int32)
    one_hot = top_e[:, :, None] == jnp.arange(N_EXPERTS, dtype=jnp.int32)
    dest = (jnp.sum(jnp.where(one_hot, pad_starts, 0), axis=-1) + rank).reshape(-1)
    nblk = (2 * n) // ROUTE_BLOCK + N_EXPERTS
    n_rows = nblk * ROUTE_BLOCK
    block_start = jnp.arange(nblk, dtype=jnp.int32) * ROUTE_BLOCK
    block_e = jnp.minimum(jnp.sum(pad_ends[None, :] <= block_start[:, None], axis=1),
                          N_EXPERTS - 1).astype(jnp.int32)
    n_active = (pad_ends[-1:] // ROUTE_BLOCK).astype(jnp.int32)

    pad_len = padded - counts
    zero_off = jnp.concatenate([pad_starts + counts, pad_len, jnp.sum(pad_len, keepdims=True),
                                n_active]).astype(jnp.int32)
    xs = _dispatch(dest, zero_off, h2, n_rows)
    ys = _experts(block_e, n_active, xs, exp_gate, exp_up, exp_down)
    return _combine(dest, ys, x1, meta, mod3, final_w, t_len)


def kernel(x, c, w_ada, b_ada, norm1_w, w_in, mu_r, mu_k, mu_v, mu_w, mu_a, mu_g, w0, w_decay1, w_decay2, a0, w_aaa1, w_aaa2, w_gate1, w_gate2, k_k, k_a, r_k, lnx_w, lnx_b, conv_w, conv_norm_w, w_out, norm2_w, w_grp, b_grp, w_exp, b_exp, exp_gate, exp_up, exp_down, final_w):
    bsz, t_len, d = x.shape
    assert w_ada.shape[0] == 1, "single-layer block"
    out = _layer(x.reshape(bsz * t_len, d), c, bsz, t_len, w_ada[0], b_ada[0], norm1_w[0], w_in[0],
                 mu_r[0], mu_k[0], mu_v[0], mu_w[0], mu_a[0], mu_g[0], w0[0], w_decay1[0], w_decay2[0],
                 a0[0], w_aaa1[0], w_aaa2[0], w_gate1[0], w_gate2[0], k_k[0], k_a[0], r_k[0], lnx_w[0],
                 lnx_b[0], conv_w[0], conv_norm_w[0], w_out[0], norm2_w[0], w_grp[0], b_grp[0],
                 w_exp[0], b_exp[0], exp_gate[0], exp_up[0], exp_down[0], final_w)
    return out.reshape(bsz, t_len, d)
```

```python
import itertools

import jax
import jax.numpy as jnp
from jax import lax
from jax.experimental import pallas as pl
from jax.experimental.pallas import tpu as pltpu

F32 = jnp.float32
BF16 = jnp.bfloat16

HEAD_DIM = 64
WKV_CHUNK = 64
GROUP_LANES = 256
N_ROUTE_GROUPS = 4
EXPERTS_PER_GROUP = 8
N_EXPERTS = N_ROUTE_GROUPS * EXPERTS_PER_GROUP
CONV_GROUP_DIM = 64
RMS_EPS = 1e-6
LNX_EPS = 64e-5
LANES = 128
LORA_DECAY, LORA_AAA, LORA_GATE = (0, 128), (128, 256), (256, 512)
LORA_PAD = LORA_GATE[1]
ADA_COLS = 1024
INPROJ_ROWS, INPROJ_COLS = 1024, 1536
LORA_ROWS = 512
WKV_ROWS = 1024
OUTPROJ_ROWS = 512
DISPATCH_ROWS = 256
COMBINE_ROWS = 256
ROUTE_BLOCK = 256
NORM_ROWS = 32
VMEM_LIMIT = 56 * 1024 * 1024


def _dot(a, b, precision=None):
    return jnp.dot(a, b, preferred_element_type=F32, precision=precision)


def _dot_nt(a, b):
    return lax.dot_general(a, b, (((1,), (1,)), ((), ())), preferred_element_type=F32)


def _dot_tn(a, b):
    return lax.dot_general(a, b, (((0,), (0,)), ((), ())), preferred_element_type=F32)


def _split_bf16(x):
    hi = x.astype(BF16)
    return hi, (x - hi.astype(F32)).astype(BF16)


def _dot_x3(a, b):
    ah, al = _split_bf16(a)
    bh, bl = _split_bf16(b)
    return _dot(ah, bh) + _dot(al, bh) + _dot(ah, bl)


def _pack_bf16_pair(x):
    c = x.shape[1] // 2
    lo = pltpu.bitcast(x[:, :c].astype(BF16).astype(F32), jnp.uint32)
    hi = pltpu.bitcast(x[:, c:].astype(BF16).astype(F32), jnp.uint32)
    return (lo >> 16) | (hi & jnp.uint32(0xFFFF0000))


def _unpack_bf16_pair(w):
    lo = pltpu.bitcast(w << 16, F32)
    hi = pltpu.bitcast(w & jnp.uint32(0xFFFF0000), F32)
    return jnp.concatenate([lo, hi], axis=1)


def _sigmoid(x):
    return 1.0 / (1.0 + jnp.exp(-x))


def _shift_rows(x, carry_row):
    rolled = pltpu.roll(x, 1, axis=0)
    row = lax.broadcasted_iota(jnp.int32, x.shape, 0)
    return jnp.where(row == 0, carry_row, rolled)


def _seg_ones(n, seg):
    r = lax.broadcasted_iota(jnp.int32, (n, n), 0) // seg
    c = lax.broadcasted_iota(jnp.int32, (n, n), 1) // seg
    return jnp.where(r == c, 1.0, 0.0).astype(BF16)


def _seg_sum(x, ones_bd):
    hi = x.astype(BF16)
    lo = (x - hi.astype(F32)).astype(BF16)
    return _dot(hi, ones_bd) + _dot(lo, ones_bd)


def _seg_sum1(x, ones_bd):
    return _dot(x.astype(BF16), ones_bd)


def _ada_kernel(cb_ref, w_ref, b_ref, o_ref):
    nb = cb_ref.shape[0]
    tn = w_ref.shape[1]
    for b in range(nb):
        cv = cb_ref[b]
        s = cv * _sigmoid(cv)
        cols = [jnp.sum(s * w_ref[:, j * LANES:(j + 1) * LANES], axis=0, keepdims=True)
                for j in range(tn // LANES)]
        o_ref[b:b + 1, :] = jnp.concatenate(cols, axis=1) + b_ref[...]


def _ada(c, w_ada, b_ada):
    nb, d = c.shape
    n_out = w_ada.shape[1]
    tn = ADA_COLS
    cb = jnp.broadcast_to(c[:, :, None], (nb, d, LANES))
    return pl.pallas_call(
        _ada_kernel,
        out_shape=jax.ShapeDtypeStruct((nb, n_out), F32),
        grid=(n_out // tn,),
        in_specs=[pl.BlockSpec((nb, d, LANES), lambda j: (0, 0, 0)),
                  pl.BlockSpec((d, tn), lambda j: (0, j)),
                  pl.BlockSpec((1, tn), lambda j: (0, j))],
        out_specs=pl.BlockSpec((nb, tn), lambda j: (0, j)),
        compiler_params=pltpu.CompilerParams(dimension_semantics=("arbitrary",),
                                             vmem_limit_bytes=VMEM_LIMIT),
        name="ada",
    )(cb, w_ada, b_ada.reshape(1, n_out))


def _inproj_kernel(x_ref, mod_ref, nw_ref, w_ref, h_ref, p_ref):
    @pl.when(pl.program_id(1) == 0)
    def _():
        rows = min(NORM_ROWS, x_ref.shape[0])
        gain = nw_ref[...] * (1.0 + mod_ref[0, 1:2, :])
        shift = mod_ref[0, 0:1, :]

        def norm_rows(r, carry):
            sl = pl.ds(pl.multiple_of(r * rows, rows), rows)
            x = x_ref[sl, :]
            ms = jnp.mean(x * x, axis=-1, keepdims=True)
            h = x * lax.rsqrt(ms + RMS_EPS) * gain + shift
            h_ref[sl, :] = h.astype(BF16)
            return carry

        lax.fori_loop(0, x_ref.shape[0] // rows, norm_rows, 0, unroll=4)

    p_ref[...] = _dot(h_ref[...], w_ref[...])


def _inproj(x2, mod3, norm_w, w_in_bf, t_len):
    n, d = x2.shape
    n_in = w_in_bf.shape[1]
    tm = min(INPROJ_ROWS, t_len)
    tn = INPROJ_COLS
    return pl.pallas_call(
        _inproj_kernel,
        out_shape=(jax.ShapeDtypeStruct((n, d), BF16), jax.ShapeDtypeStruct((n, n_in), F32)),
        grid=(n // tm, n_in // tn),
        in_specs=[pl.BlockSpec((tm, d), lambda i, j: (i, 0)),
                  pl.BlockSpec((1, 6, d), lambda i, j: (i * tm // t_len, 0, 0)),
                  pl.BlockSpec((1, d), lambda i, j: (0, 0)),
                  pl.BlockSpec((d, tn), lambda i, j: (0, j))],
        out_specs=(pl.BlockSpec((tm, d), lambda i, j: (i, 0)),
                   pl.BlockSpec((tm, tn), lambda i, j: (i, j))),
        compiler_params=pltpu.CompilerParams(dimension_semantics=("arbitrary", "arbitrary"),
                                             vmem_limit_bytes=VMEM_LIMIT),
        name="inproj",
    )(x2, mod3, norm_w.reshape(1, d), w_in_bf)


def _lora_kernel(h_ref, w1_ref, mu_ref, wd2_ref, wa2_ref, wg2_ref, w0_ref, a0_ref,
                 lw_ref, ag_ref, gg_ref, carry_ref):
    @pl.when(pl.program_id(1) == 0)
    def _():
        carry_ref[...] = jnp.zeros_like(carry_ref)

    h = h_ref[...]
    w1 = w1_ref[...]
    p1 = _dot(h, w1.astype(BF16))
    p2 = _dot(h, (w1 * mu_ref[...]).astype(BF16))
    prev = _shift_rows(p2, carry_ref[...])
    carry_ref[...] = p2[p2.shape[0] - 1:, :]
    lo = p1 - p2 + prev
    dec = _dot(jnp.tanh(lo[:, slice(*LORA_DECAY)]).astype(BF16), wd2_ref[...].astype(BF16))
    z = -(w0_ref[...] + dec)
    softplus = jnp.maximum(z, 0.0) + jnp.log(1.0 + jnp.exp(-jnp.abs(z)))
    lw_ref[...] = -jnp.exp(-softplus - 0.5)
    ag_ref[...] = _sigmoid(a0_ref[...] + _dot(lo[:, slice(*LORA_AAA)].astype(BF16), wa2_ref[...].astype(BF16)))
    gg_ref[...] = _dot(_sigmoid(lo[:, slice(*LORA_GATE)]).astype(BF16), wg2_ref[...].astype(BF16))


def _lora(h, w1p, mup, wd2p, wa2p, wg2p, w0, a0, bsz, t_len):
    n, d = h.shape
    dr = w0.shape[-1]
    tm = min(LORA_ROWS, t_len)
    nt = t_len // tm
    row = lambda b, t: (b * nt + t, 0)
    const = lambda b, t: (0, 0)
    out = jax.ShapeDtypeStruct((n, dr), F32)
    return pl.pallas_call(
        _lora_kernel,
        out_shape=(out, out, out),
        grid=(bsz, nt),
        in_specs=[pl.BlockSpec((tm, d), row),
                  pl.BlockSpec((d, LORA_PAD), const),
                  pl.BlockSpec((d, LORA_PAD), const),
                  pl.BlockSpec(wd2p.shape, const),
                  pl.BlockSpec(wa2p.shape, const),
                  pl.BlockSpec(wg2p.shape, const),
                  pl.BlockSpec((1, dr), const),
                  pl.BlockSpec((1, dr), const)],
        out_specs=(pl.BlockSpec((tm, dr), row),) * 3,
        scratch_shapes=[pltpu.VMEM((1, LORA_PAD), F32)],
        compiler_params=pltpu.CompilerParams(dimension_semantics=("arbitrary", "arbitrary"),
                                             vmem_limit_bytes=VMEM_LIMIT),
        name="lora",
    )(h, w1p, mup, wd2p, wa2p, wg2p, w0.reshape(1, dr), a0.reshape(1, dr))


def _wkv_kernel(r_ref, k_ref, v_ref, lw_ref, ag_ref, gg_ref, mur_ref, muk_ref, muv_ref,
                kk_ref, ka_ref, rk_ref, lnw_ref, lnb_ref, y_ref, s_ref, carry_ref):
    lt = r_ref.shape[0]
    gl = GROUP_LANES
    c_len = WKV_CHUNK
    heads = gl // HEAD_DIM

    @pl.when(pl.program_id(2) == 0)
    def _():
        s_ref[...] = jnp.zeros_like(s_ref)
        carry_ref[...] = jnp.zeros_like(carry_ref)

    ones_bd = _seg_ones(gl, HEAD_DIM)

    def lerp(ref, mu_ref, idx):
        z = ref[...]
        prev = _shift_rows(z, carry_ref[idx:idx + 1, :])
        carry_ref[idx:idx + 1, :] = z[lt - 1:, :]
        return z + (prev - z) * mu_ref[...]

    r1 = lerp(r_ref, mur_ref, 0)
    k1 = lerp(k_ref, muk_ref, 1)
    v1 = lerp(v_ref, muv_ref, 2)
    ag = ag_ref[...]
    kk = k1 * kk_ref[...]
    kk = kk * lax.rsqrt(jnp.maximum(_seg_sum1(kk * kk, ones_bd), 1e-24))
    k2 = k1 * (1.0 + (ag - 1.0) * ka_ref[...])
    a_all = -kk
    b_all = kk * ag
    lw = lw_ref[...]

    rows = lax.broadcasted_iota(jnp.int32, (gl, gl), 0)
    cols = lax.broadcasted_iota(jnp.int32, (gl, gl), 1)
    same_head = (rows // c_len) == (cols // HEAD_DIM)
    ts = lax.broadcasted_iota(jnp.int32, (c_len, gl), 0)
    tj = lax.broadcasted_iota(jnp.int32, (c_len, gl), 1) % c_len
    tri_strict = ts > tj
    tri_incl = ts >= tj
    eye = jnp.where(ts == tj, 1.0, 0.0).astype(F32)
    cr = lax.broadcasted_iota(jnp.int32, (c_len, c_len), 0)
    cc = lax.broadcasted_iota(jnp.int32, (c_len, c_len), 1)
    cum_b = jnp.where(cr >= cc, 1.0, 0.0).astype(BF16)

    def bd(x):
        xb = x.astype(BF16)
        return jnp.where(same_head, jnp.concatenate([xb] * heads, axis=0), jnp.zeros((), BF16))

    def fold(x):
        out = x[0:c_len]
        for hh in range(1, heads):
            out = out + x[hh * c_len:(hh + 1) * c_len]
        return out

    nch = lt // c_len
    sls = [slice(c * c_len, (c + 1) * c_len) for c in range(nch)]

    def cumsum3(x):
        hi = x.astype(BF16)
        r = x - hi.astype(F32)
        mid = r.astype(BF16)
        lo = (r - mid.astype(F32)).astype(BF16)
        return _dot(cum_b, hi) + _dot(cum_b, mid) + _dot(cum_b, lo)

    wc, g_s, n_s, rp, y0, s_in = {}, {}, {}, {}, {}, {}
    state = [s_ref[...]]

    def chunk_parallel(chunks):
        lws = {c: lw[sls[c]] for c in chunks}
        cs = {c: cumsum3(lws[c]) for c in chunks}
        yield
        e_cs = {c: jnp.exp(cs[c]) for c in chunks}
        e_neg = {c: jnp.exp(-cs[c]) for c in chunks}
        for c in chunks:
            wc[c] = e_cs[c][c_len - 1:, :]
        at = {c: a_all[sls[c]] * jnp.exp(cs[c] - lws[c]) for c in chunks}
        rt = {c: r1[sls[c]] * e_cs[c] for c in chunks}
        bt = {c: b_all[sls[c]] * e_neg[c] for c in chunks}
        kt = {c: k2[sls[c]] * e_neg[c] for c in chunks}
        vc = {c: v1[sls[c]] for c in chunks}
        yield
        lhs = {c: jnp.concatenate([at[c], rt[c]], axis=0).astype(BF16) for c in chunks}
        g_b = {c: _dot_nt(lhs[c], bd(bt[c])) for c in chunks}
        yield
        g_k = {c: _dot_nt(lhs[c], bd(kt[c])) for c in chunks}
        yield
        n_ab = {c: jnp.where(tri_strict, g_b[c][:c_len], 0.0) for c in chunks}
        a_ak = {c: jnp.where(tri_strict, g_k[c][:c_len], 0.0) for c in chunks}
        a_rb = {c: jnp.where(tri_incl, g_b[c][c_len:], 0.0).astype(BF16) for c in chunks}
        a_rk = {c: jnp.where(tri_incl, g_k[c][c_len:], 0.0).astype(BF16) for c in chunks}

        p = {c: eye + jnp.where((ts // 2) == (tj // 2), n_ab[c], 0.0) for c in chunks}
        for half in (2, 4, 8, 16, 32):
            lower_left = (((ts // (2 * half)) == (tj // (2 * half)))
                          & ((ts % (2 * half)) >= half) & ((tj % (2 * half)) < half))
            n21_ta = {c: _dot(jnp.where(lower_left, n_ab[c], 0.0).astype(BF16), bd(p[c])) for c in chunks}
            yield
            p = {c: p[c] + _dot(p[c].astype(BF16), bd(n21_ta[c])) for c in chunks}
            yield

        z = {c: _dot(p[c].astype(BF16), jnp.concatenate([bd(a_ak[c]), bd(at[c])], axis=1))
             for c in chunks}
        yield
        ap = {c: z[c][:, gl:] for c in chunks}
        bd_v = {c: bd(vc[c]) for c in chunks}
        u0 = {c: _dot(z[c][:, :gl].astype(BF16), bd_v[c]) for c in chunks}
        yield
        w = {c: _dot(a_rb[c], jnp.concatenate([bd(ap[c]), bd(u0[c])], axis=1)) for c in chunks}
        yield
        for c in chunks:
            rp[c] = rt[c] + w[c][:, :gl]
            y0[c] = w[c][:, gl:] + _dot(a_rk[c], bd_v[c])
        yield
        bh = {c: (bt[c] * wc[c]).astype(BF16) for c in chunks}
        bk = {c: jnp.concatenate([bh[c], (kt[c] * wc[c]).astype(BF16)], axis=0) for c in chunks}
        for c in chunks:
            g_s[c] = jnp.where(same_head, _dot_tn(ap[c].astype(BF16), bh[c]), 0.0).astype(BF16)
        yield
        for c in chunks:
            n_s[c] = fold(jnp.where(
                same_head, _dot_tn(jnp.concatenate([u0[c], vc[c]], axis=0).astype(BF16), bk[c]), 0.0))
        yield

    def state_update(chunks):
        for c in chunks:
            s_in[c] = state[0].astype(BF16)
            state[0] = state[0] * wc[c] + _dot(s_in[c], g_s[c]) + n_s[c]
            yield

    def epilogue(chunks):
        rows = slice(chunks[0] * c_len, (chunks[-1] + 1) * c_len)
        y = jnp.concatenate([_dot_nt(rp[c].astype(BF16), bd(s_in[c])) + y0[c] for c in chunks], axis=0)
        yield
        inv_n = 1.0 / HEAD_DIM
        mean = _seg_sum(y, ones_bd) * inv_n
        yield
        yc = y - mean
        var = _seg_sum1(yc * yc, ones_bd) * inv_n
        yield
        yn = yc * lax.rsqrt(var + LNX_EPS) * lnw_ref[...] + lnb_ref[...]
        bonus = _seg_sum(r1[rows] * k2[rows] * rk_ref[...], ones_bd) * v1[rows]
        y_ref[rows, :] = ((yn + bonus) * gg_ref[rows, :]).astype(y_ref.dtype)
        yield

    def drain(gen):
        for _ in gen:
            pass

    def interleave(serial, other, every):
        for k, _ in enumerate(other, 1):
            if k % every == 0:
                next(serial, None)
        drain(serial)

    wave = max(nch // 2, 1)
    waves = [list(range(w0, min(w0 + wave, nch))) for w0 in range(0, nch, wave)]
    par_stages, epi_stages = 20, 4
    drain(chunk_parallel(waves[0]))
    for wi, chunks in enumerate(waves):
        others, n_other = [], 0
        if wi + 1 < len(waves):
            others.append(chunk_parallel(waves[wi + 1]))
            n_other += par_stages
        if wi >= 1:
            others.append(epilogue(waves[wi - 1]))
            n_other += epi_stages
        interleave(state_update(chunks), itertools.chain(*others), max(n_other // len(chunks), 1))
    drain(epilogue(waves[-1]))
    s_ref[...] = state[0]


def _wkv(proj, lw, ag, gg, mu_r, mu_k, mu_v, k_k, k_a, r_k, lnx_w, lnx_b, bsz, t_len):
    n = proj.shape[0]
    dr = lw.shape[1]
    gl = GROUP_LANES
    ng = dr // gl
    lt = min(WKV_ROWS, t_len)
    nt = t_len // lt
    row = lambda off: (lambda b, g, t: (b * nt + t, off + g))
    par = lambda b, g, t: (0, g)
    vec = lambda a: a.reshape(1, dr)
    return pl.pallas_call(
        _wkv_kernel,
        out_shape=jax.ShapeDtypeStruct((n, dr), BF16),
        grid=(bsz, ng, nt),
        in_specs=[pl.BlockSpec((lt, gl), row(0)),
                  pl.BlockSpec((lt, gl), row(ng)),
                  pl.BlockSpec((lt, gl), row(2 * ng)),
                  pl.BlockSpec((lt, gl), row(0)),
                  pl.BlockSpec((lt, gl), row(0)),
                  pl.BlockSpec((lt, gl), row(0))] + [pl.BlockSpec((1, gl), par)] * 8,
        out_specs=pl.BlockSpec((lt, gl), row(0)),
        scratch_shapes=[pltpu.VMEM((HEAD_DIM, gl), F32), pltpu.VMEM((8, gl), F32)],
        compiler_params=pltpu.CompilerParams(
            dimension_semantics=("arbitrary", "arbitrary", "arbitrary"),
            vmem_limit_bytes=VMEM_LIMIT),
        name="wkv",
    )(proj, proj, proj, lw, ag, gg, vec(mu_r), vec(mu_k), vec(mu_v), vec(k_k), vec(k_a),
      vec(r_k), vec(lnx_w), vec(lnx_b))


def _outproj_kernel(yr_ref, u_ref, gb_ref, gc_ref, cw_ref, cnw_ref, wo_ref, x_ref, mod_ref,
                    n2w_ref, wr_ref, br_ref, x1_ref, h2_ref, meta_ref, cnt_ref,
                    ccarry_ref, cnt_acc_ref):
    tm = x_ref.shape[0]
    dc = u_ref.shape[1]

    @pl.when(pl.program_id(1) == 0)
    def _():
        ccarry_ref[...] = jnp.zeros_like(ccarry_ref)

    @pl.when((pl.program_id(0) == 0) & (pl.program_id(1) == 0))
    def _():
        cnt_acc_ref[...] = jnp.zeros_like(cnt_acc_ref)

    pre = gc_ref[...] * u_ref[...]
    prev1 = _shift_rows(pre, ccarry_ref[0:1, :])
    prev2 = _shift_rows(prev1, ccarry_ref[1:2, :])
    ccarry_ref[0:1, :] = pre[tm - 1:, :]
    ccarry_ref[1:2, :] = pre[tm - 2:tm - 1, :]
    yc = gb_ref[...] * (prev2 * cw_ref[0:1, :] + prev1 * cw_ref[1:2, :] + pre * cw_ref[2:3, :])
    ones_bd = _seg_ones(GROUP_LANES, CONV_GROUP_DIM)
    ycs = []
    for s in range(dc // GROUP_LANES):
        ys = yc[:, s * GROUP_LANES:(s + 1) * GROUP_LANES]
        ms = _seg_sum1(ys * ys, ones_bd) * (1.0 / CONV_GROUP_DIM)
        ycs.append(ys * lax.rsqrt(ms + RMS_EPS))
    y_conv = (jnp.concatenate(ycs, axis=1) * cnw_ref[...]).astype(BF16)

    dr = yr_ref.shape[1]
    y_mix = _dot(yr_ref[...], wo_ref[0:dr, :]) + _dot(y_conv, wo_ref[dr:, :])
    x1 = x_ref[...] + mod_ref[0, 2:3, :] * y_mix
    x1_ref[...] = x1
    ms = jnp.mean(x1 * x1, axis=-1, keepdims=True)
    h2 = x1 * lax.rsqrt(ms + RMS_EPS) * n2w_ref[...]
    h2 = h2 * (1.0 + mod_ref[0, 4:5, :]) + mod_ref[0, 3:4, :]
    h2_ref[...] = _pack_bf16_pair(h2)

    logits = _dot_x3(h2, wr_ref[...]) + br_ref[...]
    lane = lax.broadcasted_iota(jnp.int32, logits.shape, 1)
    neg = jnp.float32(-jnp.inf)
    big = jnp.int32(1 << 20)
    glog = jnp.where(lane < N_ROUTE_GROUPS, logits, neg)
    gmax = jnp.max(glog, axis=-1, keepdims=True)
    grp = jnp.min(jnp.where(glog == gmax, lane, big), axis=-1, keepdims=True)
    p_grp = 1.0 / jnp.sum(jnp.exp(glog - gmax), axis=-1, keepdims=True)
    e_lane = lane - N_ROUTE_GROUPS
    in_grp = (e_lane >= grp * EXPERTS_PER_GROUP) & (e_lane < (grp + 1) * EXPERTS_PER_GROUP)
    elog = jnp.where(in_grp, logits, neg)
    m1 = jnp.max(elog, axis=-1, keepdims=True)
    i1 = jnp.min(jnp.where(elog == m1, lane, big), axis=-1, keepdims=True)
    elog2 = jnp.where(lane == i1, neg, elog)
    m2 = jnp.max(elog2, axis=-1, keepdims=True)
    i2 = jnp.min(jnp.where(elog2 == m2, lane, big), axis=-1, keepdims=True)
    t2 = jnp.exp(m2 - m1)
    w1 = p_grp / (1.0 + t2)
    w2 = p_grp * t2 / (1.0 + t2)
    e1 = i1 - N_ROUTE_GROUPS
    e2 = i2 - N_ROUTE_GROUPS

    oh1 = lane == e1
    oh2 = lane == e2
    oh = jnp.where(oh1 | oh2, 1.0, 0.0)
    tr = lax.broadcasted_iota(jnp.int32, (tm, tm), 0)
    tc = lax.broadcasted_iota(jnp.int32, (tm, tm), 1)
    lower = jnp.where(tr > tc, 1.0, 0.0).astype(BF16)
    before = _dot(lower, oh.astype(BF16)) + cnt_acc_ref[...]
    rank1 = jnp.sum(jnp.where(oh1, before, 0.0), axis=-1, keepdims=True)
    rank2 = jnp.sum(jnp.where(oh2, before, 0.0), axis=-1, keepdims=True)
    cnt = cnt_acc_ref[...] + jnp.sum(oh, axis=0, keepdims=True)
    cnt_acc_ref[...] = cnt
    cnt_ref[...] = jnp.broadcast_to(cnt, cnt_ref.shape)

    meta = jnp.where(lane == 0, e1.astype(F32), 0.0)
    meta = jnp.where(lane == 1, e2.astype(F32), meta)
    meta = jnp.where(lane == 2, w1, meta)
    meta = jnp.where(lane == 3, w2, meta)
    meta = jnp.where(lane == 4, rank1, meta)
    meta = jnp.where(lane == 5, rank2, meta)
    meta_ref[...] = meta


def _outproj(yr, proj, conv_w, conv_norm_w, w_out_bf, x2, mod3, norm2_w, wr, br, bsz, t_len):
    n, d = x2.shape
    dr = yr.shape[1]
    dc = conv_w.shape[1]
    tm = min(OUTPROJ_ROWS, t_len)
    nt = t_len // tm
    nseg = dr // dc
    row = lambda off: (lambda b, t: (b * nt + t, off))
    const = lambda b, t: (0, 0)
    return pl.pallas_call(
        _outproj_kernel,
        out_shape=(jax.ShapeDtypeStruct((n, d), F32), jax.ShapeDtypeStruct((n, d // 2), jnp.uint32),
                   jax.ShapeDtypeStruct((n, LANES), F32), jax.ShapeDtypeStruct((8, LANES), F32)),
        grid=(bsz, nt),
        in_specs=[pl.BlockSpec((tm, dr), row(0)),
                  pl.BlockSpec((tm, dc), row(3 * nseg)),
                  pl.BlockSpec((tm, dc), row(4 * nseg)),
                  pl.BlockSpec((tm, dc), row(5 * nseg)),
                  pl.BlockSpec((3, dc), const),
                  pl.BlockSpec((1, dc), const),
                  pl.BlockSpec((dr + dc, d), const),
                  pl.BlockSpec((tm, d), row(0)),
                  pl.BlockSpec((1, 6, d), lambda b, t: (b, 0, 0)),
                  pl.BlockSpec((1, d), const),
                  pl.BlockSpec((d, LANES), const),
                  pl.BlockSpec((1, LANES), const)],
        out_specs=(pl.BlockSpec((tm, d), row(0)), pl.BlockSpec((tm, d // 2), row(0)),
                   pl.BlockSpec((tm, LANES), row(0)), pl.BlockSpec((8, LANES), const)),
        scratch_shapes=[pltpu.VMEM((8, dc), F32), pltpu.VMEM((1, LANES), F32)],
        compiler_params=pltpu.CompilerParams(dimension_semantics=("arbitrary", "arbitrary"),
                                             vmem_limit_bytes=VMEM_LIMIT),
        name="outproj",
    )(yr, proj, proj, proj, conv_w, conv_norm_w.reshape(1, dc), w_out_bf, x2, mod3,
      norm2_w.reshape(1, d), wr, br)


def _dispatch_kernel(zoff_ref, dest_hbm, h2_hbm, xs_hbm, idx_smem, tile_ref, zero_ref, zblk_ref,
                     sem_idx, sem_tile, sem_rows, sem_zero, sem_tail):
    i = pl.program_id(0)
    nsteps = pl.num_programs(0)
    th = idx_smem.shape[1] // 2
    slot = i % 2

    def idx_copy(step, sl):
        return pltpu.make_async_copy(dest_hbm.at[step], idx_smem.at[sl], sem_idx.at[sl])

    def tile_copy(step, sl):
        rows = pl.ds(pl.multiple_of(step * th, th), th)
        return pltpu.make_async_copy(h2_hbm.at[rows], tile_ref.at[sl], sem_tile.at[sl])

    def rows_wait(sl):
        pltpu.make_async_copy(xs_hbm.at[pl.ds(0, 2 * th)], xs_hbm.at[pl.ds(0, 2 * th)],
                              sem_rows.at[sl]).wait()

    nblk = xs_hbm.shape[0] // ROUTE_BLOCK

    def tail_copy(b):
        start = pl.multiple_of(b * ROUTE_BLOCK, ROUTE_BLOCK)
        return pltpu.make_async_copy(zblk_ref, xs_hbm.at[pl.ds(start, ROUTE_BLOCK)], sem_tail)

    @pl.when(i == 0)
    def _():
        idx_copy(0, 0).start()
        tile_copy(0, 0).start()
        zero_ref[...] = jnp.zeros_like(zero_ref)

        def zero_expert(e, carry):
            off = zoff_ref[e]

            def zero_row(j, c):
                pltpu.make_async_copy(zero_ref, xs_hbm.at[pl.ds(off + j, 1)], sem_zero).start()
                return c

            return lax.fori_loop(0, zoff_ref[N_EXPERTS + e], zero_row, carry)

        lax.fori_loop(0, N_EXPERTS, zero_expert, 0)
        total = zoff_ref[2 * N_EXPERTS]
        bulk = pl.multiple_of((total // 8) * 8, 8)

        @pl.when(bulk > 0)
        def _():
            pltpu.make_async_copy(xs_hbm.at[pl.ds(0, bulk)], xs_hbm.at[pl.ds(0, bulk)], sem_zero).wait()

        def wait_row(j, c):
            pltpu.make_async_copy(zero_ref, xs_hbm.at[pl.ds(0, 1)], sem_zero).wait()
            return c

        lax.fori_loop(0, total - bulk, wait_row, 0)

        zblk_ref[...] = jnp.zeros_like(zblk_ref)

        def tail_start(b, c):
            tail_copy(b).start()
            return c

        lax.fori_loop(zoff_ref[2 * N_EXPERTS + 1], nblk, tail_start, 0)

    idx_copy(i, slot).wait()
    tile_copy(i, slot).wait()

    @pl.when(i > 0)
    def _():
        rows_wait(1 - slot)

    @pl.when(i + 1 < nsteps)
    def _():
        idx_copy(i + 1, 1 - slot).start()
        tile_copy(i + 1, 1 - slot).start()

    for p in range(2):
        @pl.when(slot == p)
        def _(p=p):
            for t in range(th):
                for s in range(2):
                    d = idx_smem[p, 2 * t + s]
                    pltpu.make_async_copy(tile_ref.at[p, pl.ds(t, 1)], xs_hbm.at[pl.ds(d, 1)],
                                          sem_rows.at[p]).start(priority=s)

    @pl.when(i == nsteps - 1)
    def _():
        rows_wait(slot)

        def tail_wait(b, c):
            tail_copy(b).wait()
            return c

        lax.fori_loop(zoff_ref[2 * N_EXPERTS + 1], nblk, tail_wait, 0)


def _dispatch(dest, zero_off, h2, n_rows):
    n, d = h2.shape
    th = min(DISPATCH_ROWS, n)
    return pl.pallas_call(
        _dispatch_kernel,
        out_shape=jax.ShapeDtypeStruct((n_rows, d), h2.dtype),
        grid_spec=pltpu.PrefetchScalarGridSpec(
            num_scalar_prefetch=1,
            grid=(n // th,),
            in_specs=[pl.BlockSpec(memory_space=pl.ANY), pl.BlockSpec(memory_space=pl.ANY)],
            out_specs=pl.BlockSpec(memory_space=pl.ANY),
            scratch_shapes=[pltpu.SMEM((2, 2 * th), jnp.int32),
                            pltpu.VMEM((2, th, d), h2.dtype),
                            pltpu.VMEM((1, d), h2.dtype),
                            pltpu.VMEM((ROUTE_BLOCK, d), h2.dtype),
                            pltpu.SemaphoreType.DMA((2,)), pltpu.SemaphoreType.DMA((2,)),
                            pltpu.SemaphoreType.DMA((2,)),
                            pltpu.SemaphoreType.DMA, pltpu.SemaphoreType.DMA]),
        compiler_params=pltpu.CompilerParams(dimension_semantics=("arbitrary",),
                                             vmem_limit_bytes=VMEM_LIMIT),
        name="dispatch",
    )(zero_off, dest.reshape(n // th, 2 * th), h2)


def _expert_kernel(sched_ref, na_ref, xs_ref, wg_hbm, wu_hbm, wd_hbm, ys_ref,
                   wg_buf, wu_buf, wd_buf, sem):
    i = pl.program_id(0)

    def weight_copies(e, slot):
        return (pltpu.make_async_copy(wg_hbm.at[e], wg_buf.at[slot], sem.at[slot, 0]),
                pltpu.make_async_copy(wu_hbm.at[e], wu_buf.at[slot], sem.at[slot, 1]),
                pltpu.make_async_copy(wd_hbm.at[e], wd_buf.at[slot], sem.at[slot, 2]))

    @pl.when(i < na_ref[0])
    def _():
        e = sched_ref[0, i]
        slot = sched_ref[1, i]

        @pl.when(i == 0)
        def _():
            for cp in weight_copies(e, slot):
                cp.start()

        @pl.when(sched_ref[2, i] == 1)
        def _():
            for cp in weight_copies(e, slot):
                cp.wait()
            nxt = sched_ref[3, i]

            @pl.when(nxt >= 0)
            def _():
                for cp in weight_copies(nxt, 1 - slot):
                    cp.start()

        x = _unpack_bf16_pair(xs_ref[...]).astype(BF16)
        g = _dot(x, wg_buf[slot].astype(BF16))
        u = _dot(x, wu_buf[slot].astype(BF16))
        hid = (g * _sigmoid(g) * u).astype(BF16)
        ys_ref[...] = _pack_bf16_pair(_dot(hid, wd_buf[slot].astype(BF16)))

    @pl.when(i >= na_ref[0])
    def _():
        ys_ref[...] = jnp.zeros_like(ys_ref)


def _experts(block_e, n_active, xs, exp_gate, exp_up, exp_down):
    n_rows, dp = xs.shape
    _, d, de = exp_gate.shape
    nblk = n_rows // ROUTE_BLOCK
    idx = jnp.arange(nblk, dtype=jnp.int32)
    valid = idx < n_active[0]
    first = valid & ((idx == 0) | (block_e != jnp.roll(block_e, 1)))
    slot = jnp.maximum(jnp.cumsum(first.astype(jnp.int32)) - 1, 0) % 2
    first_at = jnp.where(first, idx, nblk)
    next_first = jnp.concatenate([lax.cummin(first_at, reverse=True)[1:], jnp.full((1,), nblk, jnp.int32)])
    nxt = jnp.where(next_first < nblk, block_e[jnp.minimum(next_first, nblk - 1)], -1)
    sched = jnp.stack([block_e, slot, first.astype(jnp.int32), nxt]).astype(jnp.int32)

    blk_row = lambda i, sc, na: (jnp.minimum(i, jnp.maximum(na[0] - 1, 0)), 0)
    return pl.pallas_call(
        _expert_kernel,
        out_shape=jax.ShapeDtypeStruct((n_rows, dp), jnp.uint32),
        grid_spec=pltpu.PrefetchScalarGridSpec(
            num_scalar_prefetch=2,
            grid=(nblk,),
            in_specs=[pl.BlockSpec((ROUTE_BLOCK, dp), blk_row),
                      pl.BlockSpec(memory_space=pl.ANY),
                      pl.BlockSpec(memory_space=pl.ANY),
                      pl.BlockSpec(memory_space=pl.ANY)],
            out_specs=pl.BlockSpec((ROUTE_BLOCK, dp), lambda i, sc, na: (i, 0)),
            scratch_shapes=[pltpu.VMEM((2, d, de), exp_gate.dtype),
                            pltpu.VMEM((2, d, de), exp_up.dtype),
                            pltpu.VMEM((2, de, d), exp_down.dtype),
                            pltpu.SemaphoreType.DMA((2, 3))]),
        compiler_params=pltpu.CompilerParams(dimension_semantics=("arbitrary",),
                                             vmem_limit_bytes=VMEM_LIMIT),
        name="experts",
    )(sched, n_active, xs, exp_gate, exp_up, exp_down)


def _combine_kernel(dest_hbm, ys_hbm, x1_ref, meta_ref, mod_ref, fw_ref, o_ref,
                    idx_smem, buf_ref, sem_idx, sem_rows):
    i = pl.program_id(0)
    nsteps = pl.num_programs(0)
    tj = x1_ref.shape[0]
    slot = i % 2

    def idx_copy(step, sl):
        return pltpu.make_async_copy(dest_hbm.at[step], idx_smem.at[sl], sem_idx.at[sl])

    def gather(sl):
        for t in range(tj):
            for s in range(2):
                d = idx_smem[sl, 2 * t + s]
                pltpu.make_async_copy(ys_hbm.at[pl.ds(d, 1)], buf_ref.at[sl, s, pl.ds(t, 1)],
                                      sem_rows.at[sl]).start(priority=s)

    @pl.when(i == 0)
    def _():
        first = idx_copy(0, 0)
        first.start()
        first.wait()
        gather(0)

        @pl.when(nsteps > 1)
        def _():
            idx_copy(1, 1).start()

    for p in range(2):
        @pl.when((i + 1 < nsteps) & (slot == 1 - p))
        def _(p=p):
            idx_copy(i + 1, p).wait()

            @pl.when(i + 2 < nsteps)
            def _():
                idx_copy(i + 2, 1 - p).start()

            gather(p)

    pltpu.make_async_copy(buf_ref.at[slot], buf_ref.at[slot], sem_rows.at[slot]).wait()

    meta = meta_ref[...]
    y = (meta[:, 2:3] * _unpack_bf16_pair(buf_ref[slot, 0])
         + meta[:, 3:4] * _unpack_bf16_pair(buf_ref[slot, 1]))
    x2 = x1_ref[...] + mod_ref[0, 5:6, :] * y
    ms = jnp.mean(x2 * x2, axis=-1, keepdims=True)
    o_ref[...] = x2 * lax.rsqrt(ms + RMS_EPS) * fw_ref[...]


def _combine(dest, ys, x1, meta, mod3, final_w, t_len):
    n, d = x1.shape
    tj = min(COMBINE_ROWS, t_len)
    return pl.pallas_call(
        _combine_kernel,
        out_shape=jax.ShapeDtypeStruct((n, d), F32),
        grid=(n // tj,),
        in_specs=[pl.BlockSpec(memory_space=pl.ANY),
                  pl.BlockSpec(memory_space=pl.ANY),
                  pl.BlockSpec((tj, d), lambda i: (i, 0)),
                  pl.BlockSpec((tj, LANES), lambda i: (i, 0)),
                  pl.BlockSpec((1, 6, d), lambda i: (i * tj // t_len, 0, 0)),
                  pl.BlockSpec((1, d), lambda i: (0, 0))],
        out_specs=pl.BlockSpec((tj, d), lambda i: (i, 0)),
        scratch_shapes=[pltpu.SMEM((2, 2 * tj), jnp.int32), pltpu.VMEM((2, 2, tj, d // 2), jnp.uint32),
                        pltpu.SemaphoreType.DMA((2,)), pltpu.SemaphoreType.DMA((2,))],
        compiler_params=pltpu.CompilerParams(dimension_semantics=("arbitrary",),
                                             vmem_limit_bytes=VMEM_LIMIT),
        name="combine",
    )(dest.reshape(n // tj, 2 * tj), ys, x1, meta, mod3, final_w.reshape(1, d))


def _pad_cols(w, width):
    return jnp.pad(w, ((0, 0), (0, width - w.shape[1])))


def _pad_rows(w, height):
    return jnp.pad(w, ((0, height - w.shape[0]), (0, 0)))


def _layer(x2, c, bsz, t_len, w_ada, b_ada, norm1_w, w_in, mu_r, mu_k, mu_v, mu_w, mu_a, mu_g, w0,
           w_decay1, w_decay2, a0, w_aaa1, w_aaa2, w_gate1, w_gate2, k_k, k_a, r_k, lnx_w, lnx_b,
           conv_w, conv_norm_w, w_out, norm2_w, w_grp, b_grp, w_exp, b_exp, exp_gate, exp_up,
           exp_down, final_w):
    n, d = x2.shape
    dr = w0.shape[0]
    widths = [hi - lo for lo, hi in (LORA_DECAY, LORA_AAA, LORA_GATE)]
    assert all(w.shape[1] <= n_ for w, n_ in zip((w_decay1, w_aaa1, w_gate1), widths))
    assert dr % GROUP_LANES == 0 and conv_w.shape[1] == dr and w_in.shape[1] == 6 * dr
    assert t_len % WKV_CHUNK == 0 and w_grp.shape[1] == N_ROUTE_GROUPS and w_exp.shape[1] == N_EXPERTS

    mod3 = _ada(c, w_ada, b_ada).reshape(bsz, 6, d)
    h, proj = _inproj(x2, mod3, norm1_w, w_in.astype(BF16), t_len)

    w1p = jnp.concatenate([_pad_cols(w, n_) for w, n_ in zip((w_decay1, w_aaa1, w_gate1), widths)], axis=1)
    mup = jnp.concatenate([jnp.broadcast_to(mu[:, None], (d, n_)) for mu, n_ in zip((mu_w, mu_a, mu_g), widths)],
                          axis=1)
    wd2p, wa2p, wg2p = (_pad_rows(w, n_) for w, n_ in zip((w_decay2, w_aaa2, w_gate2), widths))
    lw, ag, gg = _lora(h, w1p, mup, wd2p, wa2p, wg2p, w0, a0, bsz, t_len)

    yr = _wkv(proj, lw, ag, gg, mu_r, mu_k, mu_v, k_k, k_a, r_k.reshape(dr), lnx_w, lnx_b, bsz, t_len)

    wr = _pad_cols(jnp.concatenate([w_grp, w_exp], axis=1), LANES)
    br = _pad_cols(jnp.concatenate([b_grp, b_exp]).reshape(1, -1), LANES)
    x1, h2, meta, cnt = _outproj(yr, proj, conv_w, conv_norm_w, w_out.astype(BF16), x2, mod3,
                                 norm2_w, wr, br, bsz, t_len)

    counts = cnt[0, :N_EXPERTS].astype(jnp.int32)
    padded = (counts + ROUTE_BLOCK - 1) // ROUTE_BLOCK * ROUTE_BLOCK
    pad_ends = jnp.cumsum(padded)
    pad_starts = pad_ends - padded
    top_e = meta[:, 0:2].astype(jnp.int32)
    rank = meta[:, 4:6].astype(jnp.int32)
    one_hot = top_e[:, :, None] == jnp.arange(N_EXPERTS, dtype=jnp.int32)
    dest = (jnp.sum(jnp.where(one_hot, pad_starts, 0), axis=-1) + rank).reshape(-1)
    nblk = (2 * n) // ROUTE_BLOCK + N_EXPERTS
    n_rows = nblk * ROUTE_BLOCK
    block_start = jnp.arange(nblk, dtype=jnp.int32) * ROUTE_BLOCK
    block_e = jnp.minimum(jnp.sum(pad_ends[None, :] <= block_start[:, None], axis=1),
                          N_EXPERTS - 1).astype(jnp.int32)
    n_active = (pad_ends[-1:] // ROUTE_BLOCK).astype(jnp.int32)

    pad_len = padded - counts
    zero_off = jnp.concatenate([pad_starts + counts, pad_len, jnp.sum(pad_len, keepdims=True),
                                n_active]).astype(jnp.int32)
    xs = _dispatch(dest, zero_off, h2, n_rows)
    ys = _experts(block_e, n_active, xs, exp_gate, exp_up, exp_down)
    return _combine(dest, ys, x1, meta, mod3, final_w, t_len)


def kernel(x, c, w_ada, b_ada, norm1_w, w_in, mu_r, mu_k, mu_v, mu_w, mu_a, mu_g, w0, w_decay1, w_decay2, a0, w_aaa1, w_aaa2, w_gate1, w_gate2, k_k, k_a, r_k, lnx_w, lnx_b, conv_w, conv_norm_w, w_out, norm2_w, w_grp, b_grp, w_exp, b_exp, exp_gate, exp_up, exp_down, final_w):
    bsz, t_len, d = x.shape
    assert w_ada.shape[0] == 1, "single-layer block"
    out = _layer(x.reshape(bsz * t_len, d), c, bsz, t_len, w_ada[0], b_ada[0], norm1_w[0], w_in[0],
                 mu_r[0], mu_k[0], mu_v[0], mu_w[0], mu_a[0], mu_g[0], w0[0], w_decay1[0], w_decay2[0],
                 a0[0], w_aaa1[0], w_aaa2[0], w_gate1[0], w_gate2[0], k_k[0], k_a[0], r_k[0], lnx_w[0],
                 lnx_b[0], conv_w[0], conv_norm_w[0], w_out[0], norm2_w[0], w_grp[0], b_grp[0],
                 w_exp[0], b_exp[0], exp_gate[0], exp_up[0], exp_down[0], final_w)
    return out.reshape(bsz, t_len, d)
```

```python
import itertools

import jax
import jax.numpy as jnp
from jax import lax
from jax.experimental import pallas as pl
from jax.experimental.pallas import tpu as pltpu

F32 = jnp.float32
BF16 = jnp.bfloat16

HEAD_DIM = 64
WKV_CHUNK = 64
GROUP_LANES = 256
N_ROUTE_GROUPS = 4
EXPERTS_PER_GROUP = 8
N_EXPERTS = N_ROUTE_GROUPS * EXPERTS_PER_GROUP
CONV_GROUP_DIM = 64
RMS_EPS = 1e-6
LNX_EPS = 64e-5
LANES = 128
LORA_DECAY, LORA_AAA, LORA_GATE = (0, 128), (128, 256), (256, 512)
LORA_PAD = LORA_GATE[1]
ADA_COLS = 1024
INPROJ_ROWS, INPROJ_COLS = 1024, 1536
LORA_ROWS = 1024
WKV_ROWS = 1024
OUTPROJ_ROWS = 512
DISPATCH_ROWS = 256
COMBINE_ROWS = 256
ROUTE_BLOCK = 256
NORM_ROWS = 32
VMEM_LIMIT = 56 * 1024 * 1024


def _dot(a, b, precision=None):
    return jnp.dot(a, b, preferred_element_type=F32, precision=precision)


def _dot_nt(a, b):
    return lax.dot_general(a, b, (((1,), (1,)), ((), ())), preferred_element_type=F32)


def _dot_tn(a, b):
    return lax.dot_general(a, b, (((0,), (0,)), ((), ())), preferred_element_type=F32)


def _split_bf16(x):
    hi = x.astype(BF16)
    return hi, (x - hi.astype(F32)).astype(BF16)


def _dot_x3(a, b):
    ah, al = _split_bf16(a)
    bh, bl = _split_bf16(b)
    return _dot(ah, bh) + _dot(al, bh) + _dot(ah, bl)


def _pack_bf16_pair(x):
    c = x.shape[1] // 2
    lo = pltpu.bitcast(x[:, :c].astype(BF16).astype(F32), jnp.uint32)
    hi = pltpu.bitcast(x[:, c:].astype(BF16).astype(F32), jnp.uint32)
    return (lo >> 16) | (hi & jnp.uint32(0xFFFF0000))


def _unpack_bf16_pair(w):
    lo = pltpu.bitcast(w << 16, F32)
    hi = pltpu.bitcast(w & jnp.uint32(0xFFFF0000), F32)
    return jnp.concatenate([lo, hi], axis=1)


def _sigmoid(x):
    return 1.0 / (1.0 + jnp.exp(-x))


def _shift_rows(x, carry_row):
    rolled = pltpu.roll(x, 1, axis=0)
    row = lax.broadcasted_iota(jnp.int32, x.shape, 0)
    return jnp.where(row == 0, carry_row, rolled)


def _seg_ones(n, seg):
    r = lax.broadcasted_iota(jnp.int32, (n, n), 0) // seg
    c = lax.broadcasted_iota(jnp.int32, (n, n), 1) // seg
    return jnp.where(r == c, 1.0, 0.0).astype(BF16)


def _seg_sum(x, ones_bd):
    hi = x.astype(BF16)
    lo = (x - hi.astype(F32)).astype(BF16)
    return _dot(hi, ones_bd) + _dot(lo, ones_bd)


def _seg_sum1(x, ones_bd):
    return _dot(x.astype(BF16), ones_bd)


def _ada_kernel(cb_ref, w_ref, b_ref, o_ref):
    nb = cb_ref.shape[0]
    tn = w_ref.shape[1]
    for b in range(nb):
        cv = cb_ref[b]
        s = cv * _sigmoid(cv)
        cols = [jnp.sum(s * w_ref[:, j * LANES:(j + 1) * LANES], axis=0, keepdims=True)
                for j in range(tn // LANES)]
        o_ref[b:b + 1, :] = jnp.concatenate(cols, axis=1) + b_ref[...]


def _ada(c, w_ada, b_ada):
    nb, d = c.shape
    n_out = w_ada.shape[1]
    tn = ADA_COLS
    cb = jnp.broadcast_to(c[:, :, None], (nb, d, LANES))
    return pl.pallas_call(
        _ada_kernel,
        out_shape=jax.ShapeDtypeStruct((nb, n_out), F32),
        grid=(n_out // tn,),
        in_specs=[pl.BlockSpec((nb, d, LANES), lambda j: (0, 0, 0)),
                  pl.BlockSpec((d, tn), lambda j: (0, j)),
                  pl.BlockSpec((1, tn), lambda j: (0, j))],
        out_specs=pl.BlockSpec((nb, tn), lambda j: (0, j)),
        compiler_params=pltpu.CompilerParams(dimension_semantics=("arbitrary",),
                                             vmem_limit_bytes=VMEM_LIMIT),
        name="ada",
    )(cb, w_ada, b_ada.reshape(1, n_out))


def _inproj_kernel(x_ref, mod_ref, nw_ref, w_ref, h_ref, p_ref):
    @pl.when(pl.program_id(1) == 0)
    def _():
        rows = min(NORM_ROWS, x_ref.shape[0])
        gain = nw_ref[...] * (1.0 + mod_ref[0, 1:2, :])
        shift = mod_ref[0, 0:1, :]

        def norm_rows(r, carry):
            sl = pl.ds(pl.multiple_of(r * rows, rows), rows)
            x = x_ref[sl, :]
            ms = jnp.mean(x * x, axis=-1, keepdims=True)
            h = x * lax.rsqrt(ms + RMS_EPS) * gain + shift
            h_ref[sl, :] = h.astype(BF16)
            return carry

        lax.fori_loop(0, x_ref.shape[0] // rows, norm_rows, 0, unroll=4)

    p_ref[...] = _dot(h_ref[...], w_ref[...])


def _inproj(x2, mod3, norm_w, w_in_bf, t_len):
    n, d = x2.shape
    n_in = w_in_bf.shape[1]
    tm = min(INPROJ_ROWS, t_len)
    tn = INPROJ_COLS
    return pl.pallas_call(
        _inproj_kernel,
        out_shape=(jax.ShapeDtypeStruct((n, d), BF16), jax.ShapeDtypeStruct((n, n_in), F32)),
        grid=(n // tm, n_in // tn),
        in_specs=[pl.BlockSpec((tm, d), lambda i, j: (i, 0)),
                  pl.BlockSpec((1, 6, d), lambda i, j: (i * tm // t_len, 0, 0)),
                  pl.BlockSpec((1, d), lambda i, j: (0, 0)),
                  pl.BlockSpec((d, tn), lambda i, j: (0, j))],
        out_specs=(pl.BlockSpec((tm, d), lambda i, j: (i, 0)),
                   pl.BlockSpec((tm, tn), lambda i, j: (i, j))),
        compiler_params=pltpu.CompilerParams(dimension_semantics=("arbitrary", "arbitrary"),
                                             vmem_limit_bytes=VMEM_LIMIT),
        name="inproj",
    )(x2, mod3, norm_w.reshape(1, d), w_in_bf)


def _lora_kernel(h_ref, w1_ref, mu_ref, wd2_ref, wa2_ref, wg2_ref, w0_ref, a0_ref,
                 lw_ref, ag_ref, gg_ref, carry_ref, w12_ref):
    @pl.when(pl.program_id(1) == 0)
    def _():
        carry_ref[...] = jnp.zeros_like(carry_ref)

    @pl.when((pl.program_id(0) == 0) & (pl.program_id(1) == 0))
    def _():
        w1 = w1_ref[...]
        w12_ref[:, :LORA_PAD] = w1.astype(BF16)
        w12_ref[:, LORA_PAD:] = (w1 * mu_ref[...]).astype(BF16)

    p12 = _dot(h_ref[...], w12_ref[...])
    p1 = p12[:, :LORA_PAD]
    p2 = p12[:, LORA_PAD:]
    prev = _shift_rows(p2, carry_ref[...])
    carry_ref[...] = p2[p2.shape[0] - 1:, :]
    lo = p1 - p2 + prev
    dec = _dot(jnp.tanh(lo[:, slice(*LORA_DECAY)]).astype(BF16), wd2_ref[...].astype(BF16))
    z = -(w0_ref[...] + dec)
    softplus = jnp.maximum(z, 0.0) + jnp.log(1.0 + jnp.exp(-jnp.abs(z)))
    lw_ref[...] = -jnp.exp(-softplus - 0.5)
    ag_ref[...] = _sigmoid(a0_ref[...] + _dot(lo[:, slice(*LORA_AAA)].astype(BF16), wa2_ref[...].astype(BF16)))
    gg_ref[...] = _dot(_sigmoid(lo[:, slice(*LORA_GATE)]).astype(BF16), wg2_ref[...].astype(BF16))


def _lora(h, w1p, mup, wd2p, wa2p, wg2p, w0, a0, bsz, t_len):
    n, d = h.shape
    dr = w0.shape[-1]
    tm = min(LORA_ROWS, t_len)
    nt = t_len // tm
    row = lambda b, t: (b * nt + t, 0)
    const = lambda b, t: (0, 0)
    out = jax.ShapeDtypeStruct((n, dr), F32)
    return pl.pallas_call(
        _lora_kernel,
        out_shape=(out, out, out),
        grid=(bsz, nt),
        in_specs=[pl.BlockSpec((tm, d), row),
                  pl.BlockSpec((d, LORA_PAD), const),
                  pl.BlockSpec((d, LORA_PAD), const),
                  pl.BlockSpec(wd2p.shape, const),
                  pl.BlockSpec(wa2p.shape, const),
                  pl.BlockSpec(wg2p.shape, const),
                  pl.BlockSpec((1, dr), const),
                  pl.BlockSpec((1, dr), const)],
        out_specs=(pl.BlockSpec((tm, dr), row),) * 3,
        scratch_shapes=[pltpu.VMEM((1, LORA_PAD), F32), pltpu.VMEM((d, 2 * LORA_PAD), BF16)],
        compiler_params=pltpu.CompilerParams(dimension_semantics=("arbitrary", "arbitrary"),
                                             vmem_limit_bytes=VMEM_LIMIT),
        name="lora",
    )(h, w1p, mup, wd2p, wa2p, wg2p, w0.reshape(1, dr), a0.reshape(1, dr))


def _wkv_kernel(r_ref, k_ref, v_ref, lw_ref, ag_ref, gg_ref, mur_ref, muk_ref, muv_ref,
                kk_ref, ka_ref, rk_ref, lnw_ref, lnb_ref, y_ref, s_ref, carry_ref):
    lt = r_ref.shape[0]
    gl = GROUP_LANES
    c_len = WKV_CHUNK
    heads = gl // HEAD_DIM

    @pl.when(pl.program_id(2) == 0)
    def _():
        s_ref[...] = jnp.zeros_like(s_ref)
        carry_ref[...] = jnp.zeros_like(carry_ref)

    ones_bd = _seg_ones(gl, HEAD_DIM)

    def lerp(ref, mu_ref, idx):
        z = ref[...]
        prev = _shift_rows(z, carry_ref[idx:idx + 1, :])
        carry_ref[idx:idx + 1, :] = z[lt - 1:, :]
        return z + (prev - z) * mu_ref[...]

    r1 = lerp(r_ref, mur_ref, 0)
    k1 = lerp(k_ref, muk_ref, 1)
    v1 = lerp(v_ref, muv_ref, 2)
    ag = ag_ref[...]
    kk = k1 * kk_ref[...]
    kk = kk * lax.rsqrt(jnp.maximum(_seg_sum1(kk * kk, ones_bd), 1e-24))
    k2 = k1 * (1.0 + (ag - 1.0) * ka_ref[...])
    a_all = -kk
    b_all = kk * ag
    lw = lw_ref[...]

    rows = lax.broadcasted_iota(jnp.int32, (gl, gl), 0)
    cols = lax.broadcasted_iota(jnp.int32, (gl, gl), 1)
    same_head = (rows // c_len) == (cols // HEAD_DIM)
    ts = lax.broadcasted_iota(jnp.int32, (c_len, gl), 0)
    tj = lax.broadcasted_iota(jnp.int32, (c_len, gl), 1) % c_len
    tri_strict = ts > tj
    tri_incl = ts >= tj
    eye = jnp.where(ts == tj, 1.0, 0.0).astype(F32)
    cr = lax.broadcasted_iota(jnp.int32, (c_len, c_len), 0)
    cc = lax.broadcasted_iota(jnp.int32, (c_len, c_len), 1)
    cum_b = jnp.where(cr >= cc, 1.0, 0.0).astype(BF16)

    def bd(x):
        xb = x.astype(BF16)
        return jnp.where(same_head, jnp.concatenate([xb] * heads, axis=0), jnp.zeros((), BF16))

    def fold(x):
        out = x[0:c_len]
        for hh in range(1, heads):
            out = out + x[hh * c_len:(hh + 1) * c_len]
        return out

    nch = lt // c_len
    sls = [slice(c * c_len, (c + 1) * c_len) for c in range(nch)]

    def cumsum3(x):
        hi = x.astype(BF16)
        r = x - hi.astype(F32)
        mid = r.astype(BF16)
        lo = (r - mid.astype(F32)).astype(BF16)
        return _dot(cum_b, hi) + _dot(cum_b, mid) + _dot(cum_b, lo)

    wc, g_s, n_s, rp, y0, s_in = {}, {}, {}, {}, {}, {}
    state = [s_ref[...]]

    def chunk_parallel(chunks):
        lws = {c: lw[sls[c]] for c in chunks}
        cs = {c: cumsum3(lws[c]) for c in chunks}
        yield
        e_cs = {c: jnp.exp(cs[c]) for c in chunks}
        e_neg = {c: jnp.exp(-cs[c]) for c in chunks}
        for c in chunks:
            wc[c] = e_cs[c][c_len - 1:, :]
        at = {c: a_all[sls[c]] * jnp.exp(cs[c] - lws[c]) for c in chunks}
        rt = {c: r1[sls[c]] * e_cs[c] for c in chunks}
        bt = {c: b_all[sls[c]] * e_neg[c] for c in chunks}
        kt = {c: k2[sls[c]] * e_neg[c] for c in chunks}
        vc = {c: v1[sls[c]] for c in chunks}
        yield
        lhs = {c: jnp.concatenate([at[c], rt[c]], axis=0).astype(BF16) for c in chunks}
        g_b = {c: _dot_nt(lhs[c], bd(bt[c])) for c in chunks}
        yield
        g_k = {c: _dot_nt(lhs[c], bd(kt[c])) for c in chunks}
        yield
        n_ab = {c: jnp.where(tri_strict, g_b[c][:c_len], 0.0) for c in chunks}
        a_ak = {c: jnp.where(tri_strict, g_k[c][:c_len], 0.0) for c in chunks}
        a_rb = {c: jnp.where(tri_incl, g_b[c][c_len:], 0.0).astype(BF16) for c in chunks}
        a_rk = {c: jnp.where(tri_incl, g_k[c][c_len:], 0.0).astype(BF16) for c in chunks}

        p = {c: eye + jnp.where((ts // 2) == (tj // 2), n_ab[c], 0.0) for c in chunks}
        for half in (2, 4, 8, 16, 32):
            lower_left = (((ts // (2 * half)) == (tj // (2 * half)))
                          & ((ts % (2 * half)) >= half) & ((tj % (2 * half)) < half))
            n21_ta = {c: _dot(jnp.where(lower_left, n_ab[c], 0.0).astype(BF16), bd(p[c])) for c in chunks}
            yield
            p = {c: p[c] + _dot(p[c].astype(BF16), bd(n21_ta[c])) for c in chunks}
            yield

        z = {c: _dot(p[c].astype(BF16), jnp.concatenate([bd(a_ak[c]), bd(at[c])], axis=1))
             for c in chunks}
        yield
        ap = {c: z[c][:, gl:] for c in chunks}
        bd_v = {c: bd(vc[c]) for c in chunks}
        u0 = {c: _dot(z[c][:, :gl].astype(BF16), bd_v[c]) for c in chunks}
        yield
        w = {c: _dot(a_rb[c], jnp.concatenate([bd(ap[c]), bd(u0[c])], axis=1)) for c in chunks}
        yield
        for c in chunks:
            rp[c] = rt[c] + w[c][:, :gl]
            y0[c] = w[c][:, gl:] + _dot(a_rk[c], bd_v[c])
        yield
        bh = {c: (bt[c] * wc[c]).astype(BF16) for c in chunks}
        bk = {c: jnp.concatenate([bh[c], (kt[c] * wc[c]).astype(BF16)], axis=0) for c in chunks}
        for c in chunks:
            g_s[c] = jnp.where(same_head, _dot_tn(ap[c].astype(BF16), bh[c]), 0.0).astype(BF16)
        yield
        for c in chunks:
            n_s[c] = fold(jnp.where(
                same_head, _dot_tn(jnp.concatenate([u0[c], vc[c]], axis=0).astype(BF16), bk[c]), 0.0))
        yield

    def state_update(chunks):
        for c in chunks:
            s_in[c] = state[0].astype(BF16)
            state[0] = state[0] * wc[c] + _dot(s_in[c], g_s[c]) + n_s[c]
            yield

    def epilogue(chunks):
        rows = slice(chunks[0] * c_len, (chunks[-1] + 1) * c_len)
        y = jnp.concatenate([_dot_nt(rp[c].astype(BF16), bd(s_in[c])) + y0[c] for c in chunks], axis=0)
        yield
        inv_n = 1.0 / HEAD_DIM
        mean = _seg_sum(y, ones_bd) * inv_n
        yield
        yc = y - mean
        var = _seg_sum1(yc * yc, ones_bd) * inv_n
        yield
        yn = yc * lax.rsqrt(var + LNX_EPS) * lnw_ref[...] + lnb_ref[...]
        bonus = _seg_sum(r1[rows] * k2[rows] * rk_ref[...], ones_bd) * v1[rows]
        y_ref[rows, :] = ((yn + bonus) * gg_ref[rows, :]).astype(y_ref.dtype)
        yield

    def drain(gen):
        for _ in gen:
            pass

    def interleave(serial, other, every):
        for k, _ in enumerate(other, 1):
            if k % every == 0:
                next(serial, None)
        drain(serial)

    wave = max(nch // 2, 1)
    waves = [list(range(w0, min(w0 + wave, nch))) for w0 in range(0, nch, wave)]
    par_stages, epi_stages = 20, 4
    drain(chunk_parallel(waves[0]))
    for wi, chunks in enumerate(waves):
        others, n_other = [], 0
        if wi + 1 < len(waves):
            others.append(chunk_parallel(waves[wi + 1]))
            n_other += par_stages
        if wi >= 1:
            others.append(epilogue(waves[wi - 1]))
            n_other += epi_stages
        interleave(state_update(chunks), itertools.chain(*others), max(n_other // len(chunks), 1))
    drain(epilogue(waves[-1]))
    s_ref[...] = state[0]


def _wkv(proj, lw, ag, gg, mu_r, mu_k, mu_v, k_k, k_a, r_k, lnx_w, lnx_b, bsz, t_len):
    n = proj.shape[0]
    dr = lw.shape[1]
    gl = GROUP_LANES
    ng = dr // gl
    lt = min(WKV_ROWS, t_len)
    nt = t_len // lt
    row = lambda off: (lambda b, g, t: (b * nt + t, off + g))
    par = lambda b, g, t: (0, g)
    vec = lambda a: a.reshape(1, dr)
    return pl.pallas_call(
        _wkv_kernel,
        out_shape=jax.ShapeDtypeStruct((n, dr), BF16),
        grid=(bsz, ng, nt),
        in_specs=[pl.BlockSpec((lt, gl), row(0)),
                  pl.BlockSpec((lt, gl), row(ng)),
                  pl.BlockSpec((lt, gl), row(2 * ng)),
                  pl.BlockSpec((lt, gl), row(0)),
                  pl.BlockSpec((lt, gl), row(0)),
                  pl.BlockSpec((lt, gl), row(0))] + [pl.BlockSpec((1, gl), par)] * 8,
        out_specs=pl.BlockSpec((lt, gl), row(0)),
        scratch_shapes=[pltpu.VMEM((HEAD_DIM, gl), F32), pltpu.VMEM((8, gl), F32)],
        compiler_params=pltpu.CompilerParams(
            dimension_semantics=("arbitrary", "arbitrary", "arbitrary"),
            vmem_limit_bytes=VMEM_LIMIT),
        name="wkv",
    )(proj, proj, proj, lw, ag, gg, vec(mu_r), vec(mu_k), vec(mu_v), vec(k_k), vec(k_a),
      vec(r_k), vec(lnx_w), vec(lnx_b))


def _outproj_kernel(yr_ref, u_ref, gb_ref, gc_ref, cw_ref, cnw_ref, wo_ref, x_ref, mod_ref,
                    n2w_ref, wr_ref, br_ref, x1_ref, h2_ref, meta_ref, cnt_ref,
                    ccarry_ref, cnt_acc_ref):
    tm = x_ref.shape[0]
    dc = u_ref.shape[1]

    @pl.when(pl.program_id(1) == 0)
    def _():
        ccarry_ref[...] = jnp.zeros_like(ccarry_ref)

    @pl.when((pl.program_id(0) == 0) & (pl.program_id(1) == 0))
    def _():
        cnt_acc_ref[...] = jnp.zeros_like(cnt_acc_ref)

    pre = gc_ref[...] * u_ref[...]
    prev1 = _shift_rows(pre, ccarry_ref[0:1, :])
    prev2 = _shift_rows(prev1, ccarry_ref[1:2, :])
    ccarry_ref[0:1, :] = pre[tm - 1:, :]
    ccarry_ref[1:2, :] = pre[tm - 2:tm - 1, :]
    yc = gb_ref[...] * (prev2 * cw_ref[0:1, :] + prev1 * cw_ref[1:2, :] + pre * cw_ref[2:3, :])
    ones_bd = _seg_ones(GROUP_LANES, CONV_GROUP_DIM)
    ycs = []
    for s in range(dc // GROUP_LANES):
        ys = yc[:, s * GROUP_LANES:(s + 1) * GROUP_LANES]
        ms = _seg_sum1(ys * ys, ones_bd) * (1.0 / CONV_GROUP_DIM)
        ycs.append(ys * lax.rsqrt(ms + RMS_EPS))
    y_conv = (jnp.concatenate(ycs, axis=1) * cnw_ref[...]).astype(BF16)

    dr = yr_ref.shape[1]
    y_mix = _dot(yr_ref[...], wo_ref[0:dr, :]) + _dot(y_conv, wo_ref[dr:, :])
    x1 = x_ref[...] + mod_ref[0, 2:3, :] * y_mix
    x1_ref[...] = x1
    ms = jnp.mean(x1 * x1, axis=-1, keepdims=True)
    h2 = x1 * lax.rsqrt(ms + RMS_EPS) * n2w_ref[...]
    h2 = h2 * (1.0 + mod_ref[0, 4:5, :]) + mod_ref[0, 3:4, :]
    h2_ref[...] = _pack_bf16_pair(h2)

    logits = _dot_x3(h2, wr_ref[...]) + br_ref[...]
    lane = lax.broadcasted_iota(jnp.int32, logits.shape, 1)
    neg = jnp.float32(-jnp.inf)
    big = jnp.int32(1 << 20)
    glog = jnp.where(lane < N_ROUTE_GROUPS, logits, neg)
    gmax = jnp.max(glog, axis=-1, keepdims=True)
    grp = jnp.min(jnp.where(glog == gmax, lane, big), axis=-1, keepdims=True)
    p_grp = 1.0 / jnp.sum(jnp.exp(glog - gmax), axis=-1, keepdims=True)
    e_lane = lane - N_ROUTE_GROUPS
    in_grp = (e_lane >= grp * EXPERTS_PER_GROUP) & (e_lane < (grp + 1) * EXPERTS_PER_GROUP)
    elog = jnp.where(in_grp, logits, neg)
    m1 = jnp.max(elog, axis=-1, keepdims=True)
    i1 = jnp.min(jnp.where(elog == m1, lane, big), axis=-1, keepdims=True)
    elog2 = jnp.where(lane == i1, neg, elog)
    m2 = jnp.max(elog2, axis=-1, keepdims=True)
    i2 = jnp.min(jnp.where(elog2 == m2, lane, big), axis=-1, keepdims=True)
    t2 = jnp.exp(m2 - m1)
    w1 = p_grp / (1.0 + t2)
    w2 = p_grp * t2 / (1.0 + t2)
    e1 = i1 - N_ROUTE_GROUPS
    e2 = i2 - N_ROUTE_GROUPS

    oh1 = lane == e1
    oh2 = lane == e2
    oh = jnp.where(oh1 | oh2, 1.0, 0.0)
    tr = lax.broadcasted_iota(jnp.int32, (tm, tm), 0)
    tc = lax.broadcasted_iota(jnp.int32, (tm, tm), 1)
    lower = jnp.where(tr > tc, 1.0, 0.0).astype(BF16)
    before = _dot(lower, oh.astype(BF16)) + cnt_acc_ref[...]
    rank1 = jnp.sum(jnp.where(oh1, before, 0.0), axis=-1, keepdims=True)
    rank2 = jnp.sum(jnp.where(oh2, before, 0.0), axis=-1, keepdims=True)
    cnt = cnt_acc_ref[...] + jnp.sum(oh, axis=0, keepdims=True)
    cnt_acc_ref[...] = cnt
    cnt_ref[...] = jnp.broadcast_to(cnt, cnt_ref.shape)

    meta = jnp.where(lane == 0, e1.astype(F32), 0.0)
    meta = jnp.where(lane == 1, e2.astype(F32), meta)
    meta = jnp.where(lane == 2, w1, meta)
    meta = jnp.where(lane == 3, w2, meta)
    meta = jnp.where(lane == 4, rank1, meta)
    meta = jnp.where(lane == 5, rank2, meta)
    meta_ref[...] = meta


def _outproj(yr, proj, conv_w, conv_norm_w, w_out_bf, x2, mod3, norm2_w, wr, br, bsz, t_len):
    n, d = x2.shape
    dr = yr.shape[1]
    dc = conv_w.shape[1]
    tm = min(OUTPROJ_ROWS, t_len)
    nt = t_len // tm
    nseg = dr // dc
    row = lambda off: (lambda b, t: (b * nt + t, off))
    const = lambda b, t: (0, 0)
    return pl.pallas_call(
        _outproj_kernel,
        out_shape=(jax.ShapeDtypeStruct((n, d), F32), jax.ShapeDtypeStruct((n, d // 2), jnp.uint32),
                   jax.ShapeDtypeStruct((n, LANES), F32), jax.ShapeDtypeStruct((8, LANES), F32)),
        grid=(bsz, nt),
        in_specs=[pl.BlockSpec((tm, dr), row(0)),
                  pl.BlockSpec((tm, dc), row(3 * nseg)),
                  pl.BlockSpec((tm, dc), row(4 * nseg)),
                  pl.BlockSpec((tm, dc), row(5 * nseg)),
                  pl.BlockSpec((3, dc), const),
                  pl.BlockSpec((1, dc), const),
                  pl.BlockSpec((dr + dc, d), const),
                  pl.BlockSpec((tm, d), row(0)),
                  pl.BlockSpec((1, 6, d), lambda b, t: (b, 0, 0)),
                  pl.BlockSpec((1, d), const),
                  pl.BlockSpec((d, LANES), const),
                  pl.BlockSpec((1, LANES), const)],
        out_specs=(pl.BlockSpec((tm, d), row(0)), pl.BlockSpec((tm, d // 2), row(0)),
                   pl.BlockSpec((tm, LANES), row(0)), pl.BlockSpec((8, LANES), const)),
        scratch_shapes=[pltpu.VMEM((8, dc), F32), pltpu.VMEM((1, LANES), F32)],
        compiler_params=pltpu.CompilerParams(dimension_semantics=("arbitrary", "arbitrary"),
                                             vmem_limit_bytes=VMEM_LIMIT),
        name="outproj",
    )(yr, proj, proj, proj, conv_w, conv_norm_w.reshape(1, dc), w_out_bf, x2, mod3,
      norm2_w.reshape(1, d), wr, br)


def _dispatch_kernel(zoff_ref, dest_hbm, h2_hbm, xs_hbm, idx_smem, tile_ref, zero_ref, zblk_ref,
                     sem_idx, sem_tile, sem_rows, sem_zero, sem_tail):
    i = pl.program_id(0)
    nsteps = pl.num_programs(0)
    th = idx_smem.shape[1] // 2
    slot = i % 2

    def idx_copy(step, sl):
        return pltpu.make_async_copy(dest_hbm.at[step], idx_smem.at[sl], sem_idx.at[sl])

    def tile_copy(step, sl):
        rows = pl.ds(pl.multiple_of(step * th, th), th)
        return pltpu.make_async_copy(h2_hbm.at[rows], tile_ref.at[sl], sem_tile.at[sl])

    def rows_wait(sl):
        pltpu.make_async_copy(xs_hbm.at[pl.ds(0, 2 * th)], xs_hbm.at[pl.ds(0, 2 * th)],
                              sem_rows.at[sl]).wait()

    nblk = xs_hbm.shape[0] // ROUTE_BLOCK

    def tail_copy(b):
        start = pl.multiple_of(b * ROUTE_BLOCK, ROUTE_BLOCK)
        return pltpu.make_async_copy(zblk_ref, xs_hbm.at[pl.ds(start, ROUTE_BLOCK)], sem_tail)

    @pl.when(i == 0)
    def _():
        idx_copy(0, 0).start()
        tile_copy(0, 0).start()
        zero_ref[...] = jnp.zeros_like(zero_ref)

        def zero_expert(e, carry):
            off = zoff_ref[e]

            def zero_row(j, c):
                pltpu.make_async_copy(zero_ref, xs_hbm.at[pl.ds(off + j, 1)], sem_zero).start()
                return c

            return lax.fori_loop(0, zoff_ref[N_EXPERTS + e], zero_row, carry)

        lax.fori_loop(0, N_EXPERTS, zero_expert, 0)
        total = zoff_ref[2 * N_EXPERTS]
        bulk = pl.multiple_of((total // 8) * 8, 8)

        @pl.when(bulk > 0)
        def _():
            pltpu.make_async_copy(xs_hbm.at[pl.ds(0, bulk)], xs_hbm.at[pl.ds(0, bulk)], sem_zero).wait()

        def wait_row(j, c):
            pltpu.make_async_copy(zero_ref, xs_hbm.at[pl.ds(0, 1)], sem_zero).wait()
            return c

        lax.fori_loop(0, total - bulk, wait_row, 0)

        zblk_ref[...] = jnp.zeros_like(zblk_ref)

        def tail_start(b, c):
            tail_copy(b).start()
            return c

        lax.fori_loop(zoff_ref[2 * N_EXPERTS + 1], nblk, tail_start, 0)

    idx_copy(i, slot).wait()
    tile_copy(i, slot).wait()

    @pl.when(i > 0)
    def _():
        rows_wait(1 - slot)

    @pl.when(i + 1 < nsteps)
    def _():
        idx_copy(i + 1, 1 - slot).start()
        tile_copy(i + 1, 1 - slot).start()

    for p in range(2):
        @pl.when(slot == p)
        def _(p=p):
            for t in range(th):
                for s in range(2):
                    d = idx_smem[p, 2 * t + s]
                    pltpu.make_async_copy(tile_ref.at[p, pl.ds(t, 1)], xs_hbm.at[pl.ds(d, 1)],
                                          sem_rows.at[p]).start(priority=s)

    @pl.when(i == nsteps - 1)
    def _():
        rows_wait(slot)

        def tail_wait(b, c):
            tail_copy(b).wait()
            return c

        lax.fori_loop(zoff_ref[2 * N_EXPERTS + 1], nblk, tail_wait, 0)


def _dispatch(dest, zero_off, h2, n_rows):
    n, d = h2.shape
    th = min(DISPATCH_ROWS, n)
    return pl.pallas_call(
        _dispatch_kernel,
        out_shape=jax.ShapeDtypeStruct((n_rows, d), h2.dtype),
        grid_spec=pltpu.PrefetchScalarGridSpec(
            num_scalar_prefetch=1,
            grid=(n // th,),
            in_specs=[pl.BlockSpec(memory_space=pl.ANY), pl.BlockSpec(memory_space=pl.ANY)],
            out_specs=pl.BlockSpec(memory_space=pl.ANY),
            scratch_shapes=[pltpu.SMEM((2, 2 * th), jnp.int32),
                            pltpu.VMEM((2, th, d), h2.dtype),
                            pltpu.VMEM((1, d), h2.dtype),
                            pltpu.VMEM((ROUTE_BLOCK, d), h2.dtype),
                            pltpu.SemaphoreType.DMA((2,)), pltpu.SemaphoreType.DMA((2,)),
                            pltpu.SemaphoreType.DMA((2,)),
                            pltpu.SemaphoreType.DMA, pltpu.SemaphoreType.DMA]),
        compiler_params=pltpu.CompilerParams(dimension_semantics=("arbitrary",),
                                             vmem_limit_bytes=VMEM_LIMIT),
        name="dispatch",
    )(zero_off, dest.reshape(n // th, 2 * th), h2)


def _expert_kernel(sched_ref, na_ref, xs_ref, wg_hbm, wu_hbm, wd_hbm, ys_ref,
                   wg_buf, wu_buf, wd_buf, sem):
    i = pl.program_id(0)

    def weight_copies(e, slot):
        return (pltpu.make_async_copy(wg_hbm.at[e], wg_buf.at[slot], sem.at[slot, 0]),
                pltpu.make_async_copy(wu_hbm.at[e], wu_buf.at[slot], sem.at[slot, 1]),
                pltpu.make_async_copy(wd_hbm.at[e], wd_buf.at[slot], sem.at[slot, 2]))

    @pl.when(i < na_ref[0])
    def _():
        e = sched_ref[0, i]
        slot = sched_ref[1, i]

        @pl.when(i == 0)
        def _():
            for cp in weight_copies(e, slot):
                cp.start()

        @pl.when(sched_ref[2, i] == 1)
        def _():
            for cp in weight_copies(e, slot):
                cp.wait()
            nxt = sched_ref[3, i]

            @pl.when(nxt >= 0)
            def _():
                for cp in weight_copies(nxt, 1 - slot):
                    cp.start()

        x = _unpack_bf16_pair(xs_ref[...]).astype(BF16)
        g = _dot(x, wg_buf[slot].astype(BF16))
        u = _dot(x, wu_buf[slot].astype(BF16))
        hid = (g * _sigmoid(g) * u).astype(BF16)
        ys_ref[...] = _pack_bf16_pair(_dot(hid, wd_buf[slot].astype(BF16)))

    @pl.when(i >= na_ref[0])
    def _():
        ys_ref[...] = jnp.zeros_like(ys_ref)


def _experts(block_e, n_active, xs, exp_gate, exp_up, exp_down):
    n_rows, dp = xs.shape
    _, d, de = exp_gate.shape
    nblk = n_rows // ROUTE_BLOCK
    idx = jnp.arange(nblk, dtype=jnp.int32)
    valid = idx < n_active[0]
    first = valid & ((idx == 0) | (block_e != jnp.roll(block_e, 1)))
    slot = jnp.maximum(jnp.cumsum(first.astype(jnp.int32)) - 1, 0) % 2
    first_at = jnp.where(first, idx, nblk)
    next_first = jnp.concatenate([lax.cummin(first_at, reverse=True)[1:], jnp.full((1,), nblk, jnp.int32)])
    nxt = jnp.where(next_first < nblk, block_e[jnp.minimum(next_first, nblk - 1)], -1)
    sched = jnp.stack([block_e, slot, first.astype(jnp.int32), nxt]).astype(jnp.int32)

    blk_row = lambda i, sc, na: (jnp.minimum(i, jnp.maximum(na[0] - 1, 0)), 0)
    return pl.pallas_call(
        _expert_kernel,
        out_shape=jax.ShapeDtypeStruct((n_rows, dp), jnp.uint32),
        grid_spec=pltpu.PrefetchScalarGridSpec(
            num_scalar_prefetch=2,
            grid=(nblk,),
            in_specs=[pl.BlockSpec((ROUTE_BLOCK, dp), blk_row),
                      pl.BlockSpec(memory_space=pl.ANY),
                      pl.BlockSpec(memory_space=pl.ANY),
                      pl.BlockSpec(memory_space=pl.ANY)],
            out_specs=pl.BlockSpec((ROUTE_BLOCK, dp), lambda i, sc, na: (i, 0)),
            scratch_shapes=[pltpu.VMEM((2, d, de), exp_gate.dtype),
                            pltpu.VMEM((2, d, de), exp_up.dtype),
                            pltpu.VMEM((2, de, d), exp_down.dtype),
                            pltpu.SemaphoreType.DMA((2, 3))]),
        compiler_params=pltpu.CompilerParams(dimension_semantics=("arbitrary",),
                                             vmem_limit_bytes=VMEM_LIMIT),
        name="experts",
    )(sched, n_active, xs, exp_gate, exp_up, exp_down)


def _combine_kernel(dest_hbm, ys_hbm, x1_ref, meta_ref, mod_ref, fw_ref, o_ref,
                    idx_smem, buf_ref, sem_idx, sem_rows):
    i = pl.program_id(0)
    nsteps = pl.num_programs(0)
    tj = x1_ref.shape[0]
    slot = i % 2

    def idx_copy(step, sl):
        return pltpu.make_async_copy(dest_hbm.at[step], idx_smem.at[sl], sem_idx.at[sl])

    def gather(sl):
        for t in range(tj):
            for s in range(2):
                d = idx_smem[sl, 2 * t + s]
                pltpu.make_async_copy(ys_hbm.at[pl.ds(d, 1)], buf_ref.at[sl, s, pl.ds(t, 1)],
                                      sem_rows.at[sl]).start(priority=s)

    @pl.when(i == 0)
    def _():
        first = idx_copy(0, 0)
        first.start()
        first.wait()
        gather(0)

        @pl.when(nsteps > 1)
        def _():
            idx_copy(1, 1).start()

    for p in range(2):
        @pl.when((i + 1 < nsteps) & (slot == 1 - p))
        def _(p=p):
            idx_copy(i + 1, p).wait()

            @pl.when(i + 2 < nsteps)
            def _():
                idx_copy(i + 2, 1 - p).start()

            gather(p)

    pltpu.make_async_copy(buf_ref.at[slot], buf_ref.at[slot], sem_rows.at[slot]).wait()

    meta = meta_ref[...]
    y = (meta[:, 2:3] * _unpack_bf16_pair(buf_ref[slot, 0])
         + meta[:, 3:4] * _unpack_bf16_pair(buf_ref[slot, 1]))
    x2 = x1_ref[...] + mod_ref[0, 5:6, :] * y
    ms = jnp.mean(x2 * x2, axis=-1, keepdims=True)
    o_ref[...] = x2 * lax.rsqrt(ms + RMS_EPS) * fw_ref[...]


def _combine(dest, ys, x1, meta, mod3, final_w, t_len):
    n, d = x1.shape
    tj = min(COMBINE_ROWS, t_len)
    return pl.pallas_call(
        _combine_kernel,
        out_shape=jax.ShapeDtypeStruct((n, d), F32),
        grid=(n // tj,),
        in_specs=[pl.BlockSpec(memory_space=pl.ANY),
                  pl.BlockSpec(memory_space=pl.ANY),
                  pl.BlockSpec((tj, d), lambda i: (i, 0)),
                  pl.BlockSpec((tj, LANES), lambda i: (i, 0)),
                  pl.BlockSpec((1, 6, d), lambda i: (i * tj // t_len, 0, 0)),
                  pl.BlockSpec((1, d), lambda i: (0, 0))],
        out_specs=pl.BlockSpec((tj, d), lambda i: (i, 0)),
        scratch_shapes=[pltpu.SMEM((2, 2 * tj), jnp.int32), pltpu.VMEM((2, 2, tj, d // 2), jnp.uint32),
                        pltpu.SemaphoreType.DMA((2,)), pltpu.SemaphoreType.DMA((2,))],
        compiler_params=pltpu.CompilerParams(dimension_semantics=("arbitrary",),
                                             vmem_limit_bytes=VMEM_LIMIT),
        name="combine",
    )(dest.reshape(n // tj, 2 * tj), ys, x1, meta, mod3, final_w.reshape(1, d))


def _pad_cols(w, width):
    return jnp.pad(w, ((0, 0), (0, width - w.shape[1])))


def _pad_rows(w, height):
    return jnp.pad(w, ((0, height - w.shape[0]), (0, 0)))


def _layer(x2, c, bsz, t_len, w_ada, b_ada, norm1_w, w_in, mu_r, mu_k, mu_v, mu_w, mu_a, mu_g, w0,
           w_decay1, w_decay2, a0, w_aaa1, w_aaa2, w_gate1, w_gate2, k_k, k_a, r_k, lnx_w, lnx_b,
           conv_w, conv_norm_w, w_out, norm2_w, w_grp, b_grp, w_exp, b_exp, exp_gate, exp_up,
           exp_down, final_w):
    n, d = x2.shape
    dr = w0.shape[0]
    widths = [hi - lo for lo, hi in (LORA_DECAY, LORA_AAA, LORA_GATE)]
    assert all(w.shape[1] <= n_ for w, n_ in zip((w_decay1, w_aaa1, w_gate1), widths))
    assert dr % GROUP_LANES == 0 and conv_w.shape[1] == dr and w_in.shape[1] == 6 * dr
    assert t_len % WKV_CHUNK == 0 and w_grp.shape[1] == N_ROUTE_GROUPS and w_exp.shape[1] == N_EXPERTS

    mod3 = _ada(c, w_ada, b_ada).reshape(bsz, 6, d)
    h, proj = _inproj(x2, mod3, norm1_w, w_in.astype(BF16), t_len)

    w1p = jnp.concatenate([_pad_cols(w, n_) for w, n_ in zip((w_decay1, w_aaa1, w_gate1), widths)], axis=1)
    mup = jnp.concatenate([jnp.broadcast_to(mu[:, None], (d, n_)) for mu, n_ in zip((mu_w, mu_a, mu_g), widths)],
                          axis=1)
    wd2p, wa2p, wg2p = (_pad_rows(w, n_) for w, n_ in zip((w_decay2, w_aaa2, w_gate2), widths))
    lw, ag, gg = _lora(h, w1p, mup, wd2p, wa2p, wg2p, w0, a0, bsz, t_len)

    yr = _wkv(proj, lw, ag, gg, mu_r, mu_k, mu_v, k_k, k_a, r_k.reshape(dr), lnx_w, lnx_b, bsz, t_len)

    wr = _pad_cols(jnp.concatenate([w_grp, w_exp], axis=1), LANES)
    br = _pad_cols(jnp.concatenate([b_grp, b_exp]).reshape(1, -1), LANES)
    x1, h2, meta, cnt = _outproj(yr, proj, conv_w, conv_norm_w, w_out.astype(BF16), x2, mod3,
                                 norm2_w, wr, br, bsz, t_len)

    counts = cnt[0, :N_EXPERTS].astype(jnp.int32)
    padded = (counts + ROUTE_BLOCK - 1) // ROUTE_BLOCK * ROUTE_BLOCK
    pad_ends = jnp.cumsum(padded)
    pad_starts = pad_ends - padded
    top_e = meta[:, 0:2].astype(jnp.int32)
    rank = meta[:, 4:6].astype(jnp.int32)
    one_hot = top_e[:, :, None] == jnp.arange(N_EXPERTS, dtype=jnp.int32)
    dest = (jnp.sum(jnp.where(one_hot, pad_starts, 0), axis=-1) + rank).reshape(-1)
    nblk = (2 * n) // ROUTE_BLOCK + N_EXPERTS
    n_rows = nblk * ROUTE_BLOCK
    block_start = jnp.arange(nblk, dtype=jnp.int32) * ROUTE_BLOCK
    block_e = jnp.minimum(jnp.sum(pad_ends[None, :] <= block_start[:, None], axis=1),
                          N_EXPERTS - 1).astype(jnp.int32)
    n_active = (pad_ends[-1:] // ROUTE_BLOCK).astype(jnp.int32)

    pad_len = padded - counts
    zero_off = jnp.concatenate([pad_starts + counts, pad_len, jnp.sum(pad_len, keepdims=True),
                                n_active]).astype(jnp.int32)
    xs = _dispatch(dest, zero_off, h2, n_rows)
    ys = _experts(block_e, n_active, xs, exp_gate, exp_up, exp_down)
    return _combine(dest, ys, x1, meta, mod3, final_w, t_len)


def kernel(x, c, w_ada, b_ada, norm1_w, w_in, mu_r, mu_k, mu_v, mu_w, mu_a, mu_g, w0, w_decay1, w_decay2, a0, w_aaa1, w_aaa2, w_gate1, w_gate2, k_k, k_a, r_k, lnx_w, lnx_b, conv_w, conv_norm_w, w_out, norm2_w, w_grp, b_grp, w_exp, b_exp, exp_gate, exp_up, exp_down, final_w):
    bsz, t_len, d = x.shape
    assert w_ada.shape[0] == 1, "single-layer block"
    out = _layer(x.reshape(bsz * t_len, d), c, bsz, t_len, w_ada[0], b_ada[0], norm1_w[0], w_in[0],
                 mu_r[0], mu_k[0], mu_v[0], mu_w[0], mu_a[0], mu_g[0], w0[0], w_decay1[0], w_decay2[0],
                 a0[0], w_aaa1[0], w_aaa2[0], w_gate1[0], w_gate2[0], k_k[0], k_a[0], r_k[0], lnx_w[0],
                 lnx_b[0], conv_w[0], conv_norm_w[0], w_out[0], norm2_w[0], w_grp[0], b_grp[0],
                 w_exp[0], b_exp[0], exp_gate[0], exp_up[0], exp_down[0], final_w)
    return out.reshape(bsz, t_len, d)
```

```python
import itertools

import jax
import jax.numpy as jnp
from jax import lax
from jax.experimental import pallas as pl
from jax.experimental.pallas import tpu as pltpu

F32 = jnp.float32
BF16 = jnp.bfloat16

HEAD_DIM = 64
WKV_CHUNK = 64
GROUP_LANES = 256
N_ROUTE_GROUPS = 4
EXPERTS_PER_GROUP = 8
N_EXPERTS = N_ROUTE_GROUPS * EXPERTS_PER_GROUP
CONV_GROUP_DIM = 64
RMS_EPS = 1e-6
LNX_EPS = 64e-5
LANES = 128
LORA_DECAY, LORA_AAA, LORA_GATE = (0, 128), (128, 256), (256, 512)
LORA_PAD = LORA_GATE[1]
ADA_COLS = 1024
INPROJ_ROWS, INPROJ_COLS = 1024, 1536
LORA_ROWS = 1024
WKV_ROWS = 1024
OUTPROJ_ROWS = 512
DISPATCH_ROWS = 256
COMBINE_ROWS = 512
ROUTE_BLOCK = 256
NORM_ROWS = 32
VMEM_LIMIT = 56 * 1024 * 1024


def _dot(a, b, precision=None):
    return jnp.dot(a, b, preferred_element_type=F32, precision=precision)


def _dot_nt(a, b):
    return lax.dot_general(a, b, (((1,), (1,)), ((), ())), preferred_element_type=F32)


def _dot_tn(a, b):
    return lax.dot_general(a, b, (((0,), (0,)), ((), ())), preferred_element_type=F32)


def _split_bf16(x):
    hi = x.astype(BF16)
    return hi, (x - hi.astype(F32)).astype(BF16)


def _pack_rounded_pair(xr):
    c = xr.shape[1] // 2
    lo = pltpu.bitcast(xr[:, :c], jnp.uint32)
    hi = pltpu.bitcast(xr[:, c:], jnp.uint32)
    return (lo >> 16) | (hi & jnp.uint32(0xFFFF0000))


def _pack_bf16_pair(x):
    return _pack_rounded_pair(x.astype(BF16).astype(F32))


def _unpack_bf16_pair(w):
    lo = pltpu.bitcast(w << 16, F32)
    hi = pltpu.bitcast(w & jnp.uint32(0xFFFF0000), F32)
    return jnp.concatenate([lo, hi], axis=1)


def _sigmoid(x):
    return 1.0 / (1.0 + jnp.exp(-x))


def _shift_rows(x, carry_row):
    rolled = pltpu.roll(x, 1, axis=0)
    row = lax.broadcasted_iota(jnp.int32, x.shape, 0)
    return jnp.where(row == 0, carry_row, rolled)


def _seg_ones(n, seg):
    r = lax.broadcasted_iota(jnp.int32, (n, n), 0) // seg
    c = lax.broadcasted_iota(jnp.int32, (n, n), 1) // seg
    return jnp.where(r == c, 1.0, 0.0).astype(BF16)


def _seg_sum(x, ones_bd):
    hi = x.astype(BF16)
    lo = (x - hi.astype(F32)).astype(BF16)
    return _dot(hi, ones_bd) + _dot(lo, ones_bd)


def _seg_sum1(x, ones_bd):
    return _dot(x.astype(BF16), ones_bd)


def _ada_kernel(cb_ref, w_ref, b_ref, o_ref):
    nb = cb_ref.shape[0]
    tn = w_ref.shape[1]
    for b in range(nb):
        cv = cb_ref[b]
        s = cv * _sigmoid(cv)
        cols = [jnp.sum(s * w_ref[:, j * LANES:(j + 1) * LANES], axis=0, keepdims=True)
                for j in range(tn // LANES)]
        o_ref[b:b + 1, :] = jnp.concatenate(cols, axis=1) + b_ref[...]


def _ada(c, w_ada, b_ada):
    nb, d = c.shape
    n_out = w_ada.shape[1]
    tn = ADA_COLS
    cb = jnp.broadcast_to(c[:, :, None], (nb, d, LANES))
    return pl.pallas_call(
        _ada_kernel,
        out_shape=jax.ShapeDtypeStruct((nb, n_out), F32),
        grid=(n_out // tn,),
        in_specs=[pl.BlockSpec((nb, d, LANES), lambda j: (0, 0, 0)),
                  pl.BlockSpec((d, tn), lambda j: (0, j)),
                  pl.BlockSpec((1, tn), lambda j: (0, j))],
        out_specs=pl.BlockSpec((nb, tn), lambda j: (0, j)),
        compiler_params=pltpu.CompilerParams(dimension_semantics=("arbitrary",),
                                             vmem_limit_bytes=VMEM_LIMIT),
        name="ada",
    )(cb, w_ada, b_ada.reshape(1, n_out))


def _inproj_kernel(x_ref, mod_ref, nw_ref, w_ref, h_ref, p_ref):
    @pl.when(pl.program_id(1) == 0)
    def _():
        rows = min(NORM_ROWS, x_ref.shape[0])
        gain = nw_ref[...] * (1.0 + mod_ref[0, 1:2, :])
        shift = mod_ref[0, 0:1, :]

        def norm_rows(r, carry):
            sl = pl.ds(pl.multiple_of(r * rows, rows), rows)
            x = x_ref[sl, :]
            ms = jnp.mean(x * x, axis=-1, keepdims=True)
            h = x * lax.rsqrt(ms + RMS_EPS) * gain + shift
            h_ref[sl, :] = h.astype(BF16)
            return carry

        lax.fori_loop(0, x_ref.shape[0] // rows, norm_rows, 0, unroll=4)

    p_ref[...] = _dot(h_ref[...], w_ref[...])


def _inproj(x2, mod3, norm_w, w_in_bf, t_len):
    n, d = x2.shape
    n_in = w_in_bf.shape[1]
    tm = min(INPROJ_ROWS, t_len)
    tn = INPROJ_COLS
    return pl.pallas_call(
        _inproj_kernel,
        out_shape=(jax.ShapeDtypeStruct((n, d), BF16), jax.ShapeDtypeStruct((n, n_in), F32)),
        grid=(n // tm, n_in // tn),
        in_specs=[pl.BlockSpec((tm, d), lambda i, j: (i, 0)),
                  pl.BlockSpec((1, 6, d), lambda i, j: (i * tm // t_len, 0, 0)),
                  pl.BlockSpec((1, d), lambda i, j: (0, 0)),
                  pl.BlockSpec((d, tn), lambda i, j: (0, j))],
        out_specs=(pl.BlockSpec((tm, d), lambda i, j: (i, 0)),
                   pl.BlockSpec((tm, tn), lambda i, j: (i, j))),
        compiler_params=pltpu.CompilerParams(dimension_semantics=("arbitrary", "arbitrary"),
                                             vmem_limit_bytes=VMEM_LIMIT),
        name="inproj",
    )(x2, mod3, norm_w.reshape(1, d), w_in_bf)


def _lora_kernel(h_ref, w1_ref, mu_ref, wd2_ref, wa2_ref, wg2_ref, w0_ref, a0_ref,
                 lw_ref, ag_ref, gg_ref, carry_ref, w12_ref):
    @pl.when(pl.program_id(1) == 0)
    def _():
        carry_ref[...] = jnp.zeros_like(carry_ref)

    @pl.when((pl.program_id(0) == 0) & (pl.program_id(1) == 0))
    def _():
        w1 = w1_ref[...]
        w12_ref[:, :LORA_PAD] = w1.astype(BF16)
        w12_ref[:, LORA_PAD:] = (w1 * mu_ref[...]).astype(BF16)

    p12 = _dot(h_ref[...], w12_ref[...])
    p1 = p12[:, :LORA_PAD]
    p2 = p12[:, LORA_PAD:]
    prev = _shift_rows(p2, carry_ref[...])
    carry_ref[...] = p2[p2.shape[0] - 1:, :]
    lo = p1 - p2 + prev
    dec = _dot(jnp.tanh(lo[:, slice(*LORA_DECAY)]).astype(BF16), wd2_ref[...].astype(BF16))
    z = -(w0_ref[...] + dec)
    softplus = jnp.maximum(z, 0.0) + jnp.log(1.0 + jnp.exp(-jnp.abs(z)))
    lw_ref[...] = -jnp.exp(-softplus - 0.5)
    ag_ref[...] = _sigmoid(a0_ref[...] + _dot(lo[:, slice(*LORA_AAA)].astype(BF16), wa2_ref[...].astype(BF16)))
    gg_ref[...] = _dot(_sigmoid(lo[:, slice(*LORA_GATE)]).astype(BF16), wg2_ref[...].astype(BF16))


def _lora(h, w1p, mup, wd2p, wa2p, wg2p, w0, a0, bsz, t_len):
    n, d = h.shape
    dr = w0.shape[-1]
    tm = min(LORA_ROWS, t_len)
    nt = t_len // tm
    row = lambda b, t: (b * nt + t, 0)
    const = lambda b, t: (0, 0)
    out = jax.ShapeDtypeStruct((n, dr), F32)
    return pl.pallas_call(
        _lora_kernel,
        out_shape=(out, out, out),
        grid=(bsz, nt),
        in_specs=[pl.BlockSpec((tm, d), row),
                  pl.BlockSpec((d, LORA_PAD), const),
                  pl.BlockSpec((d, LORA_PAD), const),
                  pl.BlockSpec(wd2p.shape, const),
                  pl.BlockSpec(wa2p.shape, const),
                  pl.BlockSpec(wg2p.shape, const),
                  pl.BlockSpec((1, dr), const),
                  pl.BlockSpec((1, dr), const)],
        out_specs=(pl.BlockSpec((tm, dr), row),) * 3,
        scratch_shapes=[pltpu.VMEM((1, LORA_PAD), F32), pltpu.VMEM((d, 2 * LORA_PAD), BF16)],
        compiler_params=pltpu.CompilerParams(dimension_semantics=("arbitrary", "arbitrary"),
                                             vmem_limit_bytes=VMEM_LIMIT),
        name="lora",
    )(h, w1p, mup, wd2p, wa2p, wg2p, w0.reshape(1, dr), a0.reshape(1, dr))


def _wkv_kernel(r_ref, k_ref, v_ref, lw_ref, ag_ref, gg_ref, mur_ref, muk_ref, muv_ref,
                kk_ref, ka_ref, rk_ref, lnw_ref, lnb_ref, y_ref, s_ref, carry_ref):
    lt = r_ref.shape[0]
    gl = GROUP_LANES
    c_len = WKV_CHUNK
    heads = gl // HEAD_DIM

    @pl.when(pl.program_id(2) == 0)
    def _():
        s_ref[...] = jnp.zeros_like(s_ref)
        carry_ref[...] = jnp.zeros_like(carry_ref)

    ones_bd = _seg_ones(gl, HEAD_DIM)

    def lerp(ref, mu_ref, idx):
        z = ref[...]
        prev = _shift_rows(z, carry_ref[idx:idx + 1, :])
        carry_ref[idx:idx + 1, :] = z[lt - 1:, :]
        return z + (prev - z) * mu_ref[...]

    r1 = lerp(r_ref, mur_ref, 0)
    k1 = lerp(k_ref, muk_ref, 1)
    v1 = lerp(v_ref, muv_ref, 2)
    ag = ag_ref[...]
    kk = k1 * kk_ref[...]
    kk = kk * lax.rsqrt(jnp.maximum(_seg_sum1(kk * kk, ones_bd), 1e-24))
    k2 = k1 * (1.0 + (ag - 1.0) * ka_ref[...])
    a_all = -kk
    b_all = kk * ag
    lw = lw_ref[...]

    rows = lax.broadcasted_iota(jnp.int32, (gl, gl), 0)
    cols = lax.broadcasted_iota(jnp.int32, (gl, gl), 1)
    same_head = (rows // c_len) == (cols // HEAD_DIM)
    ts = lax.broadcasted_iota(jnp.int32, (c_len, gl), 0)
    tj = lax.broadcasted_iota(jnp.int32, (c_len, gl), 1) % c_len
    tri_strict = ts > tj
    tri_incl = ts >= tj
    eye = jnp.where(ts == tj, 1.0, 0.0).astype(F32)
    cr = lax.broadcasted_iota(jnp.int32, (c_len, c_len), 0)
    cc = lax.broadcasted_iota(jnp.int32, (c_len, c_len), 1)
    cum_b = jnp.where(cr >= cc, 1.0, 0.0).astype(BF16)

    def bd(x):
        xb = x.astype(BF16)
        return jnp.where(same_head, jnp.concatenate([xb] * heads, axis=0), jnp.zeros((), BF16))

    def fold(x):
        out = x[0:c_len]
        for hh in range(1, heads):
            out = out + x[hh * c_len:(hh + 1) * c_len]
        return out

    nch = lt // c_len
    sls = [slice(c * c_len, (c + 1) * c_len) for c in range(nch)]

    def cumsum3(x):
        hi = x.astype(BF16)
        r = x - hi.astype(F32)
        mid = r.astype(BF16)
        lo = (r - mid.astype(F32)).astype(BF16)
        return _dot(cum_b, hi) + _dot(cum_b, mid) + _dot(cum_b, lo)

    wc, g_s, n_s, rp, y0, s_in = {}, {}, {}, {}, {}, {}
    state = [s_ref[...]]

    def chunk_parallel(chunks):
        lws = {c: lw[sls[c]] for c in chunks}
        cs = {c: cumsum3(lws[c]) for c in chunks}
        yield
        e_cs = {c: jnp.exp(cs[c]) for c in chunks}
        e_neg = {c: jnp.exp(-cs[c]) for c in chunks}
        for c in chunks:
            wc[c] = e_cs[c][c_len - 1:, :]
        at = {c: a_all[sls[c]] * jnp.exp(cs[c] - lws[c]) for c in chunks}
        rt = {c: r1[sls[c]] * e_cs[c] for c in chunks}
        bt = {c: b_all[sls[c]] * e_neg[c] for c in chunks}
        kt = {c: k2[sls[c]] * e_neg[c] for c in chunks}
        vc = {c: v1[sls[c]] for c in chunks}
        yield
        lhs = {c: jnp.concatenate([at[c], rt[c]], axis=0).astype(BF16) for c in chunks}
        g_b = {c: _dot_nt(lhs[c], bd(bt[c])) for c in chunks}
        yield
        g_k = {c: _dot_nt(lhs[c], bd(kt[c])) for c in chunks}
        yield
        n_ab = {c: jnp.where(tri_strict, g_b[c][:c_len], 0.0) for c in chunks}
        a_ak = {c: jnp.where(tri_strict, g_k[c][:c_len], 0.0) for c in chunks}
        a_rb = {c: jnp.where(tri_incl, g_b[c][c_len:], 0.0).astype(BF16) for c in chunks}
        a_rk = {c: jnp.where(tri_incl, g_k[c][c_len:], 0.0).astype(BF16) for c in chunks}

        p = {c: eye + jnp.where((ts // 2) == (tj // 2), n_ab[c], 0.0) for c in chunks}
        for half in (2, 4, 8, 16, 32):
            lower_left = (((ts // (2 * half)) == (tj // (2 * half)))
                          & ((ts % (2 * half)) >= half) & ((tj % (2 * half)) < half))
            n21_ta = {c: _dot(jnp.where(lower_left, n_ab[c], 0.0).astype(BF16), bd(p[c])) for c in chunks}
            yield
            p = {c: p[c] + _dot(p[c].astype(BF16), bd(n21_ta[c])) for c in chunks}
            yield

        z = {c: _dot(p[c].astype(BF16), jnp.concatenate([bd(a_ak[c]), bd(at[c])], axis=1))
             for c in chunks}
        yield
        ap = {c: z[c][:, gl:] for c in chunks}
        bd_v = {c: bd(vc[c]) for c in chunks}
        u0 = {c: _dot(z[c][:, :gl].astype(BF16), bd_v[c]) for c in chunks}
        yield
        w = {c: _dot(a_rb[c], jnp.concatenate([bd(ap[c]), bd(u0[c])], axis=1)) for c in chunks}
        yield
        for c in chunks:
            rp[c] = rt[c] + w[c][:, :gl]
            y0[c] = w[c][:, gl:] + _dot(a_rk[c], bd_v[c])
        yield
        bh = {c: (bt[c] * wc[c]).astype(BF16) for c in chunks}
        bk = {c: jnp.concatenate([bh[c], (kt[c] * wc[c]).astype(BF16)], axis=0) for c in chunks}
        for c in chunks:
            g_s[c] = jnp.where(same_head, _dot_tn(ap[c].astype(BF16), bh[c]), 0.0).astype(BF16)
        yield
        for c in chunks:
            n_s[c] = fold(jnp.where(
                same_head, _dot_tn(jnp.concatenate([u0[c], vc[c]], axis=0).astype(BF16), bk[c]), 0.0))
        yield

    def state_update(chunks):
        for c in chunks:
            s_in[c] = state[0].astype(BF16)
            state[0] = state[0] * wc[c] + _dot(s_in[c], g_s[c]) + n_s[c]
            yield

    def epilogue(chunks):
        rows = slice(chunks[0] * c_len, (chunks[-1] + 1) * c_len)
        y = jnp.concatenate([_dot_nt(rp[c].astype(BF16), bd(s_in[c])) + y0[c] for c in chunks], axis=0)
        yield
        inv_n = 1.0 / HEAD_DIM
        mean = _seg_sum(y, ones_bd) * inv_n
        yield
        yc = y - mean
        var = _seg_sum1(yc * yc, ones_bd) * inv_n
        yield
        yn = yc * lax.rsqrt(var + LNX_EPS) * lnw_ref[...] + lnb_ref[...]
        bonus = _seg_sum(r1[rows] * k2[rows] * rk_ref[...], ones_bd) * v1[rows]
        y_ref[rows, :] = ((yn + bonus) * gg_ref[rows, :]).astype(y_ref.dtype)
        yield

    def drain(gen):
        for _ in gen:
            pass

    def interleave(serial, other, every):
        for k, _ in enumerate(other, 1):
            if k % every == 0:
                next(serial, None)
        drain(serial)

    wave = max(nch // 2, 1)
    waves = [list(range(w0, min(w0 + wave, nch))) for w0 in range(0, nch, wave)]
    par_stages, epi_stages = 20, 4
    drain(chunk_parallel(waves[0]))
    for wi, chunks in enumerate(waves):
        others, n_other = [], 0
        if wi + 1 < len(waves):
            others.append(chunk_parallel(waves[wi + 1]))
            n_other += par_stages
        if wi >= 1:
            others.append(epilogue(waves[wi - 1]))
            n_other += epi_stages
        interleave(state_update(chunks), itertools.chain(*others), max(n_other // len(chunks), 1))
    drain(epilogue(waves[-1]))
    s_ref[...] = state[0]


def _wkv(proj, lw, ag, gg, mu_r, mu_k, mu_v, k_k, k_a, r_k, lnx_w, lnx_b, bsz, t_len):
    n = proj.shape[0]
    dr = lw.shape[1]
    gl = GROUP_LANES
    ng = dr // gl
    lt = min(WKV_ROWS, t_len)
    nt = t_len // lt
    row = lambda off: (lambda b, g, t: (b * nt + t, off + g))
    par = lambda b, g, t: (0, g)
    vec = lambda a: a.reshape(1, dr)
    return pl.pallas_call(
        _wkv_kernel,
        out_shape=jax.ShapeDtypeStruct((n, dr), BF16),
        grid=(bsz, ng, nt),
        in_specs=[pl.BlockSpec((lt, gl), row(0)),
                  pl.BlockSpec((lt, gl), row(ng)),
                  pl.BlockSpec((lt, gl), row(2 * ng)),
                  pl.BlockSpec((lt, gl), row(0)),
                  pl.BlockSpec((lt, gl), row(0)),
                  pl.BlockSpec((lt, gl), row(0))] + [pl.BlockSpec((1, gl), par)] * 8,
        out_specs=pl.BlockSpec((lt, gl), row(0)),
        scratch_shapes=[pltpu.VMEM((HEAD_DIM, gl), F32), pltpu.VMEM((8, gl), F32)],
        compiler_params=pltpu.CompilerParams(
            dimension_semantics=("arbitrary", "arbitrary", "arbitrary"),
            vmem_limit_bytes=VMEM_LIMIT),
        name="wkv",
    )(proj, proj, proj, lw, ag, gg, vec(mu_r), vec(mu_k), vec(mu_v), vec(k_k), vec(k_a),
      vec(r_k), vec(lnx_w), vec(lnx_b))


def _outproj_kernel(yr_ref, u_ref, gb_ref, gc_ref, cw_ref, cnw_ref, wo_ref, x_ref, mod_ref,
                    n2w_ref, wr_ref, br_ref, x1_ref, h2_ref, meta_ref, cnt_ref,
                    ccarry_ref, cnt_acc_ref, wrs_ref):
    tm = x_ref.shape[0]
    dc = u_ref.shape[1]

    @pl.when(pl.program_id(1) == 0)
    def _():
        ccarry_ref[...] = jnp.zeros_like(ccarry_ref)

    @pl.when((pl.program_id(0) == 0) & (pl.program_id(1) == 0))
    def _():
        cnt_acc_ref[...] = jnp.zeros_like(cnt_acc_ref)
        wr_hi, wr_lo = _split_bf16(wr_ref[...])
        wrs_ref[0] = wr_hi
        wrs_ref[1] = wr_lo

    pre = gc_ref[...] * u_ref[...]
    prev1 = _shift_rows(pre, ccarry_ref[0:1, :])
    prev2 = _shift_rows(prev1, ccarry_ref[1:2, :])
    ccarry_ref[0:1, :] = pre[tm - 1:, :]
    ccarry_ref[1:2, :] = pre[tm - 2:tm - 1, :]
    yc = gb_ref[...] * (prev2 * cw_ref[0:1, :] + prev1 * cw_ref[1:2, :] + pre * cw_ref[2:3, :])
    ones_bd = _seg_ones(GROUP_LANES, CONV_GROUP_DIM)
    ycs = []
    for s in range(dc // GROUP_LANES):
        ys = yc[:, s * GROUP_LANES:(s + 1) * GROUP_LANES]
        ms = _seg_sum1(ys * ys, ones_bd) * (1.0 / CONV_GROUP_DIM)
        ycs.append(ys * lax.rsqrt(ms + RMS_EPS))
    y_conv = (jnp.concatenate(ycs, axis=1) * cnw_ref[...]).astype(BF16)

    dr = yr_ref.shape[1]
    y_mix = _dot(yr_ref[...], wo_ref[0:dr, :]) + _dot(y_conv, wo_ref[dr:, :])
    x1 = x_ref[...] + mod_ref[0, 2:3, :] * y_mix
    x1_ref[...] = x1
    ms = jnp.mean(x1 * x1, axis=-1, keepdims=True)
    gain2 = n2w_ref[...] * (1.0 + mod_ref[0, 4:5, :])
    h2 = x1 * lax.rsqrt(ms + RMS_EPS) * gain2 + mod_ref[0, 3:4, :]
    h2_hi = h2.astype(BF16)
    h2_hi32 = h2_hi.astype(F32)
    h2_ref[...] = _pack_rounded_pair(h2_hi32)
    h2_lo = (h2 - h2_hi32).astype(BF16)

    logits = _dot(h2_hi, wrs_ref[0]) + _dot(h2_lo, wrs_ref[0]) + _dot(h2_hi, wrs_ref[1]) + br_ref[...]
    lane = lax.broadcasted_iota(jnp.int32, logits.shape, 1)
    neg = jnp.float32(-jnp.inf)
    big = jnp.int32(1 << 20)
    glog = jnp.where(lane < N_ROUTE_GROUPS, logits, neg)
    gmax = jnp.max(glog, axis=-1, keepdims=True)
    grp = jnp.min(jnp.where(glog == gmax, lane, big), axis=-1, keepdims=True)
    p_grp = 1.0 / jnp.sum(jnp.exp(glog - gmax), axis=-1, keepdims=True)
    e_lane = lane - N_ROUTE_GROUPS
    in_grp = (e_lane >= grp * EXPERTS_PER_GROUP) & (e_lane < (grp + 1) * EXPERTS_PER_GROUP)
    elog = jnp.where(in_grp, logits, neg)
    m1 = jnp.max(elog, axis=-1, keepdims=True)
    i1 = jnp.min(jnp.where(elog == m1, lane, big), axis=-1, keepdims=True)
    elog2 = jnp.where(lane == i1, neg, elog)
    m2 = jnp.max(elog2, axis=-1, keepdims=True)
    i2 = jnp.min(jnp.where(elog2 == m2, lane, big), axis=-1, keepdims=True)
    t2 = jnp.exp(m2 - m1)
    w1 = p_grp / (1.0 + t2)
    w2 = p_grp * t2 / (1.0 + t2)
    e1 = i1 - N_ROUTE_GROUPS
    e2 = i2 - N_ROUTE_GROUPS

    oh1 = lane == e1
    oh2 = lane == e2
    oh = jnp.where(oh1 | oh2, 1.0, 0.0)
    tr = lax.broadcasted_iota(jnp.int32, (tm, tm), 0)
    tc = lax.broadcasted_iota(jnp.int32, (tm, tm), 1)
    lower = jnp.where(tr > tc, 1.0, 0.0).astype(BF16)
    before = _dot(lower, oh.astype(BF16)) + cnt_acc_ref[...]
    rank1 = jnp.sum(jnp.where(oh1, before, 0.0), axis=-1, keepdims=True)
    rank2 = jnp.sum(jnp.where(oh2, before, 0.0), axis=-1, keepdims=True)
    cnt = cnt_acc_ref[...] + jnp.sum(oh, axis=0, keepdims=True)
    cnt_acc_ref[...] = cnt
    cnt_ref[...] = jnp.broadcast_to(cnt, cnt_ref.shape)

    meta = jnp.where(lane == 0, e1.astype(F32), 0.0)
    meta = jnp.where(lane == 1, e2.astype(F32), meta)
    meta = jnp.where(lane == 2, w1, meta)
    meta = jnp.where(lane == 3, w2, meta)
    meta = jnp.where(lane == 4, rank1, meta)
    meta = jnp.where(lane == 5, rank2, meta)
    meta_ref[...] = meta


def _outproj(yr, proj, conv_w, conv_norm_w, w_out_bf, x2, mod3, norm2_w, wr, br, bsz, t_len):
    n, d = x2.shape
    dr = yr.shape[1]
    dc = conv_w.shape[1]
    tm = min(OUTPROJ_ROWS, t_len)
    nt = t_len // tm
    nseg = dr // dc
    row = lambda off: (lambda b, t: (b * nt + t, off))
    const = lambda b, t: (0, 0)
    return pl.pallas_call(
        _outproj_kernel,
        out_shape=(jax.ShapeDtypeStruct((n, d), F32), jax.ShapeDtypeStruct((n, d // 2), jnp.uint32),
                   jax.ShapeDtypeStruct((n, LANES), F32), jax.ShapeDtypeStruct((8, LANES), F32)),
        grid=(bsz, nt),
        in_specs=[pl.BlockSpec((tm, dr), row(0)),
                  pl.BlockSpec((tm, dc), row(3 * nseg)),
                  pl.BlockSpec((tm, dc), row(4 * nseg)),
                  pl.BlockSpec((tm, dc), row(5 * nseg)),
                  pl.BlockSpec((3, dc), const),
                  pl.BlockSpec((1, dc), const),
                  pl.BlockSpec((dr + dc, d), const),
                  pl.BlockSpec((tm, d), row(0)),
                  pl.BlockSpec((1, 6, d), lambda b, t: (b, 0, 0)),
                  pl.BlockSpec((1, d), const),
                  pl.BlockSpec((d, LANES), const),
                  pl.BlockSpec((1, LANES), const)],
        out_specs=(pl.BlockSpec((tm, d), row(0)), pl.BlockSpec((tm, d // 2), row(0)),
                   pl.BlockSpec((tm, LANES), row(0)), pl.BlockSpec((8, LANES), const)),
        scratch_shapes=[pltpu.VMEM((8, dc), F32), pltpu.VMEM((1, LANES), F32),
                        pltpu.VMEM((2, d, LANES), BF16)],
        compiler_params=pltpu.CompilerParams(dimension_semantics=("arbitrary", "arbitrary"),
                                             vmem_limit_bytes=VMEM_LIMIT),
        name="outproj",
    )(yr, proj, proj, proj, conv_w, conv_norm_w.reshape(1, dc), w_out_bf, x2, mod3,
      norm2_w.reshape(1, d), wr, br)


def _dispatch_kernel(zoff_ref, dest_hbm, h2_hbm, xs_hbm, idx_smem, tile_ref, zero_ref, zblk_ref,
                     sem_idx, sem_tile, sem_rows, sem_zero, sem_tail):
    i = pl.program_id(0)
    nsteps = pl.num_programs(0)
    th = idx_smem.shape[1] // 2
    slot = i % 2

    def idx_copy(step, sl):
        return pltpu.make_async_copy(dest_hbm.at[step], idx_smem.at[sl], sem_idx.at[sl])

    def tile_copy(step, sl):
        rows = pl.ds(pl.multiple_of(step * th, th), th)
        return pltpu.make_async_copy(h2_hbm.at[rows], tile_ref.at[sl], sem_tile.at[sl])

    def rows_wait(sl):
        pltpu.make_async_copy(xs_hbm.at[pl.ds(0, 2 * th)], xs_hbm.at[pl.ds(0, 2 * th)],
                              sem_rows.at[sl]).wait()

    nblk = xs_hbm.shape[0] // ROUTE_BLOCK

    def tail_copy(b):
        start = pl.multiple_of(b * ROUTE_BLOCK, ROUTE_BLOCK)
        return pltpu.make_async_copy(zblk_ref, xs_hbm.at[pl.ds(start, ROUTE_BLOCK)], sem_tail)

    @pl.when(i == 0)
    def _():
        idx_copy(0, 0).start()
        tile_copy(0, 0).start()
        zero_ref[...] = jnp.zeros_like(zero_ref)

        def zero_expert(e, carry):
            off = zoff_ref[e]

            def zero_row(j, c):
                pltpu.make_async_copy(zero_ref, xs_hbm.at[pl.ds(off + j, 1)], sem_zero).start()
                return c

            return lax.fori_loop(0, zoff_ref[N_EXPERTS + e], zero_row, carry)

        lax.fori_loop(0, N_EXPERTS, zero_expert, 0)
        total = zoff_ref[2 * N_EXPERTS]
        bulk = pl.multiple_of((total // 8) * 8, 8)

        @pl.when(bulk > 0)
        def _():
            pltpu.make_async_copy(xs_hbm.at[pl.ds(0, bulk)], xs_hbm.at[pl.ds(0, bulk)], sem_zero).wait()

        def wait_row(j, c):
            pltpu.make_async_copy(zero_ref, xs_hbm.at[pl.ds(0, 1)], sem_zero).wait()
            return c

        lax.fori_loop(0, total - bulk, wait_row, 0)

        zblk_ref[...] = jnp.zeros_like(zblk_ref)

        def tail_start(b, c):
            tail_copy(b).start()
            return c

        lax.fori_loop(zoff_ref[2 * N_EXPERTS + 1], nblk, tail_start, 0)

    idx_copy(i, slot).wait()
    tile_copy(i, slot).wait()

    @pl.when(i > 0)
    def _():
        rows_wait(1 - slot)

    @pl.when(i + 1 < nsteps)
    def _():
        idx_copy(i + 1, 1 - slot).start()
        tile_copy(i + 1, 1 - slot).start()

    for p in range(2):
        @pl.when(slot == p)
        def _(p=p):
            for t in range(th):
                for s in range(2):
                    d = idx_smem[p, 2 * t + s]
                    pltpu.make_async_copy(tile_ref.at[p, pl.ds(t, 1)], xs_hbm.at[pl.ds(d, 1)],
                                          sem_rows.at[p]).start(priority=s)

    @pl.when(i == nsteps - 1)
    def _():
        rows_wait(slot)

        def tail_wait(b, c):
            tail_copy(b).wait()
            return c

        lax.fori_loop(zoff_ref[2 * N_EXPERTS + 1], nblk, tail_wait, 0)


def _dispatch(dest, zero_off, h2, n_rows):
    n, d = h2.shape
    th = min(DISPATCH_ROWS, n)
    return pl.pallas_call(
        _dispatch_kernel,
        out_shape=jax.ShapeDtypeStruct((n_rows, d), h2.dtype),
        grid_spec=pltpu.PrefetchScalarGridSpec(
            num_scalar_prefetch=1,
            grid=(n // th,),
            in_specs=[pl.BlockSpec(memory_space=pl.ANY), pl.BlockSpec(memory_space=pl.ANY)],
            out_specs=pl.BlockSpec(memory_space=pl.ANY),
            scratch_shapes=[pltpu.SMEM((2, 2 * th), jnp.int32),
                            pltpu.VMEM((2, th, d), h2.dtype),
                            pltpu.VMEM((1, d), h2.dtype),
                            pltpu.VMEM((ROUTE_BLOCK, d), h2.dtype),
                            pltpu.SemaphoreType.DMA((2,)), pltpu.SemaphoreType.DMA((2,)),
                            pltpu.SemaphoreType.DMA((2,)),
                            pltpu.SemaphoreType.DMA, pltpu.SemaphoreType.DMA]),
        compiler_params=pltpu.CompilerParams(dimension_semantics=("arbitrary",),
                                             vmem_limit_bytes=VMEM_LIMIT),
        name="dispatch",
    )(zero_off, dest.reshape(n // th, 2 * th), h2)


def _expert_kernel(sched_ref, na_ref, xs_ref, wg_hbm, wu_hbm, wd_hbm, ys_ref,
                   wg_buf, wu_buf, wd_buf, sem):
    i = pl.program_id(0)

    def weight_copies(e, slot):
        return (pltpu.make_async_copy(wg_hbm.at[e], wg_buf.at[slot], sem.at[slot, 0]),
                pltpu.make_async_copy(wu_hbm.at[e], wu_buf.at[slot], sem.at[slot, 1]),
                pltpu.make_async_copy(wd_hbm.at[e], wd_buf.at[slot], sem.at[slot, 2]))

    @pl.when(i < na_ref[0])
    def _():
        e = sched_ref[0, i]
        slot = sched_ref[1, i]

        @pl.when(i == 0)
        def _():
            for cp in weight_copies(e, slot):
                cp.start()

        @pl.when(sched_ref[2, i] == 1)
        def _():
            for cp in weight_copies(e, slot):
                cp.wait()
            nxt = sched_ref[3, i]

            @pl.when(nxt >= 0)
            def _():
                for cp in weight_copies(nxt, 1 - slot):
                    cp.start()

        x = _unpack_bf16_pair(xs_ref[...]).astype(BF16)
        g = _dot(x, wg_buf[slot].astype(BF16))
        u = _dot(x, wu_buf[slot].astype(BF16))
        hid = (g * _sigmoid(g) * u).astype(BF16)
        ys_ref[...] = _pack_bf16_pair(_dot(hid, wd_buf[slot].astype(BF16)))

    @pl.when(i >= na_ref[0])
    def _():
        ys_ref[...] = jnp.zeros_like(ys_ref)


def _experts(block_e, n_active, xs, exp_gate, exp_up, exp_down):
    n_rows, dp = xs.shape
    _, d, de = exp_gate.shape
    nblk = n_rows // ROUTE_BLOCK
    idx = jnp.arange(nblk, dtype=jnp.int32)
    valid = idx < n_active[0]
    first = valid & ((idx == 0) | (block_e != jnp.roll(block_e, 1)))
    slot = jnp.maximum(jnp.cumsum(first.astype(jnp.int32)) - 1, 0) % 2
    first_at = jnp.where(first, idx, nblk)
    next_first = jnp.concatenate([lax.cummin(first_at, reverse=True)[1:], jnp.full((1,), nblk, jnp.int32)])
    nxt = jnp.where(next_first < nblk, block_e[jnp.minimum(next_first, nblk - 1)], -1)
    sched = jnp.stack([block_e, slot, first.astype(jnp.int32), nxt]).astype(jnp.int32)

    blk_row = lambda i, sc, na: (jnp.minimum(i, jnp.maximum(na[0] - 1, 0)), 0)
    return pl.pallas_call(
        _expert_kernel,
        out_shape=jax.ShapeDtypeStruct((n_rows, dp), jnp.uint32),
        grid_spec=pltpu.PrefetchScalarGridSpec(
            num_scalar_prefetch=2,
            grid=(nblk,),
            in_specs=[pl.BlockSpec((ROUTE_BLOCK, dp), blk_row),
                      pl.BlockSpec(memory_space=pl.ANY),
                      pl.BlockSpec(memory_space=pl.ANY),
                      pl.BlockSpec(memory_space=pl.ANY)],
            out_specs=pl.BlockSpec((ROUTE_BLOCK, dp), lambda i, sc, na: (i, 0)),
            scratch_shapes=[pltpu.VMEM((2, d, de), exp_gate.dtype),
                            pltpu.VMEM((2, d, de), exp_up.dtype),
                            pltpu.VMEM((2, de, d), exp_down.dtype),
                            pltpu.SemaphoreType.DMA((2, 3))]),
        compiler_params=pltpu.CompilerParams(dimension_semantics=("arbitrary",),
                                             vmem_limit_bytes=VMEM_LIMIT),
        name="experts",
    )(sched, n_active, xs, exp_gate, exp_up, exp_down)


def _combine_kernel(dest_hbm, ys_hbm, x1_ref, meta_ref, mod_ref, fw_ref, o_ref,
                    idx_smem, buf_ref, sem_idx, sem_rows):
    i = pl.program_id(0)
    nsteps = pl.num_programs(0)
    tj = x1_ref.shape[0]
    slot = i % 2

    def idx_copy(step, sl):
        return pltpu.make_async_copy(dest_hbm.at[step], idx_smem.at[sl], sem_idx.at[sl])

    def gather(sl):
        for t in range(tj):
            for s in range(2):
                d = idx_smem[sl, 2 * t + s]
                pltpu.make_async_copy(ys_hbm.at[pl.ds(d, 1)], buf_ref.at[sl, s, pl.ds(t, 1)],
                                      sem_rows.at[sl]).start(priority=s)

    @pl.when(i == 0)
    def _():
        first = idx_copy(0, 0)
        first.start()
        first.wait()
        gather(0)

        @pl.when(nsteps > 1)
        def _():
            idx_copy(1, 1).start()

    for p in range(2):
        @pl.when((i + 1 < nsteps) & (slot == 1 - p))
        def _(p=p):
            idx_copy(i + 1, p).wait()

            @pl.when(i + 2 < nsteps)
            def _():
                idx_copy(i + 2, 1 - p).start()

            gather(p)

    pltpu.make_async_copy(buf_ref.at[slot], buf_ref.at[slot], sem_rows.at[slot]).wait()

    meta = meta_ref[...]
    y = (meta[:, 2:3] * _unpack_bf16_pair(buf_ref[slot, 0])
         + meta[:, 3:4] * _unpack_bf16_pair(buf_ref[slot, 1]))
    x2 = x1_ref[...] + mod_ref[0, 5:6, :] * y
    ms = jnp.mean(x2 * x2, axis=-1, keepdims=True)
    o_ref[...] = x2 * lax.rsqrt(ms + RMS_EPS) * fw_ref[...]


def _combine(dest, ys, x1, meta, mod3, final_w, t_len):
    n, d = x1.shape
    tj = min(COMBINE_ROWS, t_len)
    return pl.pallas_call(
        _combine_kernel,
        out_shape=jax.ShapeDtypeStruct((n, d), F32),
        grid=(n // tj,),
        in_specs=[pl.BlockSpec(memory_space=pl.ANY),
                  pl.BlockSpec(memory_space=pl.ANY),
                  pl.BlockSpec((tj, d), lambda i: (i, 0)),
                  pl.BlockSpec((tj, LANES), lambda i: (i, 0)),
                  pl.BlockSpec((1, 6, d), lambda i: (i * tj // t_len, 0, 0)),
                  pl.BlockSpec((1, d), lambda i: (0, 0))],
        out_specs=pl.BlockSpec((tj, d), lambda i: (i, 0)),
        scratch_shapes=[pltpu.SMEM((2, 2 * tj), jnp.int32), pltpu.VMEM((2, 2, tj, d // 2), jnp.uint32),
                        pltpu.SemaphoreType.DMA((2,)), pltpu.SemaphoreType.DMA((2,))],
        compiler_params=pltpu.CompilerParams(dimension_semantics=("arbitrary",),
                                             vmem_limit_bytes=VMEM_LIMIT),
        name="combine",
    )(dest.reshape(n // tj, 2 * tj), ys, x1, meta, mod3, final_w.reshape(1, d))


def _pad_cols(w, width):
    return jnp.pad(w, ((0, 0), (0, width - w.shape[1])))


def _pad_rows(w, height):
    return jnp.pad(w, ((0, height - w.shape[0]), (0, 0)))


def _layer(x2, c, bsz, t_len, w_ada, b_ada, norm1_w, w_in, mu_r, mu_k, mu_v, mu_w, mu_a, mu_g, w0,
           w_decay1, w_decay2, a0, w_aaa1, w_aaa2, w_gate1, w_gate2, k_k, k_a, r_k, lnx_w, lnx_b,
           conv_w, conv_norm_w, w_out, norm2_w, w_grp, b_grp, w_exp, b_exp, exp_gate, exp_up,
           exp_down, final_w):
    n, d = x2.shape
    dr = w0.shape[0]
    widths = [hi - lo for lo, hi in (LORA_DECAY, LORA_AAA, LORA_GATE)]
    assert all(w.shape[1] <= n_ for w, n_ in zip((w_decay1, w_aaa1, w_gate1), widths))
    assert dr % GROUP_LANES == 0 and conv_w.shape[1] == dr and w_in.shape[1] == 6 * dr
    assert t_len % WKV_CHUNK == 0 and w_grp.shape[1] == N_ROUTE_GROUPS and w_exp.shape[1] == N_EXPERTS

    mod3 = _ada(c, w_ada, b_ada).reshape(bsz, 6, d)
    h, proj = _inproj(x2, mod3, norm1_w, w_in.astype(BF16), t_len)

    w1p = jnp.concatenate([_pad_cols(w, n_) for w, n_ in zip((w_decay1, w_aaa1, w_gate1), widths)], axis=1)
    mup = jnp.concatenate([jnp.broadcast_to(mu[:, None], (d, n_)) for mu, n_ in zip((mu_w, mu_a, mu_g), widths)],
                          axis=1)
    wd2p, wa2p, wg2p = (_pad_rows(w, n_) for w, n_ in zip((w_decay2, w_aaa2, w_gate2), widths))
    lw, ag, gg = _lora(h, w1p, mup, wd2p, wa2p, wg2p, w0, a0, bsz, t_len)

    yr = _wkv(proj, lw, ag, gg, mu_r, mu_k, mu_v, k_k, k_a, r_k.reshape(dr), lnx_w, lnx_b, bsz, t_len)

    wr = _pad_cols(jnp.concatenate([w_grp, w_exp], axis=1), LANES)
    br = _pad_cols(jnp.concatenate([b_grp, b_exp]).reshape(1, -1), LANES)
    x1, h2, meta, cnt = _outproj(yr, proj, conv_w, conv_norm_w, w_out.astype(BF16), x2, mod3,
                                 norm2_w, wr, br, bsz, t_len)

    counts = cnt[0, :N_EXPERTS].astype(jnp.int32)
    padded = (counts + ROUTE_BLOCK - 1) // ROUTE_BLOCK * ROUTE_BLOCK
    pad_ends = jnp.cumsum(padded)
    pad_starts = pad_ends - padded
    top_e = meta[:, 0:2].astype(jnp.int32)
    rank = meta[:, 4:6].astype(jnp.int32)
    one_hot = top_e[:, :, None] == jnp.arange(N_EXPERTS, dtype=jnp.int32)
    dest = (jnp.sum(jnp.where(one_hot, pad_starts, 0), axis=-1) + rank).reshape(-1)
    nblk = (2 * n) // ROUTE_BLOCK + N_EXPERTS
    n_rows = nblk * ROUTE_BLOCK
    block_start = jnp.arange(nblk, dtype=jnp.int32) * ROUTE_BLOCK
    block_e = jnp.minimum(jnp.sum(pad_ends[None, :] <= block_start[:, None], axis=1),
                          N_EXPERTS - 1).astype(jnp.int32)
    n_active = (pad_ends[-1:] // ROUTE_BLOCK).astype(jnp.int32)

    pad_len = padded - counts
    zero_off = jnp.concatenate([pad_starts + counts, pad_len, jnp.sum(pad_len, keepdims=True),
                                n_active]).astype(jnp.int32)
    xs = _dispatch(dest, zero_off, h2, n_rows)
    ys = _experts(block_e, n_active, xs, exp_gate, exp_up, exp_down)
    return _combine(dest, ys, x1, meta, mod3, final_w, t_len)


def kernel(x, c, w_ada, b_ada, norm1_w, w_in, mu_r, mu_k, mu_v, mu_w, mu_a, mu_g, w0, w_decay1, w_decay2, a0, w_aaa1, w_aaa2, w_gate1, w_gate2, k_k, k_a, r_k, lnx_w, lnx_b, conv_w, conv_norm_w, w_out, norm2_w, w_grp, b_grp, w_exp, b_exp, exp_gate, exp_up, exp_down, final_w):
    bsz, t_len, d = x.shape
    assert w_ada.shape[0] == 1, "single-layer block"
    out = _layer(x.reshape(bsz * t_len, d), c, bsz, t_len, w_ada[0], b_ada[0], norm1_w[0], w_in[0],
                 mu_r[0], mu_k[0], mu_v[0], mu_w[0], mu_a[0], mu_g[0], w0[0], w_decay1[0], w_decay2[0],
                 a0[0], w_aaa1[0], w_aaa2[0], w_gate1[0], w_gate2[0], k_k[0], k_a[0], r_k[0], lnx_w[0],
                 lnx_b[0], conv_w[0], conv_norm_w[0], w_out[0], norm2_w[0], w_grp[0], b_grp[0],
                 w_exp[0], b_exp[0], exp_gate[0], exp_up[0], exp_down[0], final_w)
    return out.reshape(bsz, t_len, d)
```

```python
import itertools

import jax
import jax.numpy as jnp
from jax import lax
from jax.experimental import pallas as pl
from jax.experimental.pallas import tpu as pltpu

F32 = jnp.float32
BF16 = jnp.bfloat16

HEAD_DIM = 64
WKV_CHUNK = 64
GROUP_LANES = 256
N_ROUTE_GROUPS = 4
EXPERTS_PER_GROUP = 8
N_EXPERTS = N_ROUTE_GROUPS * EXPERTS_PER_GROUP
CONV_GROUP_DIM = 64
RMS_EPS = 1e-6
LNX_EPS = 64e-5
LANES = 128
LORA_DECAY, LORA_AAA, LORA_GATE = (0, 128), (128, 256), (256, 512)
LORA_PAD = LORA_GATE[1]
ADA_COLS = 1024
INPROJ_ROWS, INPROJ_COLS = 1024, 1536
LORA_ROWS = 1024
WKV_ROWS = 1024
OUTPROJ_ROWS = 512
DISPATCH_ROWS = 256
COMBINE_ROWS = 256
ROUTE_BLOCK = 256
NORM_ROWS = 32
VMEM_LIMIT = 56 * 1024 * 1024


def _dot(a, b, precision=None):
    return jnp.dot(a, b, preferred_element_type=F32, precision=precision)


def _dot_nt(a, b):
    return lax.dot_general(a, b, (((1,), (1,)), ((), ())), preferred_element_type=F32)


def _dot_tn(a, b):
    return lax.dot_general(a, b, (((0,), (0,)), ((), ())), preferred_element_type=F32)


def _split_bf16(x):
    hi = x.astype(BF16)
    return hi, (x - hi.astype(F32)).astype(BF16)


def _pack_rounded_pair(xr):
    c = xr.shape[1] // 2
    lo = pltpu.bitcast(xr[:, :c], jnp.uint32)
    hi = pltpu.bitcast(xr[:, c:], jnp.uint32)
    return (lo >> 16) | (hi & jnp.uint32(0xFFFF0000))


def _pack_bf16_pair(x):
    return _pack_rounded_pair(x.astype(BF16).astype(F32))


def _unpack_bf16_pair(w):
    lo = pltpu.bitcast(w << 16, F32)
    hi = pltpu.bitcast(w & jnp.uint32(0xFFFF0000), F32)
    return jnp.concatenate([lo, hi], axis=1)


def _sigmoid(x):
    return 1.0 / (1.0 + jnp.exp(-x))


def _shift_rows(x, carry_row):
    rolled = pltpu.roll(x, 1, axis=0)
    row = lax.broadcasted_iota(jnp.int32, x.shape, 0)
    return jnp.where(row == 0, carry_row, rolled)


def _seg_ones(n, seg):
    r = lax.broadcasted_iota(jnp.int32, (n, n), 0) // seg
    c = lax.broadcasted_iota(jnp.int32, (n, n), 1) // seg
    return jnp.where(r == c, 1.0, 0.0).astype(BF16)


def _seg_sum(x, ones_bd):
    hi = x.astype(BF16)
    lo = (x - hi.astype(F32)).astype(BF16)
    return _dot(hi, ones_bd) + _dot(lo, ones_bd)


def _seg_sum1(x, ones_bd):
    return _dot(x.astype(BF16), ones_bd)


def _ada_kernel(cb_ref, w_ref, b_ref, o_ref):
    nb = cb_ref.shape[0]
    tn = w_ref.shape[1]
    for b in range(nb):
        cv = cb_ref[b]
        s = cv * _sigmoid(cv)
        cols = [jnp.sum(s * w_ref[:, j * LANES:(j + 1) * LANES], axis=0, keepdims=True)
                for j in range(tn // LANES)]
        o_ref[b:b + 1, :] = jnp.concatenate(cols, axis=1) + b_ref[...]


def _ada(c, w_ada, b_ada):
    nb, d = c.shape
    n_out = w_ada.shape[1]
    tn = ADA_COLS
    cb = jnp.broadcast_to(c[:, :, None], (nb, d, LANES))
    return pl.pallas_call(
        _ada_kernel,
        out_shape=jax.ShapeDtypeStruct((nb, n_out), F32),
        grid=(n_out // tn,),
        in_specs=[pl.BlockSpec((nb, d, LANES), lambda j: (0, 0, 0)),
                  pl.BlockSpec((d, tn), lambda j: (0, j)),
                  pl.BlockSpec((1, tn), lambda j: (0, j))],
        out_specs=pl.BlockSpec((nb, tn), lambda j: (0, j)),
        compiler_params=pltpu.CompilerParams(dimension_semantics=("arbitrary",),
                                             vmem_limit_bytes=VMEM_LIMIT),
        name="ada",
    )(cb, w_ada, b_ada.reshape(1, n_out))


def _inproj_kernel(x_ref, mod_ref, nw_ref, w_ref, h_ref, p_ref):
    @pl.when(pl.program_id(1) == 0)
    def _():
        rows = min(NORM_ROWS, x_ref.shape[0])
        gain = nw_ref[...] * (1.0 + mod_ref[0, 1:2, :])
        shift = mod_ref[0, 0:1, :]

        def norm_rows(r, carry):
            sl = pl.ds(pl.multiple_of(r * rows, rows), rows)
            x = x_ref[sl, :]
            ms = jnp.mean(x * x, axis=-1, keepdims=True)
            h = x * lax.rsqrt(ms + RMS_EPS) * gain + shift
            h_ref[sl, :] = h.astype(BF16)
            return carry

        lax.fori_loop(0, x_ref.shape[0] // rows, norm_rows, 0, unroll=4)

    p_ref[...] = _dot(h_ref[...], w_ref[...])


def _inproj(x2, mod3, norm_w, w_in_bf, t_len):
    n, d = x2.shape
    n_in = w_in_bf.shape[1]
    tm = min(INPROJ_ROWS, t_len)
    tn = INPROJ_COLS
    return pl.pallas_call(
        _inproj_kernel,
        out_shape=(jax.ShapeDtypeStruct((n, d), BF16), jax.ShapeDtypeStruct((n, n_in), F32)),
        grid=(n // tm, n_in // tn),
        in_specs=[pl.BlockSpec((tm, d), lambda i, j: (i, 0)),
                  pl.BlockSpec((1, 6, d), lambda i, j: (i * tm // t_len, 0, 0)),
                  pl.BlockSpec((1, d), lambda i, j: (0, 0)),
                  pl.BlockSpec((d, tn), lambda i, j: (0, j))],
        out_specs=(pl.BlockSpec((tm, d), lambda i, j: (i, 0)),
                   pl.BlockSpec((tm, tn), lambda i, j: (i, j))),
        compiler_params=pltpu.CompilerParams(dimension_semantics=("arbitrary", "arbitrary"),
                                             vmem_limit_bytes=VMEM_LIMIT),
        name="inproj",
    )(x2, mod3, norm_w.reshape(1, d), w_in_bf)


def _lora_kernel(h_ref, w1_ref, mu_ref, wd2_ref, wa2_ref, wg2_ref, w0_ref, a0_ref,
                 lw_ref, ag_ref, gg_ref, carry_ref, w12_ref):
    @pl.when(pl.program_id(1) == 0)
    def _():
        carry_ref[...] = jnp.zeros_like(carry_ref)

    @pl.when((pl.program_id(0) == 0) & (pl.program_id(1) == 0))
    def _():
        w1 = w1_ref[...]
        w12_ref[:, :LORA_PAD] = w1.astype(BF16)
        w12_ref[:, LORA_PAD:] = (w1 * mu_ref[...]).astype(BF16)

    p12 = _dot(h_ref[...], w12_ref[...])
    p1 = p12[:, :LORA_PAD]
    p2 = p12[:, LORA_PAD:]
    prev = _shift_rows(p2, carry_ref[...])
    carry_ref[...] = p2[p2.shape[0] - 1:, :]
    lo = p1 - p2 + prev
    dec = _dot(jnp.tanh(lo[:, slice(*LORA_DECAY)]).astype(BF16), wd2_ref[...].astype(BF16))
    z = -(w0_ref[...] + dec)
    softplus = jnp.maximum(z, 0.0) + jnp.log(1.0 + jnp.exp(-jnp.abs(z)))
    lw_ref[...] = -jnp.exp(-softplus - 0.5)
    ag_ref[...] = _sigmoid(a0_ref[...] + _dot(lo[:, slice(*LORA_AAA)].astype(BF16), wa2_ref[...].astype(BF16)))
    gg_ref[...] = _dot(_sigmoid(lo[:, slice(*LORA_GATE)]).astype(BF16), wg2_ref[...].astype(BF16))


def _lora(h, w1p, mup, wd2p, wa2p, wg2p, w0, a0, bsz, t_len):
    n, d = h.shape
    dr = w0.shape[-1]
    tm = min(LORA_ROWS, t_len)
    nt = t_len // tm
    row = lambda b, t: (b * nt + t, 0)
    const = lambda b, t: (0, 0)
    out = jax.ShapeDtypeStruct((n, dr), F32)
    return pl.pallas_call(
        _lora_kernel,
        out_shape=(out, out, out),
        grid=(bsz, nt),
        in_specs=[pl.BlockSpec((tm, d), row),
                  pl.BlockSpec((d, LORA_PAD), const),
                  pl.BlockSpec((d, LORA_PAD), const),
                  pl.BlockSpec(wd2p.shape, const),
                  pl.BlockSpec(wa2p.shape, const),
                  pl.BlockSpec(wg2p.shape, const),
                  pl.BlockSpec((1, dr), const),
                  pl.BlockSpec((1, dr), const)],
        out_specs=(pl.BlockSpec((tm, dr), row),) * 3,
        scratch_shapes=[pltpu.VMEM((1, LORA_PAD), F32), pltpu.VMEM((d, 2 * LORA_PAD), BF16)],
        compiler_params=pltpu.CompilerParams(dimension_semantics=("arbitrary", "arbitrary"),
                                             vmem_limit_bytes=VMEM_LIMIT),
        name="lora",
    )(h, w1p, mup, wd2p, wa2p, wg2p, w0.reshape(1, dr), a0.reshape(1, dr))


def _wkv_kernel(r_ref, k_ref, v_ref, lw_ref, ag_ref, gg_ref, mur_ref, muk_ref, muv_ref,
                kk_ref, ka_ref, rk_ref, lnw_ref, lnb_ref, y_ref, s_ref, carry_ref):
    lt = r_ref.shape[0]
    gl = GROUP_LANES
    c_len = WKV_CHUNK
    heads = gl // HEAD_DIM

    @pl.when(pl.program_id(2) == 0)
    def _():
        s_ref[...] = jnp.zeros_like(s_ref)
        carry_ref[...] = jnp.zeros_like(carry_ref)

    ones_bd = _seg_ones(gl, HEAD_DIM)

    def lerp(ref, mu_ref, idx):
        z = ref[...]
        prev = _shift_rows(z, carry_ref[idx:idx + 1, :])
        carry_ref[idx:idx + 1, :] = z[lt - 1:, :]
        return z + (prev - z) * mu_ref[...]

    r1 = lerp(r_ref, mur_ref, 0)
    k1 = lerp(k_ref, muk_ref, 1)
    v1 = lerp(v_ref, muv_ref, 2)
    ag = ag_ref[...]
    kk = k1 * kk_ref[...]
    kk = kk * lax.rsqrt(jnp.maximum(_seg_sum1(kk * kk, ones_bd), 1e-24))
    k2 = k1 * (1.0 + (ag - 1.0) * ka_ref[...])
    a_all = -kk
    b_all = kk * ag
    lw = lw_ref[...]

    rows = lax.broadcasted_iota(jnp.int32, (gl, gl), 0)
    cols = lax.broadcasted_iota(jnp.int32, (gl, gl), 1)
    same_head = (rows // c_len) == (cols // HEAD_DIM)
    ts = lax.broadcasted_iota(jnp.int32, (c_len, gl), 0)
    tj = lax.broadcasted_iota(jnp.int32, (c_len, gl), 1) % c_len
    tri_strict = ts > tj
    tri_incl = ts >= tj
    eye = jnp.where(ts == tj, 1.0, 0.0).astype(F32)
    cr = lax.broadcasted_iota(jnp.int32, (c_len, c_len), 0)
    cc = lax.broadcasted_iota(jnp.int32, (c_len, c_len), 1)
    cum_b = jnp.where(cr >= cc, 1.0, 0.0).astype(BF16)

    def bd(x):
        xb = x.astype(BF16)
        return jnp.where(same_head, jnp.concatenate([xb] * heads, axis=0), jnp.zeros((), BF16))

    def fold(x):
        out = x[0:c_len]
        for hh in range(1, heads):
            out = out + x[hh * c_len:(hh + 1) * c_len]
        return out

    nch = lt // c_len
    sls = [slice(c * c_len, (c + 1) * c_len) for c in range(nch)]

    def cumsum3(x):
        hi = x.astype(BF16)
        r = x - hi.astype(F32)
        mid = r.astype(BF16)
        lo = (r - mid.astype(F32)).astype(BF16)
        return _dot(cum_b, hi) + _dot(cum_b, mid) + _dot(cum_b, lo)

    wc, g_s, n_s, rp, y0, s_in = {}, {}, {}, {}, {}, {}
    state = [s_ref[...]]

    def chunk_parallel(chunks):
        lws = {c: lw[sls[c]] for c in chunks}
        cs = {c: cumsum3(lws[c]) for c in chunks}
        yield
        e_cs = {c: jnp.exp(cs[c]) for c in chunks}
        e_neg = {c: jnp.exp(-cs[c]) for c in chunks}
        for c in chunks:
            wc[c] = e_cs[c][c_len - 1:, :]
        at = {c: a_all[sls[c]] * jnp.exp(cs[c] - lws[c]) for c in chunks}
        rt = {c: r1[sls[c]] * e_cs[c] for c in chunks}
        bt = {c: b_all[sls[c]] * e_neg[c] for c in chunks}
        kt = {c: k2[sls[c]] * e_neg[c] for c in chunks}
        vc = {c: v1[sls[c]] for c in chunks}
        yield
        lhs = {c: jnp.concatenate([at[c], rt[c]], axis=0).astype(BF16) for c in chunks}
        g_b = {c: _dot_nt(lhs[c], bd(bt[c])) for c in chunks}
        yield
        g_k = {c: _dot_nt(lhs[c], bd(kt[c])) for c in chunks}
        yield
        n_ab = {c: jnp.where(tri_strict, g_b[c][:c_len], 0.0) for c in chunks}
        a_ak = {c: jnp.where(tri_strict, g_k[c][:c_len], 0.0) for c in chunks}
        a_rb = {c: jnp.where(tri_incl, g_b[c][c_len:], 0.0).astype(BF16) for c in chunks}
        a_rk = {c: jnp.where(tri_incl, g_k[c][c_len:], 0.0).astype(BF16) for c in chunks}

        p = {c: eye + jnp.where((ts // 2) == (tj // 2), n_ab[c], 0.0) for c in chunks}
        for half in (2, 4, 8, 16, 32):
            lower_left = (((ts // (2 * half)) == (tj // (2 * half)))
                          & ((ts % (2 * half)) >= half) & ((tj % (2 * half)) < half))
            n21_ta = {c: _dot(jnp.where(lower_left, n_ab[c], 0.0).astype(BF16), bd(p[c])) for c in chunks}
            yield
            p = {c: p[c] + _dot(p[c].astype(BF16), bd(n21_ta[c])) for c in chunks}
            yield

        z = {c: _dot(p[c].astype(BF16), jnp.concatenate([bd(a_ak[c]), bd(at[c])], axis=1))
             for c in chunks}
        yield
        ap = {c: z[c][:, gl:] for c in chunks}
        bd_v = {c: bd(vc[c]) for c in chunks}
        u0 = {c: _dot(z[c][:, :gl].astype(BF16), bd_v[c]) for c in chunks}
        yield
        w = {c: _dot(a_rb[c], jnp.concatenate([bd(ap[c]), bd(u0[c])], axis=1)) for c in chunks}
        yield
        for c in chunks:
            rp[c] = rt[c] + w[c][:, :gl]
            y0[c] = w[c][:, gl:] + _dot(a_rk[c], bd_v[c])
        yield
        bh = {c: (bt[c] * wc[c]).astype(BF16) for c in chunks}
        bk = {c: jnp.concatenate([bh[c], (kt[c] * wc[c]).astype(BF16)], axis=0) for c in chunks}
        for c in chunks:
            g_s[c] = jnp.where(same_head, _dot_tn(ap[c].astype(BF16), bh[c]), 0.0).astype(BF16)
        yield
        for c in chunks:
            n_s[c] = fold(jnp.where(
                same_head, _dot_tn(jnp.concatenate([u0[c], vc[c]], axis=0).astype(BF16), bk[c]), 0.0))
        yield

    def state_update(chunks):
        for c in chunks:
            s_in[c] = state[0].astype(BF16)
            state[0] = state[0] * wc[c] + _dot(s_in[c], g_s[c]) + n_s[c]
            yield

    def epilogue(chunks):
        rows = slice(chunks[0] * c_len, (chunks[-1] + 1) * c_len)
        y = jnp.concatenate([_dot_nt(rp[c].astype(BF16), bd(s_in[c])) + y0[c] for c in chunks], axis=0)
        yield
        inv_n = 1.0 / HEAD_DIM
        mean = _seg_sum(y, ones_bd) * inv_n
        yield
        yc = y - mean
        var = _seg_sum1(yc * yc, ones_bd) * inv_n
        yield
        yn = yc * lax.rsqrt(var + LNX_EPS) * lnw_ref[...] + lnb_ref[...]
        bonus = _seg_sum(r1[rows] * k2[rows] * rk_ref[...], ones_bd) * v1[rows]
        y_ref[rows, :] = ((yn + bonus) * gg_ref[rows, :]).astype(y_ref.dtype)
        yield

    def drain(gen):
        for _ in gen:
            pass

    def interleave(serial, other, every):
        for k, _ in enumerate(other, 1):
            if k % every == 0:
                next(serial, None)
        drain(serial)

    wave = max(nch // 2, 1)
    waves = [list(range(w0, min(w0 + wave, nch))) for w0 in range(0, nch, wave)]
    par_stages, epi_stages = 20, 4
    drain(chunk_parallel(waves[0]))
    for wi, chunks in enumerate(waves):
        others, n_other = [], 0
        if wi + 1 < len(waves):
            others.append(chunk_parallel(waves[wi + 1]))
            n_other += par_stages
        if wi >= 1:
            others.append(epilogue(waves[wi - 1]))
            n_other += epi_stages
        interleave(state_update(chunks), itertools.chain(*others), max(n_other // len(chunks), 1))
    drain(epilogue(waves[-1]))
    s_ref[...] = state[0]


def _wkv(proj, lw, ag, gg, mu_r, mu_k, mu_v, k_k, k_a, r_k, lnx_w, lnx_b, bsz, t_len):
    n = proj.shape[0]
    dr = lw.shape[1]
    gl = GROUP_LANES
    ng = dr // gl
    lt = min(WKV_ROWS, t_len)
    nt = t_len // lt
    row = lambda off: (lambda b, g, t: (b * nt + t, off + g))
    par = lambda b, g, t: (0, g)
    vec = lambda a: a.reshape(1, dr)
    return pl.pallas_call(
        _wkv_kernel,
        out_shape=jax.ShapeDtypeStruct((n, dr), BF16),
        grid=(bsz, ng, nt),
        in_specs=[pl.BlockSpec((lt, gl), row(0)),
                  pl.BlockSpec((lt, gl), row(ng)),
                  pl.BlockSpec((lt, gl), row(2 * ng)),
                  pl.BlockSpec((lt, gl), row(0)),
                  pl.BlockSpec((lt, gl), row(0)),
                  pl.BlockSpec((lt, gl), row(0))] + [pl.BlockSpec((1, gl), par)] * 8,
        out_specs=pl.BlockSpec((lt, gl), row(0)),
        scratch_shapes=[pltpu.VMEM((HEAD_DIM, gl), F32), pltpu.VMEM((8, gl), F32)],
        compiler_params=pltpu.CompilerParams(
            dimension_semantics=("arbitrary", "arbitrary", "arbitrary"),
            vmem_limit_bytes=VMEM_LIMIT),
        name="wkv",
    )(proj, proj, proj, lw, ag, gg, vec(mu_r), vec(mu_k), vec(mu_v), vec(k_k), vec(k_a),
      vec(r_k), vec(lnx_w), vec(lnx_b))


def _outproj_kernel(yr_ref, u_ref, gb_ref, gc_ref, cw_ref, cnw_ref, wo_ref, x_ref, mod_ref,
                    n2w_ref, wr_ref, br_ref, x1_ref, h2_ref, meta_ref, cnt_ref,
                    ccarry_ref, cnt_acc_ref, wrs_ref):
    tm = x_ref.shape[0]
    dc = u_ref.shape[1]

    @pl.when(pl.program_id(1) == 0)
    def _():
        ccarry_ref[...] = jnp.zeros_like(ccarry_ref)

    @pl.when((pl.program_id(0) == 0) & (pl.program_id(1) == 0))
    def _():
        cnt_acc_ref[...] = jnp.zeros_like(cnt_acc_ref)
        wr_hi, wr_lo = _split_bf16(wr_ref[...])
        wrs_ref[0] = wr_hi
        wrs_ref[1] = wr_lo

    pre = gc_ref[...] * u_ref[...]
    prev1 = _shift_rows(pre, ccarry_ref[0:1, :])
    prev2 = _shift_rows(prev1, ccarry_ref[1:2, :])
    ccarry_ref[0:1, :] = pre[tm - 1:, :]
    ccarry_ref[1:2, :] = pre[tm - 2:tm - 1, :]
    yc = gb_ref[...] * (prev2 * cw_ref[0:1, :] + prev1 * cw_ref[1:2, :] + pre * cw_ref[2:3, :])
    ones_bd = _seg_ones(GROUP_LANES, CONV_GROUP_DIM)
    ycs = []
    for s in range(dc // GROUP_LANES):
        ys = yc[:, s * GROUP_LANES:(s + 1) * GROUP_LANES]
        ms = _seg_sum1(ys * ys, ones_bd) * (1.0 / CONV_GROUP_DIM)
        ycs.append(ys * lax.rsqrt(ms + RMS_EPS))
    y_conv = (jnp.concatenate(ycs, axis=1) * cnw_ref[...]).astype(BF16)

    dr = yr_ref.shape[1]
    y_mix = _dot(yr_ref[...], wo_ref[0:dr, :]) + _dot(y_conv, wo_ref[dr:, :])
    x1 = x_ref[...] + mod_ref[0, 2:3, :] * y_mix
    x1_ref[...] = x1
    ms = jnp.mean(x1 * x1, axis=-1, keepdims=True)
    gain2 = n2w_ref[...] * (1.0 + mod_ref[0, 4:5, :])
    h2 = x1 * lax.rsqrt(ms + RMS_EPS) * gain2 + mod_ref[0, 3:4, :]
    h2_hi = h2.astype(BF16)
    h2_hi32 = h2_hi.astype(F32)
    h2_ref[...] = _pack_rounded_pair(h2_hi32)
    h2_lo = (h2 - h2_hi32).astype(BF16)

    logits = _dot(h2_hi, wrs_ref[0]) + _dot(h2_lo, wrs_ref[0]) + _dot(h2_hi, wrs_ref[1]) + br_ref[...]
    lane = lax.broadcasted_iota(jnp.int32, logits.shape, 1)
    neg = jnp.float32(-jnp.inf)
    big = jnp.int32(1 << 20)
    glog = jnp.where(lane < N_ROUTE_GROUPS, logits, neg)
    gmax = jnp.max(glog, axis=-1, keepdims=True)
    grp = jnp.min(jnp.where(glog == gmax, lane, big), axis=-1, keepdims=True)
    p_grp = 1.0 / jnp.sum(jnp.exp(glog - gmax), axis=-1, keepdims=True)
    e_lane = lane - N_ROUTE_GROUPS
    in_grp = (e_lane >= grp * EXPERTS_PER_GROUP) & (e_lane < (grp + 1) * EXPERTS_PER_GROUP)
    elog = jnp.where(in_grp, logits, neg)
    m1 = jnp.max(elog, axis=-1, keepdims=True)
    i1 = jnp.min(jnp.where(elog == m1, lane, big), axis=-1, keepdims=True)
    elog2 = jnp.where(lane == i1, neg, elog)
    m2 = jnp.max(elog2, axis=-1, keepdims=True)
    i2 = jnp.min(jnp.where(elog2 == m2, lane, big), axis=-1, keepdims=True)
    t2 = jnp.exp(m2 - m1)
    w1 = p_grp / (1.0 + t2)
    w2 = p_grp * t2 / (1.0 + t2)
    e1 = i1 - N_ROUTE_GROUPS
    e2 = i2 - N_ROUTE_GROUPS

    oh1 = lane == e1
    oh2 = lane == e2
    oh = jnp.where(oh1 | oh2, 1.0, 0.0)
    tr = lax.broadcasted_iota(jnp.int32, (tm, tm), 0)
    tc = lax.broadcasted_iota(jnp.int32, (tm, tm), 1)
    lower = jnp.where(tr > tc, 1.0, 0.0).astype(BF16)
    before = _dot(lower, oh.astype(BF16)) + cnt_acc_ref[...]
    rank1 = jnp.sum(jnp.where(oh1, before, 0.0), axis=-1, keepdims=True)
    rank2 = jnp.sum(jnp.where(oh2, before, 0.0), axis=-1, keepdims=True)
    cnt = cnt_acc_ref[...] + jnp.sum(oh, axis=0, keepdims=True)
    cnt_acc_ref[...] = cnt
    cnt_ref[...] = jnp.broadcast_to(cnt, cnt_ref.shape)

    meta = jnp.where(lane == 0, e1.astype(F32), 0.0)
    meta = jnp.where(lane == 1, e2.astype(F32), meta)
    meta = jnp.where(lane == 2, w1, meta)
    meta = jnp.where(lane == 3, w2, meta)
    meta = jnp.where(lane == 4, rank1, meta)
    meta = jnp.where(lane == 5, rank2, meta)
    meta_ref[...] = meta


def _outproj(yr, proj, conv_w, conv_norm_w, w_out_bf, x2, mod3, norm2_w, wr, br, bsz, t_len):
    n, d = x2.shape
    dr = yr.shape[1]
    dc = conv_w.shape[1]
    tm = min(OUTPROJ_ROWS, t_len)
    nt = t_len // tm
    nseg = dr // dc
    row = lambda off: (lambda b, t: (b * nt + t, off))
    const = lambda b, t: (0, 0)
    return pl.pallas_call(
        _outproj_kernel,
        out_shape=(jax.ShapeDtypeStruct((n, d), F32), jax.ShapeDtypeStruct((n, d // 2), jnp.uint32),
                   jax.ShapeDtypeStruct((n, LANES), F32), jax.ShapeDtypeStruct((8, LANES), F32)),
        grid=(bsz, nt),
        in_specs=[pl.BlockSpec((tm, dr), row(0)),
                  pl.BlockSpec((tm, dc), row(3 * nseg)),
                  pl.BlockSpec((tm, dc), row(4 * nseg)),
                  pl.BlockSpec((tm, dc), row(5 * nseg)),
                  pl.BlockSpec((3, dc), const),
                  pl.BlockSpec((1, dc), const),
                  pl.BlockSpec((dr + dc, d), const),
                  pl.BlockSpec((tm, d), row(0)),
                  pl.BlockSpec((1, 6, d), lambda b, t: (b, 0, 0)),
                  pl.BlockSpec((1, d), const),
                  pl.BlockSpec((d, LANES), const),
                  pl.BlockSpec((1, LANES), const)],
        out_specs=(pl.BlockSpec((tm, d), row(0)), pl.BlockSpec((tm, d // 2), row(0)),
                   pl.BlockSpec((tm, LANES), row(0)), pl.BlockSpec((8, LANES), const)),
        scratch_shapes=[pltpu.VMEM((8, dc), F32), pltpu.VMEM((1, LANES), F32),
                        pltpu.VMEM((2, d, LANES), BF16)],
        compiler_params=pltpu.CompilerParams(dimension_semantics=("arbitrary", "arbitrary"),
                                             vmem_limit_bytes=VMEM_LIMIT),
        name="outproj",
    )(yr, proj, proj, proj, conv_w, conv_norm_w.reshape(1, dc), w_out_bf, x2, mod3,
      norm2_w.reshape(1, d), wr, br)


def _dispatch_kernel(zoff_ref, dest_hbm, h2_hbm, xs_hbm, idx_smem, tile_ref, zero_ref, zblk_ref,
                     sem_idx, sem_tile, sem_rows, sem_zero, sem_tail):
    i = pl.program_id(0)
    nsteps = pl.num_programs(0)
    th = idx_smem.shape[1] // 2
    slot = i % 2

    def idx_copy(step, sl):
        return pltpu.make_async_copy(dest_hbm.at[step], idx_smem.at[sl], sem_idx.at[sl])

    def tile_copy(step, sl):
        rows = pl.ds(pl.multiple_of(step * th, th), th)
        return pltpu.make_async_copy(h2_hbm.at[rows], tile_ref.at[sl], sem_tile.at[sl])

    def rows_wait(sl):
        pltpu.make_async_copy(xs_hbm.at[pl.ds(0, 2 * th)], xs_hbm.at[pl.ds(0, 2 * th)],
                              sem_rows.at[sl]).wait()

    nblk = xs_hbm.shape[0] // ROUTE_BLOCK

    def tail_copy(b):
        start = pl.multiple_of(b * ROUTE_BLOCK, ROUTE_BLOCK)
        return pltpu.make_async_copy(zblk_ref, xs_hbm.at[pl.ds(start, ROUTE_BLOCK)], sem_tail)

    @pl.when(i == 0)
    def _():
        idx_copy(0, 0).start()
        tile_copy(0, 0).start()
        zero_ref[...] = jnp.zeros_like(zero_ref)

        def zero_expert(e, carry):
            off = zoff_ref[e]

            def zero_row(j, c):
                pltpu.make_async_copy(zero_ref, xs_hbm.at[pl.ds(off + j, 1)], sem_zero).start()
                return c

            return lax.fori_loop(0, zoff_ref[N_EXPERTS + e], zero_row, carry)

        lax.fori_loop(0, N_EXPERTS, zero_expert, 0)
        total = zoff_ref[2 * N_EXPERTS]
        bulk = pl.multiple_of((total // 8) * 8, 8)

        @pl.when(bulk > 0)
        def _():
            pltpu.make_async_copy(xs_hbm.at[pl.ds(0, bulk)], xs_hbm.at[pl.ds(0, bulk)], sem_zero).wait()

        def wait_row(j, c):
            pltpu.make_async_copy(zero_ref, xs_hbm.at[pl.ds(0, 1)], sem_zero).wait()
            return c

        lax.fori_loop(0, total - bulk, wait_row, 0)

        zblk_ref[...] = jnp.zeros_like(zblk_ref)

        def tail_start(b, c):
            tail_copy(b).start()
            return c

        lax.fori_loop(zoff_ref[2 * N_EXPERTS + 1], nblk, tail_start, 0)

    idx_copy(i, slot).wait()
    tile_copy(i, slot).wait()

    @pl.when(i > 0)
    def _():
        rows_wait(1 - slot)

    @pl.when(i + 1 < nsteps)
    def _():
        idx_copy(i + 1, 1 - slot).start()
        tile_copy(i + 1, 1 - slot).start()

    for p in range(2):
        @pl.when(slot == p)
        def _(p=p):
            for t in range(th):
                for s in range(2):
                    d = idx_smem[p, 2 * t + s]
                    pltpu.make_async_copy(tile_ref.at[p, pl.ds(t, 1)], xs_hbm.at[pl.ds(d, 1)],
                                          sem_rows.at[p]).start(priority=s)

    @pl.when(i == nsteps - 1)
    def _():
        rows_wait(slot)

        def tail_wait(b, c):
            tail_copy(b).wait()
            return c

        lax.fori_loop(zoff_ref[2 * N_EXPERTS + 1], nblk, tail_wait, 0)


def _dispatch(dest, zero_off, h2, n_rows):
    n, d = h2.shape
    th = min(DISPATCH_ROWS, n)
    return pl.pallas_call(
        _dispatch_kernel,
        out_shape=jax.ShapeDtypeStruct((n_rows, d), h2.dtype),
        grid_spec=pltpu.PrefetchScalarGridSpec(
            num_scalar_prefetch=1,
            grid=(n // th,),
            in_specs=[pl.BlockSpec(memory_space=pl.ANY), pl.BlockSpec(memory_space=pl.ANY)],
            out_specs=pl.BlockSpec(memory_space=pl.ANY),
            scratch_shapes=[pltpu.SMEM((2, 2 * th), jnp.int32),
                            pltpu.VMEM((2, th, d), h2.dtype),
                            pltpu.VMEM((1, d), h2.dtype),
                            pltpu.VMEM((ROUTE_BLOCK, d), h2.dtype),
                            pltpu.SemaphoreType.DMA((2,)), pltpu.SemaphoreType.DMA((2,)),
                            pltpu.SemaphoreType.DMA((2,)),
                            pltpu.SemaphoreType.DMA, pltpu.SemaphoreType.DMA]),
        compiler_params=pltpu.CompilerParams(dimension_semantics=("arbitrary",),
                                             vmem_limit_bytes=VMEM_LIMIT),
        name="dispatch",
    )(zero_off, dest.reshape(n // th, 2 * th), h2)


def _expert_kernel(sched_ref, na_ref, xs_ref, wg_hbm, wu_hbm, wd_hbm, ys_ref,
                   wg_buf, wu_buf, wd_buf, sem):
    i = pl.program_id(0)

    def weight_copies(e, slot):
        return (pltpu.make_async_copy(wg_hbm.at[e], wg_buf.at[slot], sem.at[slot, 0]),
                pltpu.make_async_copy(wu_hbm.at[e], wu_buf.at[slot], sem.at[slot, 1]),
                pltpu.make_async_copy(wd_hbm.at[e], wd_buf.at[slot], sem.at[slot, 2]))

    @pl.when(i < na_ref[0])
    def _():
        e = sched_ref[0, i]
        slot = sched_ref[1, i]

        @pl.when(i == 0)
        def _():
            for cp in weight_copies(e, slot):
                cp.start()

        @pl.when(sched_ref[2, i] == 1)
        def _():
            for cp in weight_copies(e, slot):
                cp.wait()
            nxt = sched_ref[3, i]

            @pl.when(nxt >= 0)
            def _():
                for cp in weight_copies(nxt, 1 - slot):
                    cp.start()

        x = _unpack_bf16_pair(xs_ref[...]).astype(BF16)
        g = _dot(x, wg_buf[slot].astype(BF16))
        u = _dot(x, wu_buf[slot].astype(BF16))
        hid = (g * _sigmoid(g) * u).astype(BF16)
        ys_ref[...] = _pack_bf16_pair(_dot(hid, wd_buf[slot].astype(BF16)))

    @pl.when(i >= na_ref[0])
    def _():
        ys_ref[...] = jnp.zeros_like(ys_ref)


def _experts(block_e, n_active, xs, exp_gate, exp_up, exp_down):
    n_rows, dp = xs.shape
    _, d, de = exp_gate.shape
    nblk = n_rows // ROUTE_BLOCK
    idx = jnp.arange(nblk, dtype=jnp.int32)
    valid = idx < n_active[0]
    first = valid & ((idx == 0) | (block_e != jnp.roll(block_e, 1)))
    slot = jnp.maximum(jnp.cumsum(first.astype(jnp.int32)) - 1, 0) % 2
    first_at = jnp.where(first, idx, nblk)
    next_first = jnp.concatenate([lax.cummin(first_at, reverse=True)[1:], jnp.full((1,), nblk, jnp.int32)])
    nxt = jnp.where(next_first < nblk, block_e[jnp.minimum(next_first, nblk - 1)], -1)
    sched = jnp.stack([block_e, slot, first.astype(jnp.int32), nxt]).astype(jnp.int32)

    blk_row = lambda i, sc, na: (jnp.minimum(i, jnp.maximum(na[0] - 1, 0)), 0)
    return pl.pallas_call(
        _expert_kernel,
        out_shape=jax.ShapeDtypeStruct((n_rows, dp), jnp.uint32),
        grid_spec=pltpu.PrefetchScalarGridSpec(
            num_scalar_prefetch=2,
            grid=(nblk,),
            in_specs=[pl.BlockSpec((ROUTE_BLOCK, dp), blk_row),
                      pl.BlockSpec(memory_space=pl.ANY),
                      pl.BlockSpec(memory_space=pl.ANY),
                      pl.BlockSpec(memory_space=pl.ANY)],
            out_specs=pl.BlockSpec((ROUTE_BLOCK, dp), lambda i, sc, na: (i, 0)),
            scratch_shapes=[pltpu.VMEM((2, d, de), exp_gate.dtype),
                            pltpu.VMEM((2, d, de), exp_up.dtype),
                            pltpu.VMEM((2, de, d), exp_down.dtype),
                            pltpu.SemaphoreType.DMA((2, 3))]),
        compiler_params=pltpu.CompilerParams(dimension_semantics=("arbitrary",),
                                             vmem_limit_bytes=VMEM_LIMIT),
        name="experts",
    )(sched, n_active, xs, exp_gate, exp_up, exp_down)


def _combine_kernel(dest_hbm, ys_hbm, x1_ref, meta_ref, mod_ref, fw_ref, o_ref,
                    idx_smem, buf_ref, sem_idx, sem_rows):
    i = pl.program_id(0)
    nsteps = pl.num_programs(0)
    tj = x1_ref.shape[0]
    slot = i % 2

    def idx_copy(step, sl):
        return pltpu.make_async_copy(dest_hbm.at[step], idx_smem.at[sl], sem_idx.at[sl])

    def gather(sl):
        for t in range(tj):
            for s in range(2):
                d = idx_smem[sl, 2 * t + s]
                pltpu.make_async_copy(ys_hbm.at[pl.ds(d, 1)], buf_ref.at[sl, s, pl.ds(t, 1)],
                                      sem_rows.at[sl]).start(priority=s)

    @pl.when(i == 0)
    def _():
        first = idx_copy(0, 0)
        first.start()
        first.wait()
        gather(0)

        @pl.when(nsteps > 1)
        def _():
            idx_copy(1, 1).start()

    for p in range(2):
        @pl.when((i + 1 < nsteps) & (slot == 1 - p))
        def _(p=p):
            idx_copy(i + 1, p).wait()

            @pl.when(i + 2 < nsteps)
            def _():
                idx_copy(i + 2, 1 - p).start()

            gather(p)

    pltpu.make_async_copy(buf_ref.at[slot], buf_ref.at[slot], sem_rows.at[slot]).wait()

    meta = meta_ref[...]
    y = (meta[:, 2:3] * _unpack_bf16_pair(buf_ref[slot, 0])
         + meta[:, 3:4] * _unpack_bf16_pair(buf_ref[slot, 1]))
    x2 = x1_ref[...] + mod_ref[0, 5:6, :] * y
    ms = jnp.mean(x2 * x2, axis=-1, keepdims=True)
    o_ref[...] = x2 * lax.rsqrt(ms + RMS_EPS) * fw_ref[...]


def _combine(dest, ys, x1, meta, mod3, final_w, t_len):
    n, d = x1.shape
    tj = min(COMBINE_ROWS, t_len)
    return pl.pallas_call(
        _combine_kernel,
        out_shape=jax.ShapeDtypeStruct((n, d), F32),
        grid=(n // tj,),
        in_specs=[pl.BlockSpec(memory_space=pl.ANY),
                  pl.BlockSpec(memory_space=pl.ANY),
                  pl.BlockSpec((tj, d), lambda i: (i, 0)),
                  pl.BlockSpec((tj, LANES), lambda i: (i, 0)),
                  pl.BlockSpec((1, 6, d), lambda i: (i * tj // t_len, 0, 0)),
                  pl.BlockSpec((1, d), lambda i: (0, 0))],
        out_specs=pl.BlockSpec((tj, d), lambda i: (i, 0)),
        scratch_shapes=[pltpu.SMEM((2, 2 * tj), jnp.int32), pltpu.VMEM((2, 2, tj, d // 2), jnp.uint32),
                        pltpu.SemaphoreType.DMA((2,)), pltpu.SemaphoreType.DMA((2,))],
        compiler_params=pltpu.CompilerParams(dimension_semantics=("arbitrary",),
                                             vmem_limit_bytes=VMEM_LIMIT),
        name="combine",
    )(dest.reshape(n // tj, 2 * tj), ys, x1, meta, mod3, final_w.reshape(1, d))


def _pad_cols(w, width):
    return jnp.pad(w, ((0, 0), (0, width - w.shape[1])))


def _pad_rows(w, height):
    return jnp.pad(w, ((0, height - w.shape[0]), (0, 0)))


def _layer(x2, c, bsz, t_len, w_ada, b_ada, norm1_w, w_in, mu_r, mu_k, mu_v, mu_w, mu_a, mu_g, w0,
           w_decay1, w_decay2, a0, w_aaa1, w_aaa2, w_gate1, w_gate2, k_k, k_a, r_k, lnx_w, lnx_b,
           conv_w, conv_norm_w, w_out, norm2_w, w_grp, b_grp, w_exp, b_exp, exp_gate, exp_up,
           exp_down, final_w):
    n, d = x2.shape
    dr = w0.shape[0]
    widths = [hi - lo for lo, hi in (LORA_DECAY, LORA_AAA, LORA_GATE)]
    assert all(w.shape[1] <= n_ for w, n_ in zip((w_decay1, w_aaa1, w_gate1), widths))
    assert dr % GROUP_LANES == 0 and conv_w.shape[1] == dr and w_in.shape[1] == 6 * dr
    assert t_len % WKV_CHUNK == 0 and w_grp.shape[1] == N_ROUTE_GROUPS and w_exp.shape[1] == N_EXPERTS

    mod3 = _ada(c, w_ada, b_ada).reshape(bsz, 6, d)
    h, proj = _inproj(x2, mod3, norm1_w, w_in.astype(BF16), t_len)

    w1p = jnp.concatenate([_pad_cols(w, n_) for w, n_ in zip((w_decay1, w_aaa1, w_gate1), widths)], axis=1)
    mup = jnp.concatenate([jnp.broadcast_to(mu[:, None], (d, n_)) for mu, n_ in zip((mu_w, mu_a, mu_g), widths)],
                          axis=1)
    wd2p, wa2p, wg2p = (_pad_rows(w, n_) for w, n_ in zip((w_decay2, w_aaa2, w_gate2), widths))
    lw, ag, gg = _lora(h, w1p, mup, wd2p, wa2p, wg2p, w0, a0, bsz, t_len)

    yr = _wkv(proj, lw, ag, gg, mu_r, mu_k, mu_v, k_k, k_a, r_k.reshape(dr), lnx_w, lnx_b, bsz, t_len)

    wr = _pad_cols(jnp.concatenate([w_grp, w_exp], axis=1), LANES)
    br = _pad_cols(jnp.concatenate([b_grp, b_exp]).reshape(1, -1), LANES)
    x1, h2, meta, cnt = _outproj(yr, proj, conv_w, conv_norm_w, w_out.astype(BF16), x2, mod3,
                                 norm2_w, wr, br, bsz, t_len)

    counts = cnt[0, :N_EXPERTS].astype(jnp.int32)
    padded = (counts + ROUTE_BLOCK - 1) // ROUTE_BLOCK * ROUTE_BLOCK
    pad_ends = jnp.cumsum(padded)
    pad_starts = pad_ends - padded
    top_e = meta[:, 0:2].astype(jnp.int32)
    rank = meta[:, 4:6].astype(jnp.int32)
    one_hot = top_e[:, :, None] == jnp.arange(N_EXPERTS, dtype=jnp.int32)
    dest = (jnp.sum(jnp.where(one_hot, pad_starts, 0), axis=-1) + rank).reshape(-1)
    nblk = (2 * n) // ROUTE_BLOCK + N_EXPERTS
    n_rows = nblk * ROUTE_BLOCK
    block_start = jnp.arange(nblk, dtype=jnp.int32) * ROUTE_BLOCK
    block_e = jnp.minimum(jnp.sum(pad_ends[None, :] <= block_start[:, None], axis=1),
                          N_EXPERTS - 1).astype(jnp.int32)
    n_active = (pad_ends[-1:] // ROUTE_BLOCK).astype(jnp.int32)

    pad_len = padded - counts
    zero_off = jnp.concatenate([pad_starts + counts, pad_len, jnp.sum(pad_len, keepdims=True),
                                n_active]).astype(jnp.int32)
    xs = _dispatch(dest, zero_off, h2, n_rows)
    ys = _experts(block_e, n_active, xs, exp_gate, exp_up, exp_down)
    return _combine(dest, ys, x1, meta, mod3, final_w, t_len)


def kernel(x, c, w_ada, b_ada, norm1_w, w_in, mu_r, mu_k, mu_v, mu_w, mu_a, mu_g, w0, w_decay1, w_decay2, a0, w_aaa1, w_aaa2, w_gate1, w_gate2, k_k, k_a, r_k, lnx_w, lnx_b, conv_w, conv_norm_w, w_out, norm2_w, w_grp, b_grp, w_exp, b_exp, exp_gate, exp_up, exp_down, final_w):
    bsz, t_len, d = x.shape
    assert w_ada.shape[0] == 1, "single-layer block"
    out = _layer(x.reshape(bsz * t_len, d), c, bsz, t_len, w_ada[0], b_ada[0], norm1_w[0], w_in[0],
                 mu_r[0], mu_k[0], mu_v[0], mu_w[0], mu_a[0], mu_g[0], w0[0], w_decay1[0], w_decay2[0],
                 a0[0], w_aaa1[0], w_aaa2[0], w_gate1[0], w_gate2[0], k_k[0], k_a[0], r_k[0], lnx_w[0],
                 lnx_b[0], conv_w[0], conv_norm_w[0], w_out[0], norm2_w[0], w_grp[0], b_grp[0],
                 w_exp[0], b_exp[0], exp_gate[0], exp_up[0], exp_down[0], final_w)
    return out.reshape(bsz, t_len, d)
```

```python
import itertools

import jax
import jax.numpy as jnp
from jax import lax
from jax.experimental import pallas as pl
from jax.experimental.pallas import tpu as pltpu

F32 = jnp.float32
BF16 = jnp.bfloat16

HEAD_DIM = 64
WKV_CHUNK = 64
GROUP_LANES = 256
N_ROUTE_GROUPS = 4
EXPERTS_PER_GROUP = 8
N_EXPERTS = N_ROUTE_GROUPS * EXPERTS_PER_GROUP
CONV_GROUP_DIM = 64
RMS_EPS = 1e-6
LNX_EPS = 64e-5
LANES = 128
LORA_DECAY, LORA_AAA, LORA_GATE = (0, 128), (128, 256), (256, 512)
LORA_PAD = LORA_GATE[1]
ADA_COLS = 1024
INPROJ_ROWS, INPROJ_COLS = 1024, 1536
LORA_ROWS = 1024
WKV_ROWS = 1024
OUTPROJ_ROWS = 512
DISPATCH_ROWS = 256
COMBINE_ROWS = 256
ROUTE_BLOCK = 256
NORM_ROWS = 32
VMEM_LIMIT = 56 * 1024 * 1024


def _dot(a, b, precision=None):
    return jnp.dot(a, b, preferred_element_type=F32, precision=precision)


def _dot_nt(a, b):
    return lax.dot_general(a, b, (((1,), (1,)), ((), ())), preferred_element_type=F32)


def _dot_tn(a, b):
    return lax.dot_general(a, b, (((0,), (0,)), ((), ())), preferred_element_type=F32)


def _split_bf16(x):
    hi = x.astype(BF16)
    return hi, (x - hi.astype(F32)).astype(BF16)


def _pack_rounded_pair(xr):
    c = xr.shape[1] // 2
    lo = pltpu.bitcast(xr[:, :c], jnp.uint32)
    hi = pltpu.bitcast(xr[:, c:], jnp.uint32)
    return (lo >> 16) | (hi & jnp.uint32(0xFFFF0000))


def _pack_bf16_pair(x):
    return _pack_rounded_pair(x.astype(BF16).astype(F32))


def _unpack_bf16_pair(w):
    lo = pltpu.bitcast(w << 16, F32)
    hi = pltpu.bitcast(w & jnp.uint32(0xFFFF0000), F32)
    return jnp.concatenate([lo, hi], axis=1)


def _sigmoid(x):
    return 1.0 / (1.0 + jnp.exp(-x))


def _shift_rows(x, carry_row):
    rolled = pltpu.roll(x, 1, axis=0)
    row = lax.broadcasted_iota(jnp.int32, x.shape, 0)
    return jnp.where(row == 0, carry_row, rolled)


def _seg_ones(n, seg):
    r = lax.broadcasted_iota(jnp.int32, (n, n), 0) // seg
    c = lax.broadcasted_iota(jnp.int32, (n, n), 1) // seg
    return jnp.where(r == c, 1.0, 0.0).astype(BF16)


def _seg_sum1(x, ones_bd):
    return _dot(x.astype(BF16), ones_bd)


def _ada_kernel(cb_ref, w_ref, b_ref, o_ref):
    nb = cb_ref.shape[0]
    tn = w_ref.shape[1]
    for b in range(nb):
        cv = cb_ref[b]
        s = cv * _sigmoid(cv)
        cols = [jnp.sum(s * w_ref[:, j * LANES:(j + 1) * LANES], axis=0, keepdims=True)
                for j in range(tn // LANES)]
        o_ref[b:b + 1, :] = jnp.concatenate(cols, axis=1) + b_ref[...]


def _ada(c, w_ada, b_ada):
    nb, d = c.shape
    n_out = w_ada.shape[1]
    tn = ADA_COLS
    cb = jnp.broadcast_to(c[:, :, None], (nb, d, LANES))
    return pl.pallas_call(
        _ada_kernel,
        out_shape=jax.ShapeDtypeStruct((nb, n_out), F32),
        grid=(n_out // tn,),
        in_specs=[pl.BlockSpec((nb, d, LANES), lambda j: (0, 0, 0)),
                  pl.BlockSpec((d, tn), lambda j: (0, j)),
                  pl.BlockSpec((1, tn), lambda j: (0, j))],
        out_specs=pl.BlockSpec((nb, tn), lambda j: (0, j)),
        compiler_params=pltpu.CompilerParams(dimension_semantics=("arbitrary",),
                                             vmem_limit_bytes=VMEM_LIMIT),
        name="ada",
    )(cb, w_ada, b_ada.reshape(1, n_out))


def _inproj_kernel(x_ref, mod_ref, nw_ref, w_ref, h_ref, p_ref):
    @pl.when(pl.program_id(1) == 0)
    def _():
        rows = min(NORM_ROWS, x_ref.shape[0])
        gain = nw_ref[...] * (1.0 + mod_ref[0, 1:2, :])
        shift = mod_ref[0, 0:1, :]

        def norm_rows(r, carry):
            sl = pl.ds(pl.multiple_of(r * rows, rows), rows)
            x = x_ref[sl, :]
            ms = jnp.mean(x * x, axis=-1, keepdims=True)
            h = x * lax.rsqrt(ms + RMS_EPS) * gain + shift
            h_ref[sl, :] = h.astype(BF16)
            return carry

        lax.fori_loop(0, x_ref.shape[0] // rows, norm_rows, 0, unroll=4)

    p_ref[...] = _dot(h_ref[...], w_ref[...])


def _inproj(x2, mod3, norm_w, w_in_bf, t_len):
    n, d = x2.shape
    n_in = w_in_bf.shape[1]
    tm = min(INPROJ_ROWS, t_len)
    tn = INPROJ_COLS
    return pl.pallas_call(
        _inproj_kernel,
        out_shape=(jax.ShapeDtypeStruct((n, d), BF16), jax.ShapeDtypeStruct((n, n_in), F32)),
        grid=(n // tm, n_in // tn),
        in_specs=[pl.BlockSpec((tm, d), lambda i, j: (i, 0)),
                  pl.BlockSpec((1, 6, d), lambda i, j: (i * tm // t_len, 0, 0)),
                  pl.BlockSpec((1, d), lambda i, j: (0, 0)),
                  pl.BlockSpec((d, tn), lambda i, j: (0, j))],
        out_specs=(pl.BlockSpec((tm, d), lambda i, j: (i, 0)),
                   pl.BlockSpec((tm, tn), lambda i, j: (i, j))),
        compiler_params=pltpu.CompilerParams(dimension_semantics=("arbitrary", "arbitrary"),
                                             vmem_limit_bytes=VMEM_LIMIT),
        name="inproj",
    )(x2, mod3, norm_w.reshape(1, d), w_in_bf)


def _lora_kernel(h_ref, w1_ref, mu_ref, wd2_ref, wa2_ref, wg2_ref, w0_ref, a0_ref,
                 lw_ref, ag_ref, gg_ref, carry_ref, w12_ref):
    @pl.when(pl.program_id(1) == 0)
    def _():
        carry_ref[...] = jnp.zeros_like(carry_ref)

    @pl.when((pl.program_id(0) == 0) & (pl.program_id(1) == 0))
    def _():
        w1 = w1_ref[...]
        w12_ref[:, :LORA_PAD] = w1.astype(BF16)
        w12_ref[:, LORA_PAD:] = (w1 * mu_ref[...]).astype(BF16)

    p12 = _dot(h_ref[...], w12_ref[...])
    p1 = p12[:, :LORA_PAD]
    p2 = p12[:, LORA_PAD:]
    prev = _shift_rows(p2, carry_ref[...])
    carry_ref[...] = p2[p2.shape[0] - 1:, :]
    lo = p1 - p2 + prev
    dec = _dot(jnp.tanh(lo[:, slice(*LORA_DECAY)]).astype(BF16), wd2_ref[...].astype(BF16))
    z = -(w0_ref[...] + dec)
    softplus = jnp.maximum(z, 0.0) + jnp.log(1.0 + jnp.exp(-jnp.abs(z)))
    lw_ref[...] = -jnp.exp(-softplus - 0.5)
    ag_ref[...] = _sigmoid(a0_ref[...] + _dot(lo[:, slice(*LORA_AAA)].astype(BF16), wa2_ref[...].astype(BF16)))
    gg_ref[...] = _dot(_sigmoid(lo[:, slice(*LORA_GATE)]).astype(BF16), wg2_ref[...].astype(BF16))


def _lora(h, w1p, mup, wd2p, wa2p, wg2p, w0, a0, bsz, t_len):
    n, d = h.shape
    dr = w0.shape[-1]
    tm = min(LORA_ROWS, t_len)
    nt = t_len // tm
    row = lambda b, t: (b * nt + t, 0)
    const = lambda b, t: (0, 0)
    out = jax.ShapeDtypeStruct((n, dr), F32)
    return pl.pallas_call(
        _lora_kernel,
        out_shape=(out, out, out),
        grid=(bsz, nt),
        in_specs=[pl.BlockSpec((tm, d), row),
                  pl.BlockSpec((d, LORA_PAD), const),
                  pl.BlockSpec((d, LORA_PAD), const),
                  pl.BlockSpec(wd2p.shape, const),
                  pl.BlockSpec(wa2p.shape, const),
                  pl.BlockSpec(wg2p.shape, const),
                  pl.BlockSpec((1, dr), const),
                  pl.BlockSpec((1, dr), const)],
        out_specs=(pl.BlockSpec((tm, dr), row),) * 3,
        scratch_shapes=[pltpu.VMEM((1, LORA_PAD), F32), pltpu.VMEM((d, 2 * LORA_PAD), BF16)],
        compiler_params=pltpu.CompilerParams(dimension_semantics=("arbitrary", "arbitrary"),
                                             vmem_limit_bytes=VMEM_LIMIT),
        name="lora",
    )(h, w1p, mup, wd2p, wa2p, wg2p, w0.reshape(1, dr), a0.reshape(1, dr))


def _wkv_kernel(r_ref, k_ref, v_ref, lw_ref, ag_ref, gg_ref, mur_ref, muk_ref, muv_ref,
                kk_ref, ka_ref, rk_ref, lnw_ref, lnb_ref, y_ref, s_ref, carry_ref):
    lt = r_ref.shape[0]
    gl = GROUP_LANES
    c_len = WKV_CHUNK
    heads = gl // HEAD_DIM

    @pl.when(pl.program_id(2) == 0)
    def _():
        s_ref[...] = jnp.zeros_like(s_ref)
        carry_ref[...] = jnp.zeros_like(carry_ref)

    ones_bd = _seg_ones(gl, HEAD_DIM)

    def lerp(ref, mu_ref, idx):
        z = ref[...]
        prev = _shift_rows(z, carry_ref[idx:idx + 1, :])
        carry_ref[idx:idx + 1, :] = z[lt - 1:, :]
        return z + (prev - z) * mu_ref[...]

    r1 = lerp(r_ref, mur_ref, 0)
    k1 = lerp(k_ref, muk_ref, 1)
    v1 = lerp(v_ref, muv_ref, 2)
    ag = ag_ref[...]
    kk = k1 * kk_ref[...]
    kk = kk * lax.rsqrt(jnp.maximum(_seg_sum1(kk * kk, ones_bd), 1e-24))
    k2 = k1 * (1.0 + (ag - 1.0) * ka_ref[...])
    a_all = -kk
    b_all = kk * ag
    lw = lw_ref[...]

    rows = lax.broadcasted_iota(jnp.int32, (gl, gl), 0)
    cols = lax.broadcasted_iota(jnp.int32, (gl, gl), 1)
    same_head = (rows // c_len) == (cols // HEAD_DIM)
    ts = lax.broadcasted_iota(jnp.int32, (c_len, gl), 0)
    tj = lax.broadcasted_iota(jnp.int32, (c_len, gl), 1) % c_len
    tri_strict = ts > tj
    tri_incl = ts >= tj
    eye = jnp.where(ts == tj, 1.0, 0.0).astype(F32)
    cr = lax.broadcasted_iota(jnp.int32, (c_len, c_len), 0)
    cc = lax.broadcasted_iota(jnp.int32, (c_len, c_len), 1)
    cum_b = jnp.where(cr >= cc, 1.0, 0.0).astype(BF16)

    def bd(x):
        xb = x.astype(BF16)
        return jnp.where(same_head, jnp.concatenate([xb] * heads, axis=0), jnp.zeros((), BF16))

    def fold(x):
        out = x[0:c_len]
        for hh in range(1, heads):
            out = out + x[hh * c_len:(hh + 1) * c_len]
        return out

    nch = lt // c_len
    sls = [slice(c * c_len, (c + 1) * c_len) for c in range(nch)]

    def cumsum3(x):
        hi = x.astype(BF16)
        r = x - hi.astype(F32)
        mid = r.astype(BF16)
        lo = (r - mid.astype(F32)).astype(BF16)
        return _dot(cum_b, hi) + _dot(cum_b, mid) + _dot(cum_b, lo)

    wc, g_s, n_s, rp, y0, s_in = {}, {}, {}, {}, {}, {}
    state = [s_ref[...]]

    def chunk_parallel(chunks):
        lws = {c: lw[sls[c]] for c in chunks}
        cs = {c: cumsum3(lws[c]) for c in chunks}
        yield
        e_cs = {c: jnp.exp(cs[c]) for c in chunks}
        e_neg = {c: jnp.exp(-cs[c]) for c in chunks}
        for c in chunks:
            wc[c] = e_cs[c][c_len - 1:, :]
        at = {c: a_all[sls[c]] * jnp.exp(cs[c] - lws[c]) for c in chunks}
        rt = {c: r1[sls[c]] * e_cs[c] for c in chunks}
        bt = {c: b_all[sls[c]] * e_neg[c] for c in chunks}
        kt = {c: k2[sls[c]] * e_neg[c] for c in chunks}
        vc = {c: v1[sls[c]] for c in chunks}
        yield
        lhs = {c: jnp.concatenate([at[c], rt[c]], axis=0).astype(BF16) for c in chunks}
        g_b = {c: _dot_nt(lhs[c], bd(bt[c])) for c in chunks}
        yield
        g_k = {c: _dot_nt(lhs[c], bd(kt[c])) for c in chunks}
        yield
        n_ab = {c: jnp.where(tri_strict, g_b[c][:c_len], 0.0) for c in chunks}
        a_ak = {c: jnp.where(tri_strict, g_k[c][:c_len], 0.0) for c in chunks}
        a_rb = {c: jnp.where(tri_incl, g_b[c][c_len:], 0.0).astype(BF16) for c in chunks}
        a_rk = {c: jnp.where(tri_incl, g_k[c][c_len:], 0.0).astype(BF16) for c in chunks}

        p = {c: eye + jnp.where((ts // 2) == (tj // 2), n_ab[c], 0.0) for c in chunks}
        for half in (2, 4, 8, 16, 32):
            lower_left = (((ts // (2 * half)) == (tj // (2 * half)))
                          & ((ts % (2 * half)) >= half) & ((tj % (2 * half)) < half))
            n21_ta = {c: _dot(jnp.where(lower_left, n_ab[c], 0.0).astype(BF16), bd(p[c])) for c in chunks}
            yield
            p = {c: p[c] + _dot(p[c].astype(BF16), bd(n21_ta[c])) for c in chunks}
            yield

        z = {c: _dot(p[c].astype(BF16), jnp.concatenate([bd(a_ak[c]), bd(at[c])], axis=1))
             for c in chunks}
        yield
        ap = {c: z[c][:, gl:] for c in chunks}
        bd_v = {c: bd(vc[c]) for c in chunks}
        u0 = {c: _dot(z[c][:, :gl].astype(BF16), bd_v[c]) for c in chunks}
        yield
        w = {c: _dot(a_rb[c], jnp.concatenate([bd(ap[c]), bd(u0[c])], axis=1)) for c in chunks}
        yield
        for c in chunks:
            rp[c] = rt[c] + w[c][:, :gl]
            y0[c] = w[c][:, gl:] + _dot(a_rk[c], bd_v[c])
        yield
        bh = {c: (bt[c] * wc[c]).astype(BF16) for c in chunks}
        bk = {c: jnp.concatenate([bh[c], (kt[c] * wc[c]).astype(BF16)], axis=0) for c in chunks}
        for c in chunks:
            g_s[c] = jnp.where(same_head, _dot_tn(ap[c].astype(BF16), bh[c]), 0.0).astype(BF16)
        yield
        for c in chunks:
            n_s[c] = fold(jnp.where(
                same_head, _dot_tn(jnp.concatenate([u0[c], vc[c]], axis=0).astype(BF16), bk[c]), 0.0))
        yield

    def state_update(chunks):
        for c in chunks:
            s_in[c] = state[0].astype(BF16)
            state[0] = state[0] * wc[c] + _dot(s_in[c], g_s[c]) + n_s[c]
            yield

    def epilogue(chunks):
        rows = slice(chunks[0] * c_len, (chunks[-1] + 1) * c_len)
        y = jnp.concatenate([_dot_nt(rp[c].astype(BF16), bd(s_in[c])) + y0[c] for c in chunks], axis=0)
        yield
        inv_n = 1.0 / HEAD_DIM
        mean = _seg_sum1(y, ones_bd) * inv_n
        yield
        yc = y - mean
        var = _seg_sum1(yc * yc, ones_bd) * inv_n
        yield
        yn = yc * lax.rsqrt(var + LNX_EPS) * lnw_ref[...] + lnb_ref[...]
        bonus = _seg_sum1(r1[rows] * k2[rows] * rk_ref[...], ones_bd) * v1[rows]
        y_ref[rows, :] = ((yn + bonus) * gg_ref[rows, :]).astype(y_ref.dtype)
        yield

    def drain(gen):
        for _ in gen:
            pass

    def interleave(serial, other, every):
        for k, _ in enumerate(other, 1):
            if k % every == 0:
                next(serial, None)
        drain(serial)

    wave = max(nch // 2, 1)
    waves = [list(range(w0, min(w0 + wave, nch))) for w0 in range(0, nch, wave)]
    par_stages, epi_stages = 20, 4
    drain(chunk_parallel(waves[0]))
    for wi, chunks in enumerate(waves):
        others, n_other = [], 0
        if wi + 1 < len(waves):
            others.append(chunk_parallel(waves[wi + 1]))
            n_other += par_stages
        if wi >= 1:
            others.append(epilogue(waves[wi - 1]))
            n_other += epi_stages
        interleave(state_update(chunks), itertools.chain(*others), max(n_other // len(chunks), 1))
    drain(epilogue(waves[-1]))
    s_ref[...] = state[0]


def _wkv(proj, lw, ag, gg, mu_r, mu_k, mu_v, k_k, k_a, r_k, lnx_w, lnx_b, bsz, t_len):
    n = proj.shape[0]
    dr = lw.shape[1]
    gl = GROUP_LANES
    ng = dr // gl
    lt = min(WKV_ROWS, t_len)
    nt = t_len // lt
    row = lambda off: (lambda b, g, t: (b * nt + t, off + g))
    par = lambda b, g, t: (0, g)
    vec = lambda a: a.reshape(1, dr)
    return pl.pallas_call(
        _wkv_kernel,
        out_shape=jax.ShapeDtypeStruct((n, dr), BF16),
        grid=(bsz, ng, nt),
        in_specs=[pl.BlockSpec((lt, gl), row(0)),
                  pl.BlockSpec((lt, gl), row(ng)),
                  pl.BlockSpec((lt, gl), row(2 * ng)),
                  pl.BlockSpec((lt, gl), row(0)),
                  pl.BlockSpec((lt, gl), row(0)),
                  pl.BlockSpec((lt, gl), row(0))] + [pl.BlockSpec((1, gl), par)] * 8,
        out_specs=pl.BlockSpec((lt, gl), row(0)),
        scratch_shapes=[pltpu.VMEM((HEAD_DIM, gl), F32), pltpu.VMEM((8, gl), F32)],
        compiler_params=pltpu.CompilerParams(
            dimension_semantics=("arbitrary", "arbitrary", "arbitrary"),
            vmem_limit_bytes=VMEM_LIMIT),
        name="wkv",
    )(proj, proj, proj, lw, ag, gg, vec(mu_r), vec(mu_k), vec(mu_v), vec(k_k), vec(k_a),
      vec(r_k), vec(lnx_w), vec(lnx_b))


def _outproj_kernel(yr_ref, u_ref, gb_ref, gc_ref, cw_ref, cnw_ref, wo_ref, x_ref, mod_ref,
                    n2w_ref, wr_ref, br_ref, x1_ref, h2_ref, meta_ref, cnt_ref,
                    ccarry_ref, cnt_acc_ref, wrs_ref):
    tm = x_ref.shape[0]
    dc = u_ref.shape[1]

    @pl.when(pl.program_id(1) == 0)
    def _():
        ccarry_ref[...] = jnp.zeros_like(ccarry_ref)

    @pl.when((pl.program_id(0) == 0) & (pl.program_id(1) == 0))
    def _():
        cnt_acc_ref[...] = jnp.zeros_like(cnt_acc_ref)
        wr_hi, wr_lo = _split_bf16(wr_ref[...])
        wrs_ref[0] = wr_hi
        wrs_ref[1] = wr_lo

    pre = gc_ref[...] * u_ref[...]
    prev1 = _shift_rows(pre, ccarry_ref[0:1, :])
    prev2 = _shift_rows(prev1, ccarry_ref[1:2, :])
    ccarry_ref[0:1, :] = pre[tm - 1:, :]
    ccarry_ref[1:2, :] = pre[tm - 2:tm - 1, :]
    yc = gb_ref[...] * (prev2 * cw_ref[0:1, :] + prev1 * cw_ref[1:2, :] + pre * cw_ref[2:3, :])
    ones_bd = _seg_ones(GROUP_LANES, CONV_GROUP_DIM)
    ycs = []
    for s in range(dc // GROUP_LANES):
        ys = yc[:, s * GROUP_LANES:(s + 1) * GROUP_LANES]
        ms = _seg_sum1(ys * ys, ones_bd) * (1.0 / CONV_GROUP_DIM)
        ycs.append(ys * lax.rsqrt(ms + RMS_EPS))
    y_conv = (jnp.concatenate(ycs, axis=1) * cnw_ref[...]).astype(BF16)

    dr = yr_ref.shape[1]
    y_mix = _dot(yr_ref[...], wo_ref[0:dr, :]) + _dot(y_conv, wo_ref[dr:, :])
    x1 = x_ref[...] + mod_ref[0, 2:3, :] * y_mix
    x1_ref[...] = x1
    ms = jnp.mean(x1 * x1, axis=-1, keepdims=True)
    gain2 = n2w_ref[...] * (1.0 + mod_ref[0, 4:5, :])
    h2 = x1 * lax.rsqrt(ms + RMS_EPS) * gain2 + mod_ref[0, 3:4, :]
    h2_hi = h2.astype(BF16)
    h2_hi32 = h2_hi.astype(F32)
    h2_ref[...] = _pack_rounded_pair(h2_hi32)
    h2_lo = (h2 - h2_hi32).astype(BF16)

    logits = _dot(h2_hi, wrs_ref[0]) + _dot(h2_lo, wrs_ref[0]) + _dot(h2_hi, wrs_ref[1]) + br_ref[...]
    lane = lax.broadcasted_iota(jnp.int32, logits.shape, 1)
    neg = jnp.float32(-jnp.inf)
    big = jnp.int32(1 << 20)
    glog = jnp.where(lane < N_ROUTE_GROUPS, logits, neg)
    gmax = jnp.max(glog, axis=-1, keepdims=True)
    grp = jnp.min(jnp.where(glog == gmax, lane, big), axis=-1, keepdims=True)
    p_grp = 1.0 / jnp.sum(jnp.exp(glog - gmax), axis=-1, keepdims=True)
    e_lane = lane - N_ROUTE_GROUPS
    in_grp = (e_lane >= grp * EXPERTS_PER_GROUP) & (e_lane < (grp + 1) * EXPERTS_PER_GROUP)
    elog = jnp.where(in_grp, logits, neg)
    m1 = jnp.max(elog, axis=-1, keepdims=True)
    i1 = jnp.min(jnp.where(elog == m1, lane, big), axis=-1, keepdims=True)
    elog2 = jnp.where(lane == i1, neg, elog)
    m2 = jnp.max(elog2, axis=-1, keepdims=True)
    i2 = jnp.min(jnp.where(elog2 == m2, lane, big), axis=-1, keepdims=True)
    t2 = jnp.exp(m2 - m1)
    w1 = p_grp / (1.0 + t2)
    w2 = p_grp * t2 / (1.0 + t2)
    e1 = i1 - N_ROUTE_GROUPS
    e2 = i2 - N_ROUTE_GROUPS

    oh1 = lane == e1
    oh2 = lane == e2
    oh = jnp.where(oh1 | oh2, 1.0, 0.0)
    tr = lax.broadcasted_iota(jnp.int32, (tm, tm), 0)
    tc = lax.broadcasted_iota(jnp.int32, (tm, tm), 1)
    lower = jnp.where(tr > tc, 1.0, 0.0).astype(BF16)
    before = _dot(lower, oh.astype(BF16)) + cnt_acc_ref[...]
    rank1 = jnp.sum(jnp.where(oh1, before, 0.0), axis=-1, keepdims=True)
    rank2 = jnp.sum(jnp.where(oh2, before, 0.0), axis=-1, keepdims=True)
    cnt = cnt_acc_ref[...] + jnp.sum(oh, axis=0, keepdims=True)
    cnt_acc_ref[...] = cnt
    cnt_ref[...] = jnp.broadcast_to(cnt, cnt_ref.shape)

    meta = jnp.where(lane == 0, e1.astype(F32), 0.0)
    meta = jnp.where(lane == 1, e2.astype(F32), meta)
    meta = jnp.where(lane == 2, w1, meta)
    meta = jnp.where(lane == 3, w2, meta)
    meta = jnp.where(lane == 4, rank1, meta)
    meta = jnp.where(lane == 5, rank2, meta)
    meta_ref[...] = meta


def _outproj(yr, proj, conv_w, conv_norm_w, w_out_bf, x2, mod3, norm2_w, wr, br, bsz, t_len):
    n, d = x2.shape
    dr = yr.shape[1]
    dc = conv_w.shape[1]
    tm = min(OUTPROJ_ROWS, t_len)
    nt = t_len // tm
    nseg = dr // dc
    row = lambda off: (lambda b, t: (b * nt + t, off))
    const = lambda b, t: (0, 0)
    return pl.pallas_call(
        _outproj_kernel,
        out_shape=(jax.ShapeDtypeStruct((n, d), F32), jax.ShapeDtypeStruct((n, d // 2), jnp.uint32),
                   jax.ShapeDtypeStruct((n, LANES), F32), jax.ShapeDtypeStruct((8, LANES), F32)),
        grid=(bsz, nt),
        in_specs=[pl.BlockSpec((tm, dr), row(0)),
                  pl.BlockSpec((tm, dc), row(3 * nseg)),
                  pl.BlockSpec((tm, dc), row(4 * nseg)),
                  pl.BlockSpec((tm, dc), row(5 * nseg)),
                  pl.BlockSpec((3, dc), const),
                  pl.BlockSpec((1, dc), const),
                  pl.BlockSpec((dr + dc, d), const),
                  pl.BlockSpec((tm, d), row(0)),
                  pl.BlockSpec((1, 6, d), lambda b, t: (b, 0, 0)),
                  pl.BlockSpec((1, d), const),
                  pl.BlockSpec((d, LANES), const),
                  pl.BlockSpec((1, LANES), const)],
        out_specs=(pl.BlockSpec((tm, d), row(0)), pl.BlockSpec((tm, d // 2), row(0)),
                   pl.BlockSpec((tm, LANES), row(0)), pl.BlockSpec((8, LANES), const)),
        scratch_shapes=[pltpu.VMEM((8, dc), F32), pltpu.VMEM((1, LANES), F32),
                        pltpu.VMEM((2, d, LANES), BF16)],
        compiler_params=pltpu.CompilerParams(dimension_semantics=("arbitrary", "arbitrary"),
                                             vmem_limit_bytes=VMEM_LIMIT),
        name="outproj",
    )(yr, proj, proj, proj, conv_w, conv_norm_w.reshape(1, dc), w_out_bf, x2, mod3,
      norm2_w.reshape(1, d), wr, br)


def _dispatch_kernel(zoff_ref, dest_hbm, h2_hbm, xs_hbm, idx_smem, tile_ref, zero_ref, zblk_ref,
                     sem_idx, sem_tile, sem_rows, sem_zero, sem_tail):
    i = pl.program_id(0)
    nsteps = pl.num_programs(0)
    th = idx_smem.shape[1] // 2
    slot = i % 2

    def idx_copy(step, sl):
        return pltpu.make_async_copy(dest_hbm.at[step], idx_smem.at[sl], sem_idx.at[sl])

    def tile_copy(step, sl):
        rows = pl.ds(pl.multiple_of(step * th, th), th)
        return pltpu.make_async_copy(h2_hbm.at[rows], tile_ref.at[sl], sem_tile.at[sl])

    def rows_wait(sl):
        pltpu.make_async_copy(xs_hbm.at[pl.ds(0, 2 * th)], xs_hbm.at[pl.ds(0, 2 * th)],
                              sem_rows.at[sl]).wait()

    nblk = xs_hbm.shape[0] // ROUTE_BLOCK

    def tail_copy(b):
        start = pl.multiple_of(b * ROUTE_BLOCK, ROUTE_BLOCK)
        return pltpu.make_async_copy(zblk_ref, xs_hbm.at[pl.ds(start, ROUTE_BLOCK)], sem_tail)

    @pl.when(i == 0)
    def _():
        idx_copy(0, 0).start()
        tile_copy(0, 0).start()
        zero_ref[...] = jnp.zeros_like(zero_ref)

        def zero_expert(e, carry):
            off = zoff_ref[e]

            def zero_row(j, c):
                pltpu.make_async_copy(zero_ref, xs_hbm.at[pl.ds(off + j, 1)], sem_zero).start()
                return c

            return lax.fori_loop(0, zoff_ref[N_EXPERTS + e], zero_row, carry)

        lax.fori_loop(0, N_EXPERTS, zero_expert, 0)
        total = zoff_ref[2 * N_EXPERTS]
        bulk = pl.multiple_of((total // 8) * 8, 8)

        @pl.when(bulk > 0)
        def _():
            pltpu.make_async_copy(xs_hbm.at[pl.ds(0, bulk)], xs_hbm.at[pl.ds(0, bulk)], sem_zero).wait()

        def wait_row(j, c):
            pltpu.make_async_copy(zero_ref, xs_hbm.at[pl.ds(0, 1)], sem_zero).wait()
            return c

        lax.fori_loop(0, total - bulk, wait_row, 0)

        zblk_ref[...] = jnp.zeros_like(zblk_ref)

        def tail_start(b, c):
            tail_copy(b).start()
            return c

        lax.fori_loop(zoff_ref[2 * N_EXPERTS + 1], nblk, tail_start, 0)

    idx_copy(i, slot).wait()
    tile_copy(i, slot).wait()

    @pl.when(i > 0)
    def _():
        rows_wait(1 - slot)

    @pl.when(i + 1 < nsteps)
    def _():
        idx_copy(i + 1, 1 - slot).start()
        tile_copy(i + 1, 1 - slot).start()

    for p in range(2):
        @pl.when(slot == p)
        def _(p=p):
            for t in range(th):
                for s in range(2):
                    d = idx_smem[p, 2 * t + s]
                    pltpu.make_async_copy(tile_ref.at[p, pl.ds(t, 1)], xs_hbm.at[pl.ds(d, 1)],
                                          sem_rows.at[p]).start(priority=s)

    @pl.when(i == nsteps - 1)
    def _():
        rows_wait(slot)

        def tail_wait(b, c):
            tail_copy(b).wait()
            return c

        lax.fori_loop(zoff_ref[2 * N_EXPERTS + 1], nblk, tail_wait, 0)


def _dispatch(dest, zero_off, h2, n_rows):
    n, d = h2.shape
    th = min(DISPATCH_ROWS, n)
    return pl.pallas_call(
        _dispatch_kernel,
        out_shape=jax.ShapeDtypeStruct((n_rows, d), h2.dtype),
        grid_spec=pltpu.PrefetchScalarGridSpec(
            num_scalar_prefetch=1,
            grid=(n // th,),
            in_specs=[pl.BlockSpec(memory_space=pl.ANY), pl.BlockSpec(memory_space=pl.ANY)],
            out_specs=pl.BlockSpec(memory_space=pl.ANY),
            scratch_shapes=[pltpu.SMEM((2, 2 * th), jnp.int32),
                            pltpu.VMEM((2, th, d), h2.dtype),
                            pltpu.VMEM((1, d), h2.dtype),
                            pltpu.VMEM((ROUTE_BLOCK, d), h2.dtype),
                            pltpu.SemaphoreType.DMA((2,)), pltpu.SemaphoreType.DMA((2,)),
                            pltpu.SemaphoreType.DMA((2,)),
                            pltpu.SemaphoreType.DMA, pltpu.SemaphoreType.DMA]),
        compiler_params=pltpu.CompilerParams(dimension_semantics=("arbitrary",),
                                             vmem_limit_bytes=VMEM_LIMIT),
        name="dispatch",
    )(zero_off, dest.reshape(n // th, 2 * th), h2)


def _expert_kernel(sched_ref, na_ref, xs_ref, wg_hbm, wu_hbm, wd_hbm, ys_ref,
                   wg_buf, wu_buf, wd_buf, sem):
    i = pl.program_id(0)

    def weight_copies(e, slot):
        return (pltpu.make_async_copy(wg_hbm.at[e], wg_buf.at[slot], sem.at[slot, 0]),
                pltpu.make_async_copy(wu_hbm.at[e], wu_buf.at[slot], sem.at[slot, 1]),
                pltpu.make_async_copy(wd_hbm.at[e], wd_buf.at[slot], sem.at[slot, 2]))

    @pl.when(i < na_ref[0])
    def _():
        e = sched_ref[0, i]
        slot = sched_ref[1, i]

        @pl.when(i == 0)
        def _():
            for cp in weight_copies(e, slot):
                cp.start()

        @pl.when(sched_ref[2, i] == 1)
        def _():
            for cp in weight_copies(e, slot):
                cp.wait()
            nxt = sched_ref[3, i]

            @pl.when(nxt >= 0)
            def _():
                for cp in weight_copies(nxt, 1 - slot):
                    cp.start()

        x = _unpack_bf16_pair(xs_ref[...]).astype(BF16)
        g = _dot(x, wg_buf[slot].astype(BF16))
        u = _dot(x, wu_buf[slot].astype(BF16))
        hid = (g * _sigmoid(g) * u).astype(BF16)
        ys_ref[...] = _pack_bf16_pair(_dot(hid, wd_buf[slot].astype(BF16)))

    @pl.when(i >= na_ref[0])
    def _():
        ys_ref[...] = jnp.zeros_like(ys_ref)


def _experts(block_e, n_active, xs, exp_gate, exp_up, exp_down):
    n_rows, dp = xs.shape
    _, d, de = exp_gate.shape
    nblk = n_rows // ROUTE_BLOCK
    idx = jnp.arange(nblk, dtype=jnp.int32)
    valid = idx < n_active[0]
    first = valid & ((idx == 0) | (block_e != jnp.roll(block_e, 1)))
    slot = jnp.maximum(jnp.cumsum(first.astype(jnp.int32)) - 1, 0) % 2
    first_at = jnp.where(first, idx, nblk)
    next_first = jnp.concatenate([lax.cummin(first_at, reverse=True)[1:], jnp.full((1,), nblk, jnp.int32)])
    nxt = jnp.where(next_first < nblk, block_e[jnp.minimum(next_first, nblk - 1)], -1)
    sched = jnp.stack([block_e, slot, first.astype(jnp.int32), nxt]).astype(jnp.int32)

    blk_row = lambda i, sc, na: (jnp.minimum(i, jnp.maximum(na[0] - 1, 0)), 0)
    return pl.pallas_call(
        _expert_kernel,
        out_shape=jax.ShapeDtypeStruct((n_rows, dp), jnp.uint32),
        grid_spec=pltpu.PrefetchScalarGridSpec(
            num_scalar_prefetch=2,
            grid=(nblk,),
            in_specs=[pl.BlockSpec((ROUTE_BLOCK, dp), blk_row),
                      pl.BlockSpec(memory_space=pl.ANY),
                      pl.BlockSpec(memory_space=pl.ANY),
                      pl.BlockSpec(memory_space=pl.ANY)],
            out_specs=pl.BlockSpec((ROUTE_BLOCK, dp), lambda i, sc, na: (i, 0)),
            scratch_shapes=[pltpu.VMEM((2, d, de), exp_gate.dtype),
                            pltpu.VMEM((2, d, de), exp_up.dtype),
                            pltpu.VMEM((2, de, d), exp_down.dtype),
                            pltpu.SemaphoreType.DMA((2, 3))]),
        compiler_params=pltpu.CompilerParams(dimension_semantics=("arbitrary",),
                                             vmem_limit_bytes=VMEM_LIMIT),
        name="experts",
    )(sched, n_active, xs, exp_gate, exp_up, exp_down)


def _combine_kernel(dest_hbm, ys_hbm, x1_ref, meta_ref, mod_ref, fw_ref, o_ref,
                    idx_smem, buf_ref, sem_idx, sem_rows):
    i = pl.program_id(0)
    nsteps = pl.num_programs(0)
    tj = x1_ref.shape[0]
    slot = i % 2

    def idx_copy(step, sl):
        return pltpu.make_async_copy(dest_hbm.at[step], idx_smem.at[sl], sem_idx.at[sl])

    def gather(sl):
        for t in range(tj):
            for s in range(2):
                d = idx_smem[sl, 2 * t + s]
                pltpu.make_async_copy(ys_hbm.at[pl.ds(d, 1)], buf_ref.at[sl, s, pl.ds(t, 1)],
                                      sem_rows.at[sl]).start(priority=s)

    @pl.when(i == 0)
    def _():
        first = idx_copy(0, 0)
        first.start()
        first.wait()
        gather(0)

        @pl.when(nsteps > 1)
        def _():
            idx_copy(1, 1).start()

    for p in range(2):
        @pl.when((i + 1 < nsteps) & (slot == 1 - p))
        def _(p=p):
            idx_copy(i + 1, p).wait()

            @pl.when(i + 2 < nsteps)
            def _():
                idx_copy(i + 2, 1 - p).start()

            gather(p)

    pltpu.make_async_copy(buf_ref.at[slot], buf_ref.at[slot], sem_rows.at[slot]).wait()

    meta = meta_ref[...]
    y = (meta[:, 2:3] * _unpack_bf16_pair(buf_ref[slot, 0])
         + meta[:, 3:4] * _unpack_bf16_pair(buf_ref[slot, 1]))
    x2 = x1_ref[...] + mod_ref[0, 5:6, :] * y
    ms = jnp.mean(x2 * x2, axis=-1, keepdims=True)
    o_ref[...] = x2 * lax.rsqrt(ms + RMS_EPS) * fw_ref[...]


def _combine(dest, ys, x1, meta, mod3, final_w, t_len):
    n, d = x1.shape
    tj = min(COMBINE_ROWS, t_len)
    return pl.pallas_call(
        _combine_kernel,
        out_shape=jax.ShapeDtypeStruct((n, d), F32),
        grid=(n // tj,),
        in_specs=[pl.BlockSpec(memory_space=pl.ANY),
                  pl.BlockSpec(memory_space=pl.ANY),
                  pl.BlockSpec((tj, d), lambda i: (i, 0)),
                  pl.BlockSpec((tj, LANES), lambda i: (i, 0)),
                  pl.BlockSpec((1, 6, d), lambda i: (i * tj // t_len, 0, 0)),
                  pl.BlockSpec((1, d), lambda i: (0, 0))],
        out_specs=pl.BlockSpec((tj, d), lambda i: (i, 0)),
        scratch_shapes=[pltpu.SMEM((2, 2 * tj), jnp.int32), pltpu.VMEM((2, 2, tj, d // 2), jnp.uint32),
                        pltpu.SemaphoreType.DMA((2,)), pltpu.SemaphoreType.DMA((2,))],
        compiler_params=pltpu.CompilerParams(dimension_semantics=("arbitrary",),
                                             vmem_limit_bytes=VMEM_LIMIT),
        name="combine",
    )(dest.reshape(n // tj, 2 * tj), ys, x1, meta, mod3, final_w.reshape(1, d))


def _pad_cols(w, width):
    return jnp.pad(w, ((0, 0), (0, width - w.shape[1])))


def _pad_rows(w, height):
    return jnp.pad(w, ((0, height - w.shape[0]), (0, 0)))


def _layer(x2, c, bsz, t_len, w_ada, b_ada, norm1_w, w_in, mu_r, mu_k, mu_v, mu_w, mu_a, mu_g, w0,
           w_decay1, w_decay2, a0, w_aaa1, w_aaa2, w_gate1, w_gate2, k_k, k_a, r_k, lnx_w, lnx_b,
           conv_w, conv_norm_w, w_out, norm2_w, w_grp, b_grp, w_exp, b_exp, exp_gate, exp_up,
           exp_down, final_w):
    n, d = x2.shape
    dr = w0.shape[0]
    widths = [hi - lo for lo, hi in (LORA_DECAY, LORA_AAA, LORA_GATE)]
    assert all(w.shape[1] <= n_ for w, n_ in zip((w_decay1, w_aaa1, w_gate1), widths))
    assert dr % GROUP_LANES == 0 and conv_w.shape[1] == dr and w_in.shape[1] == 6 * dr
    assert t_len % WKV_CHUNK == 0 and w_grp.shape[1] == N_ROUTE_GROUPS and w_exp.shape[1] == N_EXPERTS

    mod3 = _ada(c, w_ada, b_ada).reshape(bsz, 6, d)
    h, proj = _inproj(x2, mod3, norm1_w, w_in.astype(BF16), t_len)

    w1p = jnp.concatenate([_pad_cols(w, n_) for w, n_ in zip((w_decay1, w_aaa1, w_gate1), widths)], axis=1)
    mup = jnp.concatenate([jnp.broadcast_to(mu[:, None], (d, n_)) for mu, n_ in zip((mu_w, mu_a, mu_g), widths)],
                          axis=1)
    wd2p, wa2p, wg2p = (_pad_rows(w, n_) for w, n_ in zip((w_decay2, w_aaa2, w_gate2), widths))
    lw, ag, gg = _lora(h, w1p, mup, wd2p, wa2p, wg2p, w0, a0, bsz, t_len)

    yr = _wkv(proj, lw, ag, gg, mu_r, mu_k, mu_v, k_k, k_a, r_k.reshape(dr), lnx_w, lnx_b, bsz, t_len)

    wr = _pad_cols(jnp.concatenate([w_grp, w_exp], axis=1), LANES)
    br = _pad_cols(jnp.concatenate([b_grp, b_exp]).reshape(1, -1), LANES)
    x1, h2, meta, cnt = _outproj(yr, proj, conv_w, conv_norm_w, w_out.astype(BF16), x2, mod3,
                                 norm2_w, wr, br, bsz, t_len)

    counts = cnt[0, :N_EXPERTS].astype(jnp.int32)
    padded = (counts + ROUTE_BLOCK - 1) // ROUTE_BLOCK * ROUTE_BLOCK
    pad_ends = jnp.cumsum(padded)
    pad_starts = pad_ends - padded
    top_e = meta[:, 0:2].astype(jnp.int32)
    rank = meta[:, 4:6].astype(jnp.int32)
    one_hot = top_e[:, :, None] == jnp.arange(N_EXPERTS, dtype=jnp.int32)
    dest = (jnp.sum(jnp.where(one_hot, pad_starts, 0), axis=-1) + rank).reshape(-1)
    nblk = (2 * n) // ROUTE_BLOCK + N_EXPERTS
    n_rows = nblk * ROUTE_BLOCK
    block_start = jnp.arange(nblk, dtype=jnp.int32) * ROUTE_BLOCK
    block_e = jnp.minimum(jnp.sum(pad_ends[None, :] <= block_start[:, None], axis=1),
                          N_EXPERTS - 1).astype(jnp.int32)
    n_active = (pad_ends[-1:] // ROUTE_BLOCK).astype(jnp.int32)

    pad_len = padded - counts
    zero_off = jnp.concatenate([pad_starts + counts, pad_len, jnp.sum(pad_len, keepdims=True),
                                n_active]).astype(jnp.int32)
    xs = _dispatch(dest, zero_off, h2, n_rows)
    ys = _experts(block_e, n_active, xs, exp_gate, exp_up, exp_down)
    return _combine(dest, ys, x1, meta, mod3, final_w, t_len)


def kernel(x, c, w_ada, b_ada, norm1_w, w_in, mu_r, mu_k, mu_v, mu_w, mu_a, mu_g, w0, w_decay1, w_decay2, a0, w_aaa1, w_aaa2, w_gate1, w_gate2, k_k, k_a, r_k, lnx_w, lnx_b, conv_w, conv_norm_w, w_out, norm2_w, w_grp, b_grp, w_exp, b_exp, exp_gate, exp_up, exp_down, final_w):
    bsz, t_len, d = x.shape
    assert w_ada.shape[0] == 1, "single-layer block"
    out = _layer(x.reshape(bsz * t_len, d), c, bsz, t_len, w_ada[0], b_ada[0], norm1_w[0], w_in[0],
                 mu_r[0], mu_k[0], mu_v[0], mu_w[0], mu_a[0], mu_g[0], w0[0], w_decay1[0], w_decay2[0],
                 a0[0], w_aaa1[0], w_aaa2[0], w_gate1[0], w_gate2[0], k_k[0], k_a[0], r_k[0], lnx_w[0],
                 lnx_b[0], conv_w[0], conv_norm_w[0], w_out[0], norm2_w[0], w_grp[0], b_grp[0],
                 w_exp[0], b_exp[0], exp_gate[0], exp_up[0], exp_down[0], final_w)
    return out.reshape(bsz, t_len, d)
```

```python
import itertools

import jax
import jax.numpy as jnp
from jax import lax
from jax.experimental import pallas as pl
from jax.experimental.pallas import tpu as pltpu

F32 = jnp.float32
BF16 = jnp.bfloat16

HEAD_DIM = 64
WKV_CHUNK = 64
GROUP_LANES = 256
N_ROUTE_GROUPS = 4
EXPERTS_PER_GROUP = 8
N_EXPERTS = N_ROUTE_GROUPS * EXPERTS_PER_GROUP
CONV_GROUP_DIM = 64
RMS_EPS = 1e-6
LNX_EPS = 64e-5
DECAY_SCALE = 0.6065306597126334
LANES = 128
LORA_DECAY, LORA_AAA, LORA_GATE = (0, 128), (128, 256), (256, 512)
LORA_PAD = LORA_GATE[1]
ADA_COLS = 1024
INPROJ_ROWS, INPROJ_COLS = 1024, 1536
LORA_ROWS = 1024
WKV_ROWS = 1024
OUTPROJ_ROWS = 512
DISPATCH_ROWS = 256
COMBINE_ROWS = 256
ROUTE_BLOCK = 256
NORM_ROWS = 32
VMEM_LIMIT = 56 * 1024 * 1024


def _dot(a, b, precision=None):
    return jnp.dot(a, b, preferred_element_type=F32, precision=precision)


def _dot_nt(a, b):
    return lax.dot_general(a, b, (((1,), (1,)), ((), ())), preferred_element_type=F32)


def _dot_tn(a, b):
    return lax.dot_general(a, b, (((0,), (0,)), ((), ())), preferred_element_type=F32)


def _split_bf16(x):
    hi = x.astype(BF16)
    return hi, (x - hi.astype(F32)).astype(BF16)


def _pack_rounded_pair(xr):
    c = xr.shape[1] // 2
    lo = pltpu.bitcast(xr[:, :c], jnp.uint32)
    hi = pltpu.bitcast(xr[:, c:], jnp.uint32)
    return (lo >> 16) | (hi & jnp.uint32(0xFFFF0000))


def _pack_bf16_pair(x):
    return _pack_rounded_pair(x.astype(BF16).astype(F32))


def _unpack_bf16_pair(w):
    lo = pltpu.bitcast(w << 16, F32)
    hi = pltpu.bitcast(w & jnp.uint32(0xFFFF0000), F32)
    return jnp.concatenate([lo, hi], axis=1)


def _sigmoid(x):
    return 1.0 / (1.0 + jnp.exp(-x))


def _shift_rows(x, carry_row):
    rolled = pltpu.roll(x, 1, axis=0)
    row = lax.broadcasted_iota(jnp.int32, x.shape, 0)
    return jnp.where(row == 0, carry_row, rolled)


def _seg_ones(n, seg):
    r = lax.broadcasted_iota(jnp.int32, (n, n), 0) // seg
    c = lax.broadcasted_iota(jnp.int32, (n, n), 1) // seg
    return jnp.where(r == c, 1.0, 0.0).astype(BF16)


def _seg_sum1(x, ones_bd):
    return _dot(x.astype(BF16), ones_bd)


def _ada_kernel(cb_ref, w_ref, b_ref, o_ref):
    nb = cb_ref.shape[0]
    tn = w_ref.shape[1]
    for b in range(nb):
        cv = cb_ref[b]
        s = cv * _sigmoid(cv)
        cols = [jnp.sum(s * w_ref[:, j * LANES:(j + 1) * LANES], axis=0, keepdims=True)
                for j in range(tn // LANES)]
        o_ref[b:b + 1, :] = jnp.concatenate(cols, axis=1) + b_ref[...]


def _ada(c, w_ada, b_ada):
    nb, d = c.shape
    n_out = w_ada.shape[1]
    tn = ADA_COLS
    cb = jnp.broadcast_to(c[:, :, None], (nb, d, LANES))
    return pl.pallas_call(
        _ada_kernel,
        out_shape=jax.ShapeDtypeStruct((nb, n_out), F32),
        grid=(n_out // tn,),
        in_specs=[pl.BlockSpec((nb, d, LANES), lambda j: (0, 0, 0)),
                  pl.BlockSpec((d, tn), lambda j: (0, j)),
                  pl.BlockSpec((1, tn), lambda j: (0, j))],
        out_specs=pl.BlockSpec((nb, tn), lambda j: (0, j)),
        compiler_params=pltpu.CompilerParams(dimension_semantics=("arbitrary",),
                                             vmem_limit_bytes=VMEM_LIMIT),
        name="ada",
    )(cb, w_ada, b_ada.reshape(1, n_out))


def _inproj_kernel(x_ref, mod_ref, nw_ref, w_ref, h_ref, p_ref):
    @pl.when(pl.program_id(1) == 0)
    def _():
        rows = min(NORM_ROWS, x_ref.shape[0])
        gain = nw_ref[...] * (1.0 + mod_ref[0, 1:2, :])
        shift = mod_ref[0, 0:1, :]

        def norm_rows(r, carry):
            sl = pl.ds(pl.multiple_of(r * rows, rows), rows)
            x = x_ref[sl, :]
            ms = jnp.mean(x * x, axis=-1, keepdims=True)
            h = x * lax.rsqrt(ms + RMS_EPS) * gain + shift
            h_ref[sl, :] = h.astype(BF16)
            return carry

        lax.fori_loop(0, x_ref.shape[0] // rows, norm_rows, 0, unroll=4)

    p_ref[...] = _dot(h_ref[...], w_ref[...])


def _inproj(x2, mod3, norm_w, w_in_bf, t_len):
    n, d = x2.shape
    n_in = w_in_bf.shape[1]
    tm = min(INPROJ_ROWS, t_len)
    tn = INPROJ_COLS
    return pl.pallas_call(
        _inproj_kernel,
        out_shape=(jax.ShapeDtypeStruct((n, d), BF16), jax.ShapeDtypeStruct((n, n_in), F32)),
        grid=(n // tm, n_in // tn),
        in_specs=[pl.BlockSpec((tm, d), lambda i, j: (i, 0)),
                  pl.BlockSpec((1, 6, d), lambda i, j: (i * tm // t_len, 0, 0)),
                  pl.BlockSpec((1, d), lambda i, j: (0, 0)),
                  pl.BlockSpec((d, tn), lambda i, j: (0, j))],
        out_specs=(pl.BlockSpec((tm, d), lambda i, j: (i, 0)),
                   pl.BlockSpec((tm, tn), lambda i, j: (i, j))),
        compiler_params=pltpu.CompilerParams(dimension_semantics=("arbitrary", "arbitrary"),
                                             vmem_limit_bytes=VMEM_LIMIT),
        name="inproj",
    )(x2, mod3, norm_w.reshape(1, d), w_in_bf)


def _lora_kernel(h_ref, w1_ref, mu_ref, wd2_ref, wa2_ref, wg2_ref, w0_ref, a0_ref,
                 lw_ref, ag_ref, gg_ref, carry_ref, w12_ref):
    @pl.when(pl.program_id(1) == 0)
    def _():
        carry_ref[...] = jnp.zeros_like(carry_ref)

    @pl.when((pl.program_id(0) == 0) & (pl.program_id(1) == 0))
    def _():
        w1 = w1_ref[...]
        w12_ref[:, :LORA_PAD] = w1.astype(BF16)
        w12_ref[:, LORA_PAD:] = (w1 * mu_ref[...]).astype(BF16)

    p12 = _dot(h_ref[...], w12_ref[...])
    p1 = p12[:, :LORA_PAD]
    p2 = p12[:, LORA_PAD:]
    prev = _shift_rows(p2, carry_ref[...])
    carry_ref[...] = p2[p2.shape[0] - 1:, :]
    lo = p1 - p2 + prev
    dec = _dot(jnp.tanh(lo[:, slice(*LORA_DECAY)]).astype(BF16), wd2_ref[...].astype(BF16))
    lw_ref[...] = -DECAY_SCALE * _sigmoid(w0_ref[...] + dec)
    ag_ref[...] = _sigmoid(a0_ref[...] + _dot(lo[:, slice(*LORA_AAA)].astype(BF16), wa2_ref[...].astype(BF16)))
    gg_ref[...] = _dot(_sigmoid(lo[:, slice(*LORA_GATE)]).astype(BF16), wg2_ref[...].astype(BF16))


def _lora(h, w1p, mup, wd2p, wa2p, wg2p, w0, a0, bsz, t_len):
    n, d = h.shape
    dr = w0.shape[-1]
    tm = min(LORA_ROWS, t_len)
    nt = t_len // tm
    row = lambda b, t: (b * nt + t, 0)
    const = lambda b, t: (0, 0)
    out = jax.ShapeDtypeStruct((n, dr), F32)
    return pl.pallas_call(
        _lora_kernel,
        out_shape=(out, out, out),
        grid=(bsz, nt),
        in_specs=[pl.BlockSpec((tm, d), row),
                  pl.BlockSpec((d, LORA_PAD), const),
                  pl.BlockSpec((d, LORA_PAD), const),
                  pl.BlockSpec(wd2p.shape, const),
                  pl.BlockSpec(wa2p.shape, const),
                  pl.BlockSpec(wg2p.shape, const),
                  pl.BlockSpec((1, dr), const),
                  pl.BlockSpec((1, dr), const)],
        out_specs=(pl.BlockSpec((tm, dr), row),) * 3,
        scratch_shapes=[pltpu.VMEM((1, LORA_PAD), F32), pltpu.VMEM((d, 2 * LORA_PAD), BF16)],
        compiler_params=pltpu.CompilerParams(dimension_semantics=("arbitrary", "arbitrary"),
                                             vmem_limit_bytes=VMEM_LIMIT),
        name="lora",
    )(h, w1p, mup, wd2p, wa2p, wg2p, w0.reshape(1, dr), a0.reshape(1, dr))


def _wkv_kernel(r_ref, k_ref, v_ref, lw_ref, ag_ref, gg_ref, mur_ref, muk_ref, muv_ref,
                kk_ref, ka_ref, rk_ref, lnw_ref, lnb_ref, y_ref, s_ref, carry_ref):
    lt = r_ref.shape[0]
    gl = GROUP_LANES
    c_len = WKV_CHUNK
    heads = gl // HEAD_DIM

    @pl.when(pl.program_id(2) == 0)
    def _():
        s_ref[...] = jnp.zeros_like(s_ref)
        carry_ref[...] = jnp.zeros_like(carry_ref)

    ones_bd = _seg_ones(gl, HEAD_DIM)

    def lerp(ref, mu_ref, idx):
        z = ref[...]
        prev = _shift_rows(z, carry_ref[idx:idx + 1, :])
        carry_ref[idx:idx + 1, :] = z[lt - 1:, :]
        return z + (prev - z) * mu_ref[...]

    r1 = lerp(r_ref, mur_ref, 0)
    k1 = lerp(k_ref, muk_ref, 1)
    v1 = lerp(v_ref, muv_ref, 2)
    ag = ag_ref[...]
    kk = k1 * kk_ref[...]
    kk = kk * lax.rsqrt(jnp.maximum(_seg_sum1(kk * kk, ones_bd), 1e-24))
    k2 = k1 * (1.0 + (ag - 1.0) * ka_ref[...])
    a_all = -kk
    b_all = kk * ag
    lw = lw_ref[...]

    rows = lax.broadcasted_iota(jnp.int32, (gl, gl), 0)
    cols = lax.broadcasted_iota(jnp.int32, (gl, gl), 1)
    same_head = (rows // c_len) == (cols // HEAD_DIM)
    ts = lax.broadcasted_iota(jnp.int32, (c_len, gl), 0)
    tj = lax.broadcasted_iota(jnp.int32, (c_len, gl), 1) % c_len
    tri_strict = ts > tj
    tri_incl = ts >= tj
    eye = jnp.where(ts == tj, 1.0, 0.0).astype(F32)
    cr = lax.broadcasted_iota(jnp.int32, (c_len, c_len), 0)
    cc = lax.broadcasted_iota(jnp.int32, (c_len, c_len), 1)
    cum_b = jnp.where(cr >= cc, 1.0, 0.0).astype(BF16)

    def bd(x):
        xb = x.astype(BF16)
        return jnp.where(same_head, jnp.concatenate([xb] * heads, axis=0), jnp.zeros((), BF16))

    def fold(x):
        out = x[0:c_len]
        for hh in range(1, heads):
            out = out + x[hh * c_len:(hh + 1) * c_len]
        return out

    nch = lt // c_len
    sls = [slice(c * c_len, (c + 1) * c_len) for c in range(nch)]

    def cumsum3(x):
        hi = x.astype(BF16)
        r = x - hi.astype(F32)
        mid = r.astype(BF16)
        lo = (r - mid.astype(F32)).astype(BF16)
        return _dot(cum_b, hi) + _dot(cum_b, mid) + _dot(cum_b, lo)

    wc, g_s, n_s, rp, y0, s_in = {}, {}, {}, {}, {}, {}
    state = [s_ref[...]]

    def chunk_parallel(chunks):
        lws = {c: lw[sls[c]] for c in chunks}
        cs = {c: cumsum3(lws[c]) for c in chunks}
        yield
        e_cs = {c: jnp.exp(cs[c]) for c in chunks}
        e_neg = {c: jnp.exp(-cs[c]) for c in chunks}
        for c in chunks:
            wc[c] = e_cs[c][c_len - 1:, :]
        at = {c: a_all[sls[c]] * jnp.exp(cs[c] - lws[c]) for c in chunks}
        rt = {c: r1[sls[c]] * e_cs[c] for c in chunks}
        bt = {c: b_all[sls[c]] * e_neg[c] for c in chunks}
        kt = {c: k2[sls[c]] * e_neg[c] for c in chunks}
        vc = {c: v1[sls[c]] for c in chunks}
        yield
        lhs = {c: jnp.concatenate([at[c], rt[c]], axis=0).astype(BF16) for c in chunks}
        g_b = {c: _dot_nt(lhs[c], bd(bt[c])) for c in chunks}
        yield
        g_k = {c: _dot_nt(lhs[c], bd(kt[c])) for c in chunks}
        yield
        n_ab = {c: jnp.where(tri_strict, g_b[c][:c_len], 0.0) for c in chunks}
        a_ak = {c: jnp.where(tri_strict, g_k[c][:c_len], 0.0) for c in chunks}
        a_rb = {c: jnp.where(tri_incl, g_b[c][c_len:], 0.0).astype(BF16) for c in chunks}
        a_rk = {c: jnp.where(tri_incl, g_k[c][c_len:], 0.0).astype(BF16) for c in chunks}

        p = {c: eye + jnp.where((ts // 2) == (tj // 2), n_ab[c], 0.0) for c in chunks}
        for half in (2, 4, 8, 16, 32):
            lower_left = (((ts // (2 * half)) == (tj // (2 * half)))
                          & ((ts % (2 * half)) >= half) & ((tj % (2 * half)) < half))
            n21_ta = {c: _dot(jnp.where(lower_left, n_ab[c], 0.0).astype(BF16), bd(p[c])) for c in chunks}
            yield
            p = {c: p[c] + _dot(p[c].astype(BF16), bd(n21_ta[c])) for c in chunks}
            yield

        z = {c: _dot(p[c].astype(BF16), jnp.concatenate([bd(a_ak[c]), bd(at[c])], axis=1))
             for c in chunks}
        yield
        ap = {c: z[c][:, gl:] for c in chunks}
        bd_v = {c: bd(vc[c]) for c in chunks}
        u0 = {c: _dot(z[c][:, :gl].astype(BF16), bd_v[c]) for c in chunks}
        yield
        w = {c: _dot(a_rb[c], jnp.concatenate([bd(ap[c]), bd(u0[c])], axis=1)) for c in chunks}
        yield
        for c in chunks:
            rp[c] = rt[c] + w[c][:, :gl]
            y0[c] = w[c][:, gl:] + _dot(a_rk[c], bd_v[c])
        yield
        bh = {c: (bt[c] * wc[c]).astype(BF16) for c in chunks}
        bk = {c: jnp.concatenate([bh[c], (kt[c] * wc[c]).astype(BF16)], axis=0) for c in chunks}
        for c in chunks:
            g_s[c] = jnp.where(same_head, _dot_tn(ap[c].astype(BF16), bh[c]), 0.0).astype(BF16)
        yield
        for c in chunks:
            n_s[c] = fold(jnp.where(
                same_head, _dot_tn(jnp.concatenate([u0[c], vc[c]], axis=0).astype(BF16), bk[c]), 0.0))
        yield

    def state_update(chunks):
        for c in chunks:
            s_in[c] = state[0].astype(BF16)
            state[0] = state[0] * wc[c] + _dot(s_in[c], g_s[c]) + n_s[c]
            yield

    def epilogue(chunks):
        rows = slice(chunks[0] * c_len, (chunks[-1] + 1) * c_len)
        y = jnp.concatenate([_dot_nt(rp[c].astype(BF16), bd(s_in[c])) + y0[c] for c in chunks], axis=0)
        yield
        inv_n = 1.0 / HEAD_DIM
        mean = _seg_sum1(y, ones_bd) * inv_n
        yield
        yc = y - mean
        var = _seg_sum1(yc * yc, ones_bd) * inv_n
        yield
        yn = yc * lax.rsqrt(var + LNX_EPS) * lnw_ref[...] + lnb_ref[...]
        bonus = _seg_sum1(r1[rows] * k2[rows] * rk_ref[...], ones_bd) * v1[rows]
        y_ref[rows, :] = ((yn + bonus) * gg_ref[rows, :]).astype(y_ref.dtype)
        yield

    def drain(gen):
        for _ in gen:
            pass

    def interleave(serial, other, every):
        for k, _ in enumerate(other, 1):
            if k % every == 0:
                next(serial, None)
        drain(serial)

    wave = max(nch // 2, 1)
    waves = [list(range(w0, min(w0 + wave, nch))) for w0 in range(0, nch, wave)]
    par_stages, epi_stages = 20, 4
    drain(chunk_parallel(waves[0]))
    for wi, chunks in enumerate(waves):
        others, n_other = [], 0
        if wi + 1 < len(waves):
            others.append(chunk_parallel(waves[wi + 1]))
            n_other += par_stages
        if wi >= 1:
            others.append(epilogue(waves[wi - 1]))
            n_other += epi_stages
        interleave(state_update(chunks), itertools.chain(*others), max(n_other // len(chunks), 1))
    drain(epilogue(waves[-1]))
    s_ref[...] = state[0]


def _wkv(proj, lw, ag, gg, mu_r, mu_k, mu_v, k_k, k_a, r_k, lnx_w, lnx_b, bsz, t_len):
    n = proj.shape[0]
    dr = lw.shape[1]
    gl = GROUP_LANES
    ng = dr // gl
    lt = min(WKV_ROWS, t_len)
    nt = t_len // lt
    row = lambda off: (lambda b, g, t: (b * nt + t, off + g))
    par = lambda b, g, t: (0, g)
    vec = lambda a: a.reshape(1, dr)
    return pl.pallas_call(
        _wkv_kernel,
        out_shape=jax.ShapeDtypeStruct((n, dr), BF16),
        grid=(bsz, ng, nt),
        in_specs=[pl.BlockSpec((lt, gl), row(0)),
                  pl.BlockSpec((lt, gl), row(ng)),
                  pl.BlockSpec((lt, gl), row(2 * ng)),
                  pl.BlockSpec((lt, gl), row(0)),
                  pl.BlockSpec((lt, gl), row(0)),
                  pl.BlockSpec((lt, gl), row(0))] + [pl.BlockSpec((1, gl), par)] * 8,
        out_specs=pl.BlockSpec((lt, gl), row(0)),
        scratch_shapes=[pltpu.VMEM((HEAD_DIM, gl), F32), pltpu.VMEM((8, gl), F32)],
        compiler_params=pltpu.CompilerParams(
            dimension_semantics=("arbitrary", "arbitrary", "arbitrary"),
            vmem_limit_bytes=VMEM_LIMIT),
        name="wkv",
    )(proj, proj, proj, lw, ag, gg, vec(mu_r), vec(mu_k), vec(mu_v), vec(k_k), vec(k_a),
      vec(r_k), vec(lnx_w), vec(lnx_b))


def _outproj_kernel(yr_ref, u_ref, gb_ref, gc_ref, cw_ref, cnw_ref, wo_ref, x_ref, mod_ref,
                    n2w_ref, wr_ref, br_ref, x1_ref, h2_ref, meta_ref, cnt_ref,
                    ccarry_ref, cnt_acc_ref, wrs_ref):
    tm = x_ref.shape[0]
    dc = u_ref.shape[1]

    @pl.when(pl.program_id(1) == 0)
    def _():
        ccarry_ref[...] = jnp.zeros_like(ccarry_ref)

    @pl.when((pl.program_id(0) == 0) & (pl.program_id(1) == 0))
    def _():
        cnt_acc_ref[...] = jnp.zeros_like(cnt_acc_ref)
        wr_hi, wr_lo = _split_bf16(wr_ref[...])
        wrs_ref[0] = wr_hi
        wrs_ref[1] = wr_lo

    pre = gc_ref[...] * u_ref[...]
    prev1 = _shift_rows(pre, ccarry_ref[0:1, :])
    prev2 = _shift_rows(prev1, ccarry_ref[1:2, :])
    ccarry_ref[0:1, :] = pre[tm - 1:, :]
    ccarry_ref[1:2, :] = pre[tm - 2:tm - 1, :]
    yc = gb_ref[...] * (prev2 * cw_ref[0:1, :] + prev1 * cw_ref[1:2, :] + pre * cw_ref[2:3, :])
    ones_bd = _seg_ones(GROUP_LANES, CONV_GROUP_DIM)
    ycs = []
    for s in range(dc // GROUP_LANES):
        ys = yc[:, s * GROUP_LANES:(s + 1) * GROUP_LANES]
        ms = _seg_sum1(ys * ys, ones_bd) * (1.0 / CONV_GROUP_DIM)
        ycs.append(ys * lax.rsqrt(ms + RMS_EPS))
    y_conv = (jnp.concatenate(ycs, axis=1) * cnw_ref[...]).astype(BF16)

    dr = yr_ref.shape[1]
    y_mix = _dot(yr_ref[...], wo_ref[0:dr, :]) + _dot(y_conv, wo_ref[dr:, :])
    x1 = x_ref[...] + mod_ref[0, 2:3, :] * y_mix
    x1_ref[...] = x1
    ms = jnp.mean(x1 * x1, axis=-1, keepdims=True)
    gain2 = n2w_ref[...] * (1.0 + mod_ref[0, 4:5, :])
    h2 = x1 * lax.rsqrt(ms + RMS_EPS) * gain2 + mod_ref[0, 3:4, :]
    h2_hi = h2.astype(BF16)
    h2_hi32 = h2_hi.astype(F32)
    h2_ref[...] = _pack_rounded_pair(h2_hi32)
    h2_lo = (h2 - h2_hi32).astype(BF16)

    logits = _dot(h2_hi, wrs_ref[0]) + _dot(h2_lo, wrs_ref[0]) + _dot(h2_hi, wrs_ref[1]) + br_ref[...]
    lane = lax.broadcasted_iota(jnp.int32, logits.shape, 1)
    neg = jnp.float32(-jnp.inf)
    big = jnp.int32(1 << 20)
    glog = jnp.where(lane < N_ROUTE_GROUPS, logits, neg)
    gmax = jnp.max(glog, axis=-1, keepdims=True)
    grp = jnp.min(jnp.where(glog == gmax, lane, big), axis=-1, keepdims=True)
    p_grp = 1.0 / jnp.sum(jnp.exp(glog - gmax), axis=-1, keepdims=True)
    e_lane = lane - N_ROUTE_GROUPS
    in_grp = (e_lane >= grp * EXPERTS_PER_GROUP) & (e_lane < (grp + 1) * EXPERTS_PER_GROUP)
    elog = jnp.where(in_grp, logits, neg)
    m1 = jnp.max(elog, axis=-1, keepdims=True)
    i1 = jnp.min(jnp.where(elog == m1, lane, big), axis=-1, keepdims=True)
    elog2 = jnp.where(lane == i1, neg, elog)
    m2 = jnp.max(elog2, axis=-1, keepdims=True)
    i2 = jnp.min(jnp.where(elog2 == m2, lane, big), axis=-1, keepdims=True)
    t2 = jnp.exp(m2 - m1)
    w1 = p_grp / (1.0 + t2)
    w2 = p_grp * t2 / (1.0 + t2)
    e1 = i1 - N_ROUTE_GROUPS
    e2 = i2 - N_ROUTE_GROUPS

    oh1 = lane == e1
    oh2 = lane == e2
    oh = jnp.where(oh1 | oh2, 1.0, 0.0)
    tr = lax.broadcasted_iota(jnp.int32, (tm, tm), 0)
    tc = lax.broadcasted_iota(jnp.int32, (tm, tm), 1)
    lower = jnp.where(tr > tc, 1.0, 0.0).astype(BF16)
    before = _dot(lower, oh.astype(BF16)) + cnt_acc_ref[...]
    rank1 = jnp.sum(jnp.where(oh1, before, 0.0), axis=-1, keepdims=True)
    rank2 = jnp.sum(jnp.where(oh2, before, 0.0), axis=-1, keepdims=True)
    cnt = cnt_acc_ref[...] + jnp.sum(oh, axis=0, keepdims=True)
    cnt_acc_ref[...] = cnt
    cnt_ref[...] = jnp.broadcast_to(cnt, cnt_ref.shape)

    meta = jnp.where(lane == 0, e1.astype(F32), 0.0)
    meta = jnp.where(lane == 1, e2.astype(F32), meta)
    meta = jnp.where(lane == 2, w1, meta)
    meta = jnp.where(lane == 3, w2, meta)
    meta = jnp.where(lane == 4, rank1, meta)
    meta = jnp.where(lane == 5, rank2, meta)
    meta_ref[...] = meta


def _outproj(yr, proj, conv_w, conv_norm_w, w_out_bf, x2, mod3, norm2_w, wr, br, bsz, t_len):
    n, d = x2.shape
    dr = yr.shape[1]
    dc = conv_w.shape[1]
    tm = min(OUTPROJ_ROWS, t_len)
    nt = t_len // tm
    nseg = dr // dc
    row = lambda off: (lambda b, t: (b * nt + t, off))
    const = lambda b, t: (0, 0)
    return pl.pallas_call(
        _outproj_kernel,
        out_shape=(jax.ShapeDtypeStruct((n, d), F32), jax.ShapeDtypeStruct((n, d // 2), jnp.uint32),
                   jax.ShapeDtypeStruct((n, LANES), F32), jax.ShapeDtypeStruct((8, LANES), F32)),
        grid=(bsz, nt),
        in_specs=[pl.BlockSpec((tm, dr), row(0)),
                  pl.BlockSpec((tm, dc), row(3 * nseg)),
                  pl.BlockSpec((tm, dc), row(4 * nseg)),
                  pl.BlockSpec((tm, dc), row(5 * nseg)),
                  pl.BlockSpec((3, dc), const),
                  pl.BlockSpec((1, dc), const),
                  pl.BlockSpec((dr + dc, d), const),
                  pl.BlockSpec((tm, d), row(0)),
                  pl.BlockSpec((1, 6, d), lambda b, t: (b, 0, 0)),
                  pl.BlockSpec((1, d), const),
                  pl.BlockSpec((d, LANES), const),
                  pl.BlockSpec((1, LANES), const)],
        out_specs=(pl.BlockSpec((tm, d), row(0)), pl.BlockSpec((tm, d // 2), row(0)),
                   pl.BlockSpec((tm, LANES), row(0)), pl.BlockSpec((8, LANES), const)),
        scratch_shapes=[pltpu.VMEM((8, dc), F32), pltpu.VMEM((1, LANES), F32),
                        pltpu.VMEM((2, d, LANES), BF16)],
        compiler_params=pltpu.CompilerParams(dimension_semantics=("arbitrary", "arbitrary"),
                                             vmem_limit_bytes=VMEM_LIMIT),
        name="outproj",
    )(yr, proj, proj, proj, conv_w, conv_norm_w.reshape(1, dc), w_out_bf, x2, mod3,
      norm2_w.reshape(1, d), wr, br)


def _dispatch_kernel(zoff_ref, dest_hbm, h2_hbm, xs_hbm, idx_smem, tile_ref, zero_ref, zblk_ref,
                     sem_idx, sem_tile, sem_rows, sem_zero, sem_tail):
    i = pl.program_id(0)
    nsteps = pl.num_programs(0)
    th = idx_smem.shape[1] // 2
    slot = i % 2

    def idx_copy(step, sl):
        return pltpu.make_async_copy(dest_hbm.at[step], idx_smem.at[sl], sem_idx.at[sl])

    def tile_copy(step, sl):
        rows = pl.ds(pl.multiple_of(step * th, th), th)
        return pltpu.make_async_copy(h2_hbm.at[rows], tile_ref.at[sl], sem_tile.at[sl])

    def rows_wait(sl):
        pltpu.make_async_copy(xs_hbm.at[pl.ds(0, 2 * th)], xs_hbm.at[pl.ds(0, 2 * th)],
                              sem_rows.at[sl]).wait()

    nblk = xs_hbm.shape[0] // ROUTE_BLOCK

    def tail_copy(b):
        start = pl.multiple_of(b * ROUTE_BLOCK, ROUTE_BLOCK)
        return pltpu.make_async_copy(zblk_ref, xs_hbm.at[pl.ds(start, ROUTE_BLOCK)], sem_tail)

    @pl.when(i == 0)
    def _():
        idx_copy(0, 0).start()
        tile_copy(0, 0).start()
        zero_ref[...] = jnp.zeros_like(zero_ref)

        def zero_expert(e, carry):
            off = zoff_ref[e]

            def zero_row(j, c):
                pltpu.make_async_copy(zero_ref, xs_hbm.at[pl.ds(off + j, 1)], sem_zero).start()
                return c

            return lax.fori_loop(0, zoff_ref[N_EXPERTS + e], zero_row, carry)

        lax.fori_loop(0, N_EXPERTS, zero_expert, 0)
        total = zoff_ref[2 * N_EXPERTS]
        bulk = pl.multiple_of((total // 8) * 8, 8)

        @pl.when(bulk > 0)
        def _():
            pltpu.make_async_copy(xs_hbm.at[pl.ds(0, bulk)], xs_hbm.at[pl.ds(0, bulk)], sem_zero).wait()

        def wait_row(j, c):
            pltpu.make_async_copy(zero_ref, xs_hbm.at[pl.ds(0, 1)], sem_zero).wait()
            return c

        lax.fori_loop(0, total - bulk, wait_row, 0)

        zblk_ref[...] = jnp.zeros_like(zblk_ref)

        def tail_start(b, c):
            tail_copy(b).start()
            return c

        lax.fori_loop(zoff_ref[2 * N_EXPERTS + 1], nblk, tail_start, 0)

    idx_copy(i, slot).wait()
    tile_copy(i, slot).wait()

    @pl.when(i > 0)
    def _():
        rows_wait(1 - slot)

    @pl.when(i + 1 < nsteps)
    def _():
        idx_copy(i + 1, 1 - slot).start()
        tile_copy(i + 1, 1 - slot).start()

    for p in range(2):
        @pl.when(slot == p)
        def _(p=p):
            for t in range(th):
                for s in range(2):
                    d = idx_smem[p, 2 * t + s]
                    pltpu.make_async_copy(tile_ref.at[p, pl.ds(t, 1)], xs_hbm.at[pl.ds(d, 1)],
                                          sem_rows.at[p]).start(priority=s)

    @pl.when(i == nsteps - 1)
    def _():
        rows_wait(slot)

        def tail_wait(b, c):
            tail_copy(b).wait()
            return c

        lax.fori_loop(zoff_ref[2 * N_EXPERTS + 1], nblk, tail_wait, 0)


def _dispatch(dest, zero_off, h2, n_rows):
    n, d = h2.shape
    th = min(DISPATCH_ROWS, n)
    return pl.pallas_call(
        _dispatch_kernel,
        out_shape=jax.ShapeDtypeStruct((n_rows, d), h2.dtype),
        grid_spec=pltpu.PrefetchScalarGridSpec(
            num_scalar_prefetch=1,
            grid=(n // th,),
            in_specs=[pl.BlockSpec(memory_space=pl.ANY), pl.BlockSpec(memory_space=pl.ANY)],
            out_specs=pl.BlockSpec(memory_space=pl.ANY),
            scratch_shapes=[pltpu.SMEM((2, 2 * th), jnp.int32),
                            pltpu.VMEM((2, th, d), h2.dtype),
                            pltpu.VMEM((1, d), h2.dtype),
                            pltpu.VMEM((ROUTE_BLOCK, d), h2.dtype),
                            pltpu.SemaphoreType.DMA((2,)), pltpu.SemaphoreType.DMA((2,)),
                            pltpu.SemaphoreType.DMA((2,)),
                            pltpu.SemaphoreType.DMA, pltpu.SemaphoreType.DMA]),
        compiler_params=pltpu.CompilerParams(dimension_semantics=("arbitrary",),
                                             vmem_limit_bytes=VMEM_LIMIT),
        name="dispatch",
    )(zero_off, dest.reshape(n // th, 2 * th), h2)


def _expert_kernel(sched_ref, na_ref, xs_ref, wg_hbm, wu_hbm, wd_hbm, ys_ref,
                   wg_buf, wu_buf, wd_buf, sem):
    i = pl.program_id(0)

    def weight_copies(e, slot):
        return (pltpu.make_async_copy(wg_hbm.at[e], wg_buf.at[slot], sem.at[slot, 0]),
                pltpu.make_async_copy(wu_hbm.at[e], wu_buf.at[slot], sem.at[slot, 1]),
                pltpu.make_async_copy(wd_hbm.at[e], wd_buf.at[slot], sem.at[slot, 2]))

    @pl.when(i < na_ref[0])
    def _():
        e = sched_ref[0, i]
        slot = sched_ref[1, i]

        @pl.when(i == 0)
        def _():
            for cp in weight_copies(e, slot):
                cp.start()

        @pl.when(sched_ref[2, i] == 1)
        def _():
            for cp in weight_copies(e, slot):
                cp.wait()
            nxt = sched_ref[3, i]

            @pl.when(nxt >= 0)
            def _():
                for cp in weight_copies(nxt, 1 - slot):
                    cp.start()

        x = _unpack_bf16_pair(xs_ref[...]).astype(BF16)
        g = _dot(x, wg_buf[slot].astype(BF16))
        u = _dot(x, wu_buf[slot].astype(BF16))
        hid = (g * _sigmoid(g) * u).astype(BF16)
        ys_ref[...] = _pack_bf16_pair(_dot(hid, wd_buf[slot].astype(BF16)))

    @pl.when(i >= na_ref[0])
    def _():
        ys_ref[...] = jnp.zeros_like(ys_ref)


def _experts(block_e, n_active, xs, exp_gate, exp_up, exp_down):
    n_rows, dp = xs.shape
    _, d, de = exp_gate.shape
    nblk = n_rows // ROUTE_BLOCK
    idx = jnp.arange(nblk, dtype=jnp.int32)
    valid = idx < n_active[0]
    first = valid & ((idx == 0) | (block_e != jnp.roll(block_e, 1)))
    slot = jnp.maximum(jnp.cumsum(first.astype(jnp.int32)) - 1, 0) % 2
    first_at = jnp.where(first, idx, nblk)
    next_first = jnp.concatenate([lax.cummin(first_at, reverse=True)[1:], jnp.full((1,), nblk, jnp.int32)])
    nxt = jnp.where(next_first < nblk, block_e[jnp.minimum(next_first, nblk - 1)], -1)
    sched = jnp.stack([block_e, slot, first.astype(jnp.int32), nxt]).astype(jnp.int32)

    blk_row = lambda i, sc, na: (jnp.minimum(i, jnp.maximum(na[0] - 1, 0)), 0)
    return pl.pallas_call(
        _expert_kernel,
        out_shape=jax.ShapeDtypeStruct((n_rows, dp), jnp.uint32),
        grid_spec=pltpu.PrefetchScalarGridSpec(
            num_scalar_prefetch=2,
            grid=(nblk,),
            in_specs=[pl.BlockSpec((ROUTE_BLOCK, dp), blk_row),
                      pl.BlockSpec(memory_space=pl.ANY),
                      pl.BlockSpec(memory_space=pl.ANY),
                      pl.BlockSpec(memory_space=pl.ANY)],
            out_specs=pl.BlockSpec((ROUTE_BLOCK, dp), lambda i, sc, na: (i, 0)),
            scratch_shapes=[pltpu.VMEM((2, d, de), exp_gate.dtype),
                            pltpu.VMEM((2, d, de), exp_up.dtype),
                            pltpu.VMEM((2, de, d), exp_down.dtype),
                            pltpu.SemaphoreType.DMA((2, 3))]),
        compiler_params=pltpu.CompilerParams(dimension_semantics=("arbitrary",),
                                             vmem_limit_bytes=VMEM_LIMIT),
        name="experts",
    )(sched, n_active, xs, exp_gate, exp_up, exp_down)


def _combine_kernel(dest_hbm, ys_hbm, x1_ref, meta_ref, mod_ref, fw_ref, o_ref,
                    idx_smem, buf_ref, sem_idx, sem_rows):
    i = pl.program_id(0)
    nsteps = pl.num_programs(0)
    tj = x1_ref.shape[0]
    slot = i % 2

    def idx_copy(step, sl):
        return pltpu.make_async_copy(dest_hbm.at[step], idx_smem.at[sl], sem_idx.at[sl])

    def gather(sl):
        for t in range(tj):
            for s in range(2):
                d = idx_smem[sl, 2 * t + s]
                pltpu.make_async_copy(ys_hbm.at[pl.ds(d, 1)], buf_ref.at[sl, s, pl.ds(t, 1)],
                                      sem_rows.at[sl]).start(priority=s)

    @pl.when(i == 0)
    def _():
        first = idx_copy(0, 0)
        first.start()
        first.wait()
        gather(0)

        @pl.when(nsteps > 1)
        def _():
            idx_copy(1, 1).start()

    for p in range(2):
        @pl.when((i + 1 < nsteps) & (slot == 1 - p))
        def _(p=p):
            idx_copy(i + 1, p).wait()

            @pl.when(i + 2 < nsteps)
            def _():
                idx_copy(i + 2, 1 - p).start()

            gather(p)

    pltpu.make_async_copy(buf_ref.at[slot], buf_ref.at[slot], sem_rows.at[slot]).wait()

    meta = meta_ref[...]
    y = (meta[:, 2:3] * _unpack_bf16_pair(buf_ref[slot, 0])
         + meta[:, 3:4] * _unpack_bf16_pair(buf_ref[slot, 1]))
    x2 = x1_ref[...] + mod_ref[0, 5:6, :] * y
    ms = jnp.mean(x2 * x2, axis=-1, keepdims=True)
    o_ref[...] = x2 * lax.rsqrt(ms + RMS_EPS) * fw_ref[...]


def _combine(dest, ys, x1, meta, mod3, final_w, t_len):
    n, d = x1.shape
    tj = min(COMBINE_ROWS, t_len)
    return pl.pallas_call(
        _combine_kernel,
        out_shape=jax.ShapeDtypeStruct((n, d), F32),
        grid=(n // tj,),
        in_specs=[pl.BlockSpec(memory_space=pl.ANY),
                  pl.BlockSpec(memory_space=pl.ANY),
                  pl.BlockSpec((tj, d), lambda i: (i, 0)),
                  pl.BlockSpec((tj, LANES), lambda i: (i, 0)),
                  pl.BlockSpec((1, 6, d), lambda i: (i * tj // t_len, 0, 0)),
                  pl.BlockSpec((1, d), lambda i: (0, 0))],
        out_specs=pl.BlockSpec((tj, d), lambda i: (i, 0)),
        scratch_shapes=[pltpu.SMEM((2, 2 * tj), jnp.int32), pltpu.VMEM((2, 2, tj, d // 2), jnp.uint32),
                        pltpu.SemaphoreType.DMA((2,)), pltpu.SemaphoreType.DMA((2,))],
        compiler_params=pltpu.CompilerParams(dimension_semantics=("arbitrary",),
                                             vmem_limit_bytes=VMEM_LIMIT),
        name="combine",
    )(dest.reshape(n // tj, 2 * tj), ys, x1, meta, mod3, final_w.reshape(1, d))


def _pad_cols(w, width):
    return jnp.pad(w, ((0, 0), (0, width - w.shape[1])))


def _pad_rows(w, height):
    return jnp.pad(w, ((0, height - w.shape[0]), (0, 0)))


def _layer(x2, c, bsz, t_len, w_ada, b_ada, norm1_w, w_in, mu_r, mu_k, mu_v, mu_w, mu_a, mu_g, w0,
           w_decay1, w_decay2, a0, w_aaa1, w_aaa2, w_gate1, w_gate2, k_k, k_a, r_k, lnx_w, lnx_b,
           conv_w, conv_norm_w, w_out, norm2_w, w_grp, b_grp, w_exp, b_exp, exp_gate, exp_up,
           exp_down, final_w):
    n, d = x2.shape
    dr = w0.shape[0]
    widths = [hi - lo for lo, hi in (LORA_DECAY, LORA_AAA, LORA_GATE)]
    assert all(w.shape[1] <= n_ for w, n_ in zip((w_decay1, w_aaa1, w_gate1), widths))
    assert dr % GROUP_LANES == 0 and conv_w.shape[1] == dr and w_in.shape[1] == 6 * dr
    assert t_len % WKV_CHUNK == 0 and w_grp.shape[1] == N_ROUTE_GROUPS and w_exp.shape[1] == N_EXPERTS

    mod3 = _ada(c, w_ada, b_ada).reshape(bsz, 6, d)
    h, proj = _inproj(x2, mod3, norm1_w, w_in.astype(BF16), t_len)

    w1p = jnp.concatenate([_pad_cols(w, n_) for w, n_ in zip((w_decay1, w_aaa1, w_gate1), widths)], axis=1)
    mup = jnp.concatenate([jnp.broadcast_to(mu[:, None], (d, n_)) for mu, n_ in zip((mu_w, mu_a, mu_g), widths)],
                          axis=1)
    wd2p, wa2p, wg2p = (_pad_rows(w, n_) for w, n_ in zip((w_decay2, w_aaa2, w_gate2), widths))
    lw, ag, gg = _lora(h, w1p, mup, wd2p, wa2p, wg2p, w0, a0, bsz, t_len)

    yr = _wkv(proj, lw, ag, gg, mu_r, mu_k, mu_v, k_k, k_a, r_k.reshape(dr), lnx_w, lnx_b, bsz, t_len)

    wr = _pad_cols(jnp.concatenate([w_grp, w_exp], axis=1), LANES)
    br = _pad_cols(jnp.concatenate([b_grp, b_exp]).reshape(1, -1), LANES)
    x1, h2, meta, cnt = _outproj(yr, proj, conv_w, conv_norm_w, w_out.astype(BF16), x2, mod3,
                                 norm2_w, wr, br, bsz, t_len)

    counts = cnt[0, :N_EXPERTS].astype(jnp.int32)
    padded = (counts + ROUTE_BLOCK - 1) // ROUTE_BLOCK * ROUTE_BLOCK
    pad_ends = jnp.cumsum(padded)
    pad_starts = pad_ends - padded
    top_e = meta[:, 0:2].astype(jnp.int32)
    rank = meta[:, 4:6].astype(jnp.int32)
    one_hot = top_e[:, :, None] == jnp.arange(N_EXPERTS, dtype=jnp.int32)
    dest = (jnp.sum(jnp.where(one_hot, pad_starts, 0), axis=-1) + rank).reshape(-1)
    nblk = (2 * n) // ROUTE_BLOCK + N_EXPERTS
    n_rows = nblk * ROUTE_BLOCK
    block_start = jnp.arange(nblk, dtype=jnp.int32) * ROUTE_BLOCK
    block_e = jnp.minimum(jnp.sum(pad_ends[None, :] <= block_start[:, None], axis=1),
                          N_EXPERTS - 1).astype(jnp.int32)
    n_active = (pad_ends[-1:] // ROUTE_BLOCK).astype(jnp.int32)

    pad_len = padded - counts
    zero_off = jnp.concatenate([pad_starts + counts, pad_len, jnp.sum(pad_len, keepdims=True),
                                n_active]).astype(jnp.int32)
    xs = _dispatch(dest, zero_off, h2, n_rows)
    ys = _experts(block_e, n_active, xs, exp_gate, exp_up, exp_down)
    return _combine(dest, ys, x1, meta, mod3, final_w, t_len)


def kernel(x, c, w_ada, b_ada, norm1_w, w_in, mu_r, mu_k, mu_v, mu_w, mu_a, mu_g, w0, w_decay1, w_decay2, a0, w_aaa1, w_aaa2, w_gate1, w_gate2, k_k, k_a, r_k, lnx_w, lnx_b, conv_w, conv_norm_w, w_out, norm2_w, w_grp, b_grp, w_exp, b_exp, exp_gate, exp_up, exp_down, final_w):
    bsz, t_len, d = x.shape
    assert w_ada.shape[0] == 1, "single-layer block"
    out = _layer(x.reshape(bsz * t_len, d), c, bsz, t_len, w_ada[0], b_ada[0], norm1_w[0], w_in[0],
                 mu_r[0], mu_k[0], mu_v[0], mu_w[0], mu_a[0], mu_g[0], w0[0], w_decay1[0], w_decay2[0],
                 a0[0], w_aaa1[0], w_aaa2[0], w_gate1[0], w_gate2[0], k_k[0], k_a[0], r_k[0], lnx_w[0],
                 lnx_b[0], conv_w[0], conv_norm_w[0], w_out[0], norm2_w[0], w_grp[0], b_grp[0],
                 w_exp[0], b_exp[0], exp_gate[0], exp_up[0], exp_down[0], final_w)
    return out.reshape(bsz, t_len, d)
```

```python
import itertools

import jax
import jax.numpy as jnp
from jax import lax
from jax.experimental import pallas as pl
from jax.experimental.pallas import tpu as pltpu

F32 = jnp.float32
BF16 = jnp.bfloat16

HEAD_DIM = 64
WKV_CHUNK = 64
GROUP_LANES = 256
N_ROUTE_GROUPS = 4
EXPERTS_PER_GROUP = 8
N_EXPERTS = N_ROUTE_GROUPS * EXPERTS_PER_GROUP
CONV_GROUP_DIM = 64
RMS_EPS = 1e-6
LNX_EPS = 64e-5
DECAY_SCALE = 0.6065306597126334
LANES = 128
LORA_DECAY, LORA_AAA, LORA_GATE = (0, 128), (128, 256), (256, 512)
LORA_PAD = LORA_GATE[1]
ADA_COLS = 1024
INPROJ_ROWS, INPROJ_COLS = 1024, 1536
LORA_ROWS = 1024
WKV_ROWS = 1024
OUTPROJ_ROWS = 512
DISPATCH_ROWS = 256
COMBINE_ROWS = 128
ROUTE_BLOCK = 256
NORM_ROWS = 32
VMEM_LIMIT = 56 * 1024 * 1024


def _dot(a, b, precision=None):
    return jnp.dot(a, b, preferred_element_type=F32, precision=precision)


def _dot_nt(a, b):
    return lax.dot_general(a, b, (((1,), (1,)), ((), ())), preferred_element_type=F32)


def _dot_tn(a, b):
    return lax.dot_general(a, b, (((0,), (0,)), ((), ())), preferred_element_type=F32)


def _split_bf16(x):
    hi = x.astype(BF16)
    return hi, (x - hi.astype(F32)).astype(BF16)


def _pack_rounded_pair(xr):
    c = xr.shape[1] // 2
    lo = pltpu.bitcast(xr[:, :c], jnp.uint32)
    hi = pltpu.bitcast(xr[:, c:], jnp.uint32)
    return (lo >> 16) | (hi & jnp.uint32(0xFFFF0000))


def _pack_bf16_pair(x):
    return _pack_rounded_pair(x.astype(BF16).astype(F32))


def _unpack_bf16_pair(w):
    lo = pltpu.bitcast(w << 16, F32)
    hi = pltpu.bitcast(w & jnp.uint32(0xFFFF0000), F32)
    return jnp.concatenate([lo, hi], axis=1)


def _sigmoid(x):
    return 1.0 / (1.0 + jnp.exp(-x))


def _shift_rows(x, carry_row):
    rolled = pltpu.roll(x, 1, axis=0)
    row = lax.broadcasted_iota(jnp.int32, x.shape, 0)
    return jnp.where(row == 0, carry_row, rolled)


def _seg_ones(n, seg):
    r = lax.broadcasted_iota(jnp.int32, (n, n), 0) // seg
    c = lax.broadcasted_iota(jnp.int32, (n, n), 1) // seg
    return jnp.where(r == c, 1.0, 0.0).astype(BF16)


def _seg_sum1(x, ones_bd):
    return _dot(x.astype(BF16), ones_bd)


def _ada_kernel(cb_ref, w_ref, b_ref, o_ref):
    nb = cb_ref.shape[0]
    tn = w_ref.shape[1]
    for b in range(nb):
        cv = cb_ref[b]
        s = cv * _sigmoid(cv)
        cols = [jnp.sum(s * w_ref[:, j * LANES:(j + 1) * LANES], axis=0, keepdims=True)
                for j in range(tn // LANES)]
        o_ref[b:b + 1, :] = jnp.concatenate(cols, axis=1) + b_ref[...]


def _ada(c, w_ada, b_ada):
    nb, d = c.shape
    n_out = w_ada.shape[1]
    tn = ADA_COLS
    cb = jnp.broadcast_to(c[:, :, None], (nb, d, LANES))
    return pl.pallas_call(
        _ada_kernel,
        out_shape=jax.ShapeDtypeStruct((nb, n_out), F32),
        grid=(n_out // tn,),
        in_specs=[pl.BlockSpec((nb, d, LANES), lambda j: (0, 0, 0)),
                  pl.BlockSpec((d, tn), lambda j: (0, j)),
                  pl.BlockSpec((1, tn), lambda j: (0, j))],
        out_specs=pl.BlockSpec((nb, tn), lambda j: (0, j)),
        compiler_params=pltpu.CompilerParams(dimension_semantics=("arbitrary",),
                                             vmem_limit_bytes=VMEM_LIMIT),
        name="ada",
    )(cb, w_ada, b_ada.reshape(1, n_out))


def _inproj_kernel(x_ref, mod_ref, nw_ref, w_ref, h_ref, p_ref):
    @pl.when(pl.program_id(1) == 0)
    def _():
        rows = min(NORM_ROWS, x_ref.shape[0])
        gain = nw_ref[...] * (1.0 + mod_ref[0, 1:2, :])
        shift = mod_ref[0, 0:1, :]

        def norm_rows(r, carry):
            sl = pl.ds(pl.multiple_of(r * rows, rows), rows)
            x = x_ref[sl, :]
            ms = jnp.mean(x * x, axis=-1, keepdims=True)
            h = x * lax.rsqrt(ms + RMS_EPS) * gain + shift
            h_ref[sl, :] = h.astype(BF16)
            return carry

        lax.fori_loop(0, x_ref.shape[0] // rows, norm_rows, 0, unroll=4)

    p_ref[...] = _dot(h_ref[...], w_ref[...])


def _inproj(x2, mod3, norm_w, w_in_bf, t_len):
    n, d = x2.shape
    n_in = w_in_bf.shape[1]
    tm = min(INPROJ_ROWS, t_len)
    tn = INPROJ_COLS
    return pl.pallas_call(
        _inproj_kernel,
        out_shape=(jax.ShapeDtypeStruct((n, d), BF16), jax.ShapeDtypeStruct((n, n_in), F32)),
        grid=(n // tm, n_in // tn),
        in_specs=[pl.BlockSpec((tm, d), lambda i, j: (i, 0)),
                  pl.BlockSpec((1, 6, d), lambda i, j: (i * tm // t_len, 0, 0)),
                  pl.BlockSpec((1, d), lambda i, j: (0, 0)),
                  pl.BlockSpec((d, tn), lambda i, j: (0, j))],
        out_specs=(pl.BlockSpec((tm, d), lambda i, j: (i, 0)),
                   pl.BlockSpec((tm, tn), lambda i, j: (i, j))),
        compiler_params=pltpu.CompilerParams(dimension_semantics=("arbitrary", "arbitrary"),
                                             vmem_limit_bytes=VMEM_LIMIT),
        name="inproj",
    )(x2, mod3, norm_w.reshape(1, d), w_in_bf)


def _lora_kernel(h_ref, w1_ref, mu_ref, wd2_ref, wa2_ref, wg2_ref, w0_ref, a0_ref,
                 lw_ref, ag_ref, gg_ref, carry_ref, w12_ref):
    @pl.when(pl.program_id(1) == 0)
    def _():
        carry_ref[...] = jnp.zeros_like(carry_ref)

    @pl.when((pl.program_id(0) == 0) & (pl.program_id(1) == 0))
    def _():
        w1 = w1_ref[...]
        w12_ref[:, :LORA_PAD] = w1.astype(BF16)
        w12_ref[:, LORA_PAD:] = (w1 * mu_ref[...]).astype(BF16)

    p12 = _dot(h_ref[...], w12_ref[...])
    p1 = p12[:, :LORA_PAD]
    p2 = p12[:, LORA_PAD:]
    prev = _shift_rows(p2, carry_ref[...])
    carry_ref[...] = p2[p2.shape[0] - 1:, :]
    lo = p1 - p2 + prev
    dec = _dot(jnp.tanh(lo[:, slice(*LORA_DECAY)]).astype(BF16), wd2_ref[...].astype(BF16))
    lw_ref[...] = -DECAY_SCALE * _sigmoid(w0_ref[...] + dec)
    ag_ref[...] = _sigmoid(a0_ref[...] + _dot(lo[:, slice(*LORA_AAA)].astype(BF16), wa2_ref[...].astype(BF16)))
    gg_ref[...] = _dot(_sigmoid(lo[:, slice(*LORA_GATE)]).astype(BF16), wg2_ref[...].astype(BF16))


def _lora(h, w1p, mup, wd2p, wa2p, wg2p, w0, a0, bsz, t_len):
    n, d = h.shape
    dr = w0.shape[-1]
    tm = min(LORA_ROWS, t_len)
    nt = t_len // tm
    row = lambda b, t: (b * nt + t, 0)
    const = lambda b, t: (0, 0)
    out = jax.ShapeDtypeStruct((n, dr), F32)
    return pl.pallas_call(
        _lora_kernel,
        out_shape=(out, out, out),
        grid=(bsz, nt),
        in_specs=[pl.BlockSpec((tm, d), row),
                  pl.BlockSpec((d, LORA_PAD), const),
                  pl.BlockSpec((d, LORA_PAD), const),
                  pl.BlockSpec(wd2p.shape, const),
                  pl.BlockSpec(wa2p.shape, const),
                  pl.BlockSpec(wg2p.shape, const),
                  pl.BlockSpec((1, dr), const),
                  pl.BlockSpec((1, dr), const)],
        out_specs=(pl.BlockSpec((tm, dr), row),) * 3,
        scratch_shapes=[pltpu.VMEM((1, LORA_PAD), F32), pltpu.VMEM((d, 2 * LORA_PAD), BF16)],
        compiler_params=pltpu.CompilerParams(dimension_semantics=("arbitrary", "arbitrary"),
                                             vmem_limit_bytes=VMEM_LIMIT),
        name="lora",
    )(h, w1p, mup, wd2p, wa2p, wg2p, w0.reshape(1, dr), a0.reshape(1, dr))


def _wkv_kernel(r_ref, k_ref, v_ref, lw_ref, ag_ref, gg_ref, mur_ref, muk_ref, muv_ref,
                kk_ref, ka_ref, rk_ref, lnw_ref, lnb_ref, y_ref, s_ref, carry_ref):
    lt = r_ref.shape[0]
    gl = GROUP_LANES
    c_len = WKV_CHUNK
    heads = gl // HEAD_DIM

    @pl.when(pl.program_id(2) == 0)
    def _():
        s_ref[...] = jnp.zeros_like(s_ref)
        carry_ref[...] = jnp.zeros_like(carry_ref)

    ones_bd = _seg_ones(gl, HEAD_DIM)

    def lerp(ref, mu_ref, idx):
        z = ref[...]
        prev = _shift_rows(z, carry_ref[idx:idx + 1, :])
        carry_ref[idx:idx + 1, :] = z[lt - 1:, :]
        return z + (prev - z) * mu_ref[...]

    r1 = lerp(r_ref, mur_ref, 0)
    k1 = lerp(k_ref, muk_ref, 1)
    v1 = lerp(v_ref, muv_ref, 2)
    ag = ag_ref[...]
    kk = k1 * kk_ref[...]
    kk = kk * lax.rsqrt(jnp.maximum(_seg_sum1(kk * kk, ones_bd), 1e-24))
    k2 = k1 * (1.0 + (ag - 1.0) * ka_ref[...])
    a_all = -kk
    b_all = kk * ag
    lw = lw_ref[...]

    rows = lax.broadcasted_iota(jnp.int32, (gl, gl), 0)
    cols = lax.broadcasted_iota(jnp.int32, (gl, gl), 1)
    same_head = (rows // c_len) == (cols // HEAD_DIM)
    ts = lax.broadcasted_iota(jnp.int32, (c_len, gl), 0)
    tj = lax.broadcasted_iota(jnp.int32, (c_len, gl), 1) % c_len
    tri_strict = ts > tj
    tri_incl = ts >= tj
    eye = jnp.where(ts == tj, 1.0, 0.0).astype(F32)
    cr = lax.broadcasted_iota(jnp.int32, (c_len, c_len), 0)
    cc = lax.broadcasted_iota(jnp.int32, (c_len, c_len), 1)
    cum_b = jnp.where(cr >= cc, 1.0, 0.0).astype(BF16)

    def bd(x):
        xb = x.astype(BF16)
        return jnp.where(same_head, jnp.concatenate([xb] * heads, axis=0), jnp.zeros((), BF16))

    def fold(x):
        out = x[0:c_len]
        for hh in range(1, heads):
            out = out + x[hh * c_len:(hh + 1) * c_len]
        return out

    nch = lt // c_len
    sls = [slice(c * c_len, (c + 1) * c_len) for c in range(nch)]

    def cumsum3(x):
        hi = x.astype(BF16)
        r = x - hi.astype(F32)
        mid = r.astype(BF16)
        lo = (r - mid.astype(F32)).astype(BF16)
        return _dot(cum_b, hi) + _dot(cum_b, mid) + _dot(cum_b, lo)

    wc, g_s, n_s, rp, y0, s_in = {}, {}, {}, {}, {}, {}
    state = [s_ref[...]]

    def chunk_parallel(chunks):
        lws = {c: lw[sls[c]] for c in chunks}
        cs = {c: cumsum3(lws[c]) for c in chunks}
        yield
        e_cs = {c: jnp.exp(cs[c]) for c in chunks}
        e_neg = {c: jnp.exp(-cs[c]) for c in chunks}
        for c in chunks:
            wc[c] = e_cs[c][c_len - 1:, :]
        at = {c: a_all[sls[c]] * jnp.exp(cs[c] - lws[c]) for c in chunks}
        rt = {c: r1[sls[c]] * e_cs[c] for c in chunks}
        bt = {c: b_all[sls[c]] * e_neg[c] for c in chunks}
        kt = {c: k2[sls[c]] * e_neg[c] for c in chunks}
        vc = {c: v1[sls[c]] for c in chunks}
        yield
        lhs = {c: jnp.concatenate([at[c], rt[c]], axis=0).astype(BF16) for c in chunks}
        g_b = {c: _dot_nt(lhs[c], bd(bt[c])) for c in chunks}
        yield
        g_k = {c: _dot_nt(lhs[c], bd(kt[c])) for c in chunks}
        yield
        n_ab = {c: jnp.where(tri_strict, g_b[c][:c_len], 0.0) for c in chunks}
        a_ak = {c: jnp.where(tri_strict, g_k[c][:c_len], 0.0) for c in chunks}
        a_rb = {c: jnp.where(tri_incl, g_b[c][c_len:], 0.0).astype(BF16) for c in chunks}
        a_rk = {c: jnp.where(tri_incl, g_k[c][c_len:], 0.0).astype(BF16) for c in chunks}

        p = {c: eye + jnp.where((ts // 2) == (tj // 2), n_ab[c], 0.0) for c in chunks}
        for half in (2, 4, 8, 16, 32):
            lower_left = (((ts // (2 * half)) == (tj // (2 * half)))
                          & ((ts % (2 * half)) >= half) & ((tj % (2 * half)) < half))
            n21_ta = {c: _dot(jnp.where(lower_left, n_ab[c], 0.0).astype(BF16), bd(p[c])) for c in chunks}
            yield
            p = {c: p[c] + _dot(p[c].astype(BF16), bd(n21_ta[c])) for c in chunks}
            yield

        z = {c: _dot(p[c].astype(BF16), jnp.concatenate([bd(a_ak[c]), bd(at[c])], axis=1))
             for c in chunks}
        yield
        ap = {c: z[c][:, gl:] for c in chunks}
        bd_v = {c: bd(vc[c]) for c in chunks}
        u0 = {c: _dot(z[c][:, :gl].astype(BF16), bd_v[c]) for c in chunks}
        yield
        w = {c: _dot(a_rb[c], jnp.concatenate([bd(ap[c]), bd(u0[c])], axis=1)) for c in chunks}
        yield
        for c in chunks:
            rp[c] = rt[c] + w[c][:, :gl]
            y0[c] = w[c][:, gl:] + _dot(a_rk[c], bd_v[c])
        yield
        bh = {c: (bt[c] * wc[c]).astype(BF16) for c in chunks}
        bk = {c: jnp.concatenate([bh[c], (kt[c] * wc[c]).astype(BF16)], axis=0) for c in chunks}
        for c in chunks:
            g_s[c] = jnp.where(same_head, _dot_tn(ap[c].astype(BF16), bh[c]), 0.0).astype(BF16)
        yield
        for c in chunks:
            n_s[c] = fold(jnp.where(
                same_head, _dot_tn(jnp.concatenate([u0[c], vc[c]], axis=0).astype(BF16), bk[c]), 0.0))
        yield

    def state_update(chunks):
        for c in chunks:
            s_in[c] = state[0].astype(BF16)
            state[0] = state[0] * wc[c] + _dot(s_in[c], g_s[c]) + n_s[c]
            yield

    def epilogue(chunks):
        rows = slice(chunks[0] * c_len, (chunks[-1] + 1) * c_len)
        y = jnp.concatenate([_dot_nt(rp[c].astype(BF16), bd(s_in[c])) + y0[c] for c in chunks], axis=0)
        yield
        inv_n = 1.0 / HEAD_DIM
        mean = _seg_sum1(y, ones_bd) * inv_n
        yield
        yc = y - mean
        var = _seg_sum1(yc * yc, ones_bd) * inv_n
        yield
        yn = yc * lax.rsqrt(var + LNX_EPS) * lnw_ref[...] + lnb_ref[...]
        bonus = _seg_sum1(r1[rows] * k2[rows] * rk_ref[...], ones_bd) * v1[rows]
        y_ref[rows, :] = ((yn + bonus) * gg_ref[rows, :]).astype(y_ref.dtype)
        yield

    def drain(gen):
        for _ in gen:
            pass

    def interleave(serial, other, every):
        for k, _ in enumerate(other, 1):
            if k % every == 0:
                next(serial, None)
        drain(serial)

    wave = max(nch // 2, 1)
    waves = [list(range(w0, min(w0 + wave, nch))) for w0 in range(0, nch, wave)]
    par_stages, epi_stages = 20, 4
    drain(chunk_parallel(waves[0]))
    for wi, chunks in enumerate(waves):
        others, n_other = [], 0
        if wi + 1 < len(waves):
            others.append(chunk_parallel(waves[wi + 1]))
            n_other += par_stages
        if wi >= 1:
            others.append(epilogue(waves[wi - 1]))
            n_other += epi_stages
        interleave(state_update(chunks), itertools.chain(*others), max(n_other // len(chunks), 1))
    drain(epilogue(waves[-1]))
    s_ref[...] = state[0]


def _wkv(proj, lw, ag, gg, mu_r, mu_k, mu_v, k_k, k_a, r_k, lnx_w, lnx_b, bsz, t_len):
    n = proj.shape[0]
    dr = lw.shape[1]
    gl = GROUP_LANES
    ng = dr // gl
    lt = min(WKV_ROWS, t_len)
    nt = t_len // lt
    row = lambda off: (lambda b, g, t: (b * nt + t, off + g))
    par = lambda b, g, t: (0, g)
    vec = lambda a: a.reshape(1, dr)
    return pl.pallas_call(
        _wkv_kernel,
        out_shape=jax.ShapeDtypeStruct((n, dr), BF16),
        grid=(bsz, ng, nt),
        in_specs=[pl.BlockSpec((lt, gl), row(0)),
                  pl.BlockSpec((lt, gl), row(ng)),
                  pl.BlockSpec((lt, gl), row(2 * ng)),
                  pl.BlockSpec((lt, gl), row(0)),
                  pl.BlockSpec((lt, gl), row(0)),
                  pl.BlockSpec((lt, gl), row(0))] + [pl.BlockSpec((1, gl), par)] * 8,
        out_specs=pl.BlockSpec((lt, gl), row(0)),
        scratch_shapes=[pltpu.VMEM((HEAD_DIM, gl), F32), pltpu.VMEM((8, gl), F32)],
        compiler_params=pltpu.CompilerParams(
            dimension_semantics=("arbitrary", "arbitrary", "arbitrary"),
            vmem_limit_bytes=VMEM_LIMIT),
        name="wkv",
    )(proj, proj, proj, lw, ag, gg, vec(mu_r), vec(mu_k), vec(mu_v), vec(k_k), vec(k_a),
      vec(r_k), vec(lnx_w), vec(lnx_b))


def _outproj_kernel(yr_ref, u_ref, gb_ref, gc_ref, cw_ref, cnw_ref, wo_ref, x_ref, mod_ref,
                    n2w_ref, wr_ref, br_ref, x1_ref, h2_ref, meta_ref, cnt_ref,
                    ccarry_ref, cnt_acc_ref, wrs_ref):
    tm = x_ref.shape[0]
    dc = u_ref.shape[1]

    @pl.when(pl.program_id(1) == 0)
    def _():
        ccarry_ref[...] = jnp.zeros_like(ccarry_ref)

    @pl.when((pl.program_id(0) == 0) & (pl.program_id(1) == 0))
    def _():
        cnt_acc_ref[...] = jnp.zeros_like(cnt_acc_ref)
        wr_hi, wr_lo = _split_bf16(wr_ref[...])
        wrs_ref[0] = wr_hi
        wrs_ref[1] = wr_lo

    pre = gc_ref[...] * u_ref[...]
    prev1 = _shift_rows(pre, ccarry_ref[0:1, :])
    prev2 = _shift_rows(prev1, ccarry_ref[1:2, :])
    ccarry_ref[0:1, :] = pre[tm - 1:, :]
    ccarry_ref[1:2, :] = pre[tm - 2:tm - 1, :]
    yc = gb_ref[...] * (prev2 * cw_ref[0:1, :] + prev1 * cw_ref[1:2, :] + pre * cw_ref[2:3, :])
    ones_bd = _seg_ones(GROUP_LANES, CONV_GROUP_DIM)
    ycs = []
    for s in range(dc // GROUP_LANES):
        ys = yc[:, s * GROUP_LANES:(s + 1) * GROUP_LANES]
        ms = _seg_sum1(ys * ys, ones_bd) * (1.0 / CONV_GROUP_DIM)
        ycs.append(ys * lax.rsqrt(ms + RMS_EPS))
    y_conv = (jnp.concatenate(ycs, axis=1) * cnw_ref[...]).astype(BF16)

    dr = yr_ref.shape[1]
    y_mix = _dot(yr_ref[...], wo_ref[0:dr, :]) + _dot(y_conv, wo_ref[dr:, :])
    x1 = x_ref[...] + mod_ref[0, 2:3, :] * y_mix
    x1_ref[...] = x1
    ms = jnp.mean(x1 * x1, axis=-1, keepdims=True)
    gain2 = n2w_ref[...] * (1.0 + mod_ref[0, 4:5, :])
    h2 = x1 * lax.rsqrt(ms + RMS_EPS) * gain2 + mod_ref[0, 3:4, :]
    h2_hi = h2.astype(BF16)
    h2_hi32 = h2_hi.astype(F32)
    h2_ref[...] = _pack_rounded_pair(h2_hi32)
    h2_lo = (h2 - h2_hi32).astype(BF16)

    logits = _dot(h2_hi, wrs_ref[0]) + _dot(h2_lo, wrs_ref[0]) + _dot(h2_hi, wrs_ref[1]) + br_ref[...]
    lane = lax.broadcasted_iota(jnp.int32, logits.shape, 1)
    neg = jnp.float32(-jnp.inf)
    big = jnp.int32(1 << 20)
    glog = jnp.where(lane < N_ROUTE_GROUPS, logits, neg)
    gmax = jnp.max(glog, axis=-1, keepdims=True)
    grp = jnp.min(jnp.where(glog == gmax, lane, big), axis=-1, keepdims=True)
    p_grp = 1.0 / jnp.sum(jnp.exp(glog - gmax), axis=-1, keepdims=True)
    e_lane = lane - N_ROUTE_GROUPS
    in_grp = (e_lane >= grp * EXPERTS_PER_GROUP) & (e_lane < (grp + 1) * EXPERTS_PER_GROUP)
    elog = jnp.where(in_grp, logits, neg)
    m1 = jnp.max(elog, axis=-1, keepdims=True)
    i1 = jnp.min(jnp.where(elog == m1, lane, big), axis=-1, keepdims=True)
    elog2 = jnp.where(lane == i1, neg, elog)
    m2 = jnp.max(elog2, axis=-1, keepdims=True)
    i2 = jnp.min(jnp.where(elog2 == m2, lane, big), axis=-1, keepdims=True)
    t2 = jnp.exp(m2 - m1)
    w1 = p_grp / (1.0 + t2)
    w2 = p_grp * t2 / (1.0 + t2)
    e1 = i1 - N_ROUTE_GROUPS
    e2 = i2 - N_ROUTE_GROUPS

    oh1 = lane == e1
    oh2 = lane == e2
    oh = jnp.where(oh1 | oh2, 1.0, 0.0)
    tr = lax.broadcasted_iota(jnp.int32, (tm, tm), 0)
    tc = lax.broadcasted_iota(jnp.int32, (tm, tm), 1)
    lower = jnp.where(tr > tc, 1.0, 0.0).astype(BF16)
    before = _dot(lower, oh.astype(BF16)) + cnt_acc_ref[...]
    rank1 = jnp.sum(jnp.where(oh1, before, 0.0), axis=-1, keepdims=True)
    rank2 = jnp.sum(jnp.where(oh2, before, 0.0), axis=-1, keepdims=True)
    cnt = cnt_acc_ref[...] + jnp.sum(oh, axis=0, keepdims=True)
    cnt_acc_ref[...] = cnt
    cnt_ref[...] = jnp.broadcast_to(cnt, cnt_ref.shape)

    meta = jnp.where(lane == 0, e1.astype(F32), 0.0)
    meta = jnp.where(lane == 1, e2.astype(F32), meta)
    meta = jnp.where(lane == 2, w1, meta)
    meta = jnp.where(lane == 3, w2, meta)
    meta = jnp.where(lane == 4, rank1, meta)
    meta = jnp.where(lane == 5, rank2, meta)
    meta_ref[...] = meta


def _outproj(yr, proj, conv_w, conv_norm_w, w_out_bf, x2, mod3, norm2_w, wr, br, bsz, t_len):
    n, d = x2.shape
    dr = yr.shape[1]
    dc = conv_w.shape[1]
    tm = min(OUTPROJ_ROWS, t_len)
    nt = t_len // tm
    nseg = dr // dc
    row = lambda off: (lambda b, t: (b * nt + t, off))
    const = lambda b, t: (0, 0)
    return pl.pallas_call(
        _outproj_kernel,
        out_shape=(jax.ShapeDtypeStruct((n, d), F32), jax.ShapeDtypeStruct((n, d // 2), jnp.uint32),
                   jax.ShapeDtypeStruct((n, LANES), F32), jax.ShapeDtypeStruct((8, LANES), F32)),
        grid=(bsz, nt),
        in_specs=[pl.BlockSpec((tm, dr), row(0)),
                  pl.BlockSpec((tm, dc), row(3 * nseg)),
                  pl.BlockSpec((tm, dc), row(4 * nseg)),
                  pl.BlockSpec((tm, dc), row(5 * nseg)),
                  pl.BlockSpec((3, dc), const),
                  pl.BlockSpec((1, dc), const),
                  pl.BlockSpec((dr + dc, d), const),
                  pl.BlockSpec((tm, d), row(0)),
                  pl.BlockSpec((1, 6, d), lambda b, t: (b, 0, 0)),
                  pl.BlockSpec((1, d), const),
                  pl.BlockSpec((d, LANES), const),
                  pl.BlockSpec((1, LANES), const)],
        out_specs=(pl.BlockSpec((tm, d), row(0)), pl.BlockSpec((tm, d // 2), row(0)),
                   pl.BlockSpec((tm, LANES), row(0)), pl.BlockSpec((8, LANES), const)),
        scratch_shapes=[pltpu.VMEM((8, dc), F32), pltpu.VMEM((1, LANES), F32),
                        pltpu.VMEM((2, d, LANES), BF16)],
        compiler_params=pltpu.CompilerParams(dimension_semantics=("arbitrary", "arbitrary"),
                                             vmem_limit_bytes=VMEM_LIMIT),
        name="outproj",
    )(yr, proj, proj, proj, conv_w, conv_norm_w.reshape(1, dc), w_out_bf, x2, mod3,
      norm2_w.reshape(1, d), wr, br)


def _dispatch_kernel(zoff_ref, dest_hbm, h2_hbm, xs_hbm, idx_smem, tile_ref, zero_ref, zblk_ref,
                     sem_idx, sem_tile, sem_rows, sem_zero, sem_tail):
    i = pl.program_id(0)
    nsteps = pl.num_programs(0)
    th = idx_smem.shape[1] // 2
    slot = i % 2

    def idx_copy(step, sl):
        return pltpu.make_async_copy(dest_hbm.at[step], idx_smem.at[sl], sem_idx.at[sl])

    def tile_copy(step, sl):
        rows = pl.ds(pl.multiple_of(step * th, th), th)
        return pltpu.make_async_copy(h2_hbm.at[rows], tile_ref.at[sl], sem_tile.at[sl])

    def rows_wait(sl):
        pltpu.make_async_copy(xs_hbm.at[pl.ds(0, 2 * th)], xs_hbm.at[pl.ds(0, 2 * th)],
                              sem_rows.at[sl]).wait()

    nblk = xs_hbm.shape[0] // ROUTE_BLOCK

    def tail_copy(b):
        start = pl.multiple_of(b * ROUTE_BLOCK, ROUTE_BLOCK)
        return pltpu.make_async_copy(zblk_ref, xs_hbm.at[pl.ds(start, ROUTE_BLOCK)], sem_tail)

    @pl.when(i == 0)
    def _():
        idx_copy(0, 0).start()
        tile_copy(0, 0).start()
        zero_ref[...] = jnp.zeros_like(zero_ref)

        def zero_expert(e, carry):
            off = zoff_ref[e]

            def zero_row(j, c):
                pltpu.make_async_copy(zero_ref, xs_hbm.at[pl.ds(off + j, 1)], sem_zero).start()
                return c

            return lax.fori_loop(0, zoff_ref[N_EXPERTS + e], zero_row, carry)

        lax.fori_loop(0, N_EXPERTS, zero_expert, 0)
        total = zoff_ref[2 * N_EXPERTS]
        bulk = pl.multiple_of((total // 8) * 8, 8)

        @pl.when(bulk > 0)
        def _():
            pltpu.make_async_copy(xs_hbm.at[pl.ds(0, bulk)], xs_hbm.at[pl.ds(0, bulk)], sem_zero).wait()

        def wait_row(j, c):
            pltpu.make_async_copy(zero_ref, xs_hbm.at[pl.ds(0, 1)], sem_zero).wait()
            return c

        lax.fori_loop(0, total - bulk, wait_row, 0)

        zblk_ref[...] = jnp.zeros_like(zblk_ref)

        def tail_start(b, c):
            tail_copy(b).start()
            return c

        lax.fori_loop(zoff_ref[2 * N_EXPERTS + 1], nblk, tail_start, 0)

    idx_copy(i, slot).wait()
    tile_copy(i, slot).wait()

    @pl.when(i > 0)
    def _():
        rows_wait(1 - slot)

    @pl.when(i + 1 < nsteps)
    def _():
        idx_copy(i + 1, 1 - slot).start()
        tile_copy(i + 1, 1 - slot).start()

    for p in range(2):
        @pl.when(slot == p)
        def _(p=p):
            for t in range(th):
                for s in range(2):
                    d = idx_smem[p, 2 * t + s]
                    pltpu.make_async_copy(tile_ref.at[p, pl.ds(t, 1)], xs_hbm.at[pl.ds(d, 1)],
                                          sem_rows.at[p]).start(priority=s)

    @pl.when(i == nsteps - 1)
    def _():
        rows_wait(slot)

        def tail_wait(b, c):
            tail_copy(b).wait()
            return c

        lax.fori_loop(zoff_ref[2 * N_EXPERTS + 1], nblk, tail_wait, 0)


def _dispatch(dest, zero_off, h2, n_rows):
    n, d = h2.shape
    th = min(DISPATCH_ROWS, n)
    return pl.pallas_call(
        _dispatch_kernel,
        out_shape=jax.ShapeDtypeStruct((n_rows, d), h2.dtype),
        grid_spec=pltpu.PrefetchScalarGridSpec(
            num_scalar_prefetch=1,
            grid=(n // th,),
            in_specs=[pl.BlockSpec(memory_space=pl.ANY), pl.BlockSpec(memory_space=pl.ANY)],
            out_specs=pl.BlockSpec(memory_space=pl.ANY),
            scratch_shapes=[pltpu.SMEM((2, 2 * th), jnp.int32),
                            pltpu.VMEM((2, th, d), h2.dtype),
                            pltpu.VMEM((1, d), h2.dtype),
                            pltpu.VMEM((ROUTE_BLOCK, d), h2.dtype),
                            pltpu.SemaphoreType.DMA((2,)), pltpu.SemaphoreType.DMA((2,)),
                            pltpu.SemaphoreType.DMA((2,)),
                            pltpu.SemaphoreType.DMA, pltpu.SemaphoreType.DMA]),
        compiler_params=pltpu.CompilerParams(dimension_semantics=("arbitrary",),
                                             vmem_limit_bytes=VMEM_LIMIT),
        name="dispatch",
    )(zero_off, dest.reshape(n // th, 2 * th), h2)


def _expert_kernel(sched_ref, na_ref, xs_ref, wg_hbm, wu_hbm, wd_hbm, ys_ref,
                   wg_buf, wu_buf, wd_buf, sem):
    i = pl.program_id(0)

    def weight_copies(e, slot):
        return (pltpu.make_async_copy(wg_hbm.at[e], wg_buf.at[slot], sem.at[slot, 0]),
                pltpu.make_async_copy(wu_hbm.at[e], wu_buf.at[slot], sem.at[slot, 1]),
                pltpu.make_async_copy(wd_hbm.at[e], wd_buf.at[slot], sem.at[slot, 2]))

    @pl.when(i < na_ref[0])
    def _():
        e = sched_ref[0, i]
        slot = sched_ref[1, i]

        @pl.when(i == 0)
        def _():
            for cp in weight_copies(e, slot):
                cp.start()

        @pl.when(sched_ref[2, i] == 1)
        def _():
            for cp in weight_copies(e, slot):
                cp.wait()
            nxt = sched_ref[3, i]

            @pl.when(nxt >= 0)
            def _():
                for cp in weight_copies(nxt, 1 - slot):
                    cp.start()

        x = _unpack_bf16_pair(xs_ref[...]).astype(BF16)
        g = _dot(x, wg_buf[slot].astype(BF16))
        u = _dot(x, wu_buf[slot].astype(BF16))
        hid = (g * _sigmoid(g) * u).astype(BF16)
        ys_ref[...] = _pack_bf16_pair(_dot(hid, wd_buf[slot].astype(BF16)))

    @pl.when(i >= na_ref[0])
    def _():
        ys_ref[...] = jnp.zeros_like(ys_ref)


def _experts(block_e, n_active, xs, exp_gate, exp_up, exp_down):
    n_rows, dp = xs.shape
    _, d, de = exp_gate.shape
    nblk = n_rows // ROUTE_BLOCK
    idx = jnp.arange(nblk, dtype=jnp.int32)
    valid = idx < n_active[0]
    first = valid & ((idx == 0) | (block_e != jnp.roll(block_e, 1)))
    slot = jnp.maximum(jnp.cumsum(first.astype(jnp.int32)) - 1, 0) % 2
    first_at = jnp.where(first, idx, nblk)
    next_first = jnp.concatenate([lax.cummin(first_at, reverse=True)[1:], jnp.full((1,), nblk, jnp.int32)])
    nxt = jnp.where(next_first < nblk, block_e[jnp.minimum(next_first, nblk - 1)], -1)
    sched = jnp.stack([block_e, slot, first.astype(jnp.int32), nxt]).astype(jnp.int32)

    blk_row = lambda i, sc, na: (jnp.minimum(i, jnp.maximum(na[0] - 1, 0)), 0)
    return pl.pallas_call(
        _expert_kernel,
        out_shape=jax.ShapeDtypeStruct((n_rows, dp), jnp.uint32),
        grid_spec=pltpu.PrefetchScalarGridSpec(
            num_scalar_prefetch=2,
            grid=(nblk,),
            in_specs=[pl.BlockSpec((ROUTE_BLOCK, dp), blk_row),
                      pl.BlockSpec(memory_space=pl.ANY),
                      pl.BlockSpec(memory_space=pl.ANY),
                      pl.BlockSpec(memory_space=pl.ANY)],
            out_specs=pl.BlockSpec((ROUTE_BLOCK, dp), lambda i, sc, na: (i, 0)),
            scratch_shapes=[pltpu.VMEM((2, d, de), exp_gate.dtype),
                            pltpu.VMEM((2, d, de), exp_up.dtype),
                            pltpu.VMEM((2, de, d), exp_down.dtype),
                            pltpu.SemaphoreType.DMA((2, 3))]),
        compiler_params=pltpu.CompilerParams(dimension_semantics=("arbitrary",),
                                             vmem_limit_bytes=VMEM_LIMIT),
        name="experts",
    )(sched, n_active, xs, exp_gate, exp_up, exp_down)


def _combine_kernel(dest_hbm, ys_hbm, x1_ref, meta_ref, mod_ref, fw_ref, o_ref,
                    idx_smem, buf_ref, sem_idx, sem_rows):
    i = pl.program_id(0)
    nsteps = pl.num_programs(0)
    tj = x1_ref.shape[0]
    slot = i % 2

    def idx_copy(step, sl):
        return pltpu.make_async_copy(dest_hbm.at[step], idx_smem.at[sl], sem_idx.at[sl])

    def gather(sl):
        for t in range(tj):
            for s in range(2):
                d = idx_smem[sl, 2 * t + s]
                pltpu.make_async_copy(ys_hbm.at[pl.ds(d, 1)], buf_ref.at[sl, s, pl.ds(t, 1)],
                                      sem_rows.at[sl]).start(priority=s)

    @pl.when(i == 0)
    def _():
        first = idx_copy(0, 0)
        first.start()
        first.wait()
        gather(0)

        @pl.when(nsteps > 1)
        def _():
            idx_copy(1, 1).start()

    for p in range(2):
        @pl.when((i + 1 < nsteps) & (slot == 1 - p))
        def _(p=p):
            idx_copy(i + 1, p).wait()

            @pl.when(i + 2 < nsteps)
            def _():
                idx_copy(i + 2, 1 - p).start()

            gather(p)

    pltpu.make_async_copy(buf_ref.at[slot], buf_ref.at[slot], sem_rows.at[slot]).wait()

    meta = meta_ref[...]
    y = (meta[:, 2:3] * _unpack_bf16_pair(buf_ref[slot, 0])
         + meta[:, 3:4] * _unpack_bf16_pair(buf_ref[slot, 1]))
    x2 = x1_ref[...] + mod_ref[0, 5:6, :] * y
    ms = jnp.mean(x2 * x2, axis=-1, keepdims=True)
    o_ref[...] = x2 * lax.rsqrt(ms + RMS_EPS) * fw_ref[...]


def _combine(dest, ys, x1, meta, mod3, final_w, t_len):
    n, d = x1.shape
    tj = min(COMBINE_ROWS, t_len)
    return pl.pallas_call(
        _combine_kernel,
        out_shape=jax.ShapeDtypeStruct((n, d), F32),
        grid=(n // tj,),
        in_specs=[pl.BlockSpec(memory_space=pl.ANY),
                  pl.BlockSpec(memory_space=pl.ANY),
                  pl.BlockSpec((tj, d), lambda i: (i, 0)),
                  pl.BlockSpec((tj, LANES), lambda i: (i, 0)),
                  pl.BlockSpec((1, 6, d), lambda i: (i * tj // t_len, 0, 0)),
                  pl.BlockSpec((1, d), lambda i: (0, 0))],
        out_specs=pl.BlockSpec((tj, d), lambda i: (i, 0)),
        scratch_shapes=[pltpu.SMEM((2, 2 * tj), jnp.int32), pltpu.VMEM((2, 2, tj, d // 2), jnp.uint32),
                        pltpu.SemaphoreType.DMA((2,)), pltpu.SemaphoreType.DMA((2,))],
        compiler_params=pltpu.CompilerParams(dimension_semantics=("arbitrary",),
                                             vmem_limit_bytes=VMEM_LIMIT),
        name="combine",
    )(dest.reshape(n // tj, 2 * tj), ys, x1, meta, mod3, final_w.reshape(1, d))


def _pad_cols(w, width):
    return jnp.pad(w, ((0, 0), (0, width - w.shape[1])))


def _pad_rows(w, height):
    return jnp.pad(w, ((0, height - w.shape[0]), (0, 0)))


def _layer(x2, c, bsz, t_len, w_ada, b_ada, norm1_w, w_in, mu_r, mu_k, mu_v, mu_w, mu_a, mu_g, w0,
           w_decay1, w_decay2, a0, w_aaa1, w_aaa2, w_gate1, w_gate2, k_k, k_a, r_k, lnx_w, lnx_b,
           conv_w, conv_norm_w, w_out, norm2_w, w_grp, b_grp, w_exp, b_exp, exp_gate, exp_up,
           exp_down, final_w):
    n, d = x2.shape
    dr = w0.shape[0]
    widths = [hi - lo for lo, hi in (LORA_DECAY, LORA_AAA, LORA_GATE)]
    assert all(w.shape[1] <= n_ for w, n_ in zip((w_decay1, w_aaa1, w_gate1), widths))
    assert dr % GROUP_LANES == 0 and conv_w.shape[1] == dr and w_in.shape[1] == 6 * dr
    assert t_len % WKV_CHUNK == 0 and w_grp.shape[1] == N_ROUTE_GROUPS and w_exp.shape[1] == N_EXPERTS

    mod3 = _ada(c, w_ada, b_ada).reshape(bsz, 6, d)
    h, proj = _inproj(x2, mod3, norm1_w, w_in.astype(BF16), t_len)

    w1p = jnp.concatenate([_pad_cols(w, n_) for w, n_ in zip((w_decay1, w_aaa1, w_gate1), widths)], axis=1)
    mup = jnp.concatenate([jnp.broadcast_to(mu[:, None], (d, n_)) for mu, n_ in zip((mu_w, mu_a, mu_g), widths)],
                          axis=1)
    wd2p, wa2p, wg2p = (_pad_rows(w, n_) for w, n_ in zip((w_decay2, w_aaa2, w_gate2), widths))
    lw, ag, gg = _lora(h, w1p, mup, wd2p, wa2p, wg2p, w0, a0, bsz, t_len)

    yr = _wkv(proj, lw, ag, gg, mu_r, mu_k, mu_v, k_k, k_a, r_k.reshape(dr), lnx_w, lnx_b, bsz, t_len)

    wr = _pad_cols(jnp.concatenate([w_grp, w_exp], axis=1), LANES)
    br = _pad_cols(jnp.concatenate([b_grp, b_exp]).reshape(1, -1), LANES)
    x1, h2, meta, cnt = _outproj(yr, proj, conv_w, conv_norm_w, w_out.astype(BF16), x2, mod3,
                                 norm2_w, wr, br, bsz, t_len)

    counts = cnt[0, :N_EXPERTS].astype(jnp.int32)
    padded = (counts + ROUTE_BLOCK - 1) // ROUTE_BLOCK * ROUTE_BLOCK
    pad_ends = jnp.cumsum(padded)
    pad_starts = pad_ends - padded
    top_e = meta[:, 0:2].astype(jnp.int32)
    rank = meta[:, 4:6].astype(jnp.int32)
    one_hot = top_e[:, :, None] == jnp.arange(N_EXPERTS, dtype=jnp.int32)
    dest = (jnp.sum(jnp.where(one_hot, pad_starts, 0), axis=-1) + rank).reshape(-1)
    nblk = (2 * n) // ROUTE_BLOCK + N_EXPERTS
    n_rows = nblk * ROUTE_BLOCK
    block_start = jnp.arange(nblk, dtype=jnp.int32) * ROUTE_BLOCK
    block_e = jnp.minimum(jnp.sum(pad_ends[None, :] <= block_start[:, None], axis=1),
                          N_EXPERTS - 1).astype(jnp.int32)
    n_active = (pad_ends[-1:] // ROUTE_BLOCK).astype(jnp.int32)

    pad_len = padded - counts
    zero_off = jnp.concatenate([pad_starts + counts, pad_len, jnp.sum(pad_len, keepdims=True),
                                n_active]).astype(jnp.int32)
    xs = _dispatch(dest, zero_off, h2, n_rows)
    ys = _experts(block_e, n_active, xs, exp_gate, exp_up, exp_down)
    return _combine(dest, ys, x1, meta, mod3, final_w, t_len)


def kernel(x, c, w_ada, b_ada, norm1_w, w_in, mu_r, mu_k, mu_v, mu_w, mu_a, mu_g, w0, w_decay1, w_decay2, a0, w_aaa1, w_aaa2, w_gate1, w_gate2, k_k, k_a, r_k, lnx_w, lnx_b, conv_w, conv_norm_w, w_out, norm2_w, w_grp, b_grp, w_exp, b_exp, exp_gate, exp_up, exp_down, final_w):
    bsz, t_len, d = x.shape
    assert w_ada.shape[0] == 1, "single-layer block"
    out = _layer(x.reshape(bsz * t_len, d), c, bsz, t_len, w_ada[0], b_ada[0], norm1_w[0], w_in[0],
                 mu_r[0], mu_k[0], mu_v[0], mu_w[0], mu_a[0], mu_g[0], w0[0], w_decay1[0], w_decay2[0],
                 a0[0], w_aaa1[0], w_aaa2[0], w_gate1[0], w_gate2[0], k_k[0], k_a[0], r_k[0], lnx_w[0],
                 lnx_b[0], conv_w[0], conv_norm_w[0], w_out[0], norm2_w[0], w_grp[0], b_grp[0],
                 w_exp[0], b_exp[0], exp_gate[0], exp_up[0], exp_down[0], final_w)
    return out.reshape(bsz, t_len, d)
```

```python
import itertools

import jax
import jax.numpy as jnp
from jax import lax
from jax.experimental import pallas as pl
from jax.experimental.pallas import tpu as pltpu

F32 = jnp.float32
BF16 = jnp.bfloat16

HEAD_DIM = 64
WKV_CHUNK = 64
GROUP_LANES = 256
N_ROUTE_GROUPS = 4
EXPERTS_PER_GROUP = 8
N_EXPERTS = N_ROUTE_GROUPS * EXPERTS_PER_GROUP
CONV_GROUP_DIM = 64
RMS_EPS = 1e-6
LNX_EPS = 64e-5
DECAY_SCALE = 0.6065306597126334
LANES = 128
LORA_DECAY, LORA_AAA, LORA_GATE = (0, 128), (128, 256), (256, 512)
LORA_PAD = LORA_GATE[1]
ADA_COLS = 1024
INPROJ_ROWS, INPROJ_COLS = 1024, 1536
LORA_ROWS = 1024
WKV_ROWS = 1024
OUTPROJ_ROWS = 512
DISPATCH_ROWS = 256
COMBINE_ROWS = 256
ROUTE_BLOCK = 256
NORM_ROWS = 32
VMEM_LIMIT = 56 * 1024 * 1024


def _dot(a, b, precision=None):
    return jnp.dot(a, b, preferred_element_type=F32, precision=precision)


def _dot_nt(a, b):
    return lax.dot_general(a, b, (((1,), (1,)), ((), ())), preferred_element_type=F32)


def _dot_tn(a, b):
    return lax.dot_general(a, b, (((0,), (0,)), ((), ())), preferred_element_type=F32)


def _split_bf16(x):
    hi = x.astype(BF16)
    return hi, (x - hi.astype(F32)).astype(BF16)


def _pack_rounded_pair(xr):
    c = xr.shape[1] // 2
    lo = pltpu.bitcast(xr[:, :c], jnp.uint32)
    hi = pltpu.bitcast(xr[:, c:], jnp.uint32)
    return (lo >> 16) | (hi & jnp.uint32(0xFFFF0000))


def _pack_bf16_pair(x):
    return _pack_rounded_pair(x.astype(BF16).astype(F32))


def _unpack_bf16_pair(w):
    lo = pltpu.bitcast(w << 16, F32)
    hi = pltpu.bitcast(w & jnp.uint32(0xFFFF0000), F32)
    return jnp.concatenate([lo, hi], axis=1)


def _sigmoid(x):
    return 1.0 / (1.0 + jnp.exp(-x))


def _shift_rows(x, carry_row):
    rolled = pltpu.roll(x, 1, axis=0)
    row = lax.broadcasted_iota(jnp.int32, x.shape, 0)
    return jnp.where(row == 0, carry_row, rolled)


def _seg_ones(n, seg):
    r = lax.broadcasted_iota(jnp.int32, (n, n), 0) // seg
    c = lax.broadcasted_iota(jnp.int32, (n, n), 1) // seg
    return jnp.where(r == c, 1.0, 0.0).astype(BF16)


def _seg_sum1(x, ones_bd):
    return _dot(x.astype(BF16), ones_bd)


def _ada_kernel(cb_ref, w_ref, b_ref, o_ref):
    nb = cb_ref.shape[0]
    tn = w_ref.shape[1]
    for b in range(nb):
        cv = cb_ref[b]
        s = cv * _sigmoid(cv)
        cols = [jnp.sum(s * w_ref[:, j * LANES:(j + 1) * LANES], axis=0, keepdims=True)
                for j in range(tn // LANES)]
        o_ref[b:b + 1, :] = jnp.concatenate(cols, axis=1) + b_ref[...]


def _ada(c, w_ada, b_ada):
    nb, d = c.shape
    n_out = w_ada.shape[1]
    tn = ADA_COLS
    cb = jnp.broadcast_to(c[:, :, None], (nb, d, LANES))
    return pl.pallas_call(
        _ada_kernel,
        out_shape=jax.ShapeDtypeStruct((nb, n_out), F32),
        grid=(n_out // tn,),
        in_specs=[pl.BlockSpec((nb, d, LANES), lambda j: (0, 0, 0)),
                  pl.BlockSpec((d, tn), lambda j: (0, j)),
                  pl.BlockSpec((1, tn), lambda j: (0, j))],
        out_specs=pl.BlockSpec((nb, tn), lambda j: (0, j)),
        compiler_params=pltpu.CompilerParams(dimension_semantics=("arbitrary",),
                                             vmem_limit_bytes=VMEM_LIMIT),
        name="ada",
    )(cb, w_ada, b_ada.reshape(1, n_out))


def _inproj_kernel(x_ref, mod_ref, nw_ref, w_ref, h_ref, p_ref):
    @pl.when(pl.program_id(1) == 0)
    def _():
        rows = min(NORM_ROWS, x_ref.shape[0])
        gain = nw_ref[...] * (1.0 + mod_ref[0, 1:2, :])
        shift = mod_ref[0, 0:1, :]

        def norm_rows(r, carry):
            sl = pl.ds(pl.multiple_of(r * rows, rows), rows)
            x = x_ref[sl, :]
            ms = jnp.mean(x * x, axis=-1, keepdims=True)
            h = x * lax.rsqrt(ms + RMS_EPS) * gain + shift
            h_ref[sl, :] = h.astype(BF16)
            return carry

        lax.fori_loop(0, x_ref.shape[0] // rows, norm_rows, 0, unroll=4)

    p_ref[...] = _dot(h_ref[...], w_ref[...])


def _inproj(x2, mod3, norm_w, w_in_bf, t_len):
    n, d = x2.shape
    n_in = w_in_bf.shape[1]
    tm = min(INPROJ_ROWS, t_len)
    tn = INPROJ_COLS
    return pl.pallas_call(
        _inproj_kernel,
        out_shape=(jax.ShapeDtypeStruct((n, d), BF16), jax.ShapeDtypeStruct((n, n_in), F32)),
        grid=(n // tm, n_in // tn),
        in_specs=[pl.BlockSpec((tm, d), lambda i, j: (i, 0)),
                  pl.BlockSpec((1, 6, d), lambda i, j: (i * tm // t_len, 0, 0)),
                  pl.BlockSpec((1, d), lambda i, j: (0, 0)),
                  pl.BlockSpec((d, tn), lambda i, j: (0, j))],
        out_specs=(pl.BlockSpec((tm, d), lambda i, j: (i, 0)),
                   pl.BlockSpec((tm, tn), lambda i, j: (i, j))),
        compiler_params=pltpu.CompilerParams(dimension_semantics=("arbitrary", "arbitrary"),
                                             vmem_limit_bytes=VMEM_LIMIT),
        name="inproj",
    )(x2, mod3, norm_w.reshape(1, d), w_in_bf)


def _lora_kernel(h_ref, w1_ref, mu_ref, wd2_ref, wa2_ref, wg2_ref, w0_ref, a0_ref,
                 lw_ref, ag_ref, gg_ref, carry_ref, w12_ref):
    @pl.when(pl.program_id(1) == 0)
    def _():
        carry_ref[...] = jnp.zeros_like(carry_ref)

    @pl.when((pl.program_id(0) == 0) & (pl.program_id(1) == 0))
    def _():
        w1 = w1_ref[...]
        w12_ref[:, :LORA_PAD] = w1.astype(BF16)
        w12_ref[:, LORA_PAD:] = (w1 * mu_ref[...]).astype(BF16)

    p12 = _dot(h_ref[...], w12_ref[...])
    p1 = p12[:, :LORA_PAD]
    p2 = p12[:, LORA_PAD:]
    prev = _shift_rows(p2, carry_ref[...])
    carry_ref[...] = p2[p2.shape[0] - 1:, :]
    lo = p1 - p2 + prev
    dec = _dot(jnp.tanh(lo[:, slice(*LORA_DECAY)]).astype(BF16), wd2_ref[...].astype(BF16))
    lw_ref[...] = -DECAY_SCALE * _sigmoid(w0_ref[...] + dec)
    ag_ref[...] = _sigmoid(a0_ref[...] + _dot(lo[:, slice(*LORA_AAA)].astype(BF16), wa2_ref[...].astype(BF16)))
    gg_ref[...] = _dot(_sigmoid(lo[:, slice(*LORA_GATE)]).astype(BF16), wg2_ref[...].astype(BF16))


def _lora(h, w1p, mup, wd2p, wa2p, wg2p, w0, a0, bsz, t_len):
    n, d = h.shape
    dr = w0.shape[-1]
    tm = min(LORA_ROWS, t_len)
    nt = t_len // tm
    row = lambda b, t: (b * nt + t, 0)
    const = lambda b, t: (0, 0)
    out = jax.ShapeDtypeStruct((n, dr), F32)
    return pl.pallas_call(
        _lora_kernel,
        out_shape=(out, out, out),
        grid=(bsz, nt),
        in_specs=[pl.BlockSpec((tm, d), row),
                  pl.BlockSpec((d, LORA_PAD), const),
                  pl.BlockSpec((d, LORA_PAD), const),
                  pl.BlockSpec(wd2p.shape, const),
                  pl.BlockSpec(wa2p.shape, const),
                  pl.BlockSpec(wg2p.shape, const),
                  pl.BlockSpec((1, dr), const),
                  pl.BlockSpec((1, dr), const)],
        out_specs=(pl.BlockSpec((tm, dr), row),) * 3,
        scratch_shapes=[pltpu.VMEM((1, LORA_PAD), F32), pltpu.VMEM((d, 2 * LORA_PAD), BF16)],
        compiler_params=pltpu.CompilerParams(dimension_semantics=("arbitrary", "arbitrary"),
                                             vmem_limit_bytes=VMEM_LIMIT),
        name="lora",
    )(h, w1p, mup, wd2p, wa2p, wg2p, w0.reshape(1, dr), a0.reshape(1, dr))


def _wkv_kernel(r_ref, k_ref, v_ref, lw_ref, ag_ref, gg_ref, mur_ref, muk_ref, muv_ref,
                kk_ref, ka_ref, rk_ref, lnw_ref, lnb_ref, y_ref, s_ref, carry_ref):
    lt = r_ref.shape[0]
    gl = GROUP_LANES
    c_len = WKV_CHUNK
    heads = gl // HEAD_DIM

    @pl.when(pl.program_id(2) == 0)
    def _():
        s_ref[...] = jnp.zeros_like(s_ref)
        carry_ref[...] = jnp.zeros_like(carry_ref)

    ones_bd = _seg_ones(gl, HEAD_DIM)

    def lerp(ref, mu_ref, idx):
        z = ref[...]
        prev = _shift_rows(z, carry_ref[idx:idx + 1, :])
        carry_ref[idx:idx + 1, :] = z[lt - 1:, :]
        return z + (prev - z) * mu_ref[...]

    r1 = lerp(r_ref, mur_ref, 0)
    k1 = lerp(k_ref, muk_ref, 1)
    v1 = lerp(v_ref, muv_ref, 2)
    ag = ag_ref[...]
    kk = k1 * kk_ref[...]
    kk = kk * lax.rsqrt(jnp.maximum(_seg_sum1(kk * kk, ones_bd), 1e-24))
    k2 = k1 * (1.0 + (ag - 1.0) * ka_ref[...])
    a_all = -kk
    b_all = kk * ag
    lw = lw_ref[...]

    rows = lax.broadcasted_iota(jnp.int32, (gl, gl), 0)
    cols = lax.broadcasted_iota(jnp.int32, (gl, gl), 1)
    same_head = (rows // c_len) == (cols // HEAD_DIM)
    ts = lax.broadcasted_iota(jnp.int32, (c_len, gl), 0)
    tj = lax.broadcasted_iota(jnp.int32, (c_len, gl), 1) % c_len
    tri_strict = ts > tj
    tri_incl = ts >= tj
    eye = jnp.where(ts == tj, 1.0, 0.0).astype(F32)
    cr = lax.broadcasted_iota(jnp.int32, (c_len, c_len), 0)
    cc = lax.broadcasted_iota(jnp.int32, (c_len, c_len), 1)
    cum_b = jnp.where(cr >= cc, 1.0, 0.0).astype(BF16)

    def bd(x):
        xb = x.astype(BF16)
        return jnp.where(same_head, jnp.concatenate([xb] * heads, axis=0), jnp.zeros((), BF16))

    def fold(x):
        out = x[0:c_len]
        for hh in range(1, heads):
            out = out + x[hh * c_len:(hh + 1) * c_len]
        return out

    nch = lt // c_len
    sls = [slice(c * c_len, (c + 1) * c_len) for c in range(nch)]

    def cumsum3(x):
        hi = x.astype(BF16)
        r = x - hi.astype(F32)
        mid = r.astype(BF16)
        lo = (r - mid.astype(F32)).astype(BF16)
        return _dot(cum_b, hi) + _dot(cum_b, mid) + _dot(cum_b, lo)

    wc, g_s, n_s, rp, y0, s_in = {}, {}, {}, {}, {}, {}
    state = [s_ref[...]]

    def chunk_parallel(chunks):
        lws = {c: lw[sls[c]] for c in chunks}
        cs = {c: cumsum3(lws[c]) for c in chunks}
        yield
        e_cs = {c: jnp.exp(cs[c]) for c in chunks}
        e_neg = {c: jnp.exp(-cs[c]) for c in chunks}
        for c in chunks:
            wc[c] = e_cs[c][c_len - 1:, :]
        at = {c: a_all[sls[c]] * jnp.exp(cs[c] - lws[c]) for c in chunks}
        rt = {c: r1[sls[c]] * e_cs[c] for c in chunks}
        bt = {c: b_all[sls[c]] * e_neg[c] for c in chunks}
        kt = {c: k2[sls[c]] * e_neg[c] for c in chunks}
        vc = {c: v1[sls[c]] for c in chunks}
        yield
        lhs = {c: jnp.concatenate([at[c], rt[c]], axis=0).astype(BF16) for c in chunks}
        g_b = {c: _dot_nt(lhs[c], bd(bt[c])) for c in chunks}
        yield
        g_k = {c: _dot_nt(lhs[c], bd(kt[c])) for c in chunks}
        yield
        n_ab = {c: jnp.where(tri_strict, g_b[c][:c_len], 0.0) for c in chunks}
        a_ak = {c: jnp.where(tri_strict, g_k[c][:c_len], 0.0) for c in chunks}
        a_rb = {c: jnp.where(tri_incl, g_b[c][c_len:], 0.0).astype(BF16) for c in chunks}
        a_rk = {c: jnp.where(tri_incl, g_k[c][c_len:], 0.0).astype(BF16) for c in chunks}

        p = {c: eye + jnp.where((ts // 2) == (tj // 2), n_ab[c], 0.0) for c in chunks}
        for half in (2, 4, 8, 16, 32):
            lower_left = (((ts // (2 * half)) == (tj // (2 * half)))
                          & ((ts % (2 * half)) >= half) & ((tj % (2 * half)) < half))
            n21_ta = {c: _dot(jnp.where(lower_left, n_ab[c], 0.0).astype(BF16), bd(p[c])) for c in chunks}
            yield
            p = {c: p[c] + _dot(p[c].astype(BF16), bd(n21_ta[c])) for c in chunks}
            yield

        z = {c: _dot(p[c].astype(BF16), jnp.concatenate([bd(a_ak[c]), bd(at[c])], axis=1))
             for c in chunks}
        yield
        ap = {c: z[c][:, gl:] for c in chunks}
        bd_v = {c: bd(vc[c]) for c in chunks}
        u0 = {c: _dot(z[c][:, :gl].astype(BF16), bd_v[c]) for c in chunks}
        yield
        w = {c: _dot(a_rb[c], jnp.concatenate([bd(ap[c]), bd(u0[c])], axis=1)) for c in chunks}
        yield
        for c in chunks:
            rp[c] = rt[c] + w[c][:, :gl]
            y0[c] = w[c][:, gl:] + _dot(a_rk[c], bd_v[c])
        yield
        bh = {c: (bt[c] * wc[c]).astype(BF16) for c in chunks}
        bk = {c: jnp.concatenate([bh[c], (kt[c] * wc[c]).astype(BF16)], axis=0) for c in chunks}
        for c in chunks:
            g_s[c] = jnp.where(same_head, _dot_tn(ap[c].astype(BF16), bh[c]), 0.0).astype(BF16)
        yield
        for c in chunks:
            n_s[c] = fold(jnp.where(
                same_head, _dot_tn(jnp.concatenate([u0[c], vc[c]], axis=0).astype(BF16), bk[c]), 0.0))
        yield

    def state_update(chunks):
        for c in chunks:
            s_in[c] = state[0].astype(BF16)
            state[0] = state[0] * wc[c] + _dot(s_in[c], g_s[c]) + n_s[c]
            yield

    def epilogue(chunks):
        rows = slice(chunks[0] * c_len, (chunks[-1] + 1) * c_len)
        y = jnp.concatenate([_dot_nt(rp[c].astype(BF16), bd(s_in[c])) + y0[c] for c in chunks], axis=0)
        yield
        inv_n = 1.0 / HEAD_DIM
        mean = _seg_sum1(y, ones_bd) * inv_n
        yield
        yc = y - mean
        var = _seg_sum1(yc * yc, ones_bd) * inv_n
        yield
        yn = yc * lax.rsqrt(var + LNX_EPS) * lnw_ref[...] + lnb_ref[...]
        bonus = _seg_sum1(r1[rows] * k2[rows] * rk_ref[...], ones_bd) * v1[rows]
        y_ref[rows, :] = ((yn + bonus) * gg_ref[rows, :]).astype(y_ref.dtype)
        yield

    def drain(gen):
        for _ in gen:
            pass

    def interleave(serial, other, every):
        for k, _ in enumerate(other, 1):
            if k % every == 0:
                next(serial, None)
        drain(serial)

    wave = max(nch // 2, 1)
    waves = [list(range(w0, min(w0 + wave, nch))) for w0 in range(0, nch, wave)]
    par_stages, epi_stages = 20, 4
    drain(chunk_parallel(waves[0]))
    for wi, chunks in enumerate(waves):
        others, n_other = [], 0
        if wi + 1 < len(waves):
            others.append(chunk_parallel(waves[wi + 1]))
            n_other += par_stages
        if wi >= 1:
            others.append(epilogue(waves[wi - 1]))
            n_other += epi_stages
        interleave(state_update(chunks), itertools.chain(*others), max(n_other // len(chunks), 1))
    drain(epilogue(waves[-1]))
    s_ref[...] = state[0]


def _wkv(proj, lw, ag, gg, mu_r, mu_k, mu_v, k_k, k_a, r_k, lnx_w, lnx_b, bsz, t_len):
    n = proj.shape[0]
    dr = lw.shape[1]
    gl = GROUP_LANES
    ng = dr // gl
    lt = min(WKV_ROWS, t_len)
    nt = t_len // lt
    row = lambda off: (lambda b, g, t: (b * nt + t, off + g))
    par = lambda b, g, t: (0, g)
    vec = lambda a: a.reshape(1, dr)
    return pl.pallas_call(
        _wkv_kernel,
        out_shape=jax.ShapeDtypeStruct((n, dr), BF16),
        grid=(bsz, ng, nt),
        in_specs=[pl.BlockSpec((lt, gl), row(0)),
                  pl.BlockSpec((lt, gl), row(ng)),
                  pl.BlockSpec((lt, gl), row(2 * ng)),
                  pl.BlockSpec((lt, gl), row(0)),
                  pl.BlockSpec((lt, gl), row(0)),
                  pl.BlockSpec((lt, gl), row(0))] + [pl.BlockSpec((1, gl), par)] * 8,
        out_specs=pl.BlockSpec((lt, gl), row(0)),
        scratch_shapes=[pltpu.VMEM((HEAD_DIM, gl), F32), pltpu.VMEM((8, gl), F32)],
        compiler_params=pltpu.CompilerParams(
            dimension_semantics=("arbitrary", "arbitrary", "arbitrary"),
            vmem_limit_bytes=VMEM_LIMIT),
        name="wkv",
    )(proj, proj, proj, lw, ag, gg, vec(mu_r), vec(mu_k), vec(mu_v), vec(k_k), vec(k_a),
      vec(r_k), vec(lnx_w), vec(lnx_b))


def _outproj_kernel(yr_ref, u_ref, gb_ref, gc_ref, cw_ref, cnw_ref, wo_ref, x_ref, mod_ref,
                    n2w_ref, wr_ref, br_ref, x1_ref, h2_ref, meta_ref, cnt_ref,
                    ccarry_ref, cnt_acc_ref, wrs_ref):
    tm = x_ref.shape[0]
    dc = u_ref.shape[1]

    @pl.when(pl.program_id(1) == 0)
    def _():
        ccarry_ref[...] = jnp.zeros_like(ccarry_ref)

    @pl.when((pl.program_id(0) == 0) & (pl.program_id(1) == 0))
    def _():
        cnt_acc_ref[...] = jnp.zeros_like(cnt_acc_ref)
        wr_hi, wr_lo = _split_bf16(wr_ref[...])
        wrs_ref[0] = wr_hi
        wrs_ref[1] = wr_lo

    pre = gc_ref[...] * u_ref[...]
    prev1 = _shift_rows(pre, ccarry_ref[0:1, :])
    prev2 = _shift_rows(prev1, ccarry_ref[1:2, :])
    ccarry_ref[0:1, :] = pre[tm - 1:, :]
    ccarry_ref[1:2, :] = pre[tm - 2:tm - 1, :]
    yc = gb_ref[...] * (prev2 * cw_ref[0:1, :] + prev1 * cw_ref[1:2, :] + pre * cw_ref[2:3, :])
    ones_bd = _seg_ones(GROUP_LANES, CONV_GROUP_DIM)
    ycs = []
    for s in range(dc // GROUP_LANES):
        ys = yc[:, s * GROUP_LANES:(s + 1) * GROUP_LANES]
        ms = _seg_sum1(ys * ys, ones_bd) * (1.0 / CONV_GROUP_DIM)
        ycs.append(ys * lax.rsqrt(ms + RMS_EPS))
    y_conv = (jnp.concatenate(ycs, axis=1) * cnw_ref[...]).astype(BF16)

    dr = yr_ref.shape[1]
    y_mix = _dot(yr_ref[...], wo_ref[0:dr, :]) + _dot(y_conv, wo_ref[dr:, :])
    x1 = x_ref[...] + mod_ref[0, 2:3, :] * y_mix
    x1_ref[...] = x1
    ms = jnp.mean(x1 * x1, axis=-1, keepdims=True)
    gain2 = n2w_ref[...] * (1.0 + mod_ref[0, 4:5, :])
    h2 = x1 * lax.rsqrt(ms + RMS_EPS) * gain2 + mod_ref[0, 3:4, :]
    h2_hi = h2.astype(BF16)
    h2_hi32 = h2_hi.astype(F32)
    h2_ref[...] = _pack_rounded_pair(h2_hi32)
    h2_lo = (h2 - h2_hi32).astype(BF16)

    logits = _dot(h2_hi, wrs_ref[0]) + _dot(h2_lo, wrs_ref[0]) + _dot(h2_hi, wrs_ref[1]) + br_ref[...]
    lane = lax.broadcasted_iota(jnp.int32, logits.shape, 1)
    neg = jnp.float32(-jnp.inf)
    big = jnp.int32(1 << 20)
    glog = jnp.where(lane < N_ROUTE_GROUPS, logits, neg)
    gmax = jnp.max(glog, axis=-1, keepdims=True)
    grp = jnp.min(jnp.where(glog == gmax, lane, big), axis=-1, keepdims=True)
    p_grp = 1.0 / jnp.sum(jnp.exp(glog - gmax), axis=-1, keepdims=True)
    e_lane = lane - N_ROUTE_GROUPS
    in_grp = (e_lane >= grp * EXPERTS_PER_GROUP) & (e_lane < (grp + 1) * EXPERTS_PER_GROUP)
    elog = jnp.where(in_grp, logits, neg)
    m1 = jnp.max(elog, axis=-1, keepdims=True)
    i1 = jnp.min(jnp.where(elog == m1, lane, big), axis=-1, keepdims=True)
    elog2 = jnp.where(lane == i1, neg, elog)
    m2 = jnp.max(elog2, axis=-1, keepdims=True)
    i2 = jnp.min(jnp.where(elog2 == m2, lane, big), axis=-1, keepdims=True)
    t2 = jnp.exp(m2 - m1)
    w1 = p_grp / (1.0 + t2)
    w2 = p_grp * t2 / (1.0 + t2)
    e1 = i1 - N_ROUTE_GROUPS
    e2 = i2 - N_ROUTE_GROUPS

    oh1 = lane == e1
    oh2 = lane == e2
    oh = jnp.where(oh1 | oh2, 1.0, 0.0)
    tr = lax.broadcasted_iota(jnp.int32, (tm, tm), 0)
    tc = lax.broadcasted_iota(jnp.int32, (tm, tm), 1)
    lower = jnp.where(tr > tc, 1.0, 0.0).astype(BF16)
    before = _dot(lower, oh.astype(BF16)) + cnt_acc_ref[...]
    rank1 = jnp.sum(jnp.where(oh1, before, 0.0), axis=-1, keepdims=True)
    rank2 = jnp.sum(jnp.where(oh2, before, 0.0), axis=-1, keepdims=True)
    cnt = cnt_acc_ref[...] + jnp.sum(oh, axis=0, keepdims=True)
    cnt_acc_ref[...] = cnt
    cnt_ref[...] = jnp.broadcast_to(cnt, cnt_ref.shape)

    meta = jnp.where(lane == 0, e1.astype(F32), 0.0)
    meta = jnp.where(lane == 1, e2.astype(F32), meta)
    meta = jnp.where(lane == 2, w1, meta)
    meta = jnp.where(lane == 3, w2, meta)
    meta = jnp.where(lane == 4, rank1, meta)
    meta = jnp.where(lane == 5, rank2, meta)
    meta_ref[...] = meta


def _outproj(yr, proj, conv_w, conv_norm_w, w_out_bf, x2, mod3, norm2_w, wr, br, bsz, t_len):
    n, d = x2.shape
    dr = yr.shape[1]
    dc = conv_w.shape[1]
    tm = min(OUTPROJ_ROWS, t_len)
    nt = t_len // tm
    nseg = dr // dc
    row = lambda off: (lambda b, t: (b * nt + t, off))
    const = lambda b, t: (0, 0)
    return pl.pallas_call(
        _outproj_kernel,
        out_shape=(jax.ShapeDtypeStruct((n, d), F32), jax.ShapeDtypeStruct((n, d // 2), jnp.uint32),
                   jax.ShapeDtypeStruct((n, LANES), F32), jax.ShapeDtypeStruct((8, LANES), F32)),
        grid=(bsz, nt),
        in_specs=[pl.BlockSpec((tm, dr), row(0)),
                  pl.BlockSpec((tm, dc), row(3 * nseg)),
                  pl.BlockSpec((tm, dc), row(4 * nseg)),
                  pl.BlockSpec((tm, dc), row(5 * nseg)),
                  pl.BlockSpec((3, dc), const),
                  pl.BlockSpec((1, dc), const),
                  pl.BlockSpec((dr + dc, d), const),
                  pl.BlockSpec((tm, d), row(0)),
                  pl.BlockSpec((1, 6, d), lambda b, t: (b, 0, 0)),
                  pl.BlockSpec((1, d), const),
                  pl.BlockSpec((d, LANES), const),
                  pl.BlockSpec((1, LANES), const)],
        out_specs=(pl.BlockSpec((tm, d), row(0)), pl.BlockSpec((tm, d // 2), row(0)),
                   pl.BlockSpec((tm, LANES), row(0)), pl.BlockSpec((8, LANES), const)),
        scratch_shapes=[pltpu.VMEM((8, dc), F32), pltpu.VMEM((1, LANES), F32),
                        pltpu.VMEM((2, d, LANES), BF16)],
        compiler_params=pltpu.CompilerParams(dimension_semantics=("arbitrary", "arbitrary"),
                                             vmem_limit_bytes=VMEM_LIMIT),
        name="outproj",
    )(yr, proj, proj, proj, conv_w, conv_norm_w.reshape(1, dc), w_out_bf, x2, mod3,
      norm2_w.reshape(1, d), wr, br)


def _dispatch_kernel(zoff_ref, dest_hbm, h2_hbm, xs_hbm, idx_smem, tile_ref, zero_ref, zblk_ref,
                     sem_idx, sem_tile, sem_rows, sem_zero, sem_tail):
    i = pl.program_id(0)
    nsteps = pl.num_programs(0)
    th = idx_smem.shape[1] // 2
    slot = i % 2

    def idx_copy(step, sl):
        return pltpu.make_async_copy(dest_hbm.at[step], idx_smem.at[sl], sem_idx.at[sl])

    def tile_copy(step, sl):
        rows = pl.ds(pl.multiple_of(step * th, th), th)
        return pltpu.make_async_copy(h2_hbm.at[rows], tile_ref.at[sl], sem_tile.at[sl])

    def rows_wait(sl):
        pltpu.make_async_copy(xs_hbm.at[pl.ds(0, 2 * th)], xs_hbm.at[pl.ds(0, 2 * th)],
                              sem_rows.at[sl]).wait()

    nblk = xs_hbm.shape[0] // ROUTE_BLOCK

    def tail_copy(b):
        start = pl.multiple_of(b * ROUTE_BLOCK, ROUTE_BLOCK)
        return pltpu.make_async_copy(zblk_ref, xs_hbm.at[pl.ds(start, ROUTE_BLOCK)], sem_tail)

    @pl.when(i == 0)
    def _():
        idx_copy(0, 0).start()
        tile_copy(0, 0).start()
        zero_ref[...] = jnp.zeros_like(zero_ref)

        def zero_expert(e, carry):
            off = zoff_ref[e]

            def zero_row(j, c):
                pltpu.make_async_copy(zero_ref, xs_hbm.at[pl.ds(off + j, 1)], sem_zero).start()
                return c

            return lax.fori_loop(0, zoff_ref[N_EXPERTS + e], zero_row, carry)

        lax.fori_loop(0, N_EXPERTS, zero_expert, 0)

        zblk_ref[...] = jnp.zeros_like(zblk_ref)

        def tail_start(b, c):
            tail_copy(b).start()
            return c

        lax.fori_loop(zoff_ref[2 * N_EXPERTS + 1], nblk, tail_start, 0)

    idx_copy(i, slot).wait()
    tile_copy(i, slot).wait()

    @pl.when(i > 0)
    def _():
        rows_wait(1 - slot)

    @pl.when(i + 1 < nsteps)
    def _():
        idx_copy(i + 1, 1 - slot).start()
        tile_copy(i + 1, 1 - slot).start()

    for p in range(2):
        @pl.when(slot == p)
        def _(p=p):
            for t in range(th):
                for s in range(2):
                    d = idx_smem[p, 2 * t + s]
                    pltpu.make_async_copy(tile_ref.at[p, pl.ds(t, 1)], xs_hbm.at[pl.ds(d, 1)],
                                          sem_rows.at[p]).start(priority=s)

    @pl.when(i == nsteps - 1)
    def _():
        rows_wait(slot)

        total = zoff_ref[2 * N_EXPERTS]
        bulk = pl.multiple_of((total // 8) * 8, 8)

        @pl.when(bulk > 0)
        def _():
            pltpu.make_async_copy(xs_hbm.at[pl.ds(0, bulk)], xs_hbm.at[pl.ds(0, bulk)], sem_zero).wait()

        def wait_row(j, c):
            pltpu.make_async_copy(zero_ref, xs_hbm.at[pl.ds(0, 1)], sem_zero).wait()
            return c

        lax.fori_loop(0, total - bulk, wait_row, 0)

        def tail_wait(b, c):
            tail_copy(b).wait()
            return c

        lax.fori_loop(zoff_ref[2 * N_EXPERTS + 1], nblk, tail_wait, 0)


def _dispatch(dest, zero_off, h2, n_rows):
    n, d = h2.shape
    th = min(DISPATCH_ROWS, n)
    return pl.pallas_call(
        _dispatch_kernel,
        out_shape=jax.ShapeDtypeStruct((n_rows, d), h2.dtype),
        grid_spec=pltpu.PrefetchScalarGridSpec(
            num_scalar_prefetch=1,
            grid=(n // th,),
            in_specs=[pl.BlockSpec(memory_space=pl.ANY), pl.BlockSpec(memory_space=pl.ANY)],
            out_specs=pl.BlockSpec(memory_space=pl.ANY),
            scratch_shapes=[pltpu.SMEM((2, 2 * th), jnp.int32),
                            pltpu.VMEM((2, th, d), h2.dtype),
                            pltpu.VMEM((1, d), h2.dtype),
                            pltpu.VMEM((ROUTE_BLOCK, d), h2.dtype),
                            pltpu.SemaphoreType.DMA((2,)), pltpu.SemaphoreType.DMA((2,)),
                            pltpu.SemaphoreType.DMA((2,)),
                            pltpu.SemaphoreType.DMA, pltpu.SemaphoreType.DMA]),
        compiler_params=pltpu.CompilerParams(dimension_semantics=("arbitrary",),
                                             vmem_limit_bytes=VMEM_LIMIT),
        name="dispatch",
    )(zero_off, dest.reshape(n // th, 2 * th), h2)


def _expert_kernel(sched_ref, na_ref, xs_ref, wg_hbm, wu_hbm, wd_hbm, ys_ref,
                   wg_buf, wu_buf, wd_buf, sem):
    i = pl.program_id(0)

    def weight_copies(e, slot):
        return (pltpu.make_async_copy(wg_hbm.at[e], wg_buf.at[slot], sem.at[slot, 0]),
                pltpu.make_async_copy(wu_hbm.at[e], wu_buf.at[slot], sem.at[slot, 1]),
                pltpu.make_async_copy(wd_hbm.at[e], wd_buf.at[slot], sem.at[slot, 2]))

    @pl.when(i < na_ref[0])
    def _():
        e = sched_ref[0, i]
        slot = sched_ref[1, i]

        @pl.when(i == 0)
        def _():
            for cp in weight_copies(e, slot):
                cp.start()

        @pl.when(sched_ref[2, i] == 1)
        def _():
            for cp in weight_copies(e, slot):
                cp.wait()
            nxt = sched_ref[3, i]

            @pl.when(nxt >= 0)
            def _():
                for cp in weight_copies(nxt, 1 - slot):
                    cp.start()

        x = _unpack_bf16_pair(xs_ref[...]).astype(BF16)
        g = _dot(x, wg_buf[slot].astype(BF16))
        u = _dot(x, wu_buf[slot].astype(BF16))
        hid = (g * _sigmoid(g) * u).astype(BF16)
        ys_ref[...] = _pack_bf16_pair(_dot(hid, wd_buf[slot].astype(BF16)))

    @pl.when(i >= na_ref[0])
    def _():
        ys_ref[...] = jnp.zeros_like(ys_ref)


def _experts(block_e, n_active, xs, exp_gate, exp_up, exp_down):
    n_rows, dp = xs.shape
    _, d, de = exp_gate.shape
    nblk = n_rows // ROUTE_BLOCK
    idx = jnp.arange(nblk, dtype=jnp.int32)
    valid = idx < n_active[0]
    first = valid & ((idx == 0) | (block_e != jnp.roll(block_e, 1)))
    slot = jnp.maximum(jnp.cumsum(first.astype(jnp.int32)) - 1, 0) % 2
    first_at = jnp.where(first, idx, nblk)
    next_first = jnp.concatenate([lax.cummin(first_at, reverse=True)[1:], jnp.full((1,), nblk, jnp.int32)])
    nxt = jnp.where(next_first < nblk, block_e[jnp.minimum(next_first, nblk - 1)], -1)
    sched = jnp.stack([block_e, slot, first.astype(jnp.int32), nxt]).astype(jnp.int32)

    blk_row = lambda i, sc, na: (jnp.minimum(i, jnp.maximum(na[0] - 1, 0)), 0)
    return pl.pallas_call(
        _expert_kernel,
        out_shape=jax.ShapeDtypeStruct((n_rows, dp), jnp.uint32),
        grid_spec=pltpu.PrefetchScalarGridSpec(
            num_scalar_prefetch=2,
            grid=(nblk,),
            in_specs=[pl.BlockSpec((ROUTE_BLOCK, dp), blk_row),
                      pl.BlockSpec(memory_space=pl.ANY),
                      pl.BlockSpec(memory_space=pl.ANY),
                      pl.BlockSpec(memory_space=pl.ANY)],
            out_specs=pl.BlockSpec((ROUTE_BLOCK, dp), lambda i, sc, na: (i, 0)),
            scratch_shapes=[pltpu.VMEM((2, d, de), exp_gate.dtype),
                            pltpu.VMEM((2, d, de), exp_up.dtype),
                            pltpu.VMEM((2, de, d), exp_down.dtype),
                            pltpu.SemaphoreType.DMA((2, 3))]),
        compiler_params=pltpu.CompilerParams(dimension_semantics=("arbitrary",),
                                             vmem_limit_bytes=VMEM_LIMIT),
        name="experts",
    )(sched, n_active, xs, exp_gate, exp_up, exp_down)


def _combine_kernel(dest_hbm, ys_hbm, x1_ref, meta_ref, mod_ref, fw_ref, o_ref,
                    idx_smem, buf_ref, sem_idx, sem_rows):
    i = pl.program_id(0)
    nsteps = pl.num_programs(0)
    tj = x1_ref.shape[0]
    slot = i % 2

    def idx_copy(step, sl):
        return pltpu.make_async_copy(dest_hbm.at[step], idx_smem.at[sl], sem_idx.at[sl])

    def gather(sl):
        for t in range(tj):
            for s in range(2):
                d = idx_smem[sl, 2 * t + s]
                pltpu.make_async_copy(ys_hbm.at[pl.ds(d, 1)], buf_ref.at[sl, s, pl.ds(t, 1)],
                                      sem_rows.at[sl]).start(priority=s)

    @pl.when(i == 0)
    def _():
        first = idx_copy(0, 0)
        first.start()
        first.wait()
        gather(0)

        @pl.when(nsteps > 1)
        def _():
            idx_copy(1, 1).start()

    for p in range(2):
        @pl.when((i + 1 < nsteps) & (slot == 1 - p))
        def _(p=p):
            idx_copy(i + 1, p).wait()

            @pl.when(i + 2 < nsteps)
            def _():
                idx_copy(i + 2, 1 - p).start()

            gather(p)

    pltpu.make_async_copy(buf_ref.at[slot], buf_ref.at[slot], sem_rows.at[slot]).wait()

    meta = meta_ref[...]
    y = (meta[:, 2:3] * _unpack_bf16_pair(buf_ref[slot, 0])
         + meta[:, 3:4] * _unpack_bf16_pair(buf_ref[slot, 1]))
    x2 = x1_ref[...] + mod_ref[0, 5:6, :] * y
    ms = jnp.mean(x2 * x2, axis=-1, keepdims=True)
    o_ref[...] = x2 * lax.rsqrt(ms + RMS_EPS) * fw_ref[...]


def _combine(dest, ys, x1, meta, mod3, final_w, t_len):
    n, d = x1.shape
    tj = min(COMBINE_ROWS, t_len)
    return pl.pallas_call(
        _combine_kernel,
        out_shape=jax.ShapeDtypeStruct((n, d), F32),
        grid=(n // tj,),
        in_specs=[pl.BlockSpec(memory_space=pl.ANY),
                  pl.BlockSpec(memory_space=pl.ANY),
                  pl.BlockSpec((tj, d), lambda i: (i, 0)),
                  pl.BlockSpec((tj, LANES), lambda i: (i, 0)),
                  pl.BlockSpec((1, 6, d), lambda i: (i * tj // t_len, 0, 0)),
                  pl.BlockSpec((1, d), lambda i: (0, 0))],
        out_specs=pl.BlockSpec((tj, d), lambda i: (i, 0)),
        scratch_shapes=[pltpu.SMEM((2, 2 * tj), jnp.int32), pltpu.VMEM((2, 2, tj, d // 2), jnp.uint32),
                        pltpu.SemaphoreType.DMA((2,)), pltpu.SemaphoreType.DMA((2,))],
        compiler_params=pltpu.CompilerParams(dimension_semantics=("arbitrary",),
                                             vmem_limit_bytes=VMEM_LIMIT),
        name="combine",
    )(dest.reshape(n // tj, 2 * tj), ys, x1, meta, mod3, final_w.reshape(1, d))


def _pad_cols(w, width):
    return jnp.pad(w, ((0, 0), (0, width - w.shape[1])))


def _pad_rows(w, height):
    return jnp.pad(w, ((0, height - w.shape[0]), (0, 0)))


def _layer(x2, c, bsz, t_len, w_ada, b_ada, norm1_w, w_in, mu_r, mu_k, mu_v, mu_w, mu_a, mu_g, w0,
           w_decay1, w_decay2, a0, w_aaa1, w_aaa2, w_gate1, w_gate2, k_k, k_a, r_k, lnx_w, lnx_b,
           conv_w, conv_norm_w, w_out, norm2_w, w_grp, b_grp, w_exp, b_exp, exp_gate, exp_up,
           exp_down, final_w):
    n, d = x2.shape
    dr = w0.shape[0]
    widths = [hi - lo for lo, hi in (LORA_DECAY, LORA_AAA, LORA_GATE)]
    assert all(w.shape[1] <= n_ for w, n_ in zip((w_decay1, w_aaa1, w_gate1), widths))
    assert dr % GROUP_LANES == 0 and conv_w.shape[1] == dr and w_in.shape[1] == 6 * dr
    assert t_len % WKV_CHUNK == 0 and w_grp.shape[1] == N_ROUTE_GROUPS and w_exp.shape[1] == N_EXPERTS

    mod3 = _ada(c, w_ada, b_ada).reshape(bsz, 6, d)
    h, proj = _inproj(x2, mod3, norm1_w, w_in.astype(BF16), t_len)

    w1p = jnp.concatenate([_pad_cols(w, n_) for w, n_ in zip((w_decay1, w_aaa1, w_gate1), widths)], axis=1)
    mup = jnp.concatenate([jnp.broadcast_to(mu[:, None], (d, n_)) for mu, n_ in zip((mu_w, mu_a, mu_g), widths)],
                          axis=1)
    wd2p, wa2p, wg2p = (_pad_rows(w, n_) for w, n_ in zip((w_decay2, w_aaa2, w_gate2), widths))
    lw, ag, gg = _lora(h, w1p, mup, wd2p, wa2p, wg2p, w0, a0, bsz, t_len)

    yr = _wkv(proj, lw, ag, gg, mu_r, mu_k, mu_v, k_k, k_a, r_k.reshape(dr), lnx_w, lnx_b, bsz, t_len)

    wr = _pad_cols(jnp.concatenate([w_grp, w_exp], axis=1), LANES)
    br = _pad_cols(jnp.concatenate([b_grp, b_exp]).reshape(1, -1), LANES)
    x1, h2, meta, cnt = _outproj(yr, proj, conv_w, conv_norm_w, w_out.astype(BF16), x2, mod3,
                                 norm2_w, wr, br, bsz, t_len)

    counts = cnt[0, :N_EXPERTS].astype(jnp.int32)
    padded = (counts + ROUTE_BLOCK - 1) // ROUTE_BLOCK * ROUTE_BLOCK
    pad_ends = jnp.cumsum(padded)
    pad_starts = pad_ends - padded
    top_e = meta[:, 0:2].astype(jnp.int32)
    rank = meta[:, 4:6].astype(jnp.int32)
    one_hot = top_e[:, :, None] == jnp.arange(N_EXPERTS, dtype=jnp.int32)
    dest = (jnp.sum(jnp.where(one_hot, pad_starts, 0), axis=-1) + rank).reshape(-1)
    nblk = (2 * n) // ROUTE_BLOCK + N_EXPERTS
    n_rows = nblk * ROUTE_BLOCK
    block_start = jnp.arange(nblk, dtype=jnp.int32) * ROUTE_BLOCK
    block_e = jnp.minimum(jnp.sum(pad_ends[None, :] <= block_start[:, None], axis=1),
                          N_EXPERTS - 1).astype(jnp.int32)
    n_active = (pad_ends[-1:] // ROUTE_BLOCK).astype(jnp.int32)

    pad_len = padded - counts
    zero_off = jnp.concatenate([pad_starts + counts, pad_len, jnp.sum(pad_len, keepdims=True),
                                n_active]).astype(jnp.int32)
    xs = _dispatch(dest, zero_off, h2, n_rows)
    ys = _experts(block_e, n_active, xs, exp_gate, exp_up, exp_down)
    return _combine(dest, ys, x1, meta, mod3, final_w, t_len)


def kernel(x, c, w_ada, b_ada, norm1_w, w_in, mu_r, mu_k, mu_v, mu_w, mu_a, mu_g, w0, w_decay1, w_decay2, a0, w_aaa1, w_aaa2, w_gate1, w_gate2, k_k, k_a, r_k, lnx_w, lnx_b, conv_w, conv_norm_w, w_out, norm2_w, w_grp, b_grp, w_exp, b_exp, exp_gate, exp_up, exp_down, final_w):
    bsz, t_len, d = x.shape
    assert w_ada.shape[0] == 1, "single-layer block"
    out = _layer(x.reshape(bsz * t_len, d), c, bsz, t_len, w_ada[0], b_ada[0], norm1_w[0], w_in[0],
                 mu_r[0], mu_k[0], mu_v[0], mu_w[0], mu_a[0], mu_g[0], w0[0], w_decay1[0], w_decay2[0],
                 a0[0], w_aaa1[0], w_aaa2[0], w_gate1[0], w_gate2[0], k_k[0], k_a[0], r_k[0], lnx_w[0],
                 lnx_b[0], conv_w[0], conv_norm_w[0], w_out[0], norm2_w[0], w_grp[0], b_grp[0],
                 w_exp[0], b_exp[0], exp_gate[0], exp_up[0], exp_down[0], final_w)
    return out.reshape(bsz, t_len, d)
```
